```python
import math
import jax, jax.numpy as jnp
from jax import lax
import numpy as np

D_MODEL = 2048
BATCH = 4
SEQ = 2048
DEPTH = 2

GRID_W = 64
CTX_LEN = 256
Q_BLOCK = 128
ROPE_THETA = 10000.0
NORM_EPS = 1e-6
D_MIX = D_MODEL

RWKV_HEAD = 64
RWKV_HEADS = (D_MIX // 4) // RWKV_HEAD
RWKV_W = RWKV_HEADS * RWKV_HEAD
DECAY_RANK = 64
ICLR_RANK = 64
GATE_RANK = 128
GN_EPS = 64e-5
DECAY_SCALE = math.exp(-0.5)

GQA_HEAD = 128
GQA_Q_HEADS = (D_MIX // 2) // GQA_HEAD
GQA_KV_HEADS = 2
GQA_REP = GQA_Q_HEADS // GQA_KV_HEADS

MLA_HEADS = 4
MLA_NOPE = 128
MLA_ROPE = 64
MLA_V = (D_MIX // 4) // MLA_HEADS
MLA_Q_RANK = 384
MLA_KV_RANK = 256

N_GROUPS = 4
EXPERTS_PER_GROUP = 8
N_EXPERTS = N_GROUPS * EXPERTS_PER_GROUP
TOP_K = 2
D_EXPERT = 256

RWKV_SPLITS = (RWKV_W, RWKV_W, RWKV_W, DECAY_RANK, ICLR_RANK, GATE_RANK)
GQA_SPLITS = (GQA_Q_HEADS * GQA_HEAD, GQA_KV_HEADS * GQA_HEAD, GQA_KV_HEADS * GQA_HEAD)
MLA_SPLITS = (MLA_Q_RANK, MLA_KV_RANK, MLA_ROPE)
RWKV_COLS = sum(RWKV_SPLITS)
GQA_COLS = sum(GQA_SPLITS)
MLA_COLS = sum(MLA_SPLITS)
D_IN = RWKV_COLS + GQA_COLS + MLA_COLS

kernel_name = "hybrid_rwkv7_gqa_mla_hmoe_dit"


def split_cols(x, sizes):
    idx = np.cumsum(sizes)[:-1].tolist()
    return jnp.split(x, idx, axis=-1)


def rmsnorm(x, g):
    xf = x.astype(jnp.float32)
    y = xf * lax.rsqrt(jnp.mean(xf * xf, axis=-1, keepdims=True) + NORM_EPS)
    return y.astype(x.dtype) * g


def modulate(h, shift, scale):
    return h * (1 + scale) + shift


def to_heads(p, n_heads):
    B, T, _ = p.shape
    return p.reshape(B, T, n_heads, -1).transpose(0, 2, 1, 3)


def axial_rope(x, row, col):
    n = x.shape[-1]
    half = n // 2
    quarter = half // 2
    inv = ROPE_THETA ** (-jnp.arange(quarter, dtype=jnp.float32) / quarter)

    def rot(xa, pos):
        ang = pos[:, None] * inv[None, :]
        cos = jnp.cos(ang).astype(x.dtype)
        sin = jnp.sin(ang).astype(x.dtype)
        x1, x2 = xa[..., :quarter], xa[..., quarter:]
        return jnp.concatenate([x1 * cos - x2 * sin, x1 * sin + x2 * cos], axis=-1)

    return jnp.concatenate([rot(x[..., :half], row), rot(x[..., half:], col)], axis=-1)


def sweep_attention(q, k, v, scale):
    B, G, R, Tq, dk = q.shape
    nb = Tq // Q_BLOCK
    qb = q.reshape(B, G, R, nb, Q_BLOCK, dk).transpose(3, 0, 1, 2, 4, 5)

    def one_block(qblk):
        s = jnp.einsum('bgrqd,bgkd->bgrqk', qblk, k).astype(jnp.float32) * scale
        p = jax.nn.softmax(s, axis=-1)
        return jnp.einsum('bgrqk,bgkd->bgrqd', p.astype(v.dtype), v)

    o = lax.map(one_block, qb)
    return o.transpose(1, 2, 3, 0, 4, 5).reshape(B, G, R, Tq, v.shape[-1])


def attend_heads(q, k, v, rep, scale):
    B, Hq, T, d = q.shape
    o = sweep_attention(q.reshape(B, Hq // rep, rep, T, d), k, v, scale)
    dv = v.shape[-1]
    return o.reshape(B, Hq, T, dv).transpose(0, 2, 1, 3).reshape(B, T, Hq * dv)


def token_shift(p, mu_prev, mu_next):
    prev = jnp.pad(p[:, :-1], ((0, 0), (1, 0), (0, 0)))
    nxt = jnp.pad(p[:, 1:], ((0, 0), (0, 1), (0, 0)))
    return p + mu_prev * (prev - p) + mu_next * (nxt - p)


def rwkv_prep(p, mu_prev, mu_next, w0, w_up, a0, a_up, k_k, k_a):
    z = token_shift(p, mu_prev, mu_next)
    r, k, v, wd, ad, gd = split_cols(z, RWKV_SPLITS)
    B, T, _ = p.shape
    hd = lambda t: t.reshape(B, T, RWKV_HEADS, RWKV_HEAD)
    kap = hd(k * k_k).astype(jnp.float32)
    khat = kap * lax.rsqrt(jnp.sum(kap * kap, axis=-1, keepdims=True) + 1e-12)
    wd_t = jnp.tanh(wd)
    dirs = []
    for d in range(2):
        w = jnp.exp(-DECAY_SCALE * jax.nn.sigmoid((w0[d] + wd_t @ w_up[d]).astype(jnp.float32)))
        a = jax.nn.sigmoid((a0[d] + ad @ a_up[d]).astype(jnp.float32))
        kt = k.astype(jnp.float32) * (1 + (a - 1) * k_a)
        dirs.append((hd(w), hd(a) * khat, hd(kt)))
    return hd(r), hd(v), khat, dirs, gd


def wkv_scan(s0, r, w, khat, b, v, kt, reverse, emit):
    tm = lambda t: jnp.moveaxis(t.astype(jnp.float32), 1, 0)

    def step(s, inp):
        r_t, w_t, kh_t, b_t, v_t, kt_t = inp
        sk = jnp.einsum('bhvk,bhk->bhv', s, kh_t)
        s = s * w_t[:, :, None, :] - sk[..., None] * b_t[:, :, None, :] + v_t[..., None] * kt_t[:, :, None, :]
        y = jnp.einsum('bhvk,bhk->bhv', s, r_t) if emit else None
        return s, y

    xs = (tm(r), tm(w), tm(khat), tm(b), tm(v), tm(kt))
    s_fin, ys = lax.scan(step, s0, xs, reverse=reverse)
    return s_fin, (jnp.moveaxis(ys, 0, 1) if emit else None)


def rwkv_readout(r, v, kts, ys, gd, g_up, r_k, gn_g, gn_b):
    B, T, H, K = r.shape
    y = ys[0] + ys[1]
    mu = jnp.mean(y, axis=-1, keepdims=True)
    var = jnp.mean(jnp.square(y - mu), axis=-1, keepdims=True)
    yn = (y - mu) * lax.rsqrt(var + GN_EPS) * gn_g.reshape(H, K) + gn_b.reshape(H, K)
    bonus = jnp.sum(r.astype(jnp.float32) * (kts[0] + kts[1]) * r_k.reshape(H, K), axis=-1, keepdims=True) * v.astype(jnp.float32)
    g = jax.nn.sigmoid(gd) @ g_up
    return (yn + bonus).reshape(B, T, H * K).astype(gd.dtype) * g


def rwkv_mixer(p_lat, p_ctx, mu_prev, mu_next, w0, w_up, a0, a_up, g_up, k_k, k_a, r_k, gn_g, gn_b, need_ctx):
    B = p_lat.shape[0]
    s0 = jnp.zeros((B, RWKV_HEADS, RWKV_HEAD, RWKV_HEAD), jnp.float32)
    r_c, v_c, kh_c, dirs_c, gd_c = rwkv_prep(p_ctx, mu_prev, mu_next, w0, w_up, a0, a_up, k_k, k_a)
    r_l, v_l, kh_l, dirs_l, gd_l = rwkv_prep(p_lat, mu_prev, mu_next, w0, w_up, a0, a_up, k_k, k_a)
    ys_l, ys_c = [], []
    for d in range(2):
        rev = d == 1
        w_c, b_c, kt_c = dirs_c[d]
        w_l, b_l, kt_l = dirs_l[d]
        s_c, y_c = wkv_scan(s0, r_c, w_c, kh_c, b_c, v_c, kt_c, rev, need_ctx)
        _, y_l = wkv_scan(s_c, r_l, w_l, kh_l, b_l, v_l, kt_l, rev, True)
        ys_l.append(y_l)
        ys_c.append(y_c)
    o_l = rwkv_readout(r_l, v_l, [dirs_l[0][2], dirs_l[1][2]], ys_l, gd_l, g_up, r_k, gn_g, gn_b)
    o_c = rwkv_readout(r_c, v_c, [dirs_c[0][2], dirs_c[1][2]], ys_c, gd_c, g_up, r_k, gn_g, gn_b) if need_ctx else None
    return o_l, o_c


def gqa_mixer(p_lat, p_ctx, row, col, q_norm_g, k_norm_g, need_ctx):
    def kv(p):
        _, k, v = split_cols(p, GQA_SPLITS)
        return rmsnorm(to_heads(k, GQA_KV_HEADS), k_norm_g), to_heads(v, GQA_KV_HEADS)

    def q_of(p):
        return rmsnorm(to_heads(split_cols(p, GQA_SPLITS)[0], GQA_Q_HEADS), q_norm_g)

    scale = GQA_HEAD ** -0.5
    k_c, v_c = kv(p_ctx)
    k_l, v_l = kv(p_lat)
    k_l = axial_rope(k_l, row, col)
    q_l = axial_rope(q_of(p_lat), row, col)
    o_l = attend_heads(q_l, jnp.concatenate([k_c, k_l], axis=2), jnp.concatenate([v_c, v_l], axis=2), GQA_REP, scale)
    o_c = attend_heads(q_of(p_ctx), k_c, v_c, GQA_REP, scale) if need_ctx else None
    return o_l, o_c


def mla_kv(p, kv_norm_g, w_ukv, row, col, latent):
    _, ckv, kr = split_cols(p, MLA_SPLITS)
    B, T, _ = p.shape
    kv = to_heads(rmsnorm(ckv, kv_norm_g) @ w_ukv, MLA_HEADS)
    k_nope, v = kv[..., :MLA_NOPE], kv[..., MLA_NOPE:]
    if latent:
        kr = axial_rope(kr, row, col)
    kr = jnp.broadcast_to(kr[:, None], (B, MLA_HEADS, T, MLA_ROPE))
    return jnp.concatenate([k_nope, kr], axis=-1), v


def mla_q(p, q_norm_g, w_uq, row, col, latent):
    cq = split_cols(p, MLA_SPLITS)[0]
    q = to_heads(rmsnorm(cq, q_norm_g) @ w_uq, MLA_HEADS)
    q_nope, q_rope = q[..., :MLA_NOPE], q[..., MLA_NOPE:]
    if latent:
        q_rope = axial_rope(q_rope, row, col)
    return jnp.concatenate([q_nope, q_rope], axis=-1)


def mla_mixer(p_lat, p_ctx, row, col, q_norm_g, w_uq, kv_norm_g, w_ukv, need_ctx):
    scale = (MLA_NOPE + MLA_ROPE) ** -0.5
    k_c, v_c = mla_kv(p_ctx, kv_norm_g, w_ukv, row, col, False)
    k_l, v_l = mla_kv(p_lat, kv_norm_g, w_ukv, row, col, True)
    q_l = mla_q(p_lat, q_norm_g, w_uq, row, col, True)
    o_l = attend_heads(q_l, jnp.concatenate([k_c, k_l], axis=2), jnp.concatenate([v_c, v_l], axis=2), 1, scale)
    o_c = attend_heads(mla_q(p_ctx, q_norm_g, w_uq, row, col, False), k_c, v_c, 1, scale) if need_ctx else None
    return o_l, o_c


def hier_moe(h, wg, bg, we, be, w1, w3, w2):
    N = h.shape[0]
    g_logits = (h @ wg).astype(jnp.float32) + bg
    pg = jax.nn.softmax(g_logits, axis=-1)
    oh_g = jax.nn.one_hot(jnp.argmax(g_logits, axis=-1), N_GROUPS, dtype=jnp.float32)
    p_sel = jnp.sum(pg * oh_g, axis=-1)
    e_logits = ((h @ we).astype(jnp.float32) + be).reshape(N, N_GROUPS, EXPERTS_PER_GROUP)
    pe = jax.nn.softmax(jnp.einsum('nge,ng->ne', e_logits, oh_g), axis=-1)
    top_v, top_i = lax.top_k(pe, TOP_K)
    top_v = top_v / jnp.sum(top_v, axis=-1, keepdims=True)
    w_grp = jnp.sum(jax.nn.one_hot(top_i, EXPERTS_PER_GROUP, dtype=jnp.float32) * top_v[..., None], axis=1)
    combine = (p_sel[:, None, None] * oh_g[:, :, None] * w_grp[:, None, :]).reshape(N, N_EXPERTS).astype(h.dtype)
    a = jnp.einsum('nd,edh->neh', h, w1)
    b = jnp.einsum('nd,edh->neh', h, w3)
    act = jax.nn.silu(a) * b * combine[..., None]
    return jnp.einsum('neh,ehd->nd', act, w2)


def setup_inputs(seed: int = 0) -> dict:
    key = jax.random.key(seed)
    ks = iter(jax.random.split(key, 48))
    L, D = DEPTH, D_MODEL
    nrm = lambda shape, s: jax.random.normal(next(ks), shape, jnp.float32) * s
    uni = lambda shape, lo, hi: jax.random.uniform(next(ks), shape, jnp.float32, lo, hi)
    return {
        "x": nrm((BATCH, SEQ, D), 1.0),
        "c": nrm((BATCH, D), 1.0),
        "ctx": nrm((BATCH, CTX_LEN, D), 1.0),
        "c_ctx": nrm((D,), 1.0),
        "mod_w": nrm((L, D, 6 * D), 0.5 * D ** -0.5),
        "mod_b": nrm((L, 6 * D), 0.02),
        "norm1_g": 1.0 + nrm((L, D), 0.02),
        "norm2_g": 1.0 + nrm((L, D), 0.02),
        "w_in": nrm((L, D, D_IN), D ** -0.5),
        "w_out": nrm((L, D_MIX, D), D_MIX ** -0.5),
        "shift_prev": uni((L, RWKV_COLS), 0.0, 0.5),
        "shift_next": uni((L, RWKV_COLS), 0.0, 0.5),
        "decay_w0": nrm((L, 2, RWKV_W), 0.5),
        "decay_up": nrm((L, 2, DECAY_RANK, RWKV_W), DECAY_RANK ** -0.5),
        "iclr_a0": nrm((L, 2, RWKV_W), 0.5),
        "iclr_up": nrm((L, 2, ICLR_RANK, RWKV_W), 0.5 * ICLR_RANK ** -0.5),
        "gate_up": nrm((L, GATE_RANK, RWKV_W), GATE_RANK ** -0.5),
        "k_k": 0.85 + nrm((L, RWKV_W), 0.02),
        "k_a": 1.0 + nrm((L, RWKV_W), 0.02),
        "r_k": nrm((L, RWKV_W), 0.1),
        "gn_g": 1.0 + nrm((L, RWKV_W), 0.02),
        "gn_b": nrm((L, RWKV_W), 0.02),
        "q_norm_g": 1.0 + nrm((L, GQA_HEAD), 0.02),
        "k_norm_g": 1.0 + nrm((L, GQA_HEAD), 0.02),
        "mla_q_norm_g": 1.0 + nrm((L, MLA_Q_RANK), 0.02),
        "mla_w_uq": nrm((L, MLA_Q_RANK, MLA_HEADS * (MLA_NOPE + MLA_ROPE)), MLA_Q_RANK ** -0.5),
        "mla_kv_norm_g": 1.0 + nrm((L, MLA_KV_RANK), 0.02),
        "mla_w_ukv": nrm((L, MLA_KV_RANK, MLA_HEADS * (MLA_NOPE + MLA_V)), MLA_KV_RANK ** -0.5),
        "router_gw": nrm((L, D, N_GROUPS), D ** -0.5),
        "router_gb": nrm((L, N_GROUPS), 0.01),
        "router_ew": nrm((L, D, N_EXPERTS), D ** -0.5),
        "router_eb": nrm((L, N_EXPERTS), 0.01),
        "exp_w1": nrm((L, N_EXPERTS, D, D_EXPERT), D ** -0.5),
        "exp_w3": nrm((L, N_EXPERTS, D, D_EXPERT), D ** -0.5),
        "exp_w2": nrm((L, N_EXPERTS, D_EXPERT, D), D_EXPERT ** -0.5),
        "final_norm_g": 1.0 + nrm((D,), 0.02),
    }


def reference(x, c, ctx, c_ctx, mod_w, mod_b, norm1_g, norm2_g, w_in, w_out, shift_prev, shift_next,
              decay_w0, decay_up, iclr_a0, iclr_up, gate_up, k_k, k_a, r_k, gn_g, gn_b, q_norm_g, k_norm_g,
              mla_q_norm_g, mla_w_uq, mla_kv_norm_g, mla_w_ukv, router_gw, router_gb, router_ew, router_eb,
              exp_w1, exp_w3, exp_w2, final_norm_g):
    B, S, D = x.shape
    C = ctx.shape[1]
    rows = S // GRID_W
    row = jnp.repeat(jnp.arange(rows, dtype=jnp.float32), GRID_W)
    col = jnp.tile(jnp.arange(GRID_W, dtype=jnp.float32), rows)
    sc = jax.nn.silu(c)
    scc = jax.nn.silu(c_ctx)
    for l in range(DEPTH):
        need_ctx = l < DEPTH - 1
        m_lat = jnp.split((sc @ mod_w[l] + mod_b[l])[:, None, :], 6, axis=-1)
        m_ctx = jnp.split(scc @ mod_w[l] + mod_b[l], 6, axis=-1)
        h_lat = modulate(rmsnorm(x, norm1_g[l]), m_lat[0], m_lat[1])
        h_ctx = modulate(rmsnorm(ctx, norm1_g[l]), m_ctx[0], m_ctx[1])
        pr_l, pg_l, pm_l = split_cols(h_lat @ w_in[l], (RWKV_COLS, GQA_COLS, MLA_COLS))
        pr_c, pg_c, pm_c = split_cols(h_ctx @ w_in[l], (RWKV_COLS, GQA_COLS, MLA_COLS))
        o_r_l, o_r_c = rwkv_mixer(pr_l, pr_c, shift_prev[l], shift_next[l], decay_w0[l], decay_up[l],
                                  iclr_a0[l], iclr_up[l], gate_up[l], k_k[l], k_a[l], r_k[l], gn_g[l], gn_b[l], need_ctx)
        o_g_l, o_g_c = gqa_mixer(pg_l, pg_c, row, col, q_norm_g[l], k_norm_g[l], need_ctx)
        o_m_l, o_m_c = mla_mixer(pm_l, pm_c, row, col, mla_q_norm_g[l], mla_w_uq[l], mla_kv_norm_g[l], mla_w_ukv[l], need_ctx)
        x = x + m_lat[2] * (jnp.concatenate([o_r_l, o_g_l, o_m_l], axis=-1) @ w_out[l])
        h2 = modulate(rmsnorm(x, norm2_g[l]), m_lat[3], m_lat[4]).reshape(B * S, D)
        x = x + m_lat[5] * hier_moe(h2, router_gw[l], router_gb[l], router_ew[l], router_eb[l],
                                    exp_w1[l], exp_w3[l], exp_w2[l]).reshape(B, S, D)
        if need_ctx:
            ctx = ctx + m_ctx[2] * (jnp.concatenate([o_r_c, o_g_c, o_m_c], axis=-1) @ w_out[l])
            h2c = modulate(rmsnorm(ctx, norm2_g[l]), m_ctx[3], m_ctx[4]).reshape(B * C, D)
            ctx = ctx + m_ctx[5] * hier_moe(h2c, router_gw[l], router_gb[l], router_ew[l], router_eb[l],
                                            exp_w1[l], exp_w3[l], exp_w2[l]).reshape(B, C, D)
    return rmsnorm(x, final_norm_g)
```

```python
import functools
import math

import jax
import jax.numpy as jnp
import numpy as np
from jax import lax
from jax.experimental import pallas as pl
from jax.experimental.pallas import tpu as pltpu

F32 = jnp.float32
BF16 = jnp.bfloat16
HI = lax.Precision.HIGHEST

V7X_VMEM_BYTES = 64 * 1024 * 1024
VMEM_LIMIT = V7X_VMEM_BYTES - 8 * 1024 * 1024
LANES = 128
SUBLANES = 8

GRID_W = 64
ROPE_THETA = 10000.0
NORM_EPS = 1e-6
GN_EPS = 64e-5
DECAY_SCALE = math.exp(-0.5)

RWKV_HEAD = 64
RWKV_W = 512
DECAY_RANK = 64
ICLR_RANK = 64
GATE_RANK = 128
RWKV_COLS = 3 * RWKV_W + DECAY_RANK + ICLR_RANK + GATE_RANK
LOWRANK_OFF = 3 * RWKV_W
GATE_OFF = LOWRANK_OFF + DECAY_RANK + ICLR_RANK
CHUNK = 64

GQA_HEAD = 128
GQA_Q_HEADS = 8
GQA_KV_HEADS = 2
GQA_Q_COLS = GQA_Q_HEADS * GQA_HEAD
GQA_KV_COLS = GQA_KV_HEADS * GQA_HEAD
GQA_COLS = GQA_Q_COLS + 2 * GQA_KV_COLS

MLA_HEADS = 4
MLA_NOPE = 128
MLA_ROPE = 64
MLA_V = 128
MLA_Q_RANK = 384
MLA_KV_RANK = 256
MLA_COLS = MLA_Q_RANK + MLA_KV_RANK + MLA_ROPE
MLA_COLS_PAD = 768
MLA_DK = 2 * LANES

N_GROUPS = 4
EXPERTS_PER_GROUP = 8
N_EXPERTS = 32
D_EXPERT = 256
MOE_TM = 256

TOK_TILE = 256
N_MOD = 6
CTX_ROW = 4


def _cp(*sem):
    return pltpu.CompilerParams(dimension_semantics=sem, vmem_limit_bytes=VMEM_LIMIT)


def _sigmoid(x):
    return 1.0 / (1.0 + jnp.exp(-x))


def _mod_body(c_ref, w_ref, b_ref, o_ref):
    c = c_ref[...]
    s = c * _sigmoid(c)
    o_ref[...] = jnp.dot(s, w_ref[...], preferred_element_type=F32, precision=HI) + b_ref[...]


def _modulation(cc, mod_w, mod_b):
    L, D, N = mod_w.shape
    tn = 1024
    return pl.pallas_call(
        _mod_body,
        grid=(L, N // tn),
        in_specs=[pl.BlockSpec((SUBLANES, D), lambda l, j: (0, 0)),
                  pl.BlockSpec((None, D, tn), lambda l, j: (l, 0, j)),
                  pl.BlockSpec((None, 1, tn), lambda l, j: (l, 0, j))],
        out_specs=pl.BlockSpec((None, SUBLANES, tn), lambda l, j: (l, 0, j)),
        out_shape=jax.ShapeDtypeStruct((L, SUBLANES, N), F32),
        compiler_params=_cp("parallel", "parallel"),
        name="modulation",
    )(cc, mod_w, mod_b.reshape(L, 1, N))


def _mod_spec(which, n_ctx_tiles, D):
    return pl.BlockSpec((None, 1, D), lambda b, i: (jnp.where(i < n_ctx_tiles, CTX_ROW, b) * N_MOD + which, 0, 0))


def _rms(x, g):
    return x * lax.rsqrt(jnp.mean(x * x, axis=-1, keepdims=True) + NORM_EPS) * g


def _norm_mod_body(x_ref, g_ref, sh_ref, sc_ref, o_ref):
    y = _rms(x_ref[...], g_ref[...])
    o_ref[...] = (y * (1.0 + sc_ref[...]) + sh_ref[...]).astype(o_ref.dtype)


def _norm_body(x_ref, g_ref, o_ref):
    o_ref[...] = _rms(x_ref[...], g_ref[...]).astype(o_ref.dtype)


def _route(logits):
    lane = lax.broadcasted_iota(jnp.int32, logits.shape, 1)
    lane_f = lane.astype(F32)
    neg = jnp.float32(-1e30)
    far = jnp.float32(1e9)
    first_at = lambda hit: jnp.min(jnp.where(hit, lane_f, far), axis=-1, keepdims=True).astype(jnp.int32)
    gl = jnp.where(lane < N_GROUPS, logits, neg)
    gmax = jnp.max(gl, axis=-1, keepdims=True)
    gidx = first_at(gl == gmax)
    p_sel = 1.0 / jnp.sum(jnp.exp(gl - gmax), axis=-1, keepdims=True)
    lo = N_GROUPS + gidx * EXPERTS_PER_GROUP
    el = jnp.where((lane >= lo) & (lane < lo + EXPERTS_PER_GROUP), logits, neg)
    m1 = jnp.max(el, axis=-1, keepdims=True)
    i1 = first_at(el == m1)
    el2 = jnp.where(lane == i1, neg, el)
    m2 = jnp.max(el2, axis=-1, keepdims=True)
    i2 = first_at(el2 == m2)
    t = jnp.exp(m2 - m1)
    w1 = p_sel / (1.0 + t)
    w2 = p_sel * t / (1.0 + t)
    rw = jnp.where(lane == 0, w1, jnp.where(lane == 1, w2, 0.0))
    re = jnp.where(lane == 0, i1 - N_GROUPS, jnp.where(lane == 1, i2 - N_GROUPS, 0))
    return rw, re


def _norm_router_body(x_ref, g_ref, sh_ref, sc_ref, wr_ref, br_ref, h_ref, rw_ref, re_ref):
    y = _rms(x_ref[...], g_ref[...])
    h = y * (1.0 + sc_ref[...]) + sh_ref[...]
    h_ref[...] = h.astype(h_ref.dtype)
    logits = jnp.dot(h, wr_ref[...], preferred_element_type=F32, precision=HI) + br_ref[...]
    rw, re = _route(logits)
    rw_ref[...] = rw
    re_ref[...] = re


def _norm_mod(X, g, mods, shift_i, scale_i, n_ctx_tiles):
    B, T, D = X.shape
    tt = TOK_TILE
    return pl.pallas_call(
        _norm_mod_body,
        grid=(B, T // tt),
        in_specs=[pl.BlockSpec((None, tt, D), lambda b, i: (b, i, 0)),
                  pl.BlockSpec((1, D), lambda b, i: (0, 0)),
                  _mod_spec(shift_i, n_ctx_tiles, D),
                  _mod_spec(scale_i, n_ctx_tiles, D)],
        out_specs=pl.BlockSpec((None, tt, D), lambda b, i: (b, i, 0)),
        out_shape=jax.ShapeDtypeStruct((B, T, D), BF16),
        compiler_params=_cp("parallel", "parallel"),
        name="norm_mod",
    )(X, g.reshape(1, D), mods, mods)


def _norm_router(X, g, mods, shift_i, scale_i, n_ctx_tiles, wr, br):
    B, T, D = X.shape
    tt = TOK_TILE
    tok = lambda w, dt: (pl.BlockSpec((None, tt, w), lambda b, i: (b, i, 0)), jax.ShapeDtypeStruct((B, T, w), dt))
    outs = [tok(D, BF16), tok(LANES, F32), tok(LANES, jnp.int32)]
    return pl.pallas_call(
        _norm_router_body,
        grid=(B, T // tt),
        in_specs=[pl.BlockSpec((None, tt, D), lambda b, i: (b, i, 0)),
                  pl.BlockSpec((1, D), lambda b, i: (0, 0)),
                  _mod_spec(shift_i, n_ctx_tiles, D),
                  _mod_spec(scale_i, n_ctx_tiles, D),
                  pl.BlockSpec((D, LANES), lambda b, i: (0, 0)),
                  pl.BlockSpec((1, LANES), lambda b, i: (0, 0))],
        out_specs=[o[0] for o in outs],
        out_shape=[o[1] for o in outs],
        compiler_params=_cp("parallel", "parallel"),
        name="norm_router",
    )(X, g.reshape(1, D), mods, mods, wr, br)


def _final_norm(X, g, n_ctx_tiles):
    B, T, D = X.shape
    tt = TOK_TILE
    S = T - n_ctx_tiles * tt
    return pl.pallas_call(
        _norm_body,
        grid=(B, S // tt),
        in_specs=[pl.BlockSpec((None, tt, D), lambda b, i: (b, i + n_ctx_tiles, 0)),
                  pl.BlockSpec((1, D), lambda b, i: (0, 0))],
        out_specs=pl.BlockSpec((None, tt, D), lambda b, i: (b, i, 0)),
        out_shape=jax.ShapeDtypeStruct((B, S, D), F32),
        compiler_params=_cp("parallel", "parallel"),
        name="final_norm",
    )(X, g.reshape(1, D))


def _mm_body(a_ref, w_ref, o_ref):
    o_ref[...] = jnp.dot(a_ref[...], w_ref[...], preferred_element_type=F32).astype(o_ref.dtype)


def _matmul(a, w, tm=512):
    M, K = a.shape
    N = w.shape[1]
    return pl.pallas_call(
        _mm_body,
        grid=(M // tm,),
        in_specs=[pl.BlockSpec((tm, K), lambda i: (i, 0)),
                  pl.BlockSpec((K, N), lambda i: (0, 0))],
        out_specs=pl.BlockSpec((tm, N), lambda i: (i, 0)),
        out_shape=jax.ShapeDtypeStruct((M, N), F32),
        compiler_params=_cp("parallel"),
        name="token_matmul",
    )(a, w)


def _mm_res_body(a_ref, w_ref, x_ref, gl_ref, gc_ref, o_ref, *, n_ctx, tm):
    acc = jnp.dot(a_ref[...], w_ref[...], preferred_element_type=F32)
    row = pl.program_id(1) * tm + lax.broadcasted_iota(jnp.int32, (tm, 1), 0)
    gate = jnp.where(row < n_ctx, gc_ref[...], gl_ref[...])
    o_ref[...] = x_ref[...] + gate * acc


def _matmul_gated_residual(a, w, X, mods, gate_i, n_ctx, tm=384):
    B, T, K = a.shape
    D = w.shape[1]
    assert T % tm == 0
    return pl.pallas_call(
        functools.partial(_mm_res_body, n_ctx=n_ctx, tm=tm),
        grid=(B, T // tm),
        in_specs=[pl.BlockSpec((None, tm, K), lambda b, i: (b, i, 0)),
                  pl.BlockSpec((K, D), lambda b, i: (0, 0)),
                  pl.BlockSpec((None, tm, D), lambda b, i: (b, i, 0)),
                  pl.BlockSpec((None, 1, D), lambda b, i: (b * N_MOD + gate_i, 0, 0)),
                  pl.BlockSpec((None, 1, D), lambda b, i: (CTX_ROW * N_MOD + gate_i, 0, 0))],
        out_specs=pl.BlockSpec((None, tm, D), lambda b, i: (b, i, 0)),
        out_shape=jax.ShapeDtypeStruct((B, T, D), F32),
        compiler_params=_cp("parallel", "parallel"),
        name="out_proj_residual",
    )(a, w, X, mods, mods)


def _gated_add_body(x_ref, y_ref, g_ref, o_ref):
    o_ref[...] = x_ref[...] + g_ref[...] * y_ref[...]


def _gated_add(X, Y, mods, gate_i, n_ctx_tiles):
    B, T, D = X.shape
    tt = TOK_TILE
    blk = pl.BlockSpec((None, tt, D), lambda b, i: (b, i, 0))
    return pl.pallas_call(
        _gated_add_body,
        grid=(B, T // tt),
        in_specs=[blk, blk, _mod_spec(gate_i, n_ctx_tiles, D)],
        out_specs=blk,
        out_shape=jax.ShapeDtypeStruct((B, T, D), F32),
        compiler_params=_cp("parallel", "parallel"),
        name="gated_add",
    )(X, Y, mods)


def _rwkv_prep_body(p_ref, pv_ref, nx_ref, mup_ref, mun_ref, w0_ref, wup_ref, a0_ref, aup_ref, gup_ref,
                    kk_ref, ka_ref, rk_ref, e_ref,
                    r_o, v_o, kh_o, lw_o, b_o, kt_o, g_o, bon_o, *, n_tiles, tt):
    i = pl.program_id(1)
    p = p_ref[...]
    seq_first = i <= 1
    seq_last = (i == 0) | (i == n_tiles - 1)
    prow = jnp.where(seq_first, 0.0, pv_ref[SUBLANES - 1:SUBLANES, :])
    nrow = jnp.where(seq_last, 0.0, nx_ref[0:1, :])
    rid = lax.broadcasted_iota(jnp.int32, (tt, 1), 0)
    prev = jnp.where(rid == 0, prow, pltpu.roll(p, 1, 0))
    nxt = jnp.where(rid == tt - 1, nrow, pltpu.roll(p, tt - 1, 0))
    z = p + mup_ref[...] * (prev - p) + mun_ref[...] * (nxt - p)

    W = RWKV_W
    r = z[:, 0:W]
    k = z[:, W:2 * W]
    v = z[:, 2 * W:3 * W]
    lowrank = z[:, LOWRANK_OFF:LOWRANK_OFF + LANES]
    gd = z[:, GATE_OFF:GATE_OFF + GATE_RANK]
    head_sum = e_ref[...]

    kap = k * kk_ref[...]
    ss = jnp.dot(kap * kap, head_sum, preferred_element_type=F32, precision=HI)
    khat = kap * lax.rsqrt(ss + 1e-12)
    wd_t = jnp.tanh(lowrank)
    g_o[...] = jnp.dot(_sigmoid(gd), gup_ref[...], preferred_element_type=F32, precision=HI)
    r_o[...] = r
    v_o[...] = v
    kh_o[...] = khat
    kt_sum = None
    for d in range(2):
        dec = jnp.dot(wd_t, wup_ref[d], preferred_element_type=F32, precision=HI)
        lw_o[d] = -DECAY_SCALE * _sigmoid(w0_ref[d:d + 1, :] + dec)
        a = _sigmoid(a0_ref[d:d + 1, :] + jnp.dot(lowrank, aup_ref[d], preferred_element_type=F32, precision=HI))
        kt = k * (1.0 + (a - 1.0) * ka_ref[...])
        kt_o[d] = kt
        b_o[d] = a * khat
        kt_sum = kt if kt_sum is None else kt_sum + kt
    bsum = jnp.dot(r * kt_sum * rk_ref[...], head_sum, preferred_element_type=F32, precision=HI)
    bon_o[...] = bsum * v


def _rwkv_prep(pr, mu_prev, mu_next, w0, wup_pad, a0, aup_pad, g_up, k_k, k_a, r_k, head_sum):
    B, T, _ = pr.shape
    tt = TOK_TILE
    W = RWKV_W
    n_tiles = T // tt
    n8 = tt // SUBLANES
    row = lambda v: v.reshape(1, -1)
    full = lambda a: pl.BlockSpec(a.shape, lambda b, i: (0,) * a.ndim)
    tok = pl.BlockSpec((None, tt, W), lambda b, i: (b, i, 0))
    tok2 = pl.BlockSpec((None, 2, tt, W), lambda b, i: (b, 0, i, 0))
    s1 = jax.ShapeDtypeStruct((B, T, W), F32)
    s2 = jax.ShapeDtypeStruct((B, 2, T, W), F32)
    consts = [row(mu_prev), row(mu_next), w0, wup_pad, a0, aup_pad, g_up, row(k_k), row(k_a), row(r_k), head_sum]
    return pl.pallas_call(
        functools.partial(_rwkv_prep_body, n_tiles=n_tiles, tt=tt),
        grid=(B, n_tiles),
        in_specs=[pl.BlockSpec((None, tt, RWKV_COLS), lambda b, i: (b, i, 0)),
                  pl.BlockSpec((None, SUBLANES, RWKV_COLS), lambda b, i: (b, jnp.maximum(i * n8 - 1, 0), 0)),
                  pl.BlockSpec((None, SUBLANES, RWKV_COLS), lambda b, i: (b, jnp.minimum((i + 1) * n8, T // SUBLANES - 1), 0)),
                  ] + [full(a) for a in consts],
        out_specs=[tok, tok, tok, tok2, tok2, tok2, tok, tok],
        out_shape=[s1, s1, s1, s2, s2, s2, s1, s1],
        compiler_params=_cp("parallel", "parallel"),
        name="rwkv_prep",
    )(pr, pr, pr, *consts)


def _stack_heads(x):
    lane = lax.broadcasted_iota(jnp.int32, x.shape, 1)
    first = lane < RWKV_HEAD
    return jnp.concatenate([jnp.where(first, x, 0.0), jnp.where(first, 0.0, x)], axis=0)


def _dot(a, b):
    return jnp.dot(a.astype(BF16), b.astype(BF16), preferred_element_type=F32)


def _dot_nt(a, b):
    return lax.dot_general(a.astype(BF16), b.astype(BF16), (((1,), (1,)), ((), ())), preferred_element_type=F32)


def _dot_tn(a, b):
    return jnp.dot(a.T.astype(BF16), b.astype(BF16), preferred_element_type=F32)


def _pair_chunk(r, v, kh, lw, b, kt, S2, reverse):
    L = CHUNK
    P2 = 2 * L
    ti = lax.broadcasted_iota(jnp.int32, (L, L), 0)
    tj = lax.broadcasted_iota(jnp.int32, (L, L), 1)
    tri = jnp.where((ti <= tj) if reverse else (ti >= tj), 1.0, 0.0)
    lam = jnp.dot(tri, lw, preferred_element_type=F32, precision=HI)
    tot = lam[0:1, :] if reverse else lam[L - 1:L, :]
    e_r = jnp.exp(lam)
    e_n = jnp.exp(-lam)
    e_g = jnp.exp(tot - lam)
    A = _stack_heads(kh * jnp.exp(lam - lw))
    R = _stack_heads(r * e_r)
    Kn = _stack_heads(kt * e_n)
    Bn = _stack_heads(b * e_n)
    Kg = _stack_heads(kt * e_g)
    Bg = _stack_heads(b * e_g)
    Vs = _stack_heads(v)

    si = lax.broadcasted_iota(jnp.int32, (P2, P2), 0)
    sj = lax.broadcasted_iota(jnp.int32, (P2, P2), 1)
    same = (si >= L) == (sj >= L)
    before = (si < sj) if reverse else (si > sj)
    strict = same & before
    incl = same & (before | (si == sj))
    eye = si == sj

    big = _dot_nt(jnp.concatenate([A, R], axis=0), jnp.concatenate([Bn, Kn], axis=0))
    Mb = jnp.where(strict, big[0:P2, 0:P2], 0.0)
    Mkv = jnp.where(strict, big[0:P2, P2:2 * P2], 0.0)
    Pb = jnp.where(incl, big[P2:2 * P2, 0:P2], 0.0)
    Pkv = jnp.where(incl, big[P2:2 * P2, P2:2 * P2], 0.0)

    Pw = -Mb
    Tm = jnp.where(eye, 1.0, 0.0) + Pw
    for _ in range(int(math.log2(L)) - 1):
        Pw = _dot(Pw, Pw)
        Tm = Tm + _dot(Tm, Pw)

    MV = _dot(Mkv, Vs)
    TAM = _dot(Tm, jnp.concatenate([A, MV], axis=1))
    PB = _dot(Pb, TAM)
    RA = R - PB[:, 0:P2]
    Y0 = _dot(Pkv, Vs) - PB[:, P2:2 * P2]
    BG = _dot_tn(Bg, TAM)
    G2 = jnp.where(eye, jnp.exp(tot), 0.0) - BG[:, 0:P2]
    H2 = _dot_tn(Kg, Vs) - BG[:, P2:2 * P2]

    out = _dot(jnp.concatenate([RA, G2], axis=0), S2)
    Ys = out[0:P2] + Y0
    return Ys[0:L] + Ys[L:P2], out[P2:2 * P2] + H2


def _rwkv_scan_body(r_ref, v_ref, kh_ref, lw_ref, b_ref, kt_ref, y_ref, s_ref):
    d = pl.program_id(1)

    @pl.when(pl.program_id(2) == 0)
    def _():
        s_ref[...] = jnp.zeros_like(s_ref)

    def run(reverse):
        for p in range(RWKV_W // LANES):
            sl = slice(p * LANES, (p + 1) * LANES)
            y, s_new = _pair_chunk(r_ref[:, sl], v_ref[:, sl], kh_ref[:, sl], lw_ref[:, sl], b_ref[:, sl],
                                   kt_ref[:, sl], s_ref[p], reverse)
            y_ref[:, sl] = y
            s_ref[p] = s_new

    @pl.when(d == 0)
    def _():
        run(False)

    @pl.when(d == 1)
    def _():
        run(True)


def _rwkv_scan(r, v, kh, lw, b, kt, n_ctx):
    B, T, W = r.shape
    nc = T // CHUNK
    ncc = n_ctx // CHUNK

    def chunk_of(d, j):
        rev = jnp.where(j < ncc, ncc - 1 - j, nc + ncc - 1 - j)
        return jnp.where(d == 0, j, rev)

    one = pl.BlockSpec((None, CHUNK, W), lambda bb, d, j: (bb, chunk_of(d, j), 0))
    two = pl.BlockSpec((None, None, CHUNK, W), lambda bb, d, j: (bb, d, chunk_of(d, j), 0))
    return pl.pallas_call(
        _rwkv_scan_body,
        grid=(B, 2, nc),
        in_specs=[one, one, one, two, two, two],
        out_specs=two,
        out_shape=jax.ShapeDtypeStruct((B, 2, T, W), F32),
        scratch_shapes=[pltpu.VMEM((W // LANES, LANES, LANES), F32)],
        compiler_params=_cp("parallel", "parallel", "arbitrary"),
        name="rwkv_scan",
    )(r, v, kh, lw, b, kt)


def _rwkv_readout_body(y_ref, bon_ref, g_ref, gng_ref, gnb_ref, e_ref, o_ref):
    y = y_ref[0] + y_ref[1]
    head_mean = e_ref[...] * (1.0 / RWKV_HEAD)
    mu = jnp.dot(y, head_mean, preferred_element_type=F32, precision=HI)
    yc = y - mu
    var = jnp.dot(yc * yc, head_mean, preferred_element_type=F32, precision=HI)
    yn = yc * lax.rsqrt(var + GN_EPS) * gng_ref[...] + gnb_ref[...]
    o_ref[...] = ((yn + bon_ref[...]) * g_ref[...]).astype(o_ref.dtype)


def _rwkv_readout(y, bonus, g, gn_g, gn_b, head_sum):
    B, _, T, W = y.shape
    tt = TOK_TILE
    tok = pl.BlockSpec((None, tt, W), lambda b, i: (b, i, 0))
    row = pl.BlockSpec((1, W), lambda b, i: (0, 0))
    return pl.pallas_call(
        _rwkv_readout_body,
        grid=(B, T // tt),
        in_specs=[pl.BlockSpec((None, 2, tt, W), lambda b, i: (b, 0, i, 0)), tok, tok, row, row,
                  pl.BlockSpec((W, W), lambda b, i: (0, 0))],
        out_specs=tok,
        out_shape=jax.ShapeDtypeStruct((B, T, W), BF16),
        compiler_params=_cp("parallel", "parallel"),
        name="rwkv_readout",
    )(y, bonus, g, gn_g.reshape(1, W), gn_b.reshape(1, W), head_sum)


def _rope(y, cos, sin_signed, quarter):
    lane = lax.broadcasted_iota(jnp.int32, y.shape, 1)
    first = (lane & (2 * quarter - 1)) < quarter
    partner = jnp.where(first, pltpu.roll(y, LANES - quarter, 1), pltpu.roll(y, quarter, 1))
    return y * cos + partner * sin_signed


def _gqa_prep_body(p_ref, qg_ref, kg_ref, cos_ref, sin_ref, q_o, k_o, v_o):
    cos = cos_ref[...]
    sin = sin_ref[...]
    scale = GQA_HEAD ** -0.5
    for h in range(GQA_Q_HEADS):
        sl = slice(h * GQA_HEAD, (h + 1) * GQA_HEAD)
        q = _rms(p_ref[:, sl], qg_ref[...])
        q_o[:, sl] = (_rope(q, cos, sin, GQA_HEAD // 4) * scale).astype(q_o.dtype)
    for h in range(GQA_KV_HEADS):
        sl = slice(h * GQA_HEAD, (h + 1) * GQA_HEAD)
        k = _rms(p_ref[:, GQA_Q_COLS + h * GQA_HEAD:GQA_Q_COLS + (h + 1) * GQA_HEAD], kg_ref[...])
        k_o[:, sl] = _rope(k, cos, sin, GQA_HEAD // 4).astype(k_o.dtype)
    v_o[...] = p_ref[:, GQA_Q_COLS + GQA_KV_COLS:GQA_COLS].astype(v_o.dtype)


def _gqa_prep(pg, q_norm_g, k_norm_g, cos, sin):
    B, T, _ = pg.shape
    tt = TOK_TILE
    tok = lambda w: pl.BlockSpec((None, tt, w), lambda b, i: (b, i, 0))
    row = pl.BlockSpec((1, GQA_HEAD), lambda b, i: (0, 0))
    tab = pl.BlockSpec((tt, LANES), lambda b, i: (i, 0))
    return pl.pallas_call(
        _gqa_prep_body,
        grid=(B, T // tt),
        in_specs=[tok(GQA_COLS), row, row, tab, tab],
        out_specs=[tok(GQA_Q_COLS), tok(GQA_KV_COLS), tok(GQA_KV_COLS)],
        out_shape=[jax.ShapeDtypeStruct((B, T, GQA_Q_COLS), BF16),
                   jax.ShapeDtypeStruct((B, T, GQA_KV_COLS), BF16),
                   jax.ShapeDtypeStruct((B, T, GQA_KV_COLS), BF16)],
        compiler_params=_cp("parallel", "parallel"),
        name="gqa_prep",
    )(pg, q_norm_g.reshape(1, -1), k_norm_g.reshape(1, -1), cos, sin)


def _mla_prep_body(p_ref, qg_ref, wq_ref, kvg_ref, wkv_ref, cos_ref, sin_ref, q_o, k_o, v_o):
    cos = cos_ref[...]
    sin = sin_ref[...]
    scale = (MLA_NOPE + MLA_ROPE) ** -0.5
    cq = _rms(p_ref[:, 0:MLA_Q_RANK], qg_ref[...])
    q = jnp.dot(cq.astype(BF16), wq_ref[...], preferred_element_type=F32) * scale
    ckv = _rms(p_ref[:, MLA_Q_RANK:MLA_Q_RANK + MLA_KV_RANK], kvg_ref[...])
    kv = jnp.dot(ckv.astype(BF16), wkv_ref[...], preferred_element_type=F32)
    kr = _rope(p_ref[:, MLA_Q_RANK + MLA_KV_RANK:MLA_COLS_PAD], cos, sin, MLA_ROPE // 4).astype(k_o.dtype)
    for h in range(MLA_HEADS):
        lo = h * MLA_DK
        q_o[:, lo:lo + LANES] = q[:, lo:lo + LANES].astype(q_o.dtype)
        q_o[:, lo + LANES:lo + MLA_DK] = _rope(q[:, lo + LANES:lo + MLA_DK], cos, sin, MLA_ROPE // 4).astype(q_o.dtype)
        k_o[:, lo:lo + LANES] = kv[:, h * MLA_NOPE:(h + 1) * MLA_NOPE].astype(k_o.dtype)
        k_o[:, lo + LANES:lo + MLA_DK] = kr
    v_o[...] = kv[:, MLA_HEADS * MLA_NOPE:].astype(v_o.dtype)


def _mla_prep(pm, q_norm_g, wq, kv_norm_g, wkv, cos, sin):
    B, T, _ = pm.shape
    tt = TOK_TILE
    tok = lambda w: pl.BlockSpec((None, tt, w), lambda b, i: (b, i, 0))
    full = lambda a: pl.BlockSpec(a.shape, lambda b, i: (0,) * a.ndim)
    tab = pl.BlockSpec((tt, LANES), lambda b, i: (i, 0))
    qg = q_norm_g.reshape(1, -1)
    kvg = kv_norm_g.reshape(1, -1)
    return pl.pallas_call(
        _mla_prep_body,
        grid=(B, T // tt),
        in_specs=[tok(MLA_COLS_PAD), full(qg), full(wq), full(kvg), full(wkv), tab, tab],
        out_specs=[tok(MLA_HEADS * MLA_DK), tok(MLA_HEADS * MLA_DK), tok(MLA_HEADS * MLA_V)],
        out_shape=[jax.ShapeDtypeStruct((B, T, MLA_HEADS * MLA_DK), BF16),
                   jax.ShapeDtypeStruct((B, T, MLA_HEADS * MLA_DK), BF16),
                   jax.ShapeDtypeStruct((B, T, MLA_HEADS * MLA_V), BF16)],
        compiler_params=_cp("parallel", "parallel"),
        name="mla_prep",
    )(pm, qg, wq, kvg, wkv, cos, sin)


def _attn_body(q_ref, k_ref, v_ref, o_ref, *, hq, hkv, dk, dv, n_ctx_tiles, n_ctx, n_all):
    rep = hq // hkv

    def run(nk):
        for g in range(hkv):
            kg = k_ref[0:nk, g * dk:(g + 1) * dk]
            vg = v_ref[0:nk, g * dv:(g + 1) * dv]
            for rr in range(rep):
                h = g * rep + rr
                s = _dot_nt(q_ref[:, h * dk:(h + 1) * dk], kg)
                p = jnp.exp(s - jnp.max(s, axis=-1, keepdims=True))
                den = jnp.sum(p, axis=-1, keepdims=True)
                o = jnp.dot(p.astype(BF16), vg, preferred_element_type=F32)
                o_ref[:, h * dv:(h + 1) * dv] = (o / den).astype(o_ref.dtype)

    @pl.when(pl.program_id(1) < n_ctx_tiles)
    def _():
        run(n_ctx)

    @pl.when(pl.program_id(1) >= n_ctx_tiles)
    def _():
        run(n_all)


def _attention(q, k, v, hq, hkv, dk, dv, n_ctx):
    B, T, _ = q.shape
    tq = TOK_TILE
    return pl.pallas_call(
        functools.partial(_attn_body, hq=hq, hkv=hkv, dk=dk, dv=dv, n_ctx_tiles=n_ctx // tq, n_ctx=n_ctx, n_all=T),
        grid=(B, T // tq),
        in_specs=[pl.BlockSpec((None, tq, hq * dk), lambda b, i: (b, i, 0)),
                  pl.BlockSpec((None, T, hkv * dk), lambda b, i: (b, 0, 0)),
                  pl.BlockSpec((None, T, hkv * dv), lambda b, i: (b, 0, 0))],
        out_specs=pl.BlockSpec((None, tq, hq * dv), lambda b, i: (b, i, 0)),
        out_shape=jax.ShapeDtypeStruct((B, T, hq * dv), BF16),
        compiler_params=_cp("parallel", "parallel"),
        name="attention",
    )(q, k, v)


def _rope_tables(T, n_ctx, n_rot):
    quarter = n_rot // 4
    t = jnp.arange(T - n_ctx)
    row = (t // GRID_W).astype(F32)
    col = (t % GRID_W).astype(F32)
    inv = ROPE_THETA ** (-jnp.arange(quarter, dtype=F32) / quarter)
    ar = row[:, None] * inv[None, :]
    ac = col[:, None] * inv[None, :]
    pad = LANES - n_rot
    cos = jnp.concatenate([jnp.cos(ar), jnp.cos(ar), jnp.cos(ac), jnp.cos(ac), jnp.ones((T - n_ctx, pad), F32)], axis=1)
    sin = jnp.concatenate([-jnp.sin(ar), jnp.sin(ar), -jnp.sin(ac), jnp.sin(ac), jnp.zeros((T - n_ctx, pad), F32)], axis=1)
    cos = jnp.concatenate([jnp.ones((n_ctx, LANES), F32), cos], axis=0)
    sin = jnp.concatenate([jnp.zeros((n_ctx, LANES), F32), sin], axis=0)
    return cos, sin


def _moe_body(te_ref, nv_ref, hs_ref, w1_ref, w3_ref, w2_ref, rw_ref, y_ref):
    i = pl.program_id(0)

    @pl.when(i < nv_ref[0])
    def _():
        hs = hs_ref[...]
        a = jnp.dot(hs, w1_ref[...].astype(BF16), preferred_element_type=F32)
        b = jnp.dot(hs, w3_ref[...].astype(BF16), preferred_element_type=F32)
        act = a * _sigmoid(a) * b * rw_ref[...]
        y_ref[...] = jnp.dot(act.astype(BF16), w2_ref[...].astype(BF16), preferred_element_type=F32)

    @pl.when(i >= nv_ref[0])
    def _():
        y_ref[...] = jnp.zeros_like(y_ref)


def _moe_experts(tile_expert, n_valid, hs, w1, w3, w2, rws):
    NP, D = hs.shape
    tm = MOE_TM
    DE = w1.shape[-1]
    grid_spec = pltpu.PrefetchScalarGridSpec(
        num_scalar_prefetch=2,
        grid=(NP // tm,),
        in_specs=[pl.BlockSpec((tm, D), lambda i, te, nv: (i, 0)),
                  pl.BlockSpec((None, D, DE), lambda i, te, nv: (te[i], 0, 0)),
                  pl.BlockSpec((None, D, DE), lambda i, te, nv: (te[i], 0, 0)),
                  pl.BlockSpec((None, DE, D), lambda i, te, nv: (te[i], 0, 0)),
                  pl.BlockSpec((tm, 1), lambda i, te, nv: (i, 0))],
        out_specs=pl.BlockSpec((tm, D), lambda i, te, nv: (i, 0)),
    )
    return pl.pallas_call(
        _moe_body,
        grid_spec=grid_spec,
        out_shape=jax.ShapeDtypeStruct((NP, D), F32),
        compiler_params=_cp("arbitrary"),
        name="moe_experts",
    )(tile_expert, n_valid, hs, w1, w3, w2, rws)


def _moe(h2, rw, re, w1, w3, w2):
    N, D = h2.shape
    tm = MOE_TM
    n_tiles = (2 * N) // tm + N_EXPERTS
    e_flat = re.reshape(2 * N)
    w_flat = rw.reshape(2 * N)
    onehot = (e_flat[:, None] == jnp.arange(N_EXPERTS, dtype=jnp.int32)[None, :]).astype(jnp.int32)
    csum = jnp.cumsum(onehot, axis=0)
    rank = jnp.sum(onehot * (csum - 1), axis=1)
    counts = csum[-1]
    ptiles = (counts + tm - 1) // tm
    tile_end = jnp.cumsum(ptiles)
    tile_start = tile_end - ptiles
    pos = tile_start[e_flat] * tm + rank
    n_valid = tile_end[-1:].astype(jnp.int32)
    tile_expert = jnp.minimum(jnp.searchsorted(tile_end, jnp.arange(n_tiles, dtype=jnp.int32), side="right"),
                              N_EXPERTS - 1).astype(jnp.int32)
    src = jnp.zeros((n_tiles * tm,), jnp.int32).at[pos].set(jnp.arange(2 * N, dtype=jnp.int32) // 2)
    rws = jnp.zeros((n_tiles * tm,), F32).at[pos].set(w_flat)
    hs = jnp.take(h2, src, axis=0)
    y = _moe_experts(tile_expert, n_valid, hs, w1, w3, w2, rws.reshape(-1, 1))
    yy = jnp.take(y, pos, axis=0).reshape(N, 2, D)
    return yy[:, 0] + yy[:, 1]


def kernel(x, c, ctx, c_ctx, mod_w, mod_b, norm1_g, norm2_g, w_in, w_out, shift_prev, shift_next, decay_w0, decay_up, iclr_a0, iclr_up, gate_up, k_k, k_a, r_k, gn_g, gn_b, q_norm_g, k_norm_g, mla_q_norm_g, mla_w_uq, mla_kv_norm_g, mla_w_ukv, router_gw, router_gb, router_ew, router_eb, exp_w1, exp_w3, exp_w2, final_norm_g):
    B, S, D = x.shape
    C = ctx.shape[1]
    T = C + S
    depth = mod_w.shape[0]
    assert C == TOK_TILE and S % TOK_TILE == 0 and B <= CTX_ROW
    n_ctx_tiles = C // TOK_TILE

    X = jnp.concatenate([ctx, x], axis=1)
    cc = jnp.zeros((SUBLANES, D), F32).at[:B].set(c).at[CTX_ROW].set(c_ctx)
    mods_all = _modulation(cc, mod_w, mod_b).reshape(depth, SUBLANES * N_MOD, 1, D)

    cos_g, sin_g = _rope_tables(T, C, GQA_HEAD)
    cos_m, sin_m = _rope_tables(T, C, MLA_ROPE)
    hid = jnp.arange(RWKV_W) // RWKV_HEAD
    head_sum = (hid[:, None] == hid[None, :]).astype(F32)

    for l in range(depth):
        mods = mods_all[l]
        w_r = w_in[l][:, :RWKV_COLS].astype(BF16)
        w_g = w_in[l][:, RWKV_COLS:RWKV_COLS + GQA_COLS].astype(BF16)
        w_m = jnp.pad(w_in[l][:, RWKV_COLS + GQA_COLS:], ((0, 0), (0, MLA_COLS_PAD - MLA_COLS))).astype(BF16)

        h = _norm_mod(X, norm1_g[l], mods, 0, 1, n_ctx_tiles).reshape(B * T, D)
        pr = _matmul(h, w_r).reshape(B, T, RWKV_COLS)
        pg = _matmul(h, w_g).reshape(B, T, GQA_COLS)
        pm = _matmul(h, w_m).reshape(B, T, MLA_COLS_PAD)

        wup_pad = jnp.pad(decay_up[l], ((0, 0), (0, ICLR_RANK), (0, 0)))
        aup_pad = jnp.pad(iclr_up[l], ((0, 0), (DECAY_RANK, 0), (0, 0)))
        r, v, kh, lw, b, kt, g, bonus = _rwkv_prep(pr, shift_prev[l], shift_next[l], decay_w0[l], wup_pad,
                                                  iclr_a0[l], aup_pad, gate_up[l], k_k[l], k_a[l], r_k[l], head_sum)
        y = _rwkv_scan(r, v, kh, lw, b, kt, C)
        o_r = _rwkv_readout(y, bonus, g, gn_g[l], gn_b[l], head_sum)

        q, k, vv = _gqa_prep(pg, q_norm_g[l], k_norm_g[l], cos_g, sin_g)
        o_g = _attention(q, k, vv, GQA_Q_HEADS, GQA_KV_HEADS, GQA_HEAD, GQA_HEAD, C)

        wq = mla_w_uq[l].reshape(MLA_Q_RANK, MLA_HEADS, MLA_NOPE + MLA_ROPE)
        wq = jnp.pad(wq, ((0, 0), (0, 0), (0, MLA_DK - MLA_NOPE - MLA_ROPE))).reshape(MLA_Q_RANK, MLA_HEADS * MLA_DK)
        wkv = mla_w_ukv[l].reshape(MLA_KV_RANK, MLA_HEADS, MLA_NOPE + MLA_V)
        wkv = jnp.concatenate([wkv[:, :, :MLA_NOPE].reshape(MLA_KV_RANK, -1), wkv[:, :, MLA_NOPE:].reshape(MLA_KV_RANK, -1)], axis=1)
        qm, km, vm = _mla_prep(pm, mla_q_norm_g[l], wq.astype(BF16), mla_kv_norm_g[l], wkv.astype(BF16), cos_m, sin_m)
        o_m = _attention(qm, km, vm, MLA_HEADS, MLA_HEADS, MLA_DK, MLA_V, C)

        o = jnp.concatenate([o_r, o_g, o_m], axis=-1)
        X = _matmul_gated_residual(o, w_out[l].astype(BF16), X, mods, 2, C)

        wr = jnp.pad(jnp.concatenate([router_gw[l], router_ew[l]], axis=1), ((0, 0), (0, LANES - N_GROUPS - N_EXPERTS)))
        br = jnp.pad(jnp.concatenate([router_gb[l], router_eb[l]]), (0, LANES - N_GROUPS - N_EXPERTS)).reshape(1, LANES)
        h2, rw, re = _norm_router(X, norm2_g[l], mods, 3, 4, n_ctx_tiles, wr, br)
        moe = _moe(h2.reshape(B * T, D), rw.reshape(B * T, LANES)[:, :2], re.reshape(B * T, LANES)[:, :2],
                   exp_w1[l], exp_w3[l], exp_w2[l])
        X = _gated_add(X, moe.reshape(B, T, D), mods, 5, n_ctx_tiles)

    return _final_norm(X, final_norm_g, n_ctx_tiles)
```

```python
import functools
import math

import jax
import jax.numpy as jnp
import numpy as np
from jax import lax
from jax.experimental import pallas as pl
from jax.experimental.pallas import tpu as pltpu

F32 = jnp.float32
BF16 = jnp.bfloat16
HI = lax.Precision.HIGHEST

V7X_VMEM_BYTES = 64 * 1024 * 1024
VMEM_LIMIT = V7X_VMEM_BYTES - 8 * 1024 * 1024
LANES = 128
SUBLANES = 8

GRID_W = 64
ROPE_THETA = 10000.0
NORM_EPS = 1e-6
GN_EPS = 64e-5
DECAY_SCALE = math.exp(-0.5)

RWKV_HEAD = 64
RWKV_W = 512
DECAY_RANK = 64
ICLR_RANK = 64
GATE_RANK = 128
RWKV_COLS = 3 * RWKV_W + DECAY_RANK + ICLR_RANK + GATE_RANK
LOWRANK_OFF = 3 * RWKV_W
GATE_OFF = LOWRANK_OFF + DECAY_RANK + ICLR_RANK
CHUNK = 64

GQA_HEAD = 128
GQA_Q_HEADS = 8
GQA_KV_HEADS = 2
GQA_Q_COLS = GQA_Q_HEADS * GQA_HEAD
GQA_KV_COLS = GQA_KV_HEADS * GQA_HEAD
GQA_COLS = GQA_Q_COLS + 2 * GQA_KV_COLS

MLA_HEADS = 4
MLA_NOPE = 128
MLA_ROPE = 64
MLA_V = 128
MLA_Q_RANK = 384
MLA_KV_RANK = 256
MLA_COLS = MLA_Q_RANK + MLA_KV_RANK + MLA_ROPE
MLA_COLS_PAD = 768
MLA_DK = 2 * LANES

N_GROUPS = 4
EXPERTS_PER_GROUP = 8
N_EXPERTS = 32
D_EXPERT = 256
MOE_TM = 256

TOK_TILE = 256
N_MOD = 6
CTX_ROW = 4


def _cp(*sem):
    return pltpu.CompilerParams(dimension_semantics=sem, vmem_limit_bytes=VMEM_LIMIT)


def _sigmoid(x):
    return 1.0 / (1.0 + jnp.exp(-x))


def _mod_body(c_ref, w_ref, b_ref, o_ref):
    c = c_ref[...]
    s = c * _sigmoid(c)
    o_ref[...] = jnp.dot(s, w_ref[...], preferred_element_type=F32, precision=HI) + b_ref[...]


def _modulation(cc, mod_w, mod_b):
    L, D, N = mod_w.shape
    tn = 1024
    return pl.pallas_call(
        _mod_body,
        grid=(L, N // tn),
        in_specs=[pl.BlockSpec((SUBLANES, D), lambda l, j: (0, 0)),
                  pl.BlockSpec((None, D, tn), lambda l, j: (l, 0, j)),
                  pl.BlockSpec((None, 1, tn), lambda l, j: (l, 0, j))],
        out_specs=pl.BlockSpec((None, SUBLANES, tn), lambda l, j: (l, 0, j)),
        out_shape=jax.ShapeDtypeStruct((L, SUBLANES, N), F32),
        compiler_params=_cp("parallel", "parallel"),
        name="modulation",
    )(cc, mod_w, mod_b.reshape(L, 1, N))


def _mod_spec(which, n_ctx_tiles, D):
    return pl.BlockSpec((None, 1, D), lambda b, i: (jnp.where(i < n_ctx_tiles, CTX_ROW, b) * N_MOD + which, 0, 0))


def _rms(x, g):
    return x * lax.rsqrt(jnp.mean(x * x, axis=-1, keepdims=True) + NORM_EPS) * g


def _norm_mod_body(x_ref, g_ref, sh_ref, sc_ref, o_ref):
    y = _rms(x_ref[...], g_ref[...])
    o_ref[...] = (y * (1.0 + sc_ref[...]) + sh_ref[...]).astype(o_ref.dtype)


def _norm_body(x_ref, g_ref, o_ref):
    o_ref[...] = _rms(x_ref[...], g_ref[...]).astype(o_ref.dtype)


def _route(logits):
    lane = lax.broadcasted_iota(jnp.int32, logits.shape, 1)
    lane_f = lane.astype(F32)
    neg = jnp.float32(-1e30)
    far = jnp.float32(1e9)
    first_at = lambda hit: jnp.min(jnp.where(hit, lane_f, far), axis=-1, keepdims=True).astype(jnp.int32)
    gl = jnp.where(lane < N_GROUPS, logits, neg)
    gmax = jnp.max(gl, axis=-1, keepdims=True)
    gidx = first_at(gl == gmax)
    p_sel = 1.0 / jnp.sum(jnp.exp(gl - gmax), axis=-1, keepdims=True)
    lo = N_GROUPS + gidx * EXPERTS_PER_GROUP
    el = jnp.where((lane >= lo) & (lane < lo + EXPERTS_PER_GROUP), logits, neg)
    m1 = jnp.max(el, axis=-1, keepdims=True)
    i1 = first_at(el == m1)
    el2 = jnp.where(lane == i1, neg, el)
    m2 = jnp.max(el2, axis=-1, keepdims=True)
    i2 = first_at(el2 == m2)
    t = jnp.exp(m2 - m1)
    w1 = p_sel / (1.0 + t)
    w2 = p_sel * t / (1.0 + t)
    rw = jnp.where(lane == 0, w1, jnp.where(lane == 1, w2, 0.0))
    re = jnp.where(lane == 0, i1 - N_GROUPS, jnp.where(lane == 1, i2 - N_GROUPS, 0))
    return rw, re


def _norm_router_body(x_ref, g_ref, sh_ref, sc_ref, wr_ref, br_ref, h_ref, rw_ref, re_ref):
    y = _rms(x_ref[...], g_ref[...])
    h = y * (1.0 + sc_ref[...]) + sh_ref[...]
    h_ref[...] = h.astype(h_ref.dtype)
    logits = jnp.dot(h, wr_ref[...], preferred_element_type=F32, precision=HI) + br_ref[...]
    rw, re = _route(logits)
    rw_ref[...] = rw
    re_ref[...] = re


def _norm_mod(X, g, mods, shift_i, scale_i, n_ctx_tiles):
    B, T, D = X.shape
    tt = TOK_TILE
    return pl.pallas_call(
        _norm_mod_body,
        grid=(B, T // tt),
        in_specs=[pl.BlockSpec((None, tt, D), lambda b, i: (b, i, 0)),
                  pl.BlockSpec((1, D), lambda b, i: (0, 0)),
                  _mod_spec(shift_i, n_ctx_tiles, D),
                  _mod_spec(scale_i, n_ctx_tiles, D)],
        out_specs=pl.BlockSpec((None, tt, D), lambda b, i: (b, i, 0)),
        out_shape=jax.ShapeDtypeStruct((B, T, D), BF16),
        compiler_params=_cp("parallel", "parallel"),
        name="norm_mod",
    )(X, g.reshape(1, D), mods, mods)


def _norm_router(X, g, mods, shift_i, scale_i, n_ctx_tiles, wr, br):
    B, T, D = X.shape
    tt = TOK_TILE
    tok = lambda w, dt: (pl.BlockSpec((None, tt, w), lambda b, i: (b, i, 0)), jax.ShapeDtypeStruct((B, T, w), dt))
    outs = [tok(D, F32), tok(LANES, F32), tok(LANES, jnp.int32)]
    return pl.pallas_call(
        _norm_router_body,
        grid=(B, T // tt),
        in_specs=[pl.BlockSpec((None, tt, D), lambda b, i: (b, i, 0)),
                  pl.BlockSpec((1, D), lambda b, i: (0, 0)),
                  _mod_spec(shift_i, n_ctx_tiles, D),
                  _mod_spec(scale_i, n_ctx_tiles, D),
                  pl.BlockSpec((D, LANES), lambda b, i: (0, 0)),
                  pl.BlockSpec((1, LANES), lambda b, i: (0, 0))],
        out_specs=[o[0] for o in outs],
        out_shape=[o[1] for o in outs],
        compiler_params=_cp("parallel", "parallel"),
        name="norm_router",
    )(X, g.reshape(1, D), mods, mods, wr, br)


def _final_norm(X, g, n_ctx_tiles):
    B, T, D = X.shape
    tt = TOK_TILE
    S = T - n_ctx_tiles * tt
    return pl.pallas_call(
        _norm_body,
        grid=(B, S // tt),
        in_specs=[pl.BlockSpec((None, tt, D), lambda b, i: (b, i + n_ctx_tiles, 0)),
                  pl.BlockSpec((1, D), lambda b, i: (0, 0))],
        out_specs=pl.BlockSpec((None, tt, D), lambda b, i: (b, i, 0)),
        out_shape=jax.ShapeDtypeStruct((B, S, D), F32),
        compiler_params=_cp("parallel", "parallel"),
        name="final_norm",
    )(X, g.reshape(1, D))


def _mm_body(a_ref, w_ref, o_ref):
    o_ref[...] = jnp.dot(a_ref[...], w_ref[...], preferred_element_type=F32).astype(o_ref.dtype)


def _matmul(a, w, tm=512):
    M, K = a.shape
    N = w.shape[1]
    return pl.pallas_call(
        _mm_body,
        grid=(M // tm,),
        in_specs=[pl.BlockSpec((tm, K), lambda i: (i, 0)),
                  pl.BlockSpec((K, N), lambda i: (0, 0))],
        out_specs=pl.BlockSpec((tm, N), lambda i: (i, 0)),
        out_shape=jax.ShapeDtypeStruct((M, N), F32),
        compiler_params=_cp("parallel"),
        name="token_matmul",
    )(a, w)


def _mm_res_body(a_ref, w_ref, x_ref, gl_ref, gc_ref, o_ref, *, n_ctx, tm):
    acc = jnp.dot(a_ref[...], w_ref[...], preferred_element_type=F32)
    row = pl.program_id(1) * tm + lax.broadcasted_iota(jnp.int32, (tm, 1), 0)
    gate = jnp.where(row < n_ctx, gc_ref[...], gl_ref[...])
    o_ref[...] = x_ref[...] + gate * acc


def _matmul_gated_residual(a, w, X, mods, gate_i, n_ctx, tm=384):
    B, T, K = a.shape
    D = w.shape[1]
    assert T % tm == 0
    return pl.pallas_call(
        functools.partial(_mm_res_body, n_ctx=n_ctx, tm=tm),
        grid=(B, T // tm),
        in_specs=[pl.BlockSpec((None, tm, K), lambda b, i: (b, i, 0)),
                  pl.BlockSpec((K, D), lambda b, i: (0, 0)),
                  pl.BlockSpec((None, tm, D), lambda b, i: (b, i, 0)),
                  pl.BlockSpec((None, 1, D), lambda b, i: (b * N_MOD + gate_i, 0, 0)),
                  pl.BlockSpec((None, 1, D), lambda b, i: (CTX_ROW * N_MOD + gate_i, 0, 0))],
        out_specs=pl.BlockSpec((None, tm, D), lambda b, i: (b, i, 0)),
        out_shape=jax.ShapeDtypeStruct((B, T, D), F32),
        compiler_params=_cp("parallel", "parallel"),
        name="out_proj_residual",
    )(a, w, X, mods, mods)


def _gated_add_body(x_ref, y0_ref, y1_ref, g_ref, o_ref):
    o_ref[...] = x_ref[...] + g_ref[...] * (y0_ref[...] + y1_ref[...])


def _gated_add(X, Y0, Y1, mods, gate_i, n_ctx_tiles):
    B, T, D = X.shape
    tt = TOK_TILE
    blk = pl.BlockSpec((None, tt, D), lambda b, i: (b, i, 0))
    return pl.pallas_call(
        _gated_add_body,
        grid=(B, T // tt),
        in_specs=[blk, blk, blk, _mod_spec(gate_i, n_ctx_tiles, D)],
        out_specs=blk,
        out_shape=jax.ShapeDtypeStruct((B, T, D), F32),
        compiler_params=_cp("parallel", "parallel"),
        name="gated_add",
    )(X, Y0, Y1, mods)


def _rwkv_prep_body(p_ref, pv_ref, nx_ref, mup_ref, mun_ref, w0_ref, wup_ref, a0_ref, aup_ref, gup_ref,
                    kk_ref, ka_ref, rk_ref, e_ref,
                    r_o, v_o, kh_o, lw_o, b_o, kt_o, g_o, bon_o, *, n_tiles, tt):
    i = pl.program_id(1)
    p = p_ref[...]
    seq_first = i <= 1
    seq_last = (i == 0) | (i == n_tiles - 1)
    prow = jnp.where(seq_first, 0.0, pv_ref[SUBLANES - 1:SUBLANES, :])
    nrow = jnp.where(seq_last, 0.0, nx_ref[0:1, :])
    rid = lax.broadcasted_iota(jnp.int32, (tt, 1), 0)
    prev = jnp.where(rid == 0, prow, pltpu.roll(p, 1, 0))
    nxt = jnp.where(rid == tt - 1, nrow, pltpu.roll(p, tt - 1, 0))
    z = p + mup_ref[...] * (prev - p) + mun_ref[...] * (nxt - p)

    W = RWKV_W
    r = z[:, 0:W]
    k = z[:, W:2 * W]
    v = z[:, 2 * W:3 * W]
    lowrank = z[:, LOWRANK_OFF:LOWRANK_OFF + LANES]
    gd = z[:, GATE_OFF:GATE_OFF + GATE_RANK]
    head_sum = e_ref[...]

    kap = k * kk_ref[...]
    ss = jnp.dot(kap * kap, head_sum, preferred_element_type=F32, precision=HI)
    khat = kap * lax.rsqrt(ss + 1e-12)
    wd_t = jnp.tanh(lowrank)
    g_o[...] = jnp.dot(_sigmoid(gd), gup_ref[...], preferred_element_type=F32, precision=HI)
    r_o[...] = r
    v_o[...] = v
    kh_o[...] = khat
    kt_sum = None
    for d in range(2):
        dec = jnp.dot(wd_t, wup_ref[d], preferred_element_type=F32, precision=HI)
        lw_o[d] = -DECAY_SCALE * _sigmoid(w0_ref[d:d + 1, :] + dec)
        a = _sigmoid(a0_ref[d:d + 1, :] + jnp.dot(lowrank, aup_ref[d], preferred_element_type=F32, precision=HI))
        kt = k * (1.0 + (a - 1.0) * ka_ref[...])
        kt_o[d] = kt
        b_o[d] = a * khat
        kt_sum = kt if kt_sum is None else kt_sum + kt
    bsum = jnp.dot(r * kt_sum * rk_ref[...], head_sum, preferred_element_type=F32, precision=HI)
    bon_o[...] = bsum * v


def _rwkv_prep(pr, mu_prev, mu_next, w0, wup_pad, a0, aup_pad, g_up, k_k, k_a, r_k, head_sum):
    B, T, _ = pr.shape
    tt = TOK_TILE
    W = RWKV_W
    n_tiles = T // tt
    n8 = tt // SUBLANES
    row = lambda v: v.reshape(1, -1)
    full = lambda a: pl.BlockSpec(a.shape, lambda b, i: (0,) * a.ndim)
    tok = pl.BlockSpec((None, tt, W), lambda b, i: (b, i, 0))
    tok2 = pl.BlockSpec((None, 2, tt, W), lambda b, i: (b, 0, i, 0))
    s1 = jax.ShapeDtypeStruct((B, T, W), F32)
    s2 = jax.ShapeDtypeStruct((B, 2, T, W), F32)
    consts = [row(mu_prev), row(mu_next), w0, wup_pad, a0, aup_pad, g_up, row(k_k), row(k_a), row(r_k), head_sum]
    return pl.pallas_call(
        functools.partial(_rwkv_prep_body, n_tiles=n_tiles, tt=tt),
        grid=(B, n_tiles),
        in_specs=[pl.BlockSpec((None, tt, RWKV_COLS), lambda b, i: (b, i, 0)),
                  pl.BlockSpec((None, SUBLANES, RWKV_COLS), lambda b, i: (b, jnp.maximum(i * n8 - 1, 0), 0)),
                  pl.BlockSpec((None, SUBLANES, RWKV_COLS), lambda b, i: (b, jnp.minimum((i + 1) * n8, T // SUBLANES - 1), 0)),
                  ] + [full(a) for a in consts],
        out_specs=[tok, tok, tok, tok2, tok2, tok2, tok, tok],
        out_shape=[s1, s1, s1, s2, s2, s2, s1, s1],
        compiler_params=_cp("parallel", "parallel"),
        name="rwkv_prep",
    )(pr, pr, pr, *consts)


def _stack_heads(x):
    lane = lax.broadcasted_iota(jnp.int32, x.shape, 1)
    first = lane < RWKV_HEAD
    return jnp.concatenate([jnp.where(first, x, 0.0), jnp.where(first, 0.0, x)], axis=0)


def _dot(a, b):
    return jnp.dot(a.astype(BF16), b.astype(BF16), preferred_element_type=F32)


def _dot_nt(a, b):
    return lax.dot_general(a.astype(BF16), b.astype(BF16), (((1,), (1,)), ((), ())), preferred_element_type=F32)


def _dot_tn(a, b):
    return jnp.dot(a.T.astype(BF16), b.astype(BF16), preferred_element_type=F32)


def _chunk_operands(r, v, kh, lw, b, kt, reverse):
    L = CHUNK
    ti = lax.broadcasted_iota(jnp.int32, (L, L), 0)
    tj = lax.broadcasted_iota(jnp.int32, (L, L), 1)
    tri = jnp.where((ti <= tj) if reverse else (ti >= tj), 1.0, 0.0)
    lam = jnp.dot(tri, lw, preferred_element_type=F32, precision=HI)
    tot = lam[0:1, :] if reverse else lam[L - 1:L, :]
    e_n = jnp.exp(-lam)
    e_g = jnp.exp(tot - lam)
    full = dict(A=kh * jnp.exp(lam - lw), R=r * jnp.exp(lam), Kn=kt * e_n, Bn=b * e_n, Kg=kt * e_g, Bg=b * e_g, V=v)
    e_tot = jnp.exp(tot)
    pairs = []
    for p in range(RWKV_W // LANES):
        sl = slice(p * LANES, (p + 1) * LANES)
        ops = {k: _stack_heads(a[:, sl]) for k, a in full.items()}
        ops["e_tot"] = e_tot[:, sl]
        ops["reverse"] = reverse
        pairs.append(ops)
    return pairs


def _chunk_masks(reverse):
    L = CHUNK
    si = lax.broadcasted_iota(jnp.int32, (2 * L, 2 * L), 0)
    sj = lax.broadcasted_iota(jnp.int32, (2 * L, 2 * L), 1)
    same = (si >= L) == (sj >= L)
    before = (si < sj) if reverse else (si > sj)
    return same & before, same & (before | (si == sj)), si == sj


def _chunks_solve(chains, states):
    L = CHUNK
    P2 = 2 * L
    n = len(chains)
    masks = {rev: _chunk_masks(rev) for rev in {c["reverse"] for c in chains}}
    strict = [masks[c["reverse"]][0] for c in chains]
    incl = [masks[c["reverse"]][1] for c in chains]
    eye = masks[chains[0]["reverse"]][2]

    big = [_dot_nt(jnp.concatenate([c["A"], c["R"]], axis=0), jnp.concatenate([c["Bn"], c["Kn"]], axis=0)) for c in chains]
    Mb = [jnp.where(strict[i], big[i][0:P2, 0:P2], 0.0) for i in range(n)]
    Mkv = [jnp.where(strict[i], big[i][0:P2, P2:2 * P2], 0.0) for i in range(n)]
    Pb = [jnp.where(incl[i], big[i][P2:2 * P2, 0:P2], 0.0) for i in range(n)]
    Pkv = [jnp.where(incl[i], big[i][P2:2 * P2, P2:2 * P2], 0.0) for i in range(n)]

    Pw = [-m for m in Mb]
    Tm = [jnp.where(eye, 1.0, 0.0) + p for p in Pw]
    for _ in range(int(math.log2(L)) - 1):
        Pw = [_dot(p, p) for p in Pw]
        Tm = [t + _dot(t, p) for t, p in zip(Tm, Pw)]

    MV = [_dot(Mkv[i], chains[i]["V"]) for i in range(n)]
    TAM = [_dot(Tm[i], jnp.concatenate([chains[i]["A"], MV[i]], axis=1)) for i in range(n)]
    PB = [_dot(Pb[i], TAM[i]) for i in range(n)]
    PV = [_dot(Pkv[i], chains[i]["V"]) for i in range(n)]
    BG = [_dot_tn(chains[i]["Bg"], TAM[i]) for i in range(n)]
    KV = [_dot_tn(chains[i]["Kg"], chains[i]["V"]) for i in range(n)]
    ys, new_states = [], []
    for i in range(n):
        RA = chains[i]["R"] - PB[i][:, 0:P2]
        G2 = jnp.where(eye, chains[i]["e_tot"], 0.0) - BG[i][:, 0:P2]
        out = _dot(jnp.concatenate([RA, G2], axis=0), states[i])
        Ys = out[0:P2] + PV[i] - PB[i][:, P2:2 * P2]
        ys.append(Ys[0:L] + Ys[L:P2])
        new_states.append(out[P2:2 * P2] + KV[i] - BG[i][:, P2:2 * P2])
    return ys, new_states


def _rwkv_scan_body(rf_ref, vf_ref, khf_ref, rb_ref, vb_ref, khb_ref, lwf_ref, bf_ref, ktf_ref, lwb_ref, bb_ref, ktb_ref,
                    yf_ref, yb_ref, s_ref):
    @pl.when(pl.program_id(1) == 0)
    def _():
        s_ref[...] = jnp.zeros_like(s_ref)

    n_pairs = RWKV_W // LANES
    chains = (_chunk_operands(rf_ref[...], vf_ref[...], khf_ref[...], lwf_ref[...], bf_ref[...], ktf_ref[...], False)
              + _chunk_operands(rb_ref[...], vb_ref[...], khb_ref[...], lwb_ref[...], bb_ref[...], ktb_ref[...], True))
    ys, new_states = _chunks_solve(chains, [s_ref[i] for i in range(2 * n_pairs)])
    for i in range(2 * n_pairs):
        s_ref[i] = new_states[i]
    yf_ref[...] = jnp.concatenate(ys[:n_pairs], axis=1)
    yb_ref[...] = jnp.concatenate(ys[n_pairs:], axis=1)


def _rwkv_scan(r, v, kh, lw, b, kt, n_ctx):
    B, T, W = r.shape
    nc = T // CHUNK
    ncc = n_ctx // CHUNK
    rev = lambda j: jnp.where(j < ncc, ncc - 1 - j, nc + ncc - 1 - j)
    fwd1 = pl.BlockSpec((None, CHUNK, W), lambda bb, j: (bb, j, 0))
    bwd1 = pl.BlockSpec((None, CHUNK, W), lambda bb, j: (bb, rev(j), 0))
    fwd2 = pl.BlockSpec((None, None, CHUNK, W), lambda bb, j: (bb, 0, j, 0))
    bwd2 = pl.BlockSpec((None, None, CHUNK, W), lambda bb, j: (bb, 1, rev(j), 0))
    out = jax.ShapeDtypeStruct((B, T, W), F32)
    return pl.pallas_call(
        _rwkv_scan_body,
        grid=(B, nc),
        in_specs=[fwd1, fwd1, fwd1, bwd1, bwd1, bwd1, fwd2, fwd2, fwd2, bwd2, bwd2, bwd2],
        out_specs=[fwd1, bwd1],
        out_shape=[out, out],
        scratch_shapes=[pltpu.VMEM((2 * W // LANES, LANES, LANES), F32)],
        compiler_params=_cp("parallel", "arbitrary"),
        name="rwkv_scan",
    )(r, v, kh, r, v, kh, lw, b, kt, lw, b, kt)


def _rwkv_readout_body(yf_ref, yb_ref, bon_ref, g_ref, gng_ref, gnb_ref, e_ref, o_ref):
    y = yf_ref[...] + yb_ref[...]
    head_mean = e_ref[...] * (1.0 / RWKV_HEAD)
    mu = jnp.dot(y, head_mean, preferred_element_type=F32, precision=HI)
    yc = y - mu
    var = jnp.dot(yc * yc, head_mean, preferred_element_type=F32, precision=HI)
    yn = yc * lax.rsqrt(var + GN_EPS) * gng_ref[...] + gnb_ref[...]
    o_ref[...] = ((yn + bon_ref[...]) * g_ref[...]).astype(o_ref.dtype)


def _rwkv_readout(yf, yb, bonus, g, gn_g, gn_b, head_sum):
    B, T, W = yf.shape
    tt = TOK_TILE
    tok = pl.BlockSpec((None, tt, W), lambda b, i: (b, i, 0))
    row = pl.BlockSpec((1, W), lambda b, i: (0, 0))
    return pl.pallas_call(
        _rwkv_readout_body,
        grid=(B, T // tt),
        in_specs=[tok, tok, tok, tok, row, row, pl.BlockSpec((W, W), lambda b, i: (0, 0))],
        out_specs=tok,
        out_shape=jax.ShapeDtypeStruct((B, T, W), BF16),
        compiler_params=_cp("parallel", "parallel"),
        name="rwkv_readout",
    )(yf, yb, bonus, g, gn_g.reshape(1, W), gn_b.reshape(1, W), head_sum)


def _rope(y, cos, sin_signed, quarter):
    lane = lax.broadcasted_iota(jnp.int32, y.shape, 1)
    first = (lane & (2 * quarter - 1)) < quarter
    partner = jnp.where(first, pltpu.roll(y, LANES - quarter, 1), pltpu.roll(y, quarter, 1))
    return y * cos + partner * sin_signed


def _gqa_prep_body(p_ref, qg_ref, kg_ref, cos_ref, sin_ref, q_o, k_o, v_o):
    cos = cos_ref[...]
    sin = sin_ref[...]
    scale = GQA_HEAD ** -0.5
    for h in range(GQA_Q_HEADS):
        sl = slice(h * GQA_HEAD, (h + 1) * GQA_HEAD)
        q = _rms(p_ref[:, sl], qg_ref[...])
        q_o[:, sl] = (_rope(q, cos, sin, GQA_HEAD // 4) * scale).astype(q_o.dtype)
    for h in range(GQA_KV_HEADS):
        sl = slice(h * GQA_HEAD, (h + 1) * GQA_HEAD)
        k = _rms(p_ref[:, GQA_Q_COLS + h * GQA_HEAD:GQA_Q_COLS + (h + 1) * GQA_HEAD], kg_ref[...])
        k_o[:, sl] = _rope(k, cos, sin, GQA_HEAD // 4).astype(k_o.dtype)
    v_o[...] = p_ref[:, GQA_Q_COLS + GQA_KV_COLS:GQA_COLS].astype(v_o.dtype)


def _gqa_prep(pg, q_norm_g, k_norm_g, cos, sin):
    B, T, _ = pg.shape
    tt = TOK_TILE
    tok = lambda w: pl.BlockSpec((None, tt, w), lambda b, i: (b, i, 0))
    row = pl.BlockSpec((1, GQA_HEAD), lambda b, i: (0, 0))
    tab = pl.BlockSpec((tt, LANES), lambda b, i: (i, 0))
    return pl.pallas_call(
        _gqa_prep_body,
        grid=(B, T // tt),
        in_specs=[tok(GQA_COLS), row, row, tab, tab],
        out_specs=[tok(GQA_Q_COLS), tok(GQA_KV_COLS), tok(GQA_KV_COLS)],
        out_shape=[jax.ShapeDtypeStruct((B, T, GQA_Q_COLS), BF16),
                   jax.ShapeDtypeStruct((B, T, GQA_KV_COLS), BF16),
                   jax.ShapeDtypeStruct((B, T, GQA_KV_COLS), BF16)],
        compiler_params=_cp("parallel", "parallel"),
        name="gqa_prep",
    )(pg, q_norm_g.reshape(1, -1), k_norm_g.reshape(1, -1), cos, sin)


def _mla_prep_body(p_ref, qg_ref, wq_ref, kvg_ref, wkv_ref, cos_ref, sin_ref, q_o, k_o, v_o):
    cos = cos_ref[...]
    sin = sin_ref[...]
    scale = (MLA_NOPE + MLA_ROPE) ** -0.5
    cq = _rms(p_ref[:, 0:MLA_Q_RANK], qg_ref[...])
    q = jnp.dot(cq.astype(BF16), wq_ref[...], preferred_element_type=F32) * scale
    ckv = _rms(p_ref[:, MLA_Q_RANK:MLA_Q_RANK + MLA_KV_RANK], kvg_ref[...])
    kv = jnp.dot(ckv.astype(BF16), wkv_ref[...], preferred_element_type=F32)
    kr = _rope(p_ref[:, MLA_Q_RANK + MLA_KV_RANK:MLA_COLS_PAD], cos, sin, MLA_ROPE // 4).astype(k_o.dtype)
    for h in range(MLA_HEADS):
        lo = h * MLA_DK
        q_o[:, lo:lo + LANES] = q[:, lo:lo + LANES].astype(q_o.dtype)
        q_o[:, lo + LANES:lo + MLA_DK] = _rope(q[:, lo + LANES:lo + MLA_DK], cos, sin, MLA_ROPE // 4).astype(q_o.dtype)
        k_o[:, lo:lo + LANES] = kv[:, h * MLA_NOPE:(h + 1) * MLA_NOPE].astype(k_o.dtype)
        k_o[:, lo + LANES:lo + MLA_DK] = kr
    v_o[...] = kv[:, MLA_HEADS * MLA_NOPE:].astype(v_o.dtype)


def _mla_prep(pm, q_norm_g, wq, kv_norm_g, wkv, cos, sin):
    B, T, _ = pm.shape
    tt = TOK_TILE
    tok = lambda w: pl.BlockSpec((None, tt, w), lambda b, i: (b, i, 0))
    full = lambda a: pl.BlockSpec(a.shape, lambda b, i: (0,) * a.ndim)
    tab = pl.BlockSpec((tt, LANES), lambda b, i: (i, 0))
    qg = q_norm_g.reshape(1, -1)
    kvg = kv_norm_g.reshape(1, -1)
    return pl.pallas_call(
        _mla_prep_body,
        grid=(B, T // tt),
        in_specs=[tok(MLA_COLS_PAD), full(qg), full(wq), full(kvg), full(wkv), tab, tab],
        out_specs=[tok(MLA_HEADS * MLA_DK), tok(MLA_HEADS * MLA_DK), tok(MLA_HEADS * MLA_V)],
        out_shape=[jax.ShapeDtypeStruct((B, T, MLA_HEADS * MLA_DK), BF16),
                   jax.ShapeDtypeStruct((B, T, MLA_HEADS * MLA_DK), BF16),
                   jax.ShapeDtypeStruct((B, T, MLA_HEADS * MLA_V), BF16)],
        compiler_params=_cp("parallel", "parallel"),
        name="mla_prep",
    )(pm, qg, wq, kvg, wkv, cos, sin)


def _attn_body(q_ref, k_ref, v_ref, o_ref, *, hq, hkv, dk, dv, n_ctx_tiles, n_ctx, n_all):
    rep = hq // hkv

    def run(nk):
        for g in range(hkv):
            kg = k_ref[0:nk, g * dk:(g + 1) * dk]
            vg = v_ref[0:nk, g * dv:(g + 1) * dv]
            for rr in range(rep):
                h = g * rep + rr
                s = _dot_nt(q_ref[:, h * dk:(h + 1) * dk], kg)
                p = jnp.exp(s - jnp.max(s, axis=-1, keepdims=True))
                den = jnp.sum(p, axis=-1, keepdims=True)
                o = jnp.dot(p.astype(BF16), vg, preferred_element_type=F32)
                o_ref[:, h * dv:(h + 1) * dv] = (o / den).astype(o_ref.dtype)

    @pl.when(pl.program_id(1) < n_ctx_tiles)
    def _():
        run(n_ctx)

    @pl.when(pl.program_id(1) >= n_ctx_tiles)
    def _():
        run(n_all)


def _attention(q, k, v, hq, hkv, dk, dv, n_ctx):
    B, T, _ = q.shape
    tq = TOK_TILE
    return pl.pallas_call(
        functools.partial(_attn_body, hq=hq, hkv=hkv, dk=dk, dv=dv, n_ctx_tiles=n_ctx // tq, n_ctx=n_ctx, n_all=T),
        grid=(B, T // tq),
        in_specs=[pl.BlockSpec((None, tq, hq * dk), lambda b, i: (b, i, 0)),
                  pl.BlockSpec((None, T, hkv * dk), lambda b, i: (b, 0, 0)),
                  pl.BlockSpec((None, T, hkv * dv), lambda b, i: (b, 0, 0))],
        out_specs=pl.BlockSpec((None, tq, hq * dv), lambda b, i: (b, i, 0)),
        out_shape=jax.ShapeDtypeStruct((B, T, hq * dv), BF16),
        compiler_params=_cp("parallel", "parallel"),
        name="attention",
    )(q, k, v)


def _rope_tables(T, n_ctx, n_rot):
    quarter = n_rot // 4
    t = jnp.arange(T - n_ctx)
    row = (t // GRID_W).astype(F32)
    col = (t % GRID_W).astype(F32)
    inv = ROPE_THETA ** (-jnp.arange(quarter, dtype=F32) / quarter)
    ar = row[:, None] * inv[None, :]
    ac = col[:, None] * inv[None, :]
    pad = LANES - n_rot
    cos = jnp.concatenate([jnp.cos(ar), jnp.cos(ar), jnp.cos(ac), jnp.cos(ac), jnp.ones((T - n_ctx, pad), F32)], axis=1)
    sin = jnp.concatenate([-jnp.sin(ar), jnp.sin(ar), -jnp.sin(ac), jnp.sin(ac), jnp.zeros((T - n_ctx, pad), F32)], axis=1)
    cos = jnp.concatenate([jnp.ones((n_ctx, LANES), F32), cos], axis=0)
    sin = jnp.concatenate([jnp.zeros((n_ctx, LANES), F32), sin], axis=0)
    return cos, sin


def _moe_body(te_ref, nv_ref, hs_ref, w1_ref, w3_ref, w2_ref, rw_ref, y_ref):
    i = pl.program_id(0)

    @pl.when(i < nv_ref[0])
    def _():
        hs = hs_ref[...].astype(BF16)
        a = jnp.dot(hs, w1_ref[...].astype(BF16), preferred_element_type=F32)
        b = jnp.dot(hs, w3_ref[...].astype(BF16), preferred_element_type=F32)
        act = a * _sigmoid(a) * b * rw_ref[...]
        y_ref[...] = jnp.dot(act.astype(BF16), w2_ref[...].astype(BF16), preferred_element_type=F32)

    @pl.when(i >= nv_ref[0])
    def _():
        y_ref[...] = jnp.zeros_like(y_ref)


def _moe_experts(tile_expert, n_valid, hs, w1, w3, w2, rws, layer):
    NP, D = hs.shape
    tm = MOE_TM
    DE = w1.shape[-1]
    grid_spec = pltpu.PrefetchScalarGridSpec(
        num_scalar_prefetch=2,
        grid=(NP // tm,),
        in_specs=[pl.BlockSpec((tm, D), lambda i, te, nv: (i, 0)),
                  pl.BlockSpec((None, None, D, DE), lambda i, te, nv: (layer, te[i], 0, 0)),
                  pl.BlockSpec((None, None, D, DE), lambda i, te, nv: (layer, te[i], 0, 0)),
                  pl.BlockSpec((None, None, DE, D), lambda i, te, nv: (layer, te[i], 0, 0)),
                  pl.BlockSpec((tm, 1), lambda i, te, nv: (i, 0))],
        out_specs=pl.BlockSpec((tm, D), lambda i, te, nv: (i, 0)),
    )
    return pl.pallas_call(
        _moe_body,
        grid_spec=grid_spec,
        out_shape=jax.ShapeDtypeStruct((NP, D), F32),
        compiler_params=_cp("arbitrary"),
        name="moe_experts",
    )(tile_expert, n_valid, hs, w1, w3, w2, rws)


def _moe(h2, rw, re, w1, w3, w2, layer):
    N, D = h2.shape
    tm = MOE_TM
    n_tiles = (2 * N) // tm + N_EXPERTS
    e_flat = re.reshape(2 * N)
    w_flat = rw.reshape(2 * N)
    onehot = (e_flat[:, None] == jnp.arange(N_EXPERTS, dtype=jnp.int32)[None, :]).astype(jnp.int32)
    csum = jnp.cumsum(onehot, axis=0)
    rank = jnp.sum(onehot * (csum - 1), axis=1)
    counts = csum[-1]
    ptiles = (counts + tm - 1) // tm
    tile_end = jnp.cumsum(ptiles)
    tile_start = tile_end - ptiles
    pos = tile_start[e_flat] * tm + rank
    n_valid = tile_end[-1:].astype(jnp.int32)
    tile_ids = jnp.arange(n_tiles, dtype=jnp.int32)
    tile_expert = jnp.minimum(jnp.sum((tile_end[None, :] <= tile_ids[:, None]).astype(jnp.int32), axis=1), N_EXPERTS - 1)
    src = jnp.zeros((n_tiles * tm,), jnp.int32).at[pos].set(jnp.arange(2 * N, dtype=jnp.int32) // 2)
    rws = jnp.zeros((n_tiles * tm,), F32).at[pos].set(w_flat)
    rows = lambda a, idx: a.at[idx].get(mode="promise_in_bounds")
    hs = rows(h2, src)
    y = _moe_experts(tile_expert, n_valid, hs, w1, w3, w2, rws.reshape(-1, 1), layer)
    pos2 = pos.reshape(N, 2)
    return rows(y, pos2[:, 0]), rows(y, pos2[:, 1])


def kernel(x, c, ctx, c_ctx, mod_w, mod_b, norm1_g, norm2_g, w_in, w_out, shift_prev, shift_next, decay_w0, decay_up, iclr_a0, iclr_up, gate_up, k_k, k_a, r_k, gn_g, gn_b, q_norm_g, k_norm_g, mla_q_norm_g, mla_w_uq, mla_kv_norm_g, mla_w_ukv, router_gw, router_gb, router_ew, router_eb, exp_w1, exp_w3, exp_w2, final_norm_g):
    B, S, D = x.shape
    C = ctx.shape[1]
    T = C + S
    depth = mod_w.shape[0]
    assert C == TOK_TILE and S % TOK_TILE == 0 and B <= CTX_ROW
    n_ctx_tiles = C // TOK_TILE

    X = jnp.concatenate([ctx, x], axis=1)
    cc = jnp.zeros((SUBLANES, D), F32).at[:B].set(c).at[CTX_ROW].set(c_ctx)
    mods_all = _modulation(cc, mod_w, mod_b).reshape(depth, SUBLANES * N_MOD, 1, D)

    cos_g, sin_g = _rope_tables(T, C, GQA_HEAD)
    cos_m, sin_m = _rope_tables(T, C, MLA_ROPE)
    hid = jnp.arange(RWKV_W) // RWKV_HEAD
    head_sum = (hid[:, None] == hid[None, :]).astype(F32)

    for l in range(depth):
        mods = mods_all[l]
        w_r = w_in[l][:, :RWKV_COLS].astype(BF16)
        w_g = w_in[l][:, RWKV_COLS:RWKV_COLS + GQA_COLS].astype(BF16)
        w_m = jnp.pad(w_in[l][:, RWKV_COLS + GQA_COLS:], ((0, 0), (0, MLA_COLS_PAD - MLA_COLS))).astype(BF16)

        h = _norm_mod(X, norm1_g[l], mods, 0, 1, n_ctx_tiles).reshape(B * T, D)
        pr = _matmul(h, w_r).reshape(B, T, RWKV_COLS)
        pg = _matmul(h, w_g).reshape(B, T, GQA_COLS)
        pm = _matmul(h, w_m).reshape(B, T, MLA_COLS_PAD)

        wup_pad = jnp.pad(decay_up[l], ((0, 0), (0, ICLR_RANK), (0, 0)))
        aup_pad = jnp.pad(iclr_up[l], ((0, 0), (DECAY_RANK, 0), (0, 0)))
        r, v, kh, lw, b, kt, g, bonus = _rwkv_prep(pr, shift_prev[l], shift_next[l], decay_w0[l], wup_pad,
                                                  iclr_a0[l], aup_pad, gate_up[l], k_k[l], k_a[l], r_k[l], head_sum)
        yf, yb = _rwkv_scan(r, v, kh, lw, b, kt, C)
        o_r = _rwkv_readout(yf, yb, bonus, g, gn_g[l], gn_b[l], head_sum)

        q, k, vv = _gqa_prep(pg, q_norm_g[l], k_norm_g[l], cos_g, sin_g)
        o_g = _attention(q, k, vv, GQA_Q_HEADS, GQA_KV_HEADS, GQA_HEAD, GQA_HEAD, C)

        wq = mla_w_uq[l].reshape(MLA_Q_RANK, MLA_HEADS, MLA_NOPE + MLA_ROPE)
        wq = jnp.pad(wq, ((0, 0), (0, 0), (0, MLA_DK - MLA_NOPE - MLA_ROPE))).reshape(MLA_Q_RANK, MLA_HEADS * MLA_DK)
        wkv = mla_w_ukv[l].reshape(MLA_KV_RANK, MLA_HEADS, MLA_NOPE + MLA_V)
        wkv = jnp.concatenate([wkv[:, :, :MLA_NOPE].reshape(MLA_KV_RANK, -1), wkv[:, :, MLA_NOPE:].reshape(MLA_KV_RANK, -1)], axis=1)
        qm, km, vm = _mla_prep(pm, mla_q_norm_g[l], wq.astype(BF16), mla_kv_norm_g[l], wkv.astype(BF16), cos_m, sin_m)
        o_m = _attention(qm, km, vm, MLA_HEADS, MLA_HEADS, MLA_DK, MLA_V, C)

        o = jnp.concatenate([o_r, o_g, o_m], axis=-1)
        X = _matmul_gated_residual(o, w_out[l].astype(BF16), X, mods, 2, C)

        wr = jnp.pad(jnp.concatenate([router_gw[l], router_ew[l]], axis=1), ((0, 0), (0, LANES - N_GROUPS - N_EXPERTS)))
        br = jnp.pad(jnp.concatenate([router_gb[l], router_eb[l]]), (0, LANES - N_GROUPS - N_EXPERTS)).reshape(1, LANES)
        h2, rw, re = _norm_router(X, norm2_g[l], mods, 3, 4, n_ctx_tiles, wr, br)
        y0, y1 = _moe(h2.reshape(B * T, D), rw.reshape(B * T, LANES)[:, :2], re.reshape(B * T, LANES)[:, :2],
                      exp_w1, exp_w3, exp_w2, l)
        X = _gated_add(X, y0.reshape(B, T, D), y1.reshape(B, T, D), mods, 5, n_ctx_tiles)

    return _final_norm(X, final_norm_g, n_ctx_tiles)
```

```python
import functools
import math

import jax
import jax.numpy as jnp
import numpy as np
from jax import lax
from jax.experimental import pallas as pl
from jax.experimental.pallas import tpu as pltpu

F32 = jnp.float32
BF16 = jnp.bfloat16
HI = lax.Precision.HIGHEST

V7X_VMEM_BYTES = 64 * 1024 * 1024
VMEM_LIMIT = V7X_VMEM_BYTES - 8 * 1024 * 1024
LANES = 128
SUBLANES = 8

GRID_W = 64
ROPE_THETA = 10000.0
NORM_EPS = 1e-6
GN_EPS = 64e-5
DECAY_SCALE = math.exp(-0.5)
LOG2E = math.log2(math.e)

RWKV_HEAD = 64
RWKV_W = 512
DECAY_RANK = 64
ICLR_RANK = 64
GATE_RANK = 128
RWKV_COLS = 3 * RWKV_W + DECAY_RANK + ICLR_RANK + GATE_RANK
LOWRANK_OFF = 3 * RWKV_W
GATE_OFF = LOWRANK_OFF + DECAY_RANK + ICLR_RANK
CHUNK = 64

GQA_HEAD = 128
GQA_Q_HEADS = 8
GQA_KV_HEADS = 2
GQA_Q_COLS = GQA_Q_HEADS * GQA_HEAD
GQA_KV_COLS = GQA_KV_HEADS * GQA_HEAD
GQA_COLS = GQA_Q_COLS + 2 * GQA_KV_COLS

MLA_HEADS = 4
MLA_NOPE = 128
MLA_ROPE = 64
MLA_V = 128
MLA_Q_RANK = 384
MLA_KV_RANK = 256
MLA_COLS = MLA_Q_RANK + MLA_KV_RANK + MLA_ROPE
MLA_COLS_PAD = 768
MLA_DK = 2 * LANES

N_GROUPS = 4
EXPERTS_PER_GROUP = 8
N_EXPERTS = 32
D_EXPERT = 256
MOE_TM = 256

TOK_TILE = 256
N_MOD = 6
CTX_ROW = 4


def _cp(*sem):
    return pltpu.CompilerParams(dimension_semantics=sem, vmem_limit_bytes=VMEM_LIMIT)


def _sigmoid(x):
    return 1.0 / (1.0 + jnp.exp(-x))


def _bf16_terms(x, n):
    terms = []
    for _ in range(n):
        t = x.astype(BF16)
        terms.append(t)
        x = x - t.astype(F32)
    return terms


def _dot_terms(x, w, nx, nw):
    xs = _bf16_terms(x, nx)
    ws = _bf16_terms(w, nw)
    acc = None
    for i in range(nx):
        for j in range(nw):
            if i + j < max(nx, nw):
                p = jnp.dot(xs[i], ws[j], preferred_element_type=F32)
                acc = p if acc is None else acc + p
    return acc


def _mod_body(c_ref, w_ref, b_ref, o_ref):
    c = c_ref[...]
    s = c * _sigmoid(c)
    o_ref[...] = jnp.dot(s, w_ref[...], preferred_element_type=F32, precision=HI) + b_ref[...]


def _modulation(cc, mod_w, mod_b):
    L, D, N = mod_w.shape
    tn = 1024
    return pl.pallas_call(
        _mod_body,
        grid=(L, N // tn),
        in_specs=[pl.BlockSpec((SUBLANES, D), lambda l, j: (0, 0)),
                  pl.BlockSpec((None, D, tn), lambda l, j: (l, 0, j)),
                  pl.BlockSpec((None, 1, tn), lambda l, j: (l, 0, j))],
        out_specs=pl.BlockSpec((None, SUBLANES, tn), lambda l, j: (l, 0, j)),
        out_shape=jax.ShapeDtypeStruct((L, SUBLANES, N), F32),
        compiler_params=_cp("parallel", "parallel"),
        name="modulation",
    )(cc, mod_w, mod_b.reshape(L, 1, N))


def _mod_spec(which, n_ctx_tiles, D):
    return pl.BlockSpec((None, 1, D), lambda b, i: (jnp.where(i < n_ctx_tiles, CTX_ROW, b) * N_MOD + which, 0, 0))


def _rms(x, g):
    return x * lax.rsqrt(jnp.mean(x * x, axis=-1, keepdims=True) + NORM_EPS) * g


def _norm_mod_body(x_ref, g_ref, sh_ref, sc_ref, o_ref):
    y = _rms(x_ref[...], g_ref[...])
    o_ref[...] = (y * (1.0 + sc_ref[...]) + sh_ref[...]).astype(o_ref.dtype)


def _norm_body(x_ref, g_ref, o_ref):
    o_ref[...] = _rms(x_ref[...], g_ref[...]).astype(o_ref.dtype)


def _route(logits):
    lane = lax.broadcasted_iota(jnp.int32, logits.shape, 1)
    lane_f = lane.astype(F32)
    neg = jnp.float32(-1e30)
    far = jnp.float32(1e9)
    first_at = lambda hit: jnp.min(jnp.where(hit, lane_f, far), axis=-1, keepdims=True).astype(jnp.int32)
    gl = jnp.where(lane < N_GROUPS, logits, neg)
    gmax = jnp.max(gl, axis=-1, keepdims=True)
    gidx = first_at(gl == gmax)
    p_sel = 1.0 / jnp.sum(jnp.exp(gl - gmax), axis=-1, keepdims=True)
    lo = N_GROUPS + gidx * EXPERTS_PER_GROUP
    el = jnp.where((lane >= lo) & (lane < lo + EXPERTS_PER_GROUP), logits, neg)
    m1 = jnp.max(el, axis=-1, keepdims=True)
    i1 = first_at(el == m1)
    el2 = jnp.where(lane == i1, neg, el)
    m2 = jnp.max(el2, axis=-1, keepdims=True)
    i2 = first_at(el2 == m2)
    t = jnp.exp(m2 - m1)
    w1 = p_sel / (1.0 + t)
    w2 = p_sel * t / (1.0 + t)
    rw = jnp.where(lane == 0, w1, jnp.where(lane == 1, w2, 0.0))
    re = jnp.where(lane == 0, i1 - N_GROUPS, jnp.where(lane == 1, i2 - N_GROUPS, 0))
    return rw, re


def _norm_router_body(x_ref, g_ref, sh_ref, sc_ref, wr_ref, br_ref, h_ref, rw_ref, re_ref):
    y = _rms(x_ref[...], g_ref[...])
    h = y * (1.0 + sc_ref[...]) + sh_ref[...]
    h_ref[...] = h.astype(h_ref.dtype)
    logits = _dot_terms(h, wr_ref[...], 2, 2) + br_ref[...]
    rw, re = _route(logits)
    rw_ref[...] = rw
    re_ref[...] = re


def _norm_mod(X, g, mods, shift_i, scale_i, n_ctx_tiles):
    B, T, D = X.shape
    tt = TOK_TILE
    return pl.pallas_call(
        _norm_mod_body,
        grid=(B, T // tt),
        in_specs=[pl.BlockSpec((None, tt, D), lambda b, i: (b, i, 0)),
                  pl.BlockSpec((1, D), lambda b, i: (0, 0)),
                  _mod_spec(shift_i, n_ctx_tiles, D),
                  _mod_spec(scale_i, n_ctx_tiles, D)],
        out_specs=pl.BlockSpec((None, tt, D), lambda b, i: (b, i, 0)),
        out_shape=jax.ShapeDtypeStruct((B, T, D), BF16),
        compiler_params=_cp("parallel", "parallel"),
        name="norm_mod",
    )(X, g.reshape(1, D), mods, mods)


def _norm_router(X, g, mods, shift_i, scale_i, n_ctx_tiles, wr, br):
    B, T, D = X.shape
    tt = TOK_TILE
    tok = lambda w, dt: (pl.BlockSpec((None, tt, w), lambda b, i: (b, i, 0)), jax.ShapeDtypeStruct((B, T, w), dt))
    outs = [tok(D, F32), tok(LANES, F32), tok(LANES, jnp.int32)]
    return pl.pallas_call(
        _norm_router_body,
        grid=(B, T // tt),
        in_specs=[pl.BlockSpec((None, tt, D), lambda b, i: (b, i, 0)),
                  pl.BlockSpec((1, D), lambda b, i: (0, 0)),
                  _mod_spec(shift_i, n_ctx_tiles, D),
                  _mod_spec(scale_i, n_ctx_tiles, D),
                  pl.BlockSpec((D, LANES), lambda b, i: (0, 0)),
                  pl.BlockSpec((1, LANES), lambda b, i: (0, 0))],
        out_specs=[o[0] for o in outs],
        out_shape=[o[1] for o in outs],
        compiler_params=_cp("parallel", "parallel"),
        name="norm_router",
    )(X, g.reshape(1, D), mods, mods, wr, br)


def _final_norm(X, g, n_ctx_tiles):
    B, T, D = X.shape
    tt = TOK_TILE
    S = T - n_ctx_tiles * tt
    return pl.pallas_call(
        _norm_body,
        grid=(B, S // tt),
        in_specs=[pl.BlockSpec((None, tt, D), lambda b, i: (b, i + n_ctx_tiles, 0)),
                  pl.BlockSpec((1, D), lambda b, i: (0, 0))],
        out_specs=pl.BlockSpec((None, tt, D), lambda b, i: (b, i, 0)),
        out_shape=jax.ShapeDtypeStruct((B, S, D), F32),
        compiler_params=_cp("parallel", "parallel"),
        name="final_norm",
    )(X, g.reshape(1, D))


def _mm_body(a_ref, w_ref, o_ref):
    o_ref[...] = jnp.dot(a_ref[...], w_ref[...], preferred_element_type=F32).astype(o_ref.dtype)


def _matmul(a, w, tm=512):
    M, K = a.shape
    N = w.shape[1]
    return pl.pallas_call(
        _mm_body,
        grid=(M // tm,),
        in_specs=[pl.BlockSpec((tm, K), lambda i: (i, 0)),
                  pl.BlockSpec((K, N), lambda i: (0, 0))],
        out_specs=pl.BlockSpec((tm, N), lambda i: (i, 0)),
        out_shape=jax.ShapeDtypeStruct((M, N), F32),
        compiler_params=_cp("parallel"),
        name="token_matmul",
    )(a, w)


def _mm_res_body(a_ref, w_ref, x_ref, gl_ref, gc_ref, o_ref, *, n_ctx, tm):
    acc = jnp.dot(a_ref[...], w_ref[...], preferred_element_type=F32)
    row = pl.program_id(1) * tm + lax.broadcasted_iota(jnp.int32, (tm, 1), 0)
    gate = jnp.where(row < n_ctx, gc_ref[...], gl_ref[...])
    o_ref[...] = x_ref[...] + gate * acc


def _matmul_gated_residual(a, w, X, mods, gate_i, n_ctx, tm=384):
    B, T, K = a.shape
    D = w.shape[1]
    assert T % tm == 0
    return pl.pallas_call(
        functools.partial(_mm_res_body, n_ctx=n_ctx, tm=tm),
        grid=(B, T // tm),
        in_specs=[pl.BlockSpec((None, tm, K), lambda b, i: (b, i, 0)),
                  pl.BlockSpec((K, D), lambda b, i: (0, 0)),
                  pl.BlockSpec((None, tm, D), lambda b, i: (b, i, 0)),
                  pl.BlockSpec((None, 1, D), lambda b, i: (b * N_MOD + gate_i, 0, 0)),
                  pl.BlockSpec((None, 1, D), lambda b, i: (CTX_ROW * N_MOD + gate_i, 0, 0))],
        out_specs=pl.BlockSpec((None, tm, D), lambda b, i: (b, i, 0)),
        out_shape=jax.ShapeDtypeStruct((B, T, D), F32),
        compiler_params=_cp("parallel", "parallel"),
        name="out_proj_residual",
    )(a, w, X, mods, mods)


def _gated_add_body(x_ref, y0_ref, y1_ref, rw_ref, g_ref, o_ref):
    rw = rw_ref[...]
    moe = rw[:, 0:1] * y0_ref[...] + rw[:, 1:2] * y1_ref[...]
    o_ref[...] = x_ref[...] + g_ref[...] * moe


def _gated_add(X, Y0, Y1, rw, mods, gate_i, n_ctx_tiles):
    B, T, D = X.shape
    tt = TOK_TILE
    blk = pl.BlockSpec((None, tt, D), lambda b, i: (b, i, 0))
    return pl.pallas_call(
        _gated_add_body,
        grid=(B, T // tt),
        in_specs=[blk, blk, blk, pl.BlockSpec((None, tt, LANES), lambda b, i: (b, i, 0)),
                  _mod_spec(gate_i, n_ctx_tiles, D)],
        out_specs=blk,
        out_shape=jax.ShapeDtypeStruct((B, T, D), F32),
        compiler_params=_cp("parallel", "parallel"),
        name="gated_add",
    )(X, Y0, Y1, rw, mods)


def _rwkv_prep_body(p_ref, pv_ref, nx_ref, mup_ref, mun_ref, w0_ref, wup_ref, a0_ref, aup_ref, gup_ref,
                    kk_ref, ka_ref, rk_ref, e_ref,
                    r_o, v_o, kh_o, lw_o, b_o, kt_o, g_o, bon_o, *, n_tiles, tt):
    i = pl.program_id(1)
    p = p_ref[...]
    seq_first = i <= 1
    seq_last = (i == 0) | (i == n_tiles - 1)
    prow = jnp.where(seq_first, 0.0, pv_ref[SUBLANES - 1:SUBLANES, :])
    nrow = jnp.where(seq_last, 0.0, nx_ref[0:1, :])
    rid = lax.broadcasted_iota(jnp.int32, (tt, 1), 0)
    prev = jnp.where(rid == 0, prow, pltpu.roll(p, 1, 0))
    nxt = jnp.where(rid == tt - 1, nrow, pltpu.roll(p, tt - 1, 0))
    z = p + mup_ref[...] * (prev - p) + mun_ref[...] * (nxt - p)

    W = RWKV_W
    r = z[:, 0:W]
    k = z[:, W:2 * W]
    v = z[:, 2 * W:3 * W]
    lowrank = z[:, LOWRANK_OFF:LOWRANK_OFF + LANES]
    gd = z[:, GATE_OFF:GATE_OFF + GATE_RANK]
    head_sum = e_ref[...]

    kap = k * kk_ref[...]
    ss = _dot_terms(kap * kap, head_sum, 2, 1)
    khat = kap * lax.rsqrt(ss + 1e-12)
    wd_t = jnp.tanh(lowrank)
    g_o[...] = _dot_terms(_sigmoid(gd), gup_ref[...], 1, 1)
    r_o[...] = r
    v_o[...] = v
    kh_o[...] = khat
    kt_sum = None
    for d in range(2):
        dec = _dot_terms(wd_t, wup_ref[d], 2, 2)
        lw_o[d] = -DECAY_SCALE * _sigmoid(w0_ref[d:d + 1, :] + dec)
        a = _sigmoid(a0_ref[d:d + 1, :] + _dot_terms(lowrank, aup_ref[d], 1, 1))
        kt = k * (1.0 + (a - 1.0) * ka_ref[...])
        kt_o[d] = kt
        b_o[d] = a * khat
        kt_sum = kt if kt_sum is None else kt_sum + kt
    bsum = _dot_terms(r * kt_sum * rk_ref[...], head_sum, 2, 1)
    bon_o[...] = bsum * v


def _rwkv_prep(pr, mu_prev, mu_next, w0, wup_pad, a0, aup_pad, g_up, k_k, k_a, r_k, head_sum):
    B, T, _ = pr.shape
    tt = TOK_TILE
    W = RWKV_W
    n_tiles = T // tt
    n8 = tt // SUBLANES
    row = lambda v: v.reshape(1, -1)
    full = lambda a: pl.BlockSpec(a.shape, lambda b, i: (0,) * a.ndim)
    tok = pl.BlockSpec((None, tt, W), lambda b, i: (b, i, 0))
    tok2 = pl.BlockSpec((None, 2, tt, W), lambda b, i: (b, 0, i, 0))
    s1 = jax.ShapeDtypeStruct((B, T, W), F32)
    s2 = jax.ShapeDtypeStruct((B, 2, T, W), F32)
    consts = [row(mu_prev), row(mu_next), w0, wup_pad, a0, aup_pad, g_up, row(k_k), row(k_a), row(r_k), head_sum]
    return pl.pallas_call(
        functools.partial(_rwkv_prep_body, n_tiles=n_tiles, tt=tt),
        grid=(B, n_tiles),
        in_specs=[pl.BlockSpec((None, tt, RWKV_COLS), lambda b, i: (b, i, 0)),
                  pl.BlockSpec((None, SUBLANES, RWKV_COLS), lambda b, i: (b, jnp.maximum(i * n8 - 1, 0), 0)),
                  pl.BlockSpec((None, SUBLANES, RWKV_COLS), lambda b, i: (b, jnp.minimum((i + 1) * n8, T // SUBLANES - 1), 0)),
                  ] + [full(a) for a in consts],
        out_specs=[tok, tok, tok, tok2, tok2, tok2, tok, tok],
        out_shape=[s1, s1, s1, s2, s2, s2, s1, s1],
        compiler_params=_cp("parallel", "parallel"),
        name="rwkv_prep",
    )(pr, pr, pr, *consts)


def _stack_heads(x):
    lane = lax.broadcasted_iota(jnp.int32, x.shape, 1)
    first = lane < RWKV_HEAD
    return jnp.concatenate([jnp.where(first, x, 0.0), jnp.where(first, 0.0, x)], axis=0)


def _dot(a, b):
    return jnp.dot(a.astype(BF16), b.astype(BF16), preferred_element_type=F32)


def _dot_nt(a, b):
    return lax.dot_general(a.astype(BF16), b.astype(BF16), (((1,), (1,)), ((), ())), preferred_element_type=F32)


def _dot_tn(a, b):
    return jnp.dot(a.T.astype(BF16), b.astype(BF16), preferred_element_type=F32)


def _chunk_operands(r, v, kh, lw, b, kt, reverse):
    L = CHUNK
    ti = lax.broadcasted_iota(jnp.int32, (L, L), 0)
    tj = lax.broadcasted_iota(jnp.int32, (L, L), 1)
    tri = jnp.where((ti <= tj) if reverse else (ti >= tj), 1.0, 0.0)
    lam = _dot_terms(tri, lw, 1, 3)
    tot = lam[0:1, :] if reverse else lam[L - 1:L, :]
    e_n = jnp.exp(-lam)
    e_g = jnp.exp(tot - lam)
    full = dict(A=kh * jnp.exp(lam - lw), R=r * jnp.exp(lam), Kn=kt * e_n, Bn=b * e_n, Kg=kt * e_g, Bg=b * e_g, V=v)
    e_tot = jnp.exp(tot)
    pairs = []
    for p in range(RWKV_W // LANES):
        sl = slice(p * LANES, (p + 1) * LANES)
        ops = {k: _stack_heads(a[:, sl]) for k, a in full.items()}
        ops["e_tot"] = e_tot[:, sl]
        ops["reverse"] = reverse
        pairs.append(ops)
    return pairs


def _chunk_masks(reverse):
    L = CHUNK
    si = lax.broadcasted_iota(jnp.int32, (2 * L, 2 * L), 0)
    sj = lax.broadcasted_iota(jnp.int32, (2 * L, 2 * L), 1)
    same = (si >= L) == (sj >= L)
    before = (si < sj) if reverse else (si > sj)
    return same & before, same & (before | (si == sj)), si == sj


def _chunks_solve(chains, states):
    L = CHUNK
    P2 = 2 * L
    n = len(chains)
    masks = {rev: _chunk_masks(rev) for rev in {c["reverse"] for c in chains}}
    strict = [masks[c["reverse"]][0] for c in chains]
    incl = [masks[c["reverse"]][1] for c in chains]
    eye = masks[chains[0]["reverse"]][2]

    big = [_dot_nt(jnp.concatenate([c["A"], c["R"]], axis=0), jnp.concatenate([c["Bn"], c["Kn"]], axis=0)) for c in chains]
    Mb = [jnp.where(strict[i], big[i][0:P2, 0:P2], 0.0) for i in range(n)]
    Mkv = [jnp.where(strict[i], big[i][0:P2, P2:2 * P2], 0.0) for i in range(n)]
    Pb = [jnp.where(incl[i], big[i][P2:2 * P2, 0:P2], 0.0) for i in range(n)]
    Pkv = [jnp.where(incl[i], big[i][P2:2 * P2, P2:2 * P2], 0.0) for i in range(n)]

    Pw = [-m for m in Mb]
    Tm = [jnp.where(eye, 1.0, 0.0) + p for p in Pw]
    for _ in range(int(math.log2(L)) - 1):
        Pw = [_dot(p, p) for p in Pw]
        Tm = [t + _dot(t, p) for t, p in zip(Tm, Pw)]

    MV = [_dot(Mkv[i], chains[i]["V"]) for i in range(n)]
    TAM = [_dot(Tm[i], jnp.concatenate([chains[i]["A"], MV[i]], axis=1)) for i in range(n)]
    PB = [_dot(Pb[i], TAM[i]) for i in range(n)]
    PV = [_dot(Pkv[i], chains[i]["V"]) for i in range(n)]
    BG = [_dot_tn(chains[i]["Bg"], TAM[i]) for i in range(n)]
    KV = [_dot_tn(chains[i]["Kg"], chains[i]["V"]) for i in range(n)]
    ys, new_states = [], []
    for i in range(n):
        RA = chains[i]["R"] - PB[i][:, 0:P2]
        G2 = jnp.where(eye, chains[i]["e_tot"], 0.0) - BG[i][:, 0:P2]
        out = _dot(jnp.concatenate([RA, G2], axis=0), states[i])
        Ys = out[0:P2] + PV[i] - PB[i][:, P2:2 * P2]
        ys.append(Ys[0:L] + Ys[L:P2])
        new_states.append(out[P2:2 * P2] + KV[i] - BG[i][:, P2:2 * P2])
    return ys, new_states


def _rwkv_scan_body(rf_ref, vf_ref, khf_ref, rb_ref, vb_ref, khb_ref, lwf_ref, bf_ref, ktf_ref, lwb_ref, bb_ref, ktb_ref,
                    yf_ref, yb_ref, s_ref):
    @pl.when(pl.program_id(1) == 0)
    def _():
        s_ref[...] = jnp.zeros_like(s_ref)

    n_pairs = RWKV_W // LANES
    chains = (_chunk_operands(rf_ref[...], vf_ref[...], khf_ref[...], lwf_ref[...], bf_ref[...], ktf_ref[...], False)
              + _chunk_operands(rb_ref[...], vb_ref[...], khb_ref[...], lwb_ref[...], bb_ref[...], ktb_ref[...], True))
    ys, new_states = _chunks_solve(chains, [s_ref[i] for i in range(2 * n_pairs)])
    for i in range(2 * n_pairs):
        s_ref[i] = new_states[i]
    yf_ref[...] = jnp.concatenate(ys[:n_pairs], axis=1)
    yb_ref[...] = jnp.concatenate(ys[n_pairs:], axis=1)


def _rwkv_scan(r, v, kh, lw, b, kt, n_ctx):
    B, T, W = r.shape
    nc = T // CHUNK
    ncc = n_ctx // CHUNK
    rev = lambda j: jnp.where(j < ncc, ncc - 1 - j, nc + ncc - 1 - j)
    fwd1 = pl.BlockSpec((None, CHUNK, W), lambda bb, j: (bb, j, 0))
    bwd1 = pl.BlockSpec((None, CHUNK, W), lambda bb, j: (bb, rev(j), 0))
    fwd2 = pl.BlockSpec((None, None, CHUNK, W), lambda bb, j: (bb, 0, j, 0))
    bwd2 = pl.BlockSpec((None, None, CHUNK, W), lambda bb, j: (bb, 1, rev(j), 0))
    out = jax.ShapeDtypeStruct((B, T, W), F32)
    return pl.pallas_call(
        _rwkv_scan_body,
        grid=(B, nc),
        in_specs=[fwd1, fwd1, fwd1, bwd1, bwd1, bwd1, fwd2, fwd2, fwd2, bwd2, bwd2, bwd2],
        out_specs=[fwd1, bwd1],
        out_shape=[out, out],
        scratch_shapes=[pltpu.VMEM((2 * W // LANES, LANES, LANES), F32)],
        compiler_params=_cp("parallel", "arbitrary"),
        name="rwkv_scan",
    )(r, v, kh, r, v, kh, lw, b, kt, lw, b, kt)


def _rwkv_readout_body(yf_ref, yb_ref, bon_ref, g_ref, gng_ref, gnb_ref, e_ref, o_ref):
    y = yf_ref[...] + yb_ref[...]
    head_mean = e_ref[...] * (1.0 / RWKV_HEAD)
    mu = _dot_terms(y, head_mean, 2, 1)
    yc = y - mu
    var = _dot_terms(yc * yc, head_mean, 2, 1)
    yn = yc * lax.rsqrt(var + GN_EPS) * gng_ref[...] + gnb_ref[...]
    o_ref[...] = ((yn + bon_ref[...]) * g_ref[...]).astype(o_ref.dtype)


def _rwkv_readout(yf, yb, bonus, g, gn_g, gn_b, head_sum):
    B, T, W = yf.shape
    tt = TOK_TILE
    tok = pl.BlockSpec((None, tt, W), lambda b, i: (b, i, 0))
    row = pl.BlockSpec((1, W), lambda b, i: (0, 0))
    return pl.pallas_call(
        _rwkv_readout_body,
        grid=(B, T // tt),
        in_specs=[tok, tok, tok, tok, row, row, pl.BlockSpec((W, W), lambda b, i: (0, 0))],
        out_specs=tok,
        out_shape=jax.ShapeDtypeStruct((B, T, W), BF16),
        compiler_params=_cp("parallel", "parallel"),
        name="rwkv_readout",
    )(yf, yb, bonus, g, gn_g.reshape(1, W), gn_b.reshape(1, W), head_sum)


def _rope(y, cos, sin_signed, quarter):
    lane = lax.broadcasted_iota(jnp.int32, y.shape, 1)
    first = (lane & (2 * quarter - 1)) < quarter
    partner = jnp.where(first, pltpu.roll(y, LANES - quarter, 1), pltpu.roll(y, quarter, 1))
    return y * cos + partner * sin_signed


def _gqa_prep_body(p_ref, qg_ref, kg_ref, cos_ref, sin_ref, q_o, k_o, v_o):
    cos = cos_ref[...]
    sin = sin_ref[...]
    scale = GQA_HEAD ** -0.5 * LOG2E
    for h in range(GQA_Q_HEADS):
        sl = slice(h * GQA_HEAD, (h + 1) * GQA_HEAD)
        q = _rms(p_ref[:, sl], qg_ref[...])
        q_o[:, sl] = (_rope(q, cos, sin, GQA_HEAD // 4) * scale).astype(q_o.dtype)
    for h in range(GQA_KV_HEADS):
        sl = slice(h * GQA_HEAD, (h + 1) * GQA_HEAD)
        k = _rms(p_ref[:, GQA_Q_COLS + h * GQA_HEAD:GQA_Q_COLS + (h + 1) * GQA_HEAD], kg_ref[...])
        k_o[:, sl] = _rope(k, cos, sin, GQA_HEAD // 4).astype(k_o.dtype)
    v_o[...] = p_ref[:, GQA_Q_COLS + GQA_KV_COLS:GQA_COLS].astype(v_o.dtype)


def _gqa_prep(pg, q_norm_g, k_norm_g, cos, sin):
    B, T, _ = pg.shape
    tt = TOK_TILE
    tok = lambda w: pl.BlockSpec((None, tt, w), lambda b, i: (b, i, 0))
    row = pl.BlockSpec((1, GQA_HEAD), lambda b, i: (0, 0))
    tab = pl.BlockSpec((tt, LANES), lambda b, i: (i, 0))
    return pl.pallas_call(
        _gqa_prep_body,
        grid=(B, T // tt),
        in_specs=[tok(GQA_COLS), row, row, tab, tab],
        out_specs=[tok(GQA_Q_COLS), tok(GQA_KV_COLS), tok(GQA_KV_COLS)],
        out_shape=[jax.ShapeDtypeStruct((B, T, GQA_Q_COLS), BF16),
                   jax.ShapeDtypeStruct((B, T, GQA_KV_COLS), BF16),
                   jax.ShapeDtypeStruct((B, T, GQA_KV_COLS), BF16)],
        compiler_params=_cp("parallel", "parallel"),
        name="gqa_prep",
    )(pg, q_norm_g.reshape(1, -1), k_norm_g.reshape(1, -1), cos, sin)


def _mla_prep_body(p_ref, qg_ref, wq_ref, kvg_ref, wkv_ref, cos_ref, sin_ref, q_o, k_o, v_o):
    cos = cos_ref[...]
    sin = sin_ref[...]
    scale = (MLA_NOPE + MLA_ROPE) ** -0.5 * LOG2E
    cq = _rms(p_ref[:, 0:MLA_Q_RANK], qg_ref[...])
    q = jnp.dot(cq.astype(BF16), wq_ref[...], preferred_element_type=F32) * scale
    ckv = _rms(p_ref[:, MLA_Q_RANK:MLA_Q_RANK + MLA_KV_RANK], kvg_ref[...])
    kv = jnp.dot(ckv.astype(BF16), wkv_ref[...], preferred_element_type=F32)
    kr = _rope(p_ref[:, MLA_Q_RANK + MLA_KV_RANK:MLA_COLS_PAD], cos, sin, MLA_ROPE // 4).astype(k_o.dtype)
    for h in range(MLA_HEADS):
        lo = h * MLA_DK
        q_o[:, lo:lo + LANES] = q[:, lo:lo + LANES].astype(q_o.dtype)
        q_o[:, lo + LANES:lo + MLA_DK] = _rope(q[:, lo + LANES:lo + MLA_DK], cos, sin, MLA_ROPE // 4).astype(q_o.dtype)
        k_o[:, lo:lo + LANES] = kv[:, h * MLA_NOPE:(h + 1) * MLA_NOPE].astype(k_o.dtype)
        k_o[:, lo + LANES:lo + MLA_DK] = kr
    v_o[...] = kv[:, MLA_HEADS * MLA_NOPE:].astype(v_o.dtype)


def _mla_prep(pm, q_norm_g, wq, kv_norm_g, wkv, cos, sin):
    B, T, _ = pm.shape
    tt = TOK_TILE
    tok = lambda w: pl.BlockSpec((None, tt, w), lambda b, i: (b, i, 0))
    full = lambda a: pl.BlockSpec(a.shape, lambda b, i: (0,) * a.ndim)
    tab = pl.BlockSpec((tt, LANES), lambda b, i: (i, 0))
    qg = q_norm_g.reshape(1, -1)
    kvg = kv_norm_g.reshape(1, -1)
    return pl.pallas_call(
        _mla_prep_body,
        grid=(B, T // tt),
        in_specs=[tok(MLA_COLS_PAD), full(qg), full(wq), full(kvg), full(wkv), tab, tab],
        out_specs=[tok(MLA_HEADS * MLA_DK), tok(MLA_HEADS * MLA_DK), tok(MLA_HEADS * MLA_V)],
        out_shape=[jax.ShapeDtypeStruct((B, T, MLA_HEADS * MLA_DK), BF16),
                   jax.ShapeDtypeStruct((B, T, MLA_HEADS * MLA_DK), BF16),
                   jax.ShapeDtypeStruct((B, T, MLA_HEADS * MLA_V), BF16)],
        compiler_params=_cp("parallel", "parallel"),
        name="mla_prep",
    )(pm, qg, wq, kvg, wkv, cos, sin)


def _attn_body(q_ref, k_ref, v_ref, o_ref, *, hq, hkv, dk, dv, n_ctx_tiles, n_ctx, n_all):
    rep = hq // hkv
    tq = q_ref.shape[0]

    def run(nk):
        for g in range(hkv):
            kg = k_ref[0:nk, g * dk:(g + 1) * dk]
            vg = v_ref[0:nk, g * dv:(g + 1) * dv]
            v_aug = jnp.concatenate([vg, jnp.ones_like(vg)], axis=1)
            q = jnp.concatenate([q_ref[:, h * dk:(h + 1) * dk] for h in range(g * rep, (g + 1) * rep)], axis=0)
            s = _dot_nt(q, kg)
            p = jnp.exp2(s - jnp.max(s, axis=-1, keepdims=True))
            o = jnp.dot(p.astype(BF16), v_aug, preferred_element_type=F32)
            o = o[:, 0:dv] / o[:, dv:dv + 1]
            for rr in range(rep):
                h = g * rep + rr
                o_ref[:, h * dv:(h + 1) * dv] = o[rr * tq:(rr + 1) * tq].astype(o_ref.dtype)

    @pl.when(pl.program_id(1) < n_ctx_tiles)
    def _():
        run(n_ctx)

    @pl.when(pl.program_id(1) >= n_ctx_tiles)
    def _():
        run(n_all)


def _attention(q, k, v, hq, hkv, dk, dv, n_ctx):
    B, T, _ = q.shape
    tq = TOK_TILE
    return pl.pallas_call(
        functools.partial(_attn_body, hq=hq, hkv=hkv, dk=dk, dv=dv, n_ctx_tiles=n_ctx // tq, n_ctx=n_ctx, n_all=T),
        grid=(B, T // tq),
        in_specs=[pl.BlockSpec((None, tq, hq * dk), lambda b, i: (b, i, 0)),
                  pl.BlockSpec((None, T, hkv * dk), lambda b, i: (b, 0, 0)),
                  pl.BlockSpec((None, T, hkv * dv), lambda b, i: (b, 0, 0))],
        out_specs=pl.BlockSpec((None, tq, hq * dv), lambda b, i: (b, i, 0)),
        out_shape=jax.ShapeDtypeStruct((B, T, hq * dv), BF16),
        compiler_params=_cp("parallel", "parallel"),
        name="attention",
    )(q, k, v)


def _rope_tables(T, n_ctx, n_rot):
    quarter = n_rot // 4
    t = jnp.arange(T - n_ctx)
    row = (t // GRID_W).astype(F32)
    col = (t % GRID_W).astype(F32)
    inv = ROPE_THETA ** (-jnp.arange(quarter, dtype=F32) / quarter)
    ar = row[:, None] * inv[None, :]
    ac = col[:, None] * inv[None, :]
    pad = LANES - n_rot
    cos = jnp.concatenate([jnp.cos(ar), jnp.cos(ar), jnp.cos(ac), jnp.cos(ac), jnp.ones((T - n_ctx, pad), F32)], axis=1)
    sin = jnp.concatenate([-jnp.sin(ar), jnp.sin(ar), -jnp.sin(ac), jnp.sin(ac), jnp.zeros((T - n_ctx, pad), F32)], axis=1)
    cos = jnp.concatenate([jnp.ones((n_ctx, LANES), F32), cos], axis=0)
    sin = jnp.concatenate([jnp.zeros((n_ctx, LANES), F32), sin], axis=0)
    return cos, sin


def _moe_body(te_ref, nv_ref, hs_ref, w1_ref, w3_ref, w2_ref, y_ref):
    i = pl.program_id(0)

    @pl.when(i < nv_ref[0])
    def _():
        hs = hs_ref[...].astype(BF16)
        a = jnp.dot(hs, w1_ref[...].astype(BF16), preferred_element_type=F32)
        b = jnp.dot(hs, w3_ref[...].astype(BF16), preferred_element_type=F32)
        act = a * _sigmoid(a) * b
        y_ref[...] = jnp.dot(act.astype(BF16), w2_ref[...].astype(BF16), preferred_element_type=F32)

    @pl.when(i >= nv_ref[0])
    def _():
        y_ref[...] = jnp.zeros_like(y_ref)


def _moe_experts(tile_expert, n_valid, hs, w1, w3, w2, layer):
    NP, D = hs.shape
    tm = MOE_TM
    DE = w1.shape[-1]
    grid_spec = pltpu.PrefetchScalarGridSpec(
        num_scalar_prefetch=2,
        grid=(NP // tm,),
        in_specs=[pl.BlockSpec((tm, D), lambda i, te, nv: (i, 0)),
                  pl.BlockSpec((None, None, D, DE), lambda i, te, nv: (layer, te[i], 0, 0)),
                  pl.BlockSpec((None, None, D, DE), lambda i, te, nv: (layer, te[i], 0, 0)),
                  pl.BlockSpec((None, None, DE, D), lambda i, te, nv: (layer, te[i], 0, 0))],
        out_specs=pl.BlockSpec((tm, D), lambda i, te, nv: (i, 0)),
    )
    return pl.pallas_call(
        _moe_body,
        grid_spec=grid_spec,
        out_shape=jax.ShapeDtypeStruct((NP, D), F32),
        compiler_params=_cp("arbitrary"),
        name="moe_experts",
    )(tile_expert, n_valid, hs, w1, w3, w2)


def _moe(h2, re, w1, w3, w2, layer):
    N, D = h2.shape
    tm = MOE_TM
    n_tiles = (2 * N) // tm + N_EXPERTS
    e_flat = re.reshape(2 * N)
    onehot = (e_flat[:, None] == jnp.arange(N_EXPERTS, dtype=jnp.int32)[None, :]).astype(jnp.int32)
    csum = jnp.cumsum(onehot, axis=0)
    rank = jnp.sum(onehot * (csum - 1), axis=1)
    counts = csum[-1]
    ptiles = (counts + tm - 1) // tm
    tile_end = jnp.cumsum(ptiles)
    tile_start = tile_end - ptiles
    pos = tile_start[e_flat] * tm + rank
    n_valid = tile_end[-1:].astype(jnp.int32)
    tile_ids = jnp.arange(n_tiles, dtype=jnp.int32)
    tile_expert = jnp.minimum(jnp.sum((tile_end[None, :] <= tile_ids[:, None]).astype(jnp.int32), axis=1), N_EXPERTS - 1)
    spread = jnp.arange(n_tiles * tm, dtype=jnp.int32) % N
    src = spread.at[pos].set(jnp.arange(2 * N, dtype=jnp.int32) // 2)
    rows = lambda a, idx: a.at[idx].get(mode="promise_in_bounds")
    hs = rows(h2, src)
    y = _moe_experts(tile_expert, n_valid, hs, w1, w3, w2, layer)
    pos2 = pos.reshape(N, 2)
    return rows(y, pos2[:, 0]), rows(y, pos2[:, 1])


def kernel(x, c, ctx, c_ctx, mod_w, mod_b, norm1_g, norm2_g, w_in, w_out, shift_prev, shift_next, decay_w0, decay_up, iclr_a0, iclr_up, gate_up, k_k, k_a, r_k, gn_g, gn_b, q_norm_g, k_norm_g, mla_q_norm_g, mla_w_uq, mla_kv_norm_g, mla_w_ukv, router_gw, router_gb, router_ew, router_eb, exp_w1, exp_w3, exp_w2, final_norm_g):
    B, S, D = x.shape
    C = ctx.shape[1]
    T = C + S
    depth = mod_w.shape[0]
    assert C == TOK_TILE and S % TOK_TILE == 0 and B <= CTX_ROW
    n_ctx_tiles = C // TOK_TILE

    X = jnp.concatenate([ctx, x], axis=1)
    cc = jnp.zeros((SUBLANES, D), F32).at[:B].set(c).at[CTX_ROW].set(c_ctx)
    mods_all = _modulation(cc, mod_w, mod_b).reshape(depth, SUBLANES * N_MOD, 1, D)

    cos_g, sin_g = _rope_tables(T, C, GQA_HEAD)
    cos_m, sin_m = _rope_tables(T, C, MLA_ROPE)
    hid = jnp.arange(RWKV_W) // RWKV_HEAD
    head_sum = (hid[:, None] == hid[None, :]).astype(F32)

    for l in range(depth):
        mods = mods_all[l]
        w_r = w_in[l][:, :RWKV_COLS].astype(BF16)
        w_g = w_in[l][:, RWKV_COLS:RWKV_COLS + GQA_COLS].astype(BF16)
        w_m = jnp.pad(w_in[l][:, RWKV_COLS + GQA_COLS:], ((0, 0), (0, MLA_COLS_PAD - MLA_COLS))).astype(BF16)

        h = _norm_mod(X, norm1_g[l], mods, 0, 1, n_ctx_tiles).reshape(B * T, D)
        pr = _matmul(h, w_r).reshape(B, T, RWKV_COLS)
        pg = _matmul(h, w_g).reshape(B, T, GQA_COLS)
        pm = _matmul(h, w_m).reshape(B, T, MLA_COLS_PAD)

        wup_pad = jnp.pad(decay_up[l], ((0, 0), (0, ICLR_RANK), (0, 0)))
        aup_pad = jnp.pad(iclr_up[l], ((0, 0), (DECAY_RANK, 0), (0, 0)))
        r, v, kh, lw, b, kt, g, bonus = _rwkv_prep(pr, shift_prev[l], shift_next[l], decay_w0[l], wup_pad,
                                                  iclr_a0[l], aup_pad, gate_up[l], k_k[l], k_a[l], r_k[l], head_sum)
        yf, yb = _rwkv_scan(r, v, kh, lw, b, kt, C)
        o_r = _rwkv_readout(yf, yb, bonus, g, gn_g[l], gn_b[l], head_sum)

        q, k, vv = _gqa_prep(pg, q_norm_g[l], k_norm_g[l], cos_g, sin_g)
        o_g = _attention(q, k, vv, GQA_Q_HEADS, GQA_KV_HEADS, GQA_HEAD, GQA_HEAD, C)

        wq = mla_w_uq[l].reshape(MLA_Q_RANK, MLA_HEADS, MLA_NOPE + MLA_ROPE)
        wq = jnp.pad(wq, ((0, 0), (0, 0), (0, MLA_DK - MLA_NOPE - MLA_ROPE))).reshape(MLA_Q_RANK, MLA_HEADS * MLA_DK)
        wkv = mla_w_ukv[l].reshape(MLA_KV_RANK, MLA_HEADS, MLA_NOPE + MLA_V)
        wkv = jnp.concatenate([wkv[:, :, :MLA_NOPE].reshape(MLA_KV_RANK, -1), wkv[:, :, MLA_NOPE:].reshape(MLA_KV_RANK, -1)], axis=1)
        qm, km, vm = _mla_prep(pm, mla_q_norm_g[l], wq.astype(BF16), mla_kv_norm_g[l], wkv.astype(BF16), cos_m, sin_m)
        o_m = _attention(qm, km, vm, MLA_HEADS, MLA_HEADS, MLA_DK, MLA_V, C)

        o = jnp.concatenate([o_r, o_g, o_m], axis=-1)
        X = _matmul_gated_residual(o, w_out[l].astype(BF16), X, mods, 2, C)

        wr = jnp.pad(jnp.concatenate([router_gw[l], router_ew[l]], axis=1), ((0, 0), (0, LANES - N_GROUPS - N_EXPERTS)))
        br = jnp.pad(jnp.concatenate([router_gb[l], router_eb[l]]), (0, LANES - N_GROUPS - N_EXPERTS)).reshape(1, LANES)
        h2, rw, re = _norm_router(X, norm2_g[l], mods, 3, 4, n_ctx_tiles, wr, br)
        y0, y1 = _moe(h2.reshape(B * T, D), re.reshape(B * T, LANES)[:, :2], exp_w1, exp_w3, exp_w2, l)
        X = _gated_add(X, y0.reshape(B, T, D), y1.reshape(B, T, D), rw, mods, 5, n_ctx_tiles)

    return _final_norm(X, final_norm_g, n_ctx_tiles)
```

```python
import functools
import math

import jax
import jax.numpy as jnp
import numpy as np
from jax import lax
from jax.experimental import pallas as pl
from jax.experimental.pallas import tpu as pltpu

F32 = jnp.float32
BF16 = jnp.bfloat16
HI = lax.Precision.HIGHEST

V7X_VMEM_BYTES = 64 * 1024 * 1024
VMEM_LIMIT = V7X_VMEM_BYTES - 8 * 1024 * 1024
LANES = 128
SUBLANES = 8

GRID_W = 64
ROPE_THETA = 10000.0
NORM_EPS = 1e-6
GN_EPS = 64e-5
DECAY_SCALE = math.exp(-0.5)
LOG2E = math.log2(math.e)

RWKV_HEAD = 64
RWKV_W = 512
DECAY_RANK = 64
ICLR_RANK = 64
GATE_RANK = 128
RWKV_COLS = 3 * RWKV_W + DECAY_RANK + ICLR_RANK + GATE_RANK
LOWRANK_OFF = 3 * RWKV_W
GATE_OFF = LOWRANK_OFF + DECAY_RANK + ICLR_RANK
CHUNK = 64

GQA_HEAD = 128
GQA_Q_HEADS = 8
GQA_KV_HEADS = 2
GQA_Q_COLS = GQA_Q_HEADS * GQA_HEAD
GQA_KV_COLS = GQA_KV_HEADS * GQA_HEAD
GQA_COLS = GQA_Q_COLS + 2 * GQA_KV_COLS

MLA_HEADS = 4
MLA_NOPE = 128
MLA_ROPE = 64
MLA_V = 128
MLA_Q_RANK = 384
MLA_KV_RANK = 256
MLA_COLS = MLA_Q_RANK + MLA_KV_RANK + MLA_ROPE
MLA_COLS_PAD = 768
MLA_DK = 2 * LANES

N_GROUPS = 4
EXPERTS_PER_GROUP = 8
N_EXPERTS = 32
D_EXPERT = 256
MOE_TM = 256

TOK_TILE = 256
N_MOD = 6
CTX_ROW = 4


def _cp(*sem):
    return pltpu.CompilerParams(dimension_semantics=sem, vmem_limit_bytes=VMEM_LIMIT)


def _sigmoid(x):
    return 1.0 / (1.0 + jnp.exp(-x))


def _bf16_terms(x, n):
    terms = []
    for _ in range(n):
        t = x.astype(BF16)
        terms.append(t)
        x = x - t.astype(F32)
    return terms


def _dot_terms(x, w, nx, nw):
    xs = _bf16_terms(x, nx)
    ws = _bf16_terms(w, nw)
    acc = None
    for i in range(nx):
        for j in range(nw):
            if i + j < max(nx, nw):
                p = jnp.dot(xs[i], ws[j], preferred_element_type=F32)
                acc = p if acc is None else acc + p
    return acc


def _mod_body(c_ref, w_ref, b_ref, o_ref):
    c = c_ref[...]
    s = c * _sigmoid(c)
    o_ref[...] = jnp.dot(s, w_ref[...], preferred_element_type=F32, precision=HI) + b_ref[...]


def _modulation(cc, mod_w, mod_b):
    L, D, N = mod_w.shape
    tn = 1024
    return pl.pallas_call(
        _mod_body,
        grid=(L, N // tn),
        in_specs=[pl.BlockSpec((SUBLANES, D), lambda l, j: (0, 0)),
                  pl.BlockSpec((None, D, tn), lambda l, j: (l, 0, j)),
                  pl.BlockSpec((None, 1, tn), lambda l, j: (l, 0, j))],
        out_specs=pl.BlockSpec((None, SUBLANES, tn), lambda l, j: (l, 0, j)),
        out_shape=jax.ShapeDtypeStruct((L, SUBLANES, N), F32),
        compiler_params=_cp("parallel", "parallel"),
        name="modulation",
    )(cc, mod_w, mod_b.reshape(L, 1, N))


def _mod_spec(which, n_ctx_tiles, D):
    return pl.BlockSpec((None, 1, D), lambda b, i: (jnp.where(i < n_ctx_tiles, CTX_ROW, b) * N_MOD + which, 0, 0))


def _rms(x, g):
    return x * lax.rsqrt(jnp.mean(x * x, axis=-1, keepdims=True) + NORM_EPS) * g


def _norm_mod_body(x_ref, g_ref, sh_ref, sc_ref, o_ref):
    y = _rms(x_ref[...], g_ref[...])
    o_ref[...] = (y * (1.0 + sc_ref[...]) + sh_ref[...]).astype(o_ref.dtype)


def _route(logits):
    lane = lax.broadcasted_iota(jnp.int32, logits.shape, 1)
    lane_f = lane.astype(F32)
    neg = jnp.float32(-1e30)
    far = jnp.float32(1e9)
    first_at = lambda hit: jnp.min(jnp.where(hit, lane_f, far), axis=-1, keepdims=True).astype(jnp.int32)
    gl = jnp.where(lane < N_GROUPS, logits, neg)
    gmax = jnp.max(gl, axis=-1, keepdims=True)
    gidx = first_at(gl == gmax)
    p_sel = 1.0 / jnp.sum(jnp.exp(gl - gmax), axis=-1, keepdims=True)
    lo = N_GROUPS + gidx * EXPERTS_PER_GROUP
    el = jnp.where((lane >= lo) & (lane < lo + EXPERTS_PER_GROUP), logits, neg)
    m1 = jnp.max(el, axis=-1, keepdims=True)
    i1 = first_at(el == m1)
    el2 = jnp.where(lane == i1, neg, el)
    m2 = jnp.max(el2, axis=-1, keepdims=True)
    i2 = first_at(el2 == m2)
    t = jnp.exp(m2 - m1)
    w1 = p_sel / (1.0 + t)
    w2 = p_sel * t / (1.0 + t)
    rw = jnp.where(lane == 0, w1, jnp.where(lane == 1, w2, 0.0))
    re = jnp.where(lane == 0, i1 - N_GROUPS, jnp.where(lane == 1, i2 - N_GROUPS, 0))
    return rw, re


def _norm_router_body(x_ref, g_ref, sh_ref, sc_ref, wr_ref, br_ref, h_ref, rw_ref, re_ref):
    y = _rms(x_ref[...], g_ref[...])
    h = y * (1.0 + sc_ref[...]) + sh_ref[...]
    h_ref[...] = h.astype(h_ref.dtype)
    logits = _dot_terms(h, wr_ref[...], 2, 2) + br_ref[...]
    rw, re = _route(logits)
    rw_ref[...] = rw
    re_ref[...] = re


def _norm_mod(X, g, mods, shift_i, scale_i, n_ctx_tiles):
    B, T, D = X.shape
    tt = TOK_TILE
    return pl.pallas_call(
        _norm_mod_body,
        grid=(B, T // tt),
        in_specs=[pl.BlockSpec((None, tt, D), lambda b, i: (b, i, 0)),
                  pl.BlockSpec((1, D), lambda b, i: (0, 0)),
                  _mod_spec(shift_i, n_ctx_tiles, D),
                  _mod_spec(scale_i, n_ctx_tiles, D)],
        out_specs=pl.BlockSpec((None, tt, D), lambda b, i: (b, i, 0)),
        out_shape=jax.ShapeDtypeStruct((B, T, D), BF16),
        compiler_params=_cp("parallel", "parallel"),
        name="norm_mod",
    )(X, g.reshape(1, D), mods, mods)


def _norm_router(X, g, mods, shift_i, scale_i, n_ctx_tiles, wr, br):
    B, T, D = X.shape
    tt = TOK_TILE
    tok = lambda w, dt: (pl.BlockSpec((None, tt, w), lambda b, i: (b, i, 0)), jax.ShapeDtypeStruct((B, T, w), dt))
    outs = [tok(D, BF16), tok(LANES, F32), tok(LANES, jnp.int32)]
    return pl.pallas_call(
        _norm_router_body,
        grid=(B, T // tt),
        in_specs=[pl.BlockSpec((None, tt, D), lambda b, i: (b, i, 0)),
                  pl.BlockSpec((1, D), lambda b, i: (0, 0)),
                  _mod_spec(shift_i, n_ctx_tiles, D),
                  _mod_spec(scale_i, n_ctx_tiles, D),
                  pl.BlockSpec((D, LANES), lambda b, i: (0, 0)),
                  pl.BlockSpec((1, LANES), lambda b, i: (0, 0))],
        out_specs=[o[0] for o in outs],
        out_shape=[o[1] for o in outs],
        compiler_params=_cp("parallel", "parallel"),
        name="norm_router",
    )(X, g.reshape(1, D), mods, mods, wr, br)


def _mm_body(a_ref, w_ref, o_ref):
    o_ref[...] = jnp.dot(a_ref[...], w_ref[...], preferred_element_type=F32).astype(o_ref.dtype)


def _matmul(a, w, tm=512):
    M, K = a.shape
    N = w.shape[1]
    return pl.pallas_call(
        _mm_body,
        grid=(M // tm,),
        in_specs=[pl.BlockSpec((tm, K), lambda i: (i, 0)),
                  pl.BlockSpec((K, N), lambda i: (0, 0))],
        out_specs=pl.BlockSpec((tm, N), lambda i: (i, 0)),
        out_shape=jax.ShapeDtypeStruct((M, N), F32),
        compiler_params=_cp("parallel"),
        name="token_matmul",
    )(a, w)


def _mm_res_body(a_ref, w_ref, x_ref, gl_ref, gc_ref, o_ref, *, n_ctx, tm):
    acc = jnp.dot(a_ref[...], w_ref[...], preferred_element_type=F32)
    row = pl.program_id(1) * tm + lax.broadcasted_iota(jnp.int32, (tm, 1), 0)
    gate = jnp.where(row < n_ctx, gc_ref[...], gl_ref[...])
    o_ref[...] = x_ref[...] + gate * acc


def _matmul_gated_residual(a, w, X, mods, gate_i, n_ctx, tm=384):
    B, T, K = a.shape
    D = w.shape[1]
    assert T % tm == 0
    return pl.pallas_call(
        functools.partial(_mm_res_body, n_ctx=n_ctx, tm=tm),
        grid=(B, T // tm),
        in_specs=[pl.BlockSpec((None, tm, K), lambda b, i: (b, i, 0)),
                  pl.BlockSpec((K, D), lambda b, i: (0, 0)),
                  pl.BlockSpec((None, tm, D), lambda b, i: (b, i, 0)),
                  pl.BlockSpec((None, 1, D), lambda b, i: (b * N_MOD + gate_i, 0, 0)),
                  pl.BlockSpec((None, 1, D), lambda b, i: (CTX_ROW * N_MOD + gate_i, 0, 0))],
        out_specs=pl.BlockSpec((None, tm, D), lambda b, i: (b, i, 0)),
        out_shape=jax.ShapeDtypeStruct((B, T, D), F32),
        compiler_params=_cp("parallel", "parallel"),
        name="out_proj_residual",
    )(a, w, X, mods, mods)


def _moe_residual(x_ref, y0_ref, y1_ref, rw_ref, gate_ref):
    rw = rw_ref[...]
    moe = rw[:, 0:1] * y0_ref[...].astype(F32) + rw[:, 1:2] * y1_ref[...].astype(F32)
    return x_ref[...] + gate_ref[...] * moe


def _moe_residual_norm_body(x_ref, y0_ref, y1_ref, rw_ref, gate_ref, g_ref, sh_ref, sc_ref, x_o, h_o):
    x = _moe_residual(x_ref, y0_ref, y1_ref, rw_ref, gate_ref)
    x_o[...] = x
    h_o[...] = (_rms(x, g_ref[...]) * (1.0 + sc_ref[...]) + sh_ref[...]).astype(h_o.dtype)


def _moe_residual_final_body(x_ref, y0_ref, y1_ref, rw_ref, gate_ref, g_ref, o_ref):
    o_ref[...] = _rms(_moe_residual(x_ref, y0_ref, y1_ref, rw_ref, gate_ref), g_ref[...])


def _moe_residual_norm(X, Y0, Y1, rw, mods, gate_i, n_ctx_tiles, g_next, mods_next):
    B, T, D = X.shape
    tt = TOK_TILE
    blk = pl.BlockSpec((None, tt, D), lambda b, i: (b, i, 0))
    return pl.pallas_call(
        _moe_residual_norm_body,
        grid=(B, T // tt),
        in_specs=[blk, blk, blk, pl.BlockSpec((None, tt, LANES), lambda b, i: (b, i, 0)),
                  _mod_spec(gate_i, n_ctx_tiles, D), pl.BlockSpec((1, D), lambda b, i: (0, 0)),
                  _mod_spec(0, n_ctx_tiles, D), _mod_spec(1, n_ctx_tiles, D)],
        out_specs=[blk, blk],
        out_shape=[jax.ShapeDtypeStruct((B, T, D), F32), jax.ShapeDtypeStruct((B, T, D), BF16)],
        compiler_params=_cp("parallel", "parallel"),
        name="moe_residual_norm",
    )(X, Y0, Y1, rw, mods, g_next.reshape(1, D), mods_next, mods_next)


def _moe_residual_final(X, Y0, Y1, rw, mods, gate_i, n_ctx_tiles, g_final):
    B, T, D = X.shape
    tt = TOK_TILE
    S = T - n_ctx_tiles * tt
    lat = lambda w: pl.BlockSpec((None, tt, w), lambda b, i: (b, i + n_ctx_tiles, 0))
    return pl.pallas_call(
        _moe_residual_final_body,
        grid=(B, S // tt),
        in_specs=[lat(D), lat(D), lat(D), lat(LANES),
                  pl.BlockSpec((None, 1, D), lambda b, i: (b * N_MOD + gate_i, 0, 0)),
                  pl.BlockSpec((1, D), lambda b, i: (0, 0))],
        out_specs=pl.BlockSpec((None, tt, D), lambda b, i: (b, i, 0)),
        out_shape=jax.ShapeDtypeStruct((B, S, D), F32),
        compiler_params=_cp("parallel", "parallel"),
        name="moe_residual_final",
    )(X, Y0, Y1, rw, mods, g_final.reshape(1, D))


def _rwkv_prep_body(p_ref, pv_ref, nx_ref, mup_ref, mun_ref, w0_ref, wup_ref, a0_ref, aup_ref, gup_ref,
                    kk_ref, ka_ref, rk_ref, e_ref,
                    r_o, v_o, kh_o, lw_o, b_o, kt_o, g_o, bon_o, *, n_tiles, tt):
    i = pl.program_id(1)
    p = p_ref[...]
    seq_first = i <= 1
    seq_last = (i == 0) | (i == n_tiles - 1)
    prow = jnp.where(seq_first, 0.0, pv_ref[SUBLANES - 1:SUBLANES, :])
    nrow = jnp.where(seq_last, 0.0, nx_ref[0:1, :])
    rid = lax.broadcasted_iota(jnp.int32, (tt, 1), 0)
    prev = jnp.where(rid == 0, prow, pltpu.roll(p, 1, 0))
    nxt = jnp.where(rid == tt - 1, nrow, pltpu.roll(p, tt - 1, 0))
    z = p + mup_ref[...] * (prev - p) + mun_ref[...] * (nxt - p)

    W = RWKV_W
    r = z[:, 0:W]
    k = z[:, W:2 * W]
    v = z[:, 2 * W:3 * W]
    lowrank = z[:, LOWRANK_OFF:LOWRANK_OFF + LANES]
    gd = z[:, GATE_OFF:GATE_OFF + GATE_RANK]
    head_sum = e_ref[...]

    kap = k * kk_ref[...]
    ss = _dot_terms(kap * kap, head_sum, 2, 1)
    khat = kap * lax.rsqrt(ss + 1e-12)
    wd_t = jnp.tanh(lowrank)
    g_o[...] = _dot_terms(_sigmoid(gd), gup_ref[...], 1, 1)
    r_o[...] = r
    v_o[...] = v
    kh_o[...] = khat
    kt_sum = None
    for d in range(2):
        dec = _dot_terms(wd_t, wup_ref[d], 2, 2)
        lw_o[d] = -DECAY_SCALE * _sigmoid(w0_ref[d:d + 1, :] + dec)
        a = _sigmoid(a0_ref[d:d + 1, :] + _dot_terms(lowrank, aup_ref[d], 1, 1))
        kt = k * (1.0 + (a - 1.0) * ka_ref[...])
        kt_o[d] = kt
        b_o[d] = a * khat
        kt_sum = kt if kt_sum is None else kt_sum + kt
    bsum = _dot_terms(r * kt_sum * rk_ref[...], head_sum, 2, 1)
    bon_o[...] = bsum * v


def _rwkv_prep(pr, mu_prev, mu_next, w0, wup_pad, a0, aup_pad, g_up, k_k, k_a, r_k, head_sum):
    B, T, _ = pr.shape
    tt = TOK_TILE
    W = RWKV_W
    n_tiles = T // tt
    n8 = tt // SUBLANES
    row = lambda v: v.reshape(1, -1)
    full = lambda a: pl.BlockSpec(a.shape, lambda b, i: (0,) * a.ndim)
    tok = pl.BlockSpec((None, tt, W), lambda b, i: (b, i, 0))
    tok2 = pl.BlockSpec((None, 2, tt, W), lambda b, i: (b, 0, i, 0))
    s1 = jax.ShapeDtypeStruct((B, T, W), F32)
    s2 = jax.ShapeDtypeStruct((B, 2, T, W), F32)
    consts = [row(mu_prev), row(mu_next), w0, wup_pad, a0, aup_pad, g_up, row(k_k), row(k_a), row(r_k), head_sum]
    return pl.pallas_call(
        functools.partial(_rwkv_prep_body, n_tiles=n_tiles, tt=tt),
        grid=(B, n_tiles),
        in_specs=[pl.BlockSpec((None, tt, RWKV_COLS), lambda b, i: (b, i, 0)),
                  pl.BlockSpec((None, SUBLANES, RWKV_COLS), lambda b, i: (b, jnp.maximum(i * n8 - 1, 0), 0)),
                  pl.BlockSpec((None, SUBLANES, RWKV_COLS), lambda b, i: (b, jnp.minimum((i + 1) * n8, T // SUBLANES - 1), 0)),
                  ] + [full(a) for a in consts],
        out_specs=[tok, tok, tok, tok2, tok2, tok2, tok, tok],
        out_shape=[s1, s1, s1, s2, s2, s2, s1, s1],
        compiler_params=_cp("parallel", "parallel"),
        name="rwkv_prep",
    )(pr, pr, pr, *consts)


def _stack_heads(x):
    lane = lax.broadcasted_iota(jnp.int32, x.shape, 1)
    first = lane < RWKV_HEAD
    return jnp.concatenate([jnp.where(first, x, 0.0), jnp.where(first, 0.0, x)], axis=0)


def _dot(a, b):
    return jnp.dot(a.astype(BF16), b.astype(BF16), preferred_element_type=F32)


def _dot_nt(a, b):
    return lax.dot_general(a.astype(BF16), b.astype(BF16), (((1,), (1,)), ((), ())), preferred_element_type=F32)


def _dot_tn(a, b):
    return jnp.dot(a.T.astype(BF16), b.astype(BF16), preferred_element_type=F32)


def _chunk_operands(r, v, kh, lw, b, kt, reverse):
    L = CHUNK
    ti = lax.broadcasted_iota(jnp.int32, (L, L), 0)
    tj = lax.broadcasted_iota(jnp.int32, (L, L), 1)
    tri = jnp.where((ti <= tj) if reverse else (ti >= tj), 1.0, 0.0)
    lam = _dot_terms(tri, lw, 1, 3)
    tot = lam[0:1, :] if reverse else lam[L - 1:L, :]
    e_n = jnp.exp(-lam)
    e_g = jnp.exp(tot - lam)
    full = dict(A=kh * jnp.exp(lam - lw), R=r * jnp.exp(lam), Kn=kt * e_n, Bn=b * e_n, Kg=kt * e_g, Bg=b * e_g, V=v)
    e_tot = jnp.exp(tot)
    pairs = []
    for p in range(RWKV_W // LANES):
        sl = slice(p * LANES, (p + 1) * LANES)
        ops = {k: _stack_heads(a[:, sl]) for k, a in full.items()}
        ops["e_tot"] = e_tot[:, sl]
        ops["reverse"] = reverse
        pairs.append(ops)
    return pairs


def _chunk_masks(reverse):
    L = CHUNK
    si = lax.broadcasted_iota(jnp.int32, (2 * L, 2 * L), 0)
    sj = lax.broadcasted_iota(jnp.int32, (2 * L, 2 * L), 1)
    same = (si >= L) == (sj >= L)
    before = (si < sj) if reverse else (si > sj)
    return same & before, same & (before | (si == sj)), si == sj


def _chunks_solve(chains, states):
    L = CHUNK
    P2 = 2 * L
    n = len(chains)
    masks = {rev: _chunk_masks(rev) for rev in {c["reverse"] for c in chains}}
    strict = [masks[c["reverse"]][0] for c in chains]
    incl = [masks[c["reverse"]][1] for c in chains]
    eye = masks[chains[0]["reverse"]][2]

    big = [_dot_nt(jnp.concatenate([c["A"], c["R"]], axis=0), jnp.concatenate([c["Bn"], c["Kn"]], axis=0)) for c in chains]
    Mb = [jnp.where(strict[i], big[i][0:P2, 0:P2], 0.0) for i in range(n)]
    Mkv = [jnp.where(strict[i], big[i][0:P2, P2:2 * P2], 0.0) for i in range(n)]
    Pb = [jnp.where(incl[i], big[i][P2:2 * P2, 0:P2], 0.0) for i in range(n)]
    Pkv = [jnp.where(incl[i], big[i][P2:2 * P2, P2:2 * P2], 0.0) for i in range(n)]

    Pw = [-m for m in Mb]
    Tm = [jnp.where(eye, 1.0, 0.0) + p for p in Pw]
    Pw = [_dot(p, p) for p in Pw]
    for _ in range(int(math.log2(L)) - 2):
        PT = [_dot(p, jnp.concatenate([p, t], axis=1)) for p, t in zip(Pw, Tm)]
        Tm = [t + pt[:, P2:2 * P2] for t, pt in zip(Tm, PT)]
        Pw = [pt[:, 0:P2] for pt in PT]
    Tm = [t + _dot(p, t) for t, p in zip(Tm, Pw)]

    MV = [_dot(Mkv[i], chains[i]["V"]) for i in range(n)]
    TAM = [_dot(Tm[i], jnp.concatenate([chains[i]["A"], MV[i]], axis=1)) for i in range(n)]
    PB = [_dot(Pb[i], TAM[i]) for i in range(n)]
    PV = [_dot(Pkv[i], chains[i]["V"]) for i in range(n)]
    BG = [_dot_tn(chains[i]["Bg"], TAM[i]) for i in range(n)]
    KV = [_dot_tn(chains[i]["Kg"], chains[i]["V"]) for i in range(n)]
    ys, new_states = [], []
    for i in range(n):
        RA = chains[i]["R"] - PB[i][:, 0:P2]
        G2 = jnp.where(eye, chains[i]["e_tot"], 0.0) - BG[i][:, 0:P2]
        out = _dot(jnp.concatenate([RA, G2], axis=0), states[i])
        Ys = out[0:P2] + PV[i] - PB[i][:, P2:2 * P2]
        ys.append(Ys[0:L] + Ys[L:P2])
        new_states.append(out[P2:2 * P2] + KV[i] - BG[i][:, P2:2 * P2])
    return ys, new_states


def _rwkv_scan_body(rf_ref, vf_ref, khf_ref, rb_ref, vb_ref, khb_ref, lwf_ref, bf_ref, ktf_ref, lwb_ref, bb_ref, ktb_ref,
                    yf_ref, yb_ref, s_ref):
    @pl.when(pl.program_id(1) == 0)
    def _():
        s_ref[...] = jnp.zeros_like(s_ref)

    n_pairs = RWKV_W // LANES
    chains = (_chunk_operands(rf_ref[...], vf_ref[...], khf_ref[...], lwf_ref[...], bf_ref[...], ktf_ref[...], False)
              + _chunk_operands(rb_ref[...], vb_ref[...], khb_ref[...], lwb_ref[...], bb_ref[...], ktb_ref[...], True))
    ys, new_states = _chunks_solve(chains, [s_ref[i] for i in range(2 * n_pairs)])
    for i in range(2 * n_pairs):
        s_ref[i] = new_states[i]
    yf_ref[...] = jnp.concatenate(ys[:n_pairs], axis=1)
    yb_ref[...] = jnp.concatenate(ys[n_pairs:], axis=1)


def _rwkv_scan(r, v, kh, lw, b, kt, n_ctx):
    B, T, W = r.shape
    nc = T // CHUNK
    ncc = n_ctx // CHUNK
    rev = lambda j: jnp.where(j < ncc, ncc - 1 - j, nc + ncc - 1 - j)
    fwd1 = pl.BlockSpec((None, CHUNK, W), lambda bb, j: (bb, j, 0))
    bwd1 = pl.BlockSpec((None, CHUNK, W), lambda bb, j: (bb, rev(j), 0))
    fwd2 = pl.BlockSpec((None, None, CHUNK, W), lambda bb, j: (bb, 0, j, 0))
    bwd2 = pl.BlockSpec((None, None, CHUNK, W), lambda bb, j: (bb, 1, rev(j), 0))
    out = jax.ShapeDtypeStruct((B, T, W), F32)
    return pl.pallas_call(
        _rwkv_scan_body,
        grid=(B, nc),
        in_specs=[fwd1, fwd1, fwd1, bwd1, bwd1, bwd1, fwd2, fwd2, fwd2, bwd2, bwd2, bwd2],
        out_specs=[fwd1, bwd1],
        out_shape=[out, out],
        scratch_shapes=[pltpu.VMEM((2 * W // LANES, LANES, LANES), F32)],
        compiler_params=_cp("parallel", "arbitrary"),
        name="rwkv_scan",
    )(r, v, kh, r, v, kh, lw, b, kt, lw, b, kt)


def _rwkv_readout_body(yf_ref, yb_ref, bon_ref, g_ref, gng_ref, gnb_ref, e_ref, o_ref):
    y = yf_ref[...] + yb_ref[...]
    head_mean = e_ref[...] * (1.0 / RWKV_HEAD)
    mu = _dot_terms(y, head_mean, 2, 1)
    yc = y - mu
    var = _dot_terms(yc * yc, head_mean, 2, 1)
    yn = yc * lax.rsqrt(var + GN_EPS) * gng_ref[...] + gnb_ref[...]
    o_ref[...] = ((yn + bon_ref[...]) * g_ref[...]).astype(o_ref.dtype)


def _rwkv_readout(yf, yb, bonus, g, gn_g, gn_b, head_sum):
    B, T, W = yf.shape
    tt = TOK_TILE
    tok = pl.BlockSpec((None, tt, W), lambda b, i: (b, i, 0))
    row = pl.BlockSpec((1, W), lambda b, i: (0, 0))
    return pl.pallas_call(
        _rwkv_readout_body,
        grid=(B, T // tt),
        in_specs=[tok, tok, tok, tok, row, row, pl.BlockSpec((W, W), lambda b, i: (0, 0))],
        out_specs=tok,
        out_shape=jax.ShapeDtypeStruct((B, T, W), BF16),
        compiler_params=_cp("parallel", "parallel"),
        name="rwkv_readout",
    )(yf, yb, bonus, g, gn_g.reshape(1, W), gn_b.reshape(1, W), head_sum)


def _rope(y, cos, sin_signed, quarter):
    lane = lax.broadcasted_iota(jnp.int32, y.shape, 1)
    first = (lane & (2 * quarter - 1)) < quarter
    partner = jnp.where(first, pltpu.roll(y, LANES - quarter, 1), pltpu.roll(y, quarter, 1))
    return y * cos + partner * sin_signed


def _gqa_prep_body(p_ref, qg_ref, kg_ref, cos_ref, sin_ref, q_o, k_o, v_o):
    cos = cos_ref[...]
    sin = sin_ref[...]
    scale = GQA_HEAD ** -0.5 * LOG2E
    for h in range(GQA_Q_HEADS):
        sl = slice(h * GQA_HEAD, (h + 1) * GQA_HEAD)
        q = _rms(p_ref[:, sl], qg_ref[...])
        q_o[:, sl] = (_rope(q, cos, sin, GQA_HEAD // 4) * scale).astype(q_o.dtype)
    for h in range(GQA_KV_HEADS):
        sl = slice(h * GQA_HEAD, (h + 1) * GQA_HEAD)
        k = _rms(p_ref[:, GQA_Q_COLS + h * GQA_HEAD:GQA_Q_COLS + (h + 1) * GQA_HEAD], kg_ref[...])
        k_o[:, sl] = _rope(k, cos, sin, GQA_HEAD // 4).astype(k_o.dtype)
    v_o[...] = p_ref[:, GQA_Q_COLS + GQA_KV_COLS:GQA_COLS].astype(v_o.dtype)


def _gqa_prep(pg, q_norm_g, k_norm_g, cos, sin):
    B, T, _ = pg.shape
    tt = TOK_TILE
    tok = lambda w: pl.BlockSpec((None, tt, w), lambda b, i: (b, i, 0))
    row = pl.BlockSpec((1, GQA_HEAD), lambda b, i: (0, 0))
    tab = pl.BlockSpec((tt, LANES), lambda b, i: (i, 0))
    return pl.pallas_call(
        _gqa_prep_body,
        grid=(B, T // tt),
        in_specs=[tok(GQA_COLS), row, row, tab, tab],
        out_specs=[tok(GQA_Q_COLS), tok(GQA_KV_COLS), tok(GQA_KV_COLS)],
        out_shape=[jax.ShapeDtypeStruct((B, T, GQA_Q_COLS), BF16),
                   jax.ShapeDtypeStruct((B, T, GQA_KV_COLS), BF16),
                   jax.ShapeDtypeStruct((B, T, GQA_KV_COLS), BF16)],
        compiler_params=_cp("parallel", "parallel"),
        name="gqa_prep",
    )(pg, q_norm_g.reshape(1, -1), k_norm_g.reshape(1, -1), cos, sin)


def _mla_prep_body(p_ref, qg_ref, wq_ref, kvg_ref, wkv_ref, cos_ref, sin_ref, q_o, k_o, v_o):
    cos = cos_ref[...]
    sin = sin_ref[...]
    scale = (MLA_NOPE + MLA_ROPE) ** -0.5 * LOG2E
    cq = _rms(p_ref[:, 0:MLA_Q_RANK], qg_ref[...])
    q = jnp.dot(cq.astype(BF16), wq_ref[...], preferred_element_type=F32) * scale
    ckv = _rms(p_ref[:, MLA_Q_RANK:MLA_Q_RANK + MLA_KV_RANK], kvg_ref[...])
    kv = jnp.dot(ckv.astype(BF16), wkv_ref[...], preferred_element_type=F32)
    kr = _rope(p_ref[:, MLA_Q_RANK + MLA_KV_RANK:MLA_COLS_PAD], cos, sin, MLA_ROPE // 4).astype(k_o.dtype)
    for h in range(MLA_HEADS):
        lo = h * MLA_DK
        q_o[:, lo:lo + LANES] = q[:, lo:lo + LANES].astype(q_o.dtype)
        q_o[:, lo + LANES:lo + MLA_DK] = _rope(q[:, lo + LANES:lo + MLA_DK], cos, sin, MLA_ROPE // 4).astype(q_o.dtype)
        k_o[:, lo:lo + LANES] = kv[:, h * MLA_NOPE:(h + 1) * MLA_NOPE].astype(k_o.dtype)
        k_o[:, lo + LANES:lo + MLA_DK] = kr
    v_o[...] = kv[:, MLA_HEADS * MLA_NOPE:].astype(v_o.dtype)


def _mla_prep(pm, q_norm_g, wq, kv_norm_g, wkv, cos, sin):
    B, T, _ = pm.shape
    tt = TOK_TILE
    tok = lambda w: pl.BlockSpec((None, tt, w), lambda b, i: (b, i, 0))
    full = lambda a: pl.BlockSpec(a.shape, lambda b, i: (0,) * a.ndim)
    tab = pl.BlockSpec((tt, LANES), lambda b, i: (i, 0))
    qg = q_norm_g.reshape(1, -1)
    kvg = kv_norm_g.reshape(1, -1)
    return pl.pallas_call(
        _mla_prep_body,
        grid=(B, T // tt),
        in_specs=[tok(MLA_COLS_PAD), full(qg), full(wq), full(kvg), full(wkv), tab, tab],
        out_specs=[tok(MLA_HEADS * MLA_DK), tok(MLA_HEADS * MLA_DK), tok(MLA_HEADS * MLA_V)],
        out_shape=[jax.ShapeDtypeStruct((B, T, MLA_HEADS * MLA_DK), BF16),
                   jax.ShapeDtypeStruct((B, T, MLA_HEADS * MLA_DK), BF16),
                   jax.ShapeDtypeStruct((B, T, MLA_HEADS * MLA_V), BF16)],
        compiler_params=_cp("parallel", "parallel"),
        name="mla_prep",
    )(pm, qg, wq, kvg, wkv, cos, sin)


def _attn_body(q_ref, k_ref, v_ref, o_ref, *, hq, hkv, dk, dv, n_ctx_tiles, n_ctx, n_all):
    rep = hq // hkv
    tq = q_ref.shape[0]

    def run(nk):
        for g in range(hkv):
            kg = k_ref[0:nk, g * dk:(g + 1) * dk]
            vg = v_ref[0:nk, g * dv:(g + 1) * dv]
            v_aug = jnp.concatenate([vg, jnp.ones_like(vg)], axis=1)
            q = jnp.concatenate([q_ref[:, h * dk:(h + 1) * dk] for h in range(g * rep, (g + 1) * rep)], axis=0)
            s = _dot_nt(q, kg)
            p = jnp.exp2(s - jnp.max(s, axis=-1, keepdims=True))
            o = jnp.dot(p.astype(BF16), v_aug, preferred_element_type=F32)
            o = o[:, 0:dv] / o[:, dv:dv + 1]
            for rr in range(rep):
                h = g * rep + rr
                o_ref[:, h * dv:(h + 1) * dv] = o[rr * tq:(rr + 1) * tq].astype(o_ref.dtype)

    @pl.when(pl.program_id(1) < n_ctx_tiles)
    def _():
        run(n_ctx)

    @pl.when(pl.program_id(1) >= n_ctx_tiles)
    def _():
        run(n_all)


def _attention(q, k, v, hq, hkv, dk, dv, n_ctx):
    B, T, _ = q.shape
    tq = TOK_TILE
    return pl.pallas_call(
        functools.partial(_attn_body, hq=hq, hkv=hkv, dk=dk, dv=dv, n_ctx_tiles=n_ctx // tq, n_ctx=n_ctx, n_all=T),
        grid=(B, T // tq),
        in_specs=[pl.BlockSpec((None, tq, hq * dk), lambda b, i: (b, i, 0)),
                  pl.BlockSpec((None, T, hkv * dk), lambda b, i: (b, 0, 0)),
                  pl.BlockSpec((None, T, hkv * dv), lambda b, i: (b, 0, 0))],
        out_specs=pl.BlockSpec((None, tq, hq * dv), lambda b, i: (b, i, 0)),
        out_shape=jax.ShapeDtypeStruct((B, T, hq * dv), BF16),
        compiler_params=_cp("parallel", "parallel"),
        name="attention",
    )(q, k, v)


def _rope_tables(T, n_ctx, n_rot):
    quarter = n_rot // 4
    t = jnp.arange(T - n_ctx)
    row = (t // GRID_W).astype(F32)
    col = (t % GRID_W).astype(F32)
    inv = ROPE_THETA ** (-jnp.arange(quarter, dtype=F32) / quarter)
    ar = row[:, None] * inv[None, :]
    ac = col[:, None] * inv[None, :]
    pad = LANES - n_rot
    cos = jnp.concatenate([jnp.cos(ar), jnp.cos(ar), jnp.cos(ac), jnp.cos(ac), jnp.ones((T - n_ctx, pad), F32)], axis=1)
    sin = jnp.concatenate([-jnp.sin(ar), jnp.sin(ar), -jnp.sin(ac), jnp.sin(ac), jnp.zeros((T - n_ctx, pad), F32)], axis=1)
    cos = jnp.concatenate([jnp.ones((n_ctx, LANES), F32), cos], axis=0)
    sin = jnp.concatenate([jnp.zeros((n_ctx, LANES), F32), sin], axis=0)
    return cos, sin


def _moe_body(te_ref, nv_ref, hs_ref, w1_ref, w3_ref, w2_ref, y_ref):
    i = pl.program_id(0)

    @pl.when(i < nv_ref[0])
    def _():
        hs = hs_ref[...]
        a = jnp.dot(hs, w1_ref[...].astype(BF16), preferred_element_type=F32)
        b = jnp.dot(hs, w3_ref[...].astype(BF16), preferred_element_type=F32)
        act = a * _sigmoid(a) * b
        y = jnp.dot(act.astype(BF16), w2_ref[...].astype(BF16), preferred_element_type=F32)
        y_ref[...] = y.astype(y_ref.dtype)

    @pl.when(i >= nv_ref[0])
    def _():
        y_ref[...] = jnp.zeros_like(y_ref)


def _moe_experts(tile_expert, n_valid, hs, w1, w3, w2, layer):
    NP, D = hs.shape
    tm = MOE_TM
    DE = w1.shape[-1]
    grid_spec = pltpu.PrefetchScalarGridSpec(
        num_scalar_prefetch=2,
        grid=(NP // tm,),
        in_specs=[pl.BlockSpec((tm, D), lambda i, te, nv: (i, 0)),
                  pl.BlockSpec((None, None, D, DE), lambda i, te, nv: (layer, te[i], 0, 0)),
                  pl.BlockSpec((None, None, D, DE), lambda i, te, nv: (layer, te[i], 0, 0)),
                  pl.BlockSpec((None, None, DE, D), lambda i, te, nv: (layer, te[i], 0, 0))],
        out_specs=pl.BlockSpec((tm, D), lambda i, te, nv: (i, 0)),
    )
    return pl.pallas_call(
        _moe_body,
        grid_spec=grid_spec,
        out_shape=jax.ShapeDtypeStruct((NP, D), BF16),
        compiler_params=_cp("arbitrary"),
        name="moe_experts",
    )(tile_expert, n_valid, hs, w1, w3, w2)


def _moe(h2, re, w1, w3, w2, layer):
    N = h2.shape[0]
    tm = MOE_TM
    n_tiles = (2 * N) // tm + N_EXPERTS
    e_flat = re.reshape(2 * N)
    onehot = (e_flat[:, None] == jnp.arange(N_EXPERTS, dtype=jnp.int32)[None, :]).astype(jnp.int32)
    csum = jnp.cumsum(onehot, axis=0)
    rank = jnp.sum(onehot * (csum - 1), axis=1)
    counts = csum[-1]
    ptiles = (counts + tm - 1) // tm
    tile_end = jnp.cumsum(ptiles)
    tile_start = tile_end - ptiles
    pos = tile_start[e_flat] * tm + rank
    n_valid = tile_end[-1:].astype(jnp.int32)
    tile_ids = jnp.arange(n_tiles, dtype=jnp.int32)
    tile_expert = jnp.minimum(jnp.sum((tile_end[None, :] <= tile_ids[:, None]).astype(jnp.int32), axis=1), N_EXPERTS - 1)
    spread = jnp.arange(n_tiles * tm, dtype=jnp.int32) % N
    src = spread.at[pos].set(jnp.arange(2 * N, dtype=jnp.int32) // 2)
    rows = lambda a, idx: a.at[idx].get(mode="promise_in_bounds")
    hs = rows(h2, src)
    y = _moe_experts(tile_expert, n_valid, hs, w1, w3, w2, layer)
    pos2 = pos.reshape(N, 2)
    return rows(y, pos2[:, 0]), rows(y, pos2[:, 1])


def kernel(x, c, ctx, c_ctx, mod_w, mod_b, norm1_g, norm2_g, w_in, w_out, shift_prev, shift_next, decay_w0, decay_up, iclr_a0, iclr_up, gate_up, k_k, k_a, r_k, gn_g, gn_b, q_norm_g, k_norm_g, mla_q_norm_g, mla_w_uq, mla_kv_norm_g, mla_w_ukv, router_gw, router_gb, router_ew, router_eb, exp_w1, exp_w3, exp_w2, final_norm_g):
    B, S, D = x.shape
    C = ctx.shape[1]
    T = C + S
    depth = mod_w.shape[0]
    assert C == TOK_TILE and S % TOK_TILE == 0 and B <= CTX_ROW
    n_ctx_tiles = C // TOK_TILE

    X = jnp.concatenate([ctx, x], axis=1)
    cc = jnp.zeros((SUBLANES, D), F32).at[:B].set(c).at[CTX_ROW].set(c_ctx)
    mods_all = _modulation(cc, mod_w, mod_b).reshape(depth, SUBLANES * N_MOD, 1, D)

    cos_g, sin_g = _rope_tables(T, C, GQA_HEAD)
    cos_m, sin_m = _rope_tables(T, C, MLA_ROPE)
    hid = jnp.arange(RWKV_W) // RWKV_HEAD
    head_sum = (hid[:, None] == hid[None, :]).astype(F32)

    for l in range(depth):
        mods = mods_all[l]
        w_r = w_in[l][:, :RWKV_COLS].astype(BF16)
        w_g = w_in[l][:, RWKV_COLS:RWKV_COLS + GQA_COLS].astype(BF16)
        w_m = jnp.pad(w_in[l][:, RWKV_COLS + GQA_COLS:], ((0, 0), (0, MLA_COLS_PAD - MLA_COLS))).astype(BF16)

        if l == 0:
            h = _norm_mod(X, norm1_g[l], mods, 0, 1, n_ctx_tiles)
        h = h.reshape(B * T, D)
        pr = _matmul(h, w_r).reshape(B, T, RWKV_COLS)
        pg = _matmul(h, w_g).reshape(B, T, GQA_COLS)
        pm = _matmul(h, w_m).reshape(B, T, MLA_COLS_PAD)

        wup_pad = jnp.pad(decay_up[l], ((0, 0), (0, ICLR_RANK), (0, 0)))
        aup_pad = jnp.pad(iclr_up[l], ((0, 0), (DECAY_RANK, 0), (0, 0)))
        r, v, kh, lw, b, kt, g, bonus = _rwkv_prep(pr, shift_prev[l], shift_next[l], decay_w0[l], wup_pad,
                                                  iclr_a0[l], aup_pad, gate_up[l], k_k[l], k_a[l], r_k[l], head_sum)
        yf, yb = _rwkv_scan(r, v, kh, lw, b, kt, C)
        o_r = _rwkv_readout(yf, yb, bonus, g, gn_g[l], gn_b[l], head_sum)

        q, k, vv = _gqa_prep(pg, q_norm_g[l], k_norm_g[l], cos_g, sin_g)
        o_g = _attention(q, k, vv, GQA_Q_HEADS, GQA_KV_HEADS, GQA_HEAD, GQA_HEAD, C)

        wq = mla_w_uq[l].reshape(MLA_Q_RANK, MLA_HEADS, MLA_NOPE + MLA_ROPE)
        wq = jnp.pad(wq, ((0, 0), (0, 0), (0, MLA_DK - MLA_NOPE - MLA_ROPE))).reshape(MLA_Q_RANK, MLA_HEADS * MLA_DK)
        wkv = mla_w_ukv[l].reshape(MLA_KV_RANK, MLA_HEADS, MLA_NOPE + MLA_V)
        wkv = jnp.concatenate([wkv[:, :, :MLA_NOPE].reshape(MLA_KV_RANK, -1), wkv[:, :, MLA_NOPE:].reshape(MLA_KV_RANK, -1)], axis=1)
        qm, km, vm = _mla_prep(pm, mla_q_norm_g[l], wq.astype(BF16), mla_kv_norm_g[l], wkv.astype(BF16), cos_m, sin_m)
        o_m = _attention(qm, km, vm, MLA_HEADS, MLA_HEADS, MLA_DK, MLA_V, C)

        o = jnp.concatenate([o_r, o_g, o_m], axis=-1)
        X = _matmul_gated_residual(o, w_out[l].astype(BF16), X, mods, 2, C)

        wr = jnp.pad(jnp.concatenate([router_gw[l], router_ew[l]], axis=1), ((0, 0), (0, LANES - N_GROUPS - N_EXPERTS)))
        br = jnp.pad(jnp.concatenate([router_gb[l], router_eb[l]]), (0, LANES - N_GROUPS - N_EXPERTS)).reshape(1, LANES)
        h2, rw, re = _norm_router(X, norm2_g[l], mods, 3, 4, n_ctx_tiles, wr, br)
        y0, y1 = _moe(h2.reshape(B * T, D), re.reshape(B * T, LANES)[:, :2], exp_w1, exp_w3, exp_w2, l)
        y0 = y0.reshape(B, T, D)
        y1 = y1.reshape(B, T, D)
        if l + 1 < depth:
            X, h = _moe_residual_norm(X, y0, y1, rw, mods, 5, n_ctx_tiles, norm1_g[l + 1], mods_all[l + 1])
        else:
            out = _moe_residual_final(X, y0, y1, rw, mods, 5, n_ctx_tiles, final_norm_g)
    return out
```

```python
import functools
import math

import jax
import jax.numpy as jnp
import numpy as np
from jax import lax
from jax.experimental import pallas as pl
from jax.experimental.pallas import tpu as pltpu

F32 = jnp.float32
BF16 = jnp.bfloat16
HI = lax.Precision.HIGHEST

V7X_VMEM_BYTES = 64 * 1024 * 1024
VMEM_LIMIT = V7X_VMEM_BYTES - 8 * 1024 * 1024
LANES = 128
SUBLANES = 8

GRID_W = 64
ROPE_THETA = 10000.0
NORM_EPS = 1e-6
GN_EPS = 64e-5
DECAY_SCALE = math.exp(-0.5)
LOG2E = math.log2(math.e)

RWKV_HEAD = 64
RWKV_W = 512
DECAY_RANK = 64
ICLR_RANK = 64
GATE_RANK = 128
RWKV_COLS = 3 * RWKV_W + DECAY_RANK + ICLR_RANK + GATE_RANK
LOWRANK_OFF = 3 * RWKV_W
GATE_OFF = LOWRANK_OFF + DECAY_RANK + ICLR_RANK
CHUNK = 64

GQA_HEAD = 128
GQA_Q_HEADS = 8
GQA_KV_HEADS = 2
GQA_Q_COLS = GQA_Q_HEADS * GQA_HEAD
GQA_KV_COLS = GQA_KV_HEADS * GQA_HEAD
GQA_COLS = GQA_Q_COLS + 2 * GQA_KV_COLS

MLA_HEADS = 4
MLA_NOPE = 128
MLA_ROPE = 64
MLA_V = 128
MLA_Q_RANK = 384
MLA_KV_RANK = 256
MLA_COLS = MLA_Q_RANK + MLA_KV_RANK + MLA_ROPE
MLA_COLS_PAD = 768
MLA_DK = 2 * LANES

N_GROUPS = 4
EXPERTS_PER_GROUP = 8
N_EXPERTS = 32
D_EXPERT = 256
MOE_TM = 256

TOK_TILE = 256
N_MOD = 6
CTX_ROW = 4


def _cp(*sem):
    return pltpu.CompilerParams(dimension_semantics=sem, vmem_limit_bytes=VMEM_LIMIT)


def _sigmoid(x):
    return 1.0 / (1.0 + jnp.exp(-x))


def _bf16_terms(x, n):
    terms = []
    for _ in range(n):
        t = x.astype(BF16)
        terms.append(t)
        x = x - t.astype(F32)
    return terms


def _dot_terms(x, w, nx, nw):
    xs = _bf16_terms(x, nx)
    ws = _bf16_terms(w, nw)
    acc = None
    for i in range(nx):
        for j in range(nw):
            if i + j < max(nx, nw):
                p = jnp.dot(xs[i], ws[j], preferred_element_type=F32)
                acc = p if acc is None else acc + p
    return acc


def _mod_body(c_ref, w_ref, b_ref, o_ref):
    c = c_ref[...]
    s = c * _sigmoid(c)
    o_ref[...] = jnp.dot(s, w_ref[...], preferred_element_type=F32, precision=HI) + b_ref[...]


def _modulation(cc, mod_w, mod_b):
    L, D, N = mod_w.shape
    tn = 1024
    return pl.pallas_call(
        _mod_body,
        grid=(L, N // tn),
        in_specs=[pl.BlockSpec((SUBLANES, D), lambda l, j: (0, 0)),
                  pl.BlockSpec((None, D, tn), lambda l, j: (l, 0, j)),
                  pl.BlockSpec((None, 1, tn), lambda l, j: (l, 0, j))],
        out_specs=pl.BlockSpec((None, SUBLANES, tn), lambda l, j: (l, 0, j)),
        out_shape=jax.ShapeDtypeStruct((L, SUBLANES, N), F32),
        compiler_params=_cp("parallel", "parallel"),
        name="modulation",
    )(cc, mod_w, mod_b.reshape(L, 1, N))


def _mod_spec(which, n_ctx_tiles, D):
    return pl.BlockSpec((None, 1, D), lambda b, i: (jnp.where(i < n_ctx_tiles, CTX_ROW, b) * N_MOD + which, 0, 0))


def _rms(x, g):
    return x * lax.rsqrt(jnp.mean(x * x, axis=-1, keepdims=True) + NORM_EPS) * g


def _norm_mod_body(x_ref, g_ref, sh_ref, sc_ref, o_ref):
    y = _rms(x_ref[...], g_ref[...])
    o_ref[...] = (y * (1.0 + sc_ref[...]) + sh_ref[...]).astype(o_ref.dtype)


def _route(logits):
    lane = lax.broadcasted_iota(jnp.int32, logits.shape, 1)
    lane_f = lane.astype(F32)
    neg = jnp.float32(-1e30)
    far = jnp.float32(1e9)
    first_at = lambda hit: jnp.min(jnp.where(hit, lane_f, far), axis=-1, keepdims=True).astype(jnp.int32)
    gl = jnp.where(lane < N_GROUPS, logits, neg)
    gmax = jnp.max(gl, axis=-1, keepdims=True)
    gidx = first_at(gl == gmax)
    p_sel = 1.0 / jnp.sum(jnp.exp(gl - gmax), axis=-1, keepdims=True)
    lo = N_GROUPS + gidx * EXPERTS_PER_GROUP
    el = jnp.where((lane >= lo) & (lane < lo + EXPERTS_PER_GROUP), logits, neg)
    m1 = jnp.max(el, axis=-1, keepdims=True)
    i1 = first_at(el == m1)
    el2 = jnp.where(lane == i1, neg, el)
    m2 = jnp.max(el2, axis=-1, keepdims=True)
    i2 = first_at(el2 == m2)
    t = jnp.exp(m2 - m1)
    w1 = p_sel / (1.0 + t)
    w2 = p_sel * t / (1.0 + t)
    rw = jnp.where(lane == 0, w1, jnp.where(lane == 1, w2, 0.0))
    re = jnp.where(lane == 0, i1 - N_GROUPS, jnp.where(lane == 1, i2 - N_GROUPS, 0))
    return rw, re


def _norm_router_body(x_ref, g_ref, sh_ref, sc_ref, wr_ref, br_ref, h_ref, rw_ref, re_ref):
    y = _rms(x_ref[...], g_ref[...])
    h = y * (1.0 + sc_ref[...]) + sh_ref[...]
    h_ref[...] = h.astype(h_ref.dtype)
    logits = _dot_terms(h, wr_ref[...], 2, 2) + br_ref[...]
    rw, re = _route(logits)
    rw_ref[...] = rw
    re_ref[...] = re


def _norm_mod(X, g, mods, shift_i, scale_i, n_ctx_tiles):
    B, T, D = X.shape
    tt = TOK_TILE
    return pl.pallas_call(
        _norm_mod_body,
        grid=(B, T // tt),
        in_specs=[pl.BlockSpec((None, tt, D), lambda b, i: (b, i, 0)),
                  pl.BlockSpec((1, D), lambda b, i: (0, 0)),
                  _mod_spec(shift_i, n_ctx_tiles, D),
                  _mod_spec(scale_i, n_ctx_tiles, D)],
        out_specs=pl.BlockSpec((None, tt, D), lambda b, i: (b, i, 0)),
        out_shape=jax.ShapeDtypeStruct((B, T, D), BF16),
        compiler_params=_cp("parallel", "parallel"),
        name="norm_mod",
    )(X, g.reshape(1, D), mods, mods)


def _norm_router(X, g, mods, shift_i, scale_i, n_ctx_tiles, wr, br):
    B, T, D = X.shape
    tt = TOK_TILE
    tok = lambda w, dt: (pl.BlockSpec((None, tt, w), lambda b, i: (b, i, 0)), jax.ShapeDtypeStruct((B, T, w), dt))
    outs = [tok(D, F32), tok(LANES, F32), tok(LANES, jnp.int32)]
    return pl.pallas_call(
        _norm_router_body,
        grid=(B, T // tt),
        in_specs=[pl.BlockSpec((None, tt, D), lambda b, i: (b, i, 0)),
                  pl.BlockSpec((1, D), lambda b, i: (0, 0)),
                  _mod_spec(shift_i, n_ctx_tiles, D),
                  _mod_spec(scale_i, n_ctx_tiles, D),
                  pl.BlockSpec((D, LANES), lambda b, i: (0, 0)),
                  pl.BlockSpec((1, LANES), lambda b, i: (0, 0))],
        out_specs=[o[0] for o in outs],
        out_shape=[o[1] for o in outs],
        compiler_params=_cp("parallel", "parallel"),
        name="norm_router",
    )(X, g.reshape(1, D), mods, mods, wr, br)


def _mm_body(a_ref, w_ref, o_ref):
    o_ref[...] = jnp.dot(a_ref[...], w_ref[...], preferred_element_type=F32).astype(o_ref.dtype)


def _matmul(a, w, tm=512):
    M, K = a.shape
    N = w.shape[1]
    return pl.pallas_call(
        _mm_body,
        grid=(M // tm,),
        in_specs=[pl.BlockSpec((tm, K), lambda i: (i, 0)),
                  pl.BlockSpec((K, N), lambda i: (0, 0))],
        out_specs=pl.BlockSpec((tm, N), lambda i: (i, 0)),
        out_shape=jax.ShapeDtypeStruct((M, N), F32),
        compiler_params=_cp("parallel"),
        name="token_matmul",
    )(a, w)


def _mm_res_body(a_ref, w_ref, x_ref, gl_ref, gc_ref, o_ref, *, n_ctx, tm):
    acc = jnp.dot(a_ref[...], w_ref[...], preferred_element_type=F32)
    row = pl.program_id(1) * tm + lax.broadcasted_iota(jnp.int32, (tm, 1), 0)
    gate = jnp.where(row < n_ctx, gc_ref[...], gl_ref[...])
    o_ref[...] = x_ref[...] + gate * acc


def _matmul_gated_residual(a, w, X, mods, gate_i, n_ctx, tm=384):
    B, T, K = a.shape
    D = w.shape[1]
    assert T % tm == 0
    return pl.pallas_call(
        functools.partial(_mm_res_body, n_ctx=n_ctx, tm=tm),
        grid=(B, T // tm),
        in_specs=[pl.BlockSpec((None, tm, K), lambda b, i: (b, i, 0)),
                  pl.BlockSpec((K, D), lambda b, i: (0, 0)),
                  pl.BlockSpec((None, tm, D), lambda b, i: (b, i, 0)),
                  pl.BlockSpec((None, 1, D), lambda b, i: (b * N_MOD + gate_i, 0, 0)),
                  pl.BlockSpec((None, 1, D), lambda b, i: (CTX_ROW * N_MOD + gate_i, 0, 0))],
        out_specs=pl.BlockSpec((None, tm, D), lambda b, i: (b, i, 0)),
        out_shape=jax.ShapeDtypeStruct((B, T, D), F32),
        compiler_params=_cp("parallel", "parallel"),
        name="out_proj_residual",
    )(a, w, X, mods, mods)


def _moe_residual(x_ref, y0_ref, y1_ref, rw_ref, gate_ref):
    rw = rw_ref[...]
    moe = rw[:, 0:1] * y0_ref[...].astype(F32) + rw[:, 1:2] * y1_ref[...].astype(F32)
    return x_ref[...] + gate_ref[...] * moe


def _moe_residual_norm_body(x_ref, y0_ref, y1_ref, rw_ref, gate_ref, g_ref, sh_ref, sc_ref, x_o, h_o):
    x = _moe_residual(x_ref, y0_ref, y1_ref, rw_ref, gate_ref)
    x_o[...] = x
    h_o[...] = (_rms(x, g_ref[...]) * (1.0 + sc_ref[...]) + sh_ref[...]).astype(h_o.dtype)


def _moe_residual_final_body(x_ref, y0_ref, y1_ref, rw_ref, gate_ref, g_ref, o_ref):
    o_ref[...] = _rms(_moe_residual(x_ref, y0_ref, y1_ref, rw_ref, gate_ref), g_ref[...])


def _moe_residual_norm(X, Y0, Y1, rw, mods, gate_i, n_ctx_tiles, g_next, mods_next):
    B, T, D = X.shape
    tt = TOK_TILE
    blk = pl.BlockSpec((None, tt, D), lambda b, i: (b, i, 0))
    return pl.pallas_call(
        _moe_residual_norm_body,
        grid=(B, T // tt),
        in_specs=[blk, blk, blk, pl.BlockSpec((None, tt, LANES), lambda b, i: (b, i, 0)),
                  _mod_spec(gate_i, n_ctx_tiles, D), pl.BlockSpec((1, D), lambda b, i: (0, 0)),
                  _mod_spec(0, n_ctx_tiles, D), _mod_spec(1, n_ctx_tiles, D)],
        out_specs=[blk, blk],
        out_shape=[jax.ShapeDtypeStruct((B, T, D), F32), jax.ShapeDtypeStruct((B, T, D), BF16)],
        compiler_params=_cp("parallel", "parallel"),
        name="moe_residual_norm",
    )(X, Y0, Y1, rw, mods, g_next.reshape(1, D), mods_next, mods_next)


def _moe_residual_final(X, Y0, Y1, rw, mods, gate_i, n_ctx_tiles, g_final):
    B, T, D = X.shape
    tt = TOK_TILE
    S = T - n_ctx_tiles * tt
    lat = lambda w: pl.BlockSpec((None, tt, w), lambda b, i: (b, i + n_ctx_tiles, 0))
    return pl.pallas_call(
        _moe_residual_final_body,
        grid=(B, S // tt),
        in_specs=[lat(D), lat(D), lat(D), lat(LANES),
                  pl.BlockSpec((None, 1, D), lambda b, i: (b * N_MOD + gate_i, 0, 0)),
                  pl.BlockSpec((1, D), lambda b, i: (0, 0))],
        out_specs=pl.BlockSpec((None, tt, D), lambda b, i: (b, i, 0)),
        out_shape=jax.ShapeDtypeStruct((B, S, D), F32),
        compiler_params=_cp("parallel", "parallel"),
        name="moe_residual_final",
    )(X, Y0, Y1, rw, mods, g_final.reshape(1, D))


def _rwkv_prep_body(p_ref, pv_ref, nx_ref, mup_ref, mun_ref, w0_ref, wup_ref, a0_ref, aup_ref, gup_ref,
                    kk_ref, ka_ref, rk_ref, e_ref,
                    r_o, v_o, kh_o, lw_o, b_o, kt_o, g_o, bon_o, *, n_tiles, tt):
    i = pl.program_id(1)
    p = p_ref[...]
    seq_first = i <= 1
    seq_last = (i == 0) | (i == n_tiles - 1)
    prow = jnp.where(seq_first, 0.0, pv_ref[SUBLANES - 1:SUBLANES, :])
    nrow = jnp.where(seq_last, 0.0, nx_ref[0:1, :])
    rid = lax.broadcasted_iota(jnp.int32, (tt, 1), 0)
    prev = jnp.where(rid == 0, prow, pltpu.roll(p, 1, 0))
    nxt = jnp.where(rid == tt - 1, nrow, pltpu.roll(p, tt - 1, 0))
    z = p + mup_ref[...] * (prev - p) + mun_ref[...] * (nxt - p)

    W = RWKV_W
    r = z[:, 0:W]
    k = z[:, W:2 * W]
    v = z[:, 2 * W:3 * W]
    lowrank = z[:, LOWRANK_OFF:LOWRANK_OFF + LANES]
    gd = z[:, GATE_OFF:GATE_OFF + GATE_RANK]
    head_sum = e_ref[...]

    kap = k * kk_ref[...]
    ss = _dot_terms(kap * kap, head_sum, 2, 1)
    khat = kap * lax.rsqrt(ss + 1e-12)
    wd_t = jnp.tanh(lowrank)
    g_o[...] = _dot_terms(_sigmoid(gd), gup_ref[...], 1, 1)
    r_o[...] = r
    v_o[...] = v
    kh_o[...] = khat
    kt_sum = None
    for d in range(2):
        dec = _dot_terms(wd_t, wup_ref[d], 2, 2)
        lw_o[d] = -DECAY_SCALE * _sigmoid(w0_ref[d:d + 1, :] + dec)
        a = _sigmoid(a0_ref[d:d + 1, :] + _dot_terms(lowrank, aup_ref[d], 1, 1))
        kt = k * (1.0 + (a - 1.0) * ka_ref[...])
        kt_o[d] = kt
        b_o[d] = a * khat
        kt_sum = kt if kt_sum is None else kt_sum + kt
    bsum = _dot_terms(r * kt_sum * rk_ref[...], head_sum, 2, 1)
    bon_o[...] = bsum * v


def _rwkv_prep(pr, mu_prev, mu_next, w0, wup_pad, a0, aup_pad, g_up, k_k, k_a, r_k, head_sum):
    B, T, _ = pr.shape
    tt = TOK_TILE
    W = RWKV_W
    n_tiles = T // tt
    n8 = tt // SUBLANES
    row = lambda v: v.reshape(1, -1)
    full = lambda a: pl.BlockSpec(a.shape, lambda b, i: (0,) * a.ndim)
    tok = pl.BlockSpec((None, tt, W), lambda b, i: (b, i, 0))
    tok2 = pl.BlockSpec((None, 2, tt, W), lambda b, i: (b, 0, i, 0))
    s1 = jax.ShapeDtypeStruct((B, T, W), F32)
    s2 = jax.ShapeDtypeStruct((B, 2, T, W), F32)
    consts = [row(mu_prev), row(mu_next), w0, wup_pad, a0, aup_pad, g_up, row(k_k), row(k_a), row(r_k), head_sum]
    return pl.pallas_call(
        functools.partial(_rwkv_prep_body, n_tiles=n_tiles, tt=tt),
        grid=(B, n_tiles),
        in_specs=[pl.BlockSpec((None, tt, RWKV_COLS), lambda b, i: (b, i, 0)),
                  pl.BlockSpec((None, SUBLANES, RWKV_COLS), lambda b, i: (b, jnp.maximum(i * n8 - 1, 0), 0)),
                  pl.BlockSpec((None, SUBLANES, RWKV_COLS), lambda b, i: (b, jnp.minimum((i + 1) * n8, T // SUBLANES - 1), 0)),
                  ] + [full(a) for a in consts],
        out_specs=[tok, tok, tok, tok2, tok2, tok2, tok, tok],
        out_shape=[s1, s1, s1, s2, s2, s2, s1, s1],
        compiler_params=_cp("parallel", "parallel"),
        name="rwkv_prep",
    )(pr, pr, pr, *consts)


def _stack_heads(x):
    lane = lax.broadcasted_iota(jnp.int32, x.shape, 1)
    first = lane < RWKV_HEAD
    return jnp.concatenate([jnp.where(first, x, 0.0), jnp.where(first, 0.0, x)], axis=0)


def _dot(a, b):
    return jnp.dot(a.astype(BF16), b.astype(BF16), preferred_element_type=F32)


def _dot_nt(a, b):
    return lax.dot_general(a.astype(BF16), b.astype(BF16), (((1,), (1,)), ((), ())), preferred_element_type=F32)


def _dot_tn(a, b):
    return jnp.dot(a.T.astype(BF16), b.astype(BF16), preferred_element_type=F32)


def _chunk_operands(r, v, kh, lw, b, kt, reverse):
    L = CHUNK
    ti = lax.broadcasted_iota(jnp.int32, (L, L), 0)
    tj = lax.broadcasted_iota(jnp.int32, (L, L), 1)
    tri = jnp.where((ti <= tj) if reverse else (ti >= tj), 1.0, 0.0)
    lam = _dot_terms(tri, lw, 1, 3)
    tot = lam[0:1, :] if reverse else lam[L - 1:L, :]
    e_n = jnp.exp(-lam)
    e_g = jnp.exp(tot - lam)
    full = dict(A=kh * jnp.exp(lam - lw), R=r * jnp.exp(lam), Kn=kt * e_n, Bn=b * e_n, Kg=kt * e_g, Bg=b * e_g, V=v)
    e_tot = jnp.exp(tot)
    pairs = []
    for p in range(RWKV_W // LANES):
        sl = slice(p * LANES, (p + 1) * LANES)
        ops = {k: _stack_heads(a[:, sl]) for k, a in full.items()}
        ops["e_tot"] = e_tot[:, sl]
        ops["reverse"] = reverse
        pairs.append(ops)
    return pairs


def _chunk_masks(reverse):
    L = CHUNK
    si = lax.broadcasted_iota(jnp.int32, (2 * L, 2 * L), 0)
    sj = lax.broadcasted_iota(jnp.int32, (2 * L, 2 * L), 1)
    same = (si >= L) == (sj >= L)
    before = (si < sj) if reverse else (si > sj)
    return same & before, same & (before | (si == sj)), si == sj


def _chunks_solve(chains, states):
    L = CHUNK
    P2 = 2 * L
    n = len(chains)
    masks = {rev: _chunk_masks(rev) for rev in {c["reverse"] for c in chains}}
    strict = [masks[c["reverse"]][0] for c in chains]
    incl = [masks[c["reverse"]][1] for c in chains]
    eye = masks[chains[0]["reverse"]][2]

    big = [_dot_nt(jnp.concatenate([c["A"], c["R"]], axis=0), jnp.concatenate([c["Bn"], c["Kn"]], axis=0)) for c in chains]
    Mb = [jnp.where(strict[i], big[i][0:P2, 0:P2], 0.0) for i in range(n)]
    Mkv = [jnp.where(strict[i], big[i][0:P2, P2:2 * P2], 0.0) for i in range(n)]
    Pb = [jnp.where(incl[i], big[i][P2:2 * P2, 0:P2], 0.0) for i in range(n)]
    Pkv = [jnp.where(incl[i], big[i][P2:2 * P2, P2:2 * P2], 0.0) for i in range(n)]

    Pw = [-m for m in Mb]
    Tm = [jnp.where(eye, 1.0, 0.0) + p for p in Pw]
    Pw = [_dot(p, p) for p in Pw]
    for _ in range(int(math.log2(L)) - 2):
        PT = [_dot(p, jnp.concatenate([p, t], axis=1)) for p, t in zip(Pw, Tm)]
        Tm = [t + pt[:, P2:2 * P2] for t, pt in zip(Tm, PT)]
        Pw = [pt[:, 0:P2] for pt in PT]
    Tm = [t + _dot(p, t) for t, p in zip(Tm, Pw)]

    MV = [_dot(Mkv[i], chains[i]["V"]) for i in range(n)]
    TAM = [_dot(Tm[i], jnp.concatenate([chains[i]["A"], MV[i]], axis=1)) for i in range(n)]
    PB = [_dot(Pb[i], TAM[i]) for i in range(n)]
    PV = [_dot(Pkv[i], chains[i]["V"]) for i in range(n)]
    BG = [_dot_tn(chains[i]["Bg"], TAM[i]) for i in range(n)]
    KV = [_dot_tn(chains[i]["Kg"], chains[i]["V"]) for i in range(n)]
    ys, new_states = [], []
    for i in range(n):
        RA = chains[i]["R"] - PB[i][:, 0:P2]
        G2 = jnp.where(eye, chains[i]["e_tot"], 0.0) - BG[i][:, 0:P2]
        out = _dot(jnp.concatenate([RA, G2], axis=0), states[i])
        Ys = out[0:P2] + PV[i] - PB[i][:, P2:2 * P2]
        ys.append(Ys[0:L] + Ys[L:P2])
        new_states.append(out[P2:2 * P2] + KV[i] - BG[i][:, P2:2 * P2])
    return ys, new_states


def _rwkv_scan_body(rf_ref, vf_ref, khf_ref, rb_ref, vb_ref, khb_ref, lwf_ref, bf_ref, ktf_ref, lwb_ref, bb_ref, ktb_ref,
                    yf_ref, yb_ref, s_ref):
    @pl.when(pl.program_id(1) == 0)
    def _():
        s_ref[...] = jnp.zeros_like(s_ref)

    n_pairs = RWKV_W // LANES
    chains = (_chunk_operands(rf_ref[...], vf_ref[...], khf_ref[...], lwf_ref[...], bf_ref[...], ktf_ref[...], False)
              + _chunk_operands(rb_ref[...], vb_ref[...], khb_ref[...], lwb_ref[...], bb_ref[...], ktb_ref[...], True))
    ys, new_states = _chunks_solve(chains, [s_ref[i] for i in range(2 * n_pairs)])
    for i in range(2 * n_pairs):
        s_ref[i] = new_states[i]
    yf_ref[...] = jnp.concatenate(ys[:n_pairs], axis=1)
    yb_ref[...] = jnp.concatenate(ys[n_pairs:], axis=1)


def _rwkv_scan(r, v, kh, lw, b, kt, n_ctx):
    B, T, W = r.shape
    nc = T // CHUNK
    ncc = n_ctx // CHUNK
    rev = lambda j: jnp.where(j < ncc, ncc - 1 - j, nc + ncc - 1 - j)
    fwd1 = pl.BlockSpec((None, CHUNK, W), lambda bb, j: (bb, j, 0))
    bwd1 = pl.BlockSpec((None, CHUNK, W), lambda bb, j: (bb, rev(j), 0))
    fwd2 = pl.BlockSpec((None, None, CHUNK, W), lambda bb, j: (bb, 0, j, 0))
    bwd2 = pl.BlockSpec((None, None, CHUNK, W), lambda bb, j: (bb, 1, rev(j), 0))
    out = jax.ShapeDtypeStruct((B, T, W), F32)
    return pl.pallas_call(
        _rwkv_scan_body,
        grid=(B, nc),
        in_specs=[fwd1, fwd1, fwd1, bwd1, bwd1, bwd1, fwd2, fwd2, fwd2, bwd2, bwd2, bwd2],
        out_specs=[fwd1, bwd1],
        out_shape=[out, out],
        scratch_shapes=[pltpu.VMEM((2 * W // LANES, LANES, LANES), F32)],
        compiler_params=_cp("parallel", "arbitrary"),
        name="rwkv_scan",
    )(r, v, kh, r, v, kh, lw, b, kt, lw, b, kt)


def _rwkv_readout_body(yf_ref, yb_ref, bon_ref, g_ref, gng_ref, gnb_ref, e_ref, o_ref):
    y = yf_ref[...] + yb_ref[...]
    head_mean = e_ref[...] * (1.0 / RWKV_HEAD)
    mu = _dot_terms(y, head_mean, 2, 1)
    yc = y - mu
    var = _dot_terms(yc * yc, head_mean, 2, 1)
    yn = yc * lax.rsqrt(var + GN_EPS) * gng_ref[...] + gnb_ref[...]
    o_ref[...] = ((yn + bon_ref[...]) * g_ref[...]).astype(o_ref.dtype)


def _rwkv_readout(yf, yb, bonus, g, gn_g, gn_b, head_sum):
    B, T, W = yf.shape
    tt = TOK_TILE
    tok = pl.BlockSpec((None, tt, W), lambda b, i: (b, i, 0))
    row = pl.BlockSpec((1, W), lambda b, i: (0, 0))
    return pl.pallas_call(
        _rwkv_readout_body,
        grid=(B, T // tt),
        in_specs=[tok, tok, tok, tok, row, row, pl.BlockSpec((W, W), lambda b, i: (0, 0))],
        out_specs=tok,
        out_shape=jax.ShapeDtypeStruct((B, T, W), BF16),
        compiler_params=_cp("parallel", "parallel"),
        name="rwkv_readout",
    )(yf, yb, bonus, g, gn_g.reshape(1, W), gn_b.reshape(1, W), head_sum)


def _rope(y, cos, sin_signed, quarter):
    lane = lax.broadcasted_iota(jnp.int32, y.shape, 1)
    first = (lane & (2 * quarter - 1)) < quarter
    partner = jnp.where(first, pltpu.roll(y, LANES - quarter, 1), pltpu.roll(y, quarter, 1))
    return y * cos + partner * sin_signed


def _gqa_prep_body(p_ref, qg_ref, kg_ref, cos_ref, sin_ref, q_o, k_o, v_o):
    cos = cos_ref[...]
    sin = sin_ref[...]
    scale = GQA_HEAD ** -0.5 * LOG2E
    for h in range(GQA_Q_HEADS):
        sl = slice(h * GQA_HEAD, (h + 1) * GQA_HEAD)
        q = _rms(p_ref[:, sl], qg_ref[...])
        q_o[:, sl] = (_rope(q, cos, sin, GQA_HEAD // 4) * scale).astype(q_o.dtype)
    for h in range(GQA_KV_HEADS):
        sl = slice(h * GQA_HEAD, (h + 1) * GQA_HEAD)
        k = _rms(p_ref[:, GQA_Q_COLS + h * GQA_HEAD:GQA_Q_COLS + (h + 1) * GQA_HEAD], kg_ref[...])
        k_o[:, sl] = _rope(k, cos, sin, GQA_HEAD // 4).astype(k_o.dtype)
    v_o[...] = p_ref[:, GQA_Q_COLS + GQA_KV_COLS:GQA_COLS].astype(v_o.dtype)


def _gqa_prep(pg, q_norm_g, k_norm_g, cos, sin):
    B, T, _ = pg.shape
    tt = TOK_TILE
    tok = lambda w: pl.BlockSpec((None, tt, w), lambda b, i: (b, i, 0))
    row = pl.BlockSpec((1, GQA_HEAD), lambda b, i: (0, 0))
    tab = pl.BlockSpec((tt, LANES), lambda b, i: (i, 0))
    return pl.pallas_call(
        _gqa_prep_body,
        grid=(B, T // tt),
        in_specs=[tok(GQA_COLS), row, row, tab, tab],
        out_specs=[tok(GQA_Q_COLS), tok(GQA_KV_COLS), tok(GQA_KV_COLS)],
        out_shape=[jax.ShapeDtypeStruct((B, T, GQA_Q_COLS), BF16),
                   jax.ShapeDtypeStruct((B, T, GQA_KV_COLS), BF16),
                   jax.ShapeDtypeStruct((B, T, GQA_KV_COLS), BF16)],
        compiler_params=_cp("parallel", "parallel"),
        name="gqa_prep",
    )(pg, q_norm_g.reshape(1, -1), k_norm_g.reshape(1, -1), cos, sin)


def _mla_prep_body(p_ref, qg_ref, wq_ref, kvg_ref, wkv_ref, cos_ref, sin_ref, q_o, k_o, v_o):
    cos = cos_ref[...]
    sin = sin_ref[...]
    scale = (MLA_NOPE + MLA_ROPE) ** -0.5 * LOG2E
    cq = _rms(p_ref[:, 0:MLA_Q_RANK], qg_ref[...])
    q = jnp.dot(cq.astype(BF16), wq_ref[...], preferred_element_type=F32) * scale
    ckv = _rms(p_ref[:, MLA_Q_RANK:MLA_Q_RANK + MLA_KV_RANK], kvg_ref[...])
    kv = jnp.dot(ckv.astype(BF16), wkv_ref[...], preferred_element_type=F32)
    kr = _rope(p_ref[:, MLA_Q_RANK + MLA_KV_RANK:MLA_COLS_PAD], cos, sin, MLA_ROPE // 4).astype(k_o.dtype)
    for h in range(MLA_HEADS):
        lo = h * MLA_DK
        q_o[:, lo:lo + LANES] = q[:, lo:lo + LANES].astype(q_o.dtype)
        q_o[:, lo + LANES:lo + MLA_DK] = _rope(q[:, lo + LANES:lo + MLA_DK], cos, sin, MLA_ROPE // 4).astype(q_o.dtype)
        k_o[:, lo:lo + LANES] = kv[:, h * MLA_NOPE:(h + 1) * MLA_NOPE].astype(k_o.dtype)
        k_o[:, lo + LANES:lo + MLA_DK] = kr
    v_o[...] = kv[:, MLA_HEADS * MLA_NOPE:].astype(v_o.dtype)


def _mla_prep(pm, q_norm_g, wq, kv_norm_g, wkv, cos, sin):
    B, T, _ = pm.shape
    tt = TOK_TILE
    tok = lambda w: pl.BlockSpec((None, tt, w), lambda b, i: (b, i, 0))
    full = lambda a: pl.BlockSpec(a.shape, lambda b, i: (0,) * a.ndim)
    tab = pl.BlockSpec((tt, LANES), lambda b, i: (i, 0))
    qg = q_norm_g.reshape(1, -1)
    kvg = kv_norm_g.reshape(1, -1)
    return pl.pallas_call(
        _mla_prep_body,
        grid=(B, T // tt),
        in_specs=[tok(MLA_COLS_PAD), full(qg), full(wq), full(kvg), full(wkv), tab, tab],
        out_specs=[tok(MLA_HEADS * MLA_DK), tok(MLA_HEADS * MLA_DK), tok(MLA_HEADS * MLA_V)],
        out_shape=[jax.ShapeDtypeStruct((B, T, MLA_HEADS * MLA_DK), BF16),
                   jax.ShapeDtypeStruct((B, T, MLA_HEADS * MLA_DK), BF16),
                   jax.ShapeDtypeStruct((B, T, MLA_HEADS * MLA_V), BF16)],
        compiler_params=_cp("parallel", "parallel"),
        name="mla_prep",
    )(pm, qg, wq, kvg, wkv, cos, sin)


def _attn_body(q_ref, k_ref, v_ref, o_ref, *, hq, hkv, dk, dv, n_ctx_tiles, n_ctx, n_all):
    rep = hq // hkv
    tq = q_ref.shape[0]

    def run(nk):
        for g in range(hkv):
            kg = k_ref[0:nk, g * dk:(g + 1) * dk]
            vg = v_ref[0:nk, g * dv:(g + 1) * dv]
            v_aug = jnp.concatenate([vg, jnp.ones_like(vg)], axis=1)
            q = jnp.concatenate([q_ref[:, h * dk:(h + 1) * dk] for h in range(g * rep, (g + 1) * rep)], axis=0)
            s = _dot_nt(q, kg)
            p = jnp.exp2(s - jnp.max(s, axis=-1, keepdims=True))
            o = jnp.dot(p.astype(BF16), v_aug, preferred_element_type=F32)
            o = o[:, 0:dv] / o[:, dv:dv + 1]
            for rr in range(rep):
                h = g * rep + rr
                o_ref[:, h * dv:(h + 1) * dv] = o[rr * tq:(rr + 1) * tq].astype(o_ref.dtype)

    @pl.when(pl.program_id(1) < n_ctx_tiles)
    def _():
        run(n_ctx)

    @pl.when(pl.program_id(1) >= n_ctx_tiles)
    def _():
        run(n_all)


def _attention(q, k, v, hq, hkv, dk, dv, n_ctx):
    B, T, _ = q.shape
    tq = TOK_TILE
    return pl.pallas_call(
        functools.partial(_attn_body, hq=hq, hkv=hkv, dk=dk, dv=dv, n_ctx_tiles=n_ctx // tq, n_ctx=n_ctx, n_all=T),
        grid=(B, T // tq),
        in_specs=[pl.BlockSpec((None, tq, hq * dk), lambda b, i: (b, i, 0)),
                  pl.BlockSpec((None, T, hkv * dk), lambda b, i: (b, 0, 0)),
                  pl.BlockSpec((None, T, hkv * dv), lambda b, i: (b, 0, 0))],
        out_specs=pl.BlockSpec((None, tq, hq * dv), lambda b, i: (b, i, 0)),
        out_shape=jax.ShapeDtypeStruct((B, T, hq * dv), BF16),
        compiler_params=_cp("parallel", "parallel"),
        name="attention",
    )(q, k, v)


def _rope_tables(T, n_ctx, n_rot):
    quarter = n_rot // 4
    t = jnp.arange(T - n_ctx)
    row = (t // GRID_W).astype(F32)
    col = (t % GRID_W).astype(F32)
    inv = ROPE_THETA ** (-jnp.arange(quarter, dtype=F32) / quarter)
    ar = row[:, None] * inv[None, :]
    ac = col[:, None] * inv[None, :]
    pad = LANES - n_rot
    cos = jnp.concatenate([jnp.cos(ar), jnp.cos(ar), jnp.cos(ac), jnp.cos(ac), jnp.ones((T - n_ctx, pad), F32)], axis=1)
    sin = jnp.concatenate([-jnp.sin(ar), jnp.sin(ar), -jnp.sin(ac), jnp.sin(ac), jnp.zeros((T - n_ctx, pad), F32)], axis=1)
    cos = jnp.concatenate([jnp.ones((n_ctx, LANES), F32), cos], axis=0)
    sin = jnp.concatenate([jnp.zeros((n_ctx, LANES), F32), sin], axis=0)
    return cos, sin


def _moe_body(te_ref, nv_ref, hs_ref, w1_ref, w3_ref, w2_ref, y_ref):
    i = pl.program_id(0)

    @pl.when(i < nv_ref[0])
    def _():
        hs = hs_ref[...].astype(BF16)
        a = jnp.dot(hs, w1_ref[...].astype(BF16), preferred_element_type=F32)
        b = jnp.dot(hs, w3_ref[...].astype(BF16), preferred_element_type=F32)
        act = a * _sigmoid(a) * b
        y = jnp.dot(act.astype(BF16), w2_ref[...].astype(BF16), preferred_element_type=F32)
        y_ref[...] = y.astype(y_ref.dtype)

    @pl.when(i >= nv_ref[0])
    def _():
        y_ref[...] = jnp.zeros_like(y_ref)


def _moe_experts(tile_expert, n_valid, hs, w1, w3, w2, layer):
    NP, D = hs.shape
    tm = MOE_TM
    DE = w1.shape[-1]
    grid_spec = pltpu.PrefetchScalarGridSpec(
        num_scalar_prefetch=2,
        grid=(NP // tm,),
        in_specs=[pl.BlockSpec((tm, D), lambda i, te, nv: (i, 0)),
                  pl.BlockSpec((None, None, D, DE), lambda i, te, nv: (layer, te[i], 0, 0)),
                  pl.BlockSpec((None, None, D, DE), lambda i, te, nv: (layer, te[i], 0, 0)),
                  pl.BlockSpec((None, None, DE, D), lambda i, te, nv: (layer, te[i], 0, 0))],
        out_specs=pl.BlockSpec((tm, D), lambda i, te, nv: (i, 0)),
    )
    return pl.pallas_call(
        _moe_body,
        grid_spec=grid_spec,
        out_shape=jax.ShapeDtypeStruct((NP, D), BF16),
        compiler_params=_cp("arbitrary"),
        name="moe_experts",
    )(tile_expert, n_valid, hs, w1, w3, w2)


def _moe(h2, re, w1, w3, w2, layer):
    N = h2.shape[0]
    tm = MOE_TM
    n_tiles = (2 * N) // tm + N_EXPERTS
    e_flat = re.reshape(2 * N)
    onehot = (e_flat[:, None] == jnp.arange(N_EXPERTS, dtype=jnp.int32)[None, :]).astype(jnp.int32)
    csum = jnp.cumsum(onehot, axis=0)
    rank = jnp.sum(onehot * (csum - 1), axis=1)
    counts = csum[-1]
    ptiles = (counts + tm - 1) // tm
    tile_end = jnp.cumsum(ptiles)
    tile_start = tile_end - ptiles
    pos = tile_start[e_flat] * tm + rank
    n_valid = tile_end[-1:].astype(jnp.int32)
    tile_ids = jnp.arange(n_tiles, dtype=jnp.int32)
    tile_expert = jnp.minimum(jnp.sum((tile_end[None, :] <= tile_ids[:, None]).astype(jnp.int32), axis=1), N_EXPERTS - 1)
    spread = jnp.arange(n_tiles * tm, dtype=jnp.int32) % N
    src = spread.at[pos].set(jnp.arange(2 * N, dtype=jnp.int32) // 2)
    rows = lambda a, idx: a.at[idx].get(mode="promise_in_bounds")
    hs = rows(h2, src)
    y = _moe_experts(tile_expert, n_valid, hs, w1, w3, w2, layer)
    pos2 = pos.reshape(N, 2)
    return rows(y, pos2[:, 0]), rows(y, pos2[:, 1])


def kernel(x, c, ctx, c_ctx, mod_w, mod_b, norm1_g, norm2_g, w_in, w_out, shift_prev, shift_next, decay_w0, decay_up, iclr_a0, iclr_up, gate_up, k_k, k_a, r_k, gn_g, gn_b, q_norm_g, k_norm_g, mla_q_norm_g, mla_w_uq, mla_kv_norm_g, mla_w_ukv, router_gw, router_gb, router_ew, router_eb, exp_w1, exp_w3, exp_w2, final_norm_g):
    B, S, D = x.shape
    C = ctx.shape[1]
    T = C + S
    depth = mod_w.shape[0]
    assert C == TOK_TILE and S % TOK_TILE == 0 and B <= CTX_ROW
    n_ctx_tiles = C // TOK_TILE

    X = jnp.concatenate([ctx, x], axis=1)
    cc = jnp.zeros((SUBLANES, D), F32).at[:B].set(c).at[CTX_ROW].set(c_ctx)
    mods_all = _modulation(cc, mod_w, mod_b).reshape(depth, SUBLANES * N_MOD, 1, D)

    cos_g, sin_g = _rope_tables(T, C, GQA_HEAD)
    cos_m, sin_m = _rope_tables(T, C, MLA_ROPE)
    hid = jnp.arange(RWKV_W) // RWKV_HEAD
    head_sum = (hid[:, None] == hid[None, :]).astype(F32)

    for l in range(depth):
        mods = mods_all[l]
        w_r = w_in[l][:, :RWKV_COLS].astype(BF16)
        w_g = w_in[l][:, RWKV_COLS:RWKV_COLS + GQA_COLS].astype(BF16)
        w_m = jnp.pad(w_in[l][:, RWKV_COLS + GQA_COLS:], ((0, 0), (0, MLA_COLS_PAD - MLA_COLS))).astype(BF16)

        if l == 0:
            h = _norm_mod(X, norm1_g[l], mods, 0, 1, n_ctx_tiles)
        h = h.reshape(B * T, D)
        pr = _matmul(h, w_r).reshape(B, T, RWKV_COLS)
        pg = _matmul(h, w_g).reshape(B, T, GQA_COLS)
        pm = _matmul(h, w_m).reshape(B, T, MLA_COLS_PAD)

        wup_pad = jnp.pad(decay_up[l], ((0, 0), (0, ICLR_RANK), (0, 0)))
        aup_pad = jnp.pad(iclr_up[l], ((0, 0), (DECAY_RANK, 0), (0, 0)))
        r, v, kh, lw, b, kt, g, bonus = _rwkv_prep(pr, shift_prev[l], shift_next[l], decay_w0[l], wup_pad,
                                                  iclr_a0[l], aup_pad, gate_up[l], k_k[l], k_a[l], r_k[l], head_sum)
        yf, yb = _rwkv_scan(r, v, kh, lw, b, kt, C)
        o_r = _rwkv_readout(yf, yb, bonus, g, gn_g[l], gn_b[l], head_sum)

        q, k, vv = _gqa_prep(pg, q_norm_g[l], k_norm_g[l], cos_g, sin_g)
        o_g = _attention(q, k, vv, GQA_Q_HEADS, GQA_KV_HEADS, GQA_HEAD, GQA_HEAD, C)

        wq = mla_w_uq[l].reshape(MLA_Q_RANK, MLA_HEADS, MLA_NOPE + MLA_ROPE)
        wq = jnp.pad(wq, ((0, 0), (0, 0), (0, MLA_DK - MLA_NOPE - MLA_ROPE))).reshape(MLA_Q_RANK, MLA_HEADS * MLA_DK)
        wkv = mla_w_ukv[l].reshape(MLA_KV_RANK, MLA_HEADS, MLA_NOPE + MLA_V)
        wkv = jnp.concatenate([wkv[:, :, :MLA_NOPE].reshape(MLA_KV_RANK, -1), wkv[:, :, MLA_NOPE:].reshape(MLA_KV_RANK, -1)], axis=1)
        qm, km, vm = _mla_prep(pm, mla_q_norm_g[l], wq.astype(BF16), mla_kv_norm_g[l], wkv.astype(BF16), cos_m, sin_m)
        o_m = _attention(qm, km, vm, MLA_HEADS, MLA_HEADS, MLA_DK, MLA_V, C)

        o = jnp.concatenate([o_r, o_g, o_m], axis=-1)
        X = _matmul_gated_residual(o, w_out[l].astype(BF16), X, mods, 2, C)

        wr = jnp.pad(jnp.concatenate([router_gw[l], router_ew[l]], axis=1), ((0, 0), (0, LANES - N_GROUPS - N_EXPERTS)))
        br = jnp.pad(jnp.concatenate([router_gb[l], router_eb[l]]), (0, LANES - N_GROUPS - N_EXPERTS)).reshape(1, LANES)
        h2, rw, re = _norm_router(X, norm2_g[l], mods, 3, 4, n_ctx_tiles, wr, br)
        y0, y1 = _moe(h2.reshape(B * T, D), re.reshape(B * T, LANES)[:, :2], exp_w1, exp_w3, exp_w2, l)
        y0 = y0.reshape(B, T, D)
        y1 = y1.reshape(B, T, D)
        if l + 1 < depth:
            X, h = _moe_residual_norm(X, y0, y1, rw, mods, 5, n_ctx_tiles, norm1_g[l + 1], mods_all[l + 1])
        else:
            out = _moe_residual_final(X, y0, y1, rw, mods, 5, n_ctx_tiles, final_norm_g)
    return out
```

```python
import functools
import math

import jax
import jax.numpy as jnp
import numpy as np
from jax import lax
from jax.experimental import pallas as pl
from jax.experimental.pallas import tpu as pltpu

F32 = jnp.float32
BF16 = jnp.bfloat16
HI = lax.Precision.HIGHEST

V7X_VMEM_BYTES = 64 * 1024 * 1024
VMEM_LIMIT = V7X_VMEM_BYTES - 8 * 1024 * 1024
LANES = 128
SUBLANES = 8

GRID_W = 64
ROPE_THETA = 10000.0
NORM_EPS = 1e-6
GN_EPS = 64e-5
DECAY_SCALE = math.exp(-0.5)
LOG2E = math.log2(math.e)

RWKV_HEAD = 64
RWKV_W = 512
DECAY_RANK = 64
ICLR_RANK = 64
GATE_RANK = 128
RWKV_COLS = 3 * RWKV_W + DECAY_RANK + ICLR_RANK + GATE_RANK
LOWRANK_OFF = 3 * RWKV_W
GATE_OFF = LOWRANK_OFF + DECAY_RANK + ICLR_RANK
CHUNK = 64

GQA_HEAD = 128
GQA_Q_HEADS = 8
GQA_KV_HEADS = 2
GQA_Q_COLS = GQA_Q_HEADS * GQA_HEAD
GQA_KV_COLS = GQA_KV_HEADS * GQA_HEAD
GQA_COLS = GQA_Q_COLS + 2 * GQA_KV_COLS

MLA_HEADS = 4
MLA_NOPE = 128
MLA_ROPE = 64
MLA_V = 128
MLA_Q_RANK = 384
MLA_KV_RANK = 256
MLA_COLS = MLA_Q_RANK + MLA_KV_RANK + MLA_ROPE
MLA_COLS_PAD = 768
MLA_DK = 2 * LANES

N_GROUPS = 4
EXPERTS_PER_GROUP = 8
N_EXPERTS = 32
D_EXPERT = 256
MOE_TM = 256

TOK_TILE = 256
PROJ_TILE = 768
N_MOD = 6
CTX_ROW = 4


def _cp(*sem):
    return pltpu.CompilerParams(dimension_semantics=sem, vmem_limit_bytes=VMEM_LIMIT)


def _sigmoid(x):
    return 1.0 / (1.0 + jnp.exp(-x))


def _bf16_terms(x, n):
    terms = []
    for _ in range(n):
        t = x.astype(BF16)
        terms.append(t)
        x = x - t.astype(F32)
    return terms


def _dot_terms(x, w, nx, nw):
    xs = _bf16_terms(x, nx)
    ws = _bf16_terms(w, nw)
    acc = None
    for i in range(nx):
        for j in range(nw):
            if i + j < max(nx, nw):
                p = jnp.dot(xs[i], ws[j], preferred_element_type=F32)
                acc = p if acc is None else acc + p
    return acc


def _mod_body(c_ref, w_ref, b_ref, o_ref):
    c = c_ref[...]
    s = c * _sigmoid(c)
    o_ref[...] = jnp.dot(s, w_ref[...], preferred_element_type=F32, precision=HI) + b_ref[...]


def _modulation(cc, mod_w, mod_b):
    L, D, N = mod_w.shape
    tn = 1024
    return pl.pallas_call(
        _mod_body,
        grid=(L, N // tn),
        in_specs=[pl.BlockSpec((SUBLANES, D), lambda l, j: (0, 0)),
                  pl.BlockSpec((None, D, tn), lambda l, j: (l, 0, j)),
                  pl.BlockSpec((None, 1, tn), lambda l, j: (l, 0, j))],
        out_specs=pl.BlockSpec((None, SUBLANES, tn), lambda l, j: (l, 0, j)),
        out_shape=jax.ShapeDtypeStruct((L, SUBLANES, N), F32),
        compiler_params=_cp("parallel", "parallel"),
        name="modulation",
    )(cc, mod_w, mod_b.reshape(L, 1, N))


def _mod_spec(which, n_ctx_tiles, D):
    return pl.BlockSpec((None, 1, D), lambda b, i: (jnp.where(i < n_ctx_tiles, CTX_ROW, b) * N_MOD + which, 0, 0))


def _rms(x, g):
    return x * lax.rsqrt(jnp.mean(x * x, axis=-1, keepdims=True) + NORM_EPS) * g


def _norm_mod_body(x_ref, g_ref, sh_ref, sc_ref, o_ref):
    y = _rms(x_ref[...], g_ref[...])
    o_ref[...] = (y * (1.0 + sc_ref[...]) + sh_ref[...]).astype(o_ref.dtype)


def _route(logits):
    lane = lax.broadcasted_iota(jnp.int32, logits.shape, 1)
    lane_f = lane.astype(F32)
    neg = jnp.float32(-1e30)
    far = jnp.float32(1e9)
    first_at = lambda hit: jnp.min(jnp.where(hit, lane_f, far), axis=-1, keepdims=True).astype(jnp.int32)
    gl = jnp.where(lane < N_GROUPS, logits, neg)
    gmax = jnp.max(gl, axis=-1, keepdims=True)
    gidx = first_at(gl == gmax)
    p_sel = 1.0 / jnp.sum(jnp.exp(gl - gmax), axis=-1, keepdims=True)
    lo = N_GROUPS + gidx * EXPERTS_PER_GROUP
    el = jnp.where((lane >= lo) & (lane < lo + EXPERTS_PER_GROUP), logits, neg)
    m1 = jnp.max(el, axis=-1, keepdims=True)
    i1 = first_at(el == m1)
    el2 = jnp.where(lane == i1, neg, el)
    m2 = jnp.max(el2, axis=-1, keepdims=True)
    i2 = first_at(el2 == m2)
    t = jnp.exp(m2 - m1)
    w1 = p_sel / (1.0 + t)
    w2 = p_sel * t / (1.0 + t)
    rw = jnp.where(lane == 0, w1, jnp.where(lane == 1, w2, 0.0))
    re = jnp.where(lane == 0, i1 - N_GROUPS, jnp.where(lane == 1, i2 - N_GROUPS, 0))
    return rw, re


def _norm_router_body(x_ref, g_ref, sh_ref, sc_ref, wr_ref, br_ref, h_ref, rw_ref, re_ref):
    y = _rms(x_ref[...], g_ref[...])
    h = y * (1.0 + sc_ref[...]) + sh_ref[...]
    h_ref[...] = h.astype(h_ref.dtype)
    logits = _dot_terms(h, wr_ref[...], 2, 2) + br_ref[...]
    rw, re = _route(logits)
    rw_ref[...] = rw
    re_ref[...] = re


def _norm_mod(X, g, mods, shift_i, scale_i, n_ctx_tiles):
    B, T, D = X.shape
    tt = TOK_TILE
    return pl.pallas_call(
        _norm_mod_body,
        grid=(B, T // tt),
        in_specs=[pl.BlockSpec((None, tt, D), lambda b, i: (b, i, 0)),
                  pl.BlockSpec((1, D), lambda b, i: (0, 0)),
                  _mod_spec(shift_i, n_ctx_tiles, D),
                  _mod_spec(scale_i, n_ctx_tiles, D)],
        out_specs=pl.BlockSpec((None, tt, D), lambda b, i: (b, i, 0)),
        out_shape=jax.ShapeDtypeStruct((B, T, D), BF16),
        compiler_params=_cp("parallel", "parallel"),
        name="norm_mod",
    )(X, g.reshape(1, D), mods, mods)


def _norm_router(X, g, mods, shift_i, scale_i, n_ctx_tiles, wr, br):
    B, T, D = X.shape
    tt = TOK_TILE
    tok = lambda w, dt: (pl.BlockSpec((None, tt, w), lambda b, i: (b, i, 0)), jax.ShapeDtypeStruct((B, T, w), dt))
    outs = [tok(D, F32), tok(LANES, F32), tok(LANES, jnp.int32)]
    return pl.pallas_call(
        _norm_router_body,
        grid=(B, T // tt),
        in_specs=[pl.BlockSpec((None, tt, D), lambda b, i: (b, i, 0)),
                  pl.BlockSpec((1, D), lambda b, i: (0, 0)),
                  _mod_spec(shift_i, n_ctx_tiles, D),
                  _mod_spec(scale_i, n_ctx_tiles, D),
                  pl.BlockSpec((D, LANES), lambda b, i: (0, 0)),
                  pl.BlockSpec((1, LANES), lambda b, i: (0, 0))],
        out_specs=[o[0] for o in outs],
        out_shape=[o[1] for o in outs],
        compiler_params=_cp("parallel", "parallel"),
        name="norm_router",
    )(X, g.reshape(1, D), mods, mods, wr, br)


def _mm_body(a_ref, w_ref, o_ref):
    o_ref[...] = jnp.dot(a_ref[...], w_ref[...], preferred_element_type=F32).astype(o_ref.dtype)


def _matmul(a, w, tm=512):
    M, K = a.shape
    N = w.shape[1]
    return pl.pallas_call(
        _mm_body,
        grid=(M // tm,),
        in_specs=[pl.BlockSpec((tm, K), lambda i: (i, 0)),
                  pl.BlockSpec((K, N), lambda i: (0, 0))],
        out_specs=pl.BlockSpec((tm, N), lambda i: (i, 0)),
        out_shape=jax.ShapeDtypeStruct((M, N), F32),
        compiler_params=_cp("parallel"),
        name="token_matmul",
    )(a, w)


def _mm_res_body(*refs, n_parts, n_ctx, tm):
    a_refs = refs[:n_parts]
    w_ref, x_ref, gl_ref, gc_ref, o_ref = refs[n_parts:]
    acc = None
    k0 = 0
    for a_ref in a_refs:
        k1 = k0 + a_ref.shape[-1]
        part = jnp.dot(a_ref[...], w_ref[k0:k1, :], preferred_element_type=F32)
        acc = part if acc is None else acc + part
        k0 = k1
    row = pl.program_id(1) * tm + lax.broadcasted_iota(jnp.int32, (tm, 1), 0)
    gate = jnp.where(row < n_ctx, gc_ref[...], gl_ref[...])
    o_ref[...] = x_ref[...] + gate * acc


def _matmul_gated_residual(parts, w, X, mods, gate_i, n_ctx, tm=384):
    B, T, D = X.shape
    assert T % tm == 0 and sum(p.shape[-1] for p in parts) == w.shape[0]
    return pl.pallas_call(
        functools.partial(_mm_res_body, n_parts=len(parts), n_ctx=n_ctx, tm=tm),
        grid=(B, T // tm),
        in_specs=[pl.BlockSpec((None, tm, p.shape[-1]), lambda b, i: (b, i, 0)) for p in parts] + [
                  pl.BlockSpec(w.shape, lambda b, i: (0, 0)),
                  pl.BlockSpec((None, tm, D), lambda b, i: (b, i, 0)),
                  pl.BlockSpec((None, 1, D), lambda b, i: (b * N_MOD + gate_i, 0, 0)),
                  pl.BlockSpec((None, 1, D), lambda b, i: (CTX_ROW * N_MOD + gate_i, 0, 0))],
        out_specs=pl.BlockSpec((None, tm, D), lambda b, i: (b, i, 0)),
        out_shape=jax.ShapeDtypeStruct((B, T, D), F32),
        compiler_params=_cp("parallel", "parallel"),
        name="out_proj_residual",
    )(*parts, w, X, mods, mods)


def _moe_residual(x_ref, y0_ref, y1_ref, rw_ref, gate_ref):
    rw = rw_ref[...]
    moe = rw[:, 0:1] * y0_ref[...].astype(F32) + rw[:, 1:2] * y1_ref[...].astype(F32)
    return x_ref[...] + gate_ref[...] * moe


def _moe_residual_norm_body(x_ref, y0_ref, y1_ref, rw_ref, gate_ref, g_ref, sh_ref, sc_ref, x_o, h_o):
    x = _moe_residual(x_ref, y0_ref, y1_ref, rw_ref, gate_ref)
    x_o[...] = x
    h_o[...] = (_rms(x, g_ref[...]) * (1.0 + sc_ref[...]) + sh_ref[...]).astype(h_o.dtype)


def _moe_residual_final_body(x_ref, y0_ref, y1_ref, rw_ref, gate_ref, g_ref, o_ref):
    o_ref[...] = _rms(_moe_residual(x_ref, y0_ref, y1_ref, rw_ref, gate_ref), g_ref[...])


def _moe_residual_norm(X, Y0, Y1, rw, mods, gate_i, n_ctx_tiles, g_next, mods_next):
    B, T, D = X.shape
    tt = TOK_TILE
    blk = pl.BlockSpec((None, tt, D), lambda b, i: (b, i, 0))
    return pl.pallas_call(
        _moe_residual_norm_body,
        grid=(B, T // tt),
        in_specs=[blk, blk, blk, pl.BlockSpec((None, tt, LANES), lambda b, i: (b, i, 0)),
                  _mod_spec(gate_i, n_ctx_tiles, D), pl.BlockSpec((1, D), lambda b, i: (0, 0)),
                  _mod_spec(0, n_ctx_tiles, D), _mod_spec(1, n_ctx_tiles, D)],
        out_specs=[blk, blk],
        out_shape=[jax.ShapeDtypeStruct((B, T, D), F32), jax.ShapeDtypeStruct((B, T, D), BF16)],
        compiler_params=_cp("parallel", "parallel"),
        name="moe_residual_norm",
    )(X, Y0, Y1, rw, mods, g_next.reshape(1, D), mods_next, mods_next)


def _moe_residual_final(X, Y0, Y1, rw, mods, gate_i, n_ctx_tiles, g_final):
    B, T, D = X.shape
    tt = TOK_TILE
    S = T - n_ctx_tiles * tt
    lat = lambda w: pl.BlockSpec((None, tt, w), lambda b, i: (b, i + n_ctx_tiles, 0))
    return pl.pallas_call(
        _moe_residual_final_body,
        grid=(B, S // tt),
        in_specs=[lat(D), lat(D), lat(D), lat(LANES),
                  pl.BlockSpec((None, 1, D), lambda b, i: (b * N_MOD + gate_i, 0, 0)),
                  pl.BlockSpec((1, D), lambda b, i: (0, 0))],
        out_specs=pl.BlockSpec((None, tt, D), lambda b, i: (b, i, 0)),
        out_shape=jax.ShapeDtypeStruct((B, S, D), F32),
        compiler_params=_cp("parallel", "parallel"),
        name="moe_residual_final",
    )(X, Y0, Y1, rw, mods, g_final.reshape(1, D))


def _rwkv_prep_body(p_ref, pv_ref, nx_ref, mup_ref, mun_ref, w0_ref, wup_ref, a0_ref, aup_ref, gup_ref,
                    kk_ref, ka_ref, rk_ref, e_ref,
                    r_o, v_o, kh_o, lw_o, b_o, kt_o, g_o, bon_o, *, n_tiles, tt):
    i = pl.program_id(1)
    p = p_ref[...]
    seq_first = i <= 1
    seq_last = (i == 0) | (i == n_tiles - 1)
    prow = jnp.where(seq_first, 0.0, pv_ref[SUBLANES - 1:SUBLANES, :])
    nrow = jnp.where(seq_last, 0.0, nx_ref[0:1, :])
    rid = lax.broadcasted_iota(jnp.int32, (tt, 1), 0)
    prev = jnp.where(rid == 0, prow, pltpu.roll(p, 1, 0))
    nxt = jnp.where(rid == tt - 1, nrow, pltpu.roll(p, tt - 1, 0))
    z = p + mup_ref[...] * (prev - p) + mun_ref[...] * (nxt - p)

    W = RWKV_W
    r = z[:, 0:W]
    k = z[:, W:2 * W]
    v = z[:, 2 * W:3 * W]
    lowrank = z[:, LOWRANK_OFF:LOWRANK_OFF + LANES]
    gd = z[:, GATE_OFF:GATE_OFF + GATE_RANK]
    head_sum = e_ref[...]

    kap = k * kk_ref[...]
    ss = _dot_terms(kap * kap, head_sum, 2, 1)
    khat = kap * lax.rsqrt(ss + 1e-12)
    wd_t = jnp.tanh(lowrank)
    g_o[...] = _dot_terms(_sigmoid(gd), gup_ref[...], 1, 1)
    r_o[...] = r
    v_o[...] = v
    kh_o[...] = khat
    kt_sum = None
    for d in range(2):
        dec = _dot_terms(wd_t, wup_ref[d], 2, 2)
        lw_o[d] = -DECAY_SCALE * _sigmoid(w0_ref[d:d + 1, :] + dec)
        a = _sigmoid(a0_ref[d:d + 1, :] + _dot_terms(lowrank, aup_ref[d], 1, 1))
        kt = k * (1.0 + (a - 1.0) * ka_ref[...])
        kt_o[d] = kt
        b_o[d] = a * khat
        kt_sum = kt if kt_sum is None else kt_sum + kt
    bsum = _dot_terms(r * kt_sum * rk_ref[...], head_sum, 2, 1)
    bon_o[...] = bsum * v


def _rwkv_prep(pr, mu_prev, mu_next, w0, wup_pad, a0, aup_pad, g_up, k_k, k_a, r_k, head_sum):
    B, T, _ = pr.shape
    tt = TOK_TILE
    W = RWKV_W
    n_tiles = T // tt
    n8 = tt // SUBLANES
    row = lambda v: v.reshape(1, -1)
    full = lambda a: pl.BlockSpec(a.shape, lambda b, i: (0,) * a.ndim)
    tok = pl.BlockSpec((None, tt, W), lambda b, i: (b, i, 0))
    tok2 = pl.BlockSpec((None, 2, tt, W), lambda b, i: (b, 0, i, 0))
    s1 = jax.ShapeDtypeStruct((B, T, W), F32)
    s2 = jax.ShapeDtypeStruct((B, 2, T, W), F32)
    consts = [row(mu_prev), row(mu_next), w0, wup_pad, a0, aup_pad, g_up, row(k_k), row(k_a), row(r_k), head_sum]
    return pl.pallas_call(
        functools.partial(_rwkv_prep_body, n_tiles=n_tiles, tt=tt),
        grid=(B, n_tiles),
        in_specs=[pl.BlockSpec((None, tt, RWKV_COLS), lambda b, i: (b, i, 0)),
                  pl.BlockSpec((None, SUBLANES, RWKV_COLS), lambda b, i: (b, jnp.maximum(i * n8 - 1, 0), 0)),
                  pl.BlockSpec((None, SUBLANES, RWKV_COLS), lambda b, i: (b, jnp.minimum((i + 1) * n8, T // SUBLANES - 1), 0)),
                  ] + [full(a) for a in consts],
        out_specs=[tok, tok, tok, tok2, tok2, tok2, tok, tok],
        out_shape=[s1, s1, s1, s2, s2, s2, s1, s1],
        compiler_params=_cp("parallel", "parallel"),
        name="rwkv_prep",
    )(pr, pr, pr, *consts)


def _stack_heads(x):
    lane = lax.broadcasted_iota(jnp.int32, x.shape, 1)
    first = lane < RWKV_HEAD
    return jnp.concatenate([jnp.where(first, x, 0.0), jnp.where(first, 0.0, x)], axis=0)


def _dot(a, b):
    return jnp.dot(a.astype(BF16), b.astype(BF16), preferred_element_type=F32)


def _dot_nt(a, b):
    return lax.dot_general(a.astype(BF16), b.astype(BF16), (((1,), (1,)), ((), ())), preferred_element_type=F32)


def _dot_tn(a, b):
    return jnp.dot(a.T.astype(BF16), b.astype(BF16), preferred_element_type=F32)


def _chunk_operands(r, v, kh, lw, b, kt, reverse):
    L = CHUNK
    ti = lax.broadcasted_iota(jnp.int32, (L, L), 0)
    tj = lax.broadcasted_iota(jnp.int32, (L, L), 1)
    tri = jnp.where((ti <= tj) if reverse else (ti >= tj), 1.0, 0.0)
    lam = _dot_terms(tri, lw, 1, 3)
    tot = lam[0:1, :] if reverse else lam[L - 1:L, :]
    e_n = jnp.exp(-lam)
    e_g = jnp.exp(tot - lam)
    full = dict(A=kh * jnp.exp(lam - lw), R=r * jnp.exp(lam), Kn=kt * e_n, Bn=b * e_n, Kg=kt * e_g, Bg=b * e_g, V=v)
    e_tot = jnp.exp(tot)
    pairs = []
    for p in range(RWKV_W // LANES):
        sl = slice(p * LANES, (p + 1) * LANES)
        ops = {k: _stack_heads(a[:, sl]) for k, a in full.items()}
        ops["e_tot"] = e_tot[:, sl]
        ops["reverse"] = reverse
        pairs.append(ops)
    return pairs


def _chunk_masks(reverse):
    L = CHUNK
    si = lax.broadcasted_iota(jnp.int32, (2 * L, 2 * L), 0)
    sj = lax.broadcasted_iota(jnp.int32, (2 * L, 2 * L), 1)
    same = (si >= L) == (sj >= L)
    before = (si < sj) if reverse else (si > sj)
    return same & before, same & (before | (si == sj)), si == sj


def _chunks_solve(chains, states):
    L = CHUNK
    P2 = 2 * L
    n = len(chains)
    masks = {rev: _chunk_masks(rev) for rev in {c["reverse"] for c in chains}}
    strict = [masks[c["reverse"]][0] for c in chains]
    incl = [masks[c["reverse"]][1] for c in chains]
    eye = masks[chains[0]["reverse"]][2]

    big = [_dot_nt(jnp.concatenate([c["A"], c["R"]], axis=0), jnp.concatenate([c["Bn"], c["Kn"]], axis=0)) for c in chains]
    Mb = [jnp.where(strict[i], big[i][0:P2, 0:P2], 0.0) for i in range(n)]
    Mkv = [jnp.where(strict[i], big[i][0:P2, P2:2 * P2], 0.0) for i in range(n)]
    Pb = [jnp.where(incl[i], big[i][P2:2 * P2, 0:P2], 0.0) for i in range(n)]
    Pkv = [jnp.where(incl[i], big[i][P2:2 * P2, P2:2 * P2], 0.0) for i in range(n)]

    Pw = [-m for m in Mb]
    Tm = [jnp.where(eye, 1.0, 0.0) + p for p in Pw]
    Pw = [_dot(p, p) for p in Pw]
    for _ in range(int(math.log2(L)) - 2):
        PT = [_dot(p, jnp.concatenate([p, t], axis=1)) for p, t in zip(Pw, Tm)]
        Tm = [t + pt[:, P2:2 * P2] for t, pt in zip(Tm, PT)]
        Pw = [pt[:, 0:P2] for pt in PT]
    Tm = [t + _dot(p, t) for t, p in zip(Tm, Pw)]

    MV = [_dot(Mkv[i], chains[i]["V"]) for i in range(n)]
    TAM = [_dot(Tm[i], jnp.concatenate([chains[i]["A"], MV[i]], axis=1)) for i in range(n)]
    PB = [_dot(Pb[i], TAM[i]) for i in range(n)]
    PV = [_dot(Pkv[i], chains[i]["V"]) for i in range(n)]
    BG = [_dot_tn(chains[i]["Bg"], TAM[i]) for i in range(n)]
    KV = [_dot_tn(chains[i]["Kg"], chains[i]["V"]) for i in range(n)]
    ys, new_states = [], []
    for i in range(n):
        RA = chains[i]["R"] - PB[i][:, 0:P2]
        G2 = jnp.where(eye, chains[i]["e_tot"], 0.0) - BG[i][:, 0:P2]
        out = _dot(jnp.concatenate([RA, G2], axis=0), states[i])
        Ys = out[0:P2] + PV[i] - PB[i][:, P2:2 * P2]
        ys.append(Ys[0:L] + Ys[L:P2])
        new_states.append(out[P2:2 * P2] + KV[i] - BG[i][:, P2:2 * P2])
    return ys, new_states


def _rwkv_scan_body(rf_ref, vf_ref, khf_ref, rb_ref, vb_ref, khb_ref, lwf_ref, bf_ref, ktf_ref, lwb_ref, bb_ref, ktb_ref,
                    yf_ref, yb_ref, s_ref):
    @pl.when(pl.program_id(1) == 0)
    def _():
        s_ref[...] = jnp.zeros_like(s_ref)

    n_pairs = RWKV_W // LANES
    chains = (_chunk_operands(rf_ref[...], vf_ref[...], khf_ref[...], lwf_ref[...], bf_ref[...], ktf_ref[...], False)
              + _chunk_operands(rb_ref[...], vb_ref[...], khb_ref[...], lwb_ref[...], bb_ref[...], ktb_ref[...], True))
    ys, new_states = _chunks_solve(chains, [s_ref[i] for i in range(2 * n_pairs)])
    for i in range(2 * n_pairs):
        s_ref[i] = new_states[i]
    yf_ref[...] = jnp.concatenate(ys[:n_pairs], axis=1)
    yb_ref[...] = jnp.concatenate(ys[n_pairs:], axis=1)


def _rwkv_scan(r, v, kh, lw, b, kt, n_ctx):
    B, T, W = r.shape
    nc = T // CHUNK
    ncc = n_ctx // CHUNK
    rev = lambda j: jnp.where(j < ncc, ncc - 1 - j, nc + ncc - 1 - j)
    fwd1 = pl.BlockSpec((None, CHUNK, W), lambda bb, j: (bb, j, 0))
    bwd1 = pl.BlockSpec((None, CHUNK, W), lambda bb, j: (bb, rev(j), 0))
    fwd2 = pl.BlockSpec((None, None, CHUNK, W), lambda bb, j: (bb, 0, j, 0))
    bwd2 = pl.BlockSpec((None, None, CHUNK, W), lambda bb, j: (bb, 1, rev(j), 0))
    out = jax.ShapeDtypeStruct((B, T, W), F32)
    return pl.pallas_call(
        _rwkv_scan_body,
        grid=(B, nc),
        in_specs=[fwd1, fwd1, fwd1, bwd1, bwd1, bwd1, fwd2, fwd2, fwd2, bwd2, bwd2, bwd2],
        out_specs=[fwd1, bwd1],
        out_shape=[out, out],
        scratch_shapes=[pltpu.VMEM((2 * W // LANES, LANES, LANES), F32)],
        compiler_params=_cp("parallel", "arbitrary"),
        name="rwkv_scan",
    )(r, v, kh, r, v, kh, lw, b, kt, lw, b, kt)


def _rwkv_readout_body(yf_ref, yb_ref, bon_ref, g_ref, gng_ref, gnb_ref, e_ref, o_ref):
    y = yf_ref[...] + yb_ref[...]
    head_mean = e_ref[...] * (1.0 / RWKV_HEAD)
    mu = _dot_terms(y, head_mean, 2, 1)
    yc = y - mu
    var = _dot_terms(yc * yc, head_mean, 2, 1)
    yn = yc * lax.rsqrt(var + GN_EPS) * gng_ref[...] + gnb_ref[...]
    o_ref[...] = ((yn + bon_ref[...]) * g_ref[...]).astype(o_ref.dtype)


def _rwkv_readout(yf, yb, bonus, g, gn_g, gn_b, head_sum):
    B, T, W = yf.shape
    tt = TOK_TILE
    tok = pl.BlockSpec((None, tt, W), lambda b, i: (b, i, 0))
    row = pl.BlockSpec((1, W), lambda b, i: (0, 0))
    return pl.pallas_call(
        _rwkv_readout_body,
        grid=(B, T // tt),
        in_specs=[tok, tok, tok, tok, row, row, pl.BlockSpec((W, W), lambda b, i: (0, 0))],
        out_specs=tok,
        out_shape=jax.ShapeDtypeStruct((B, T, W), BF16),
        compiler_params=_cp("parallel", "parallel"),
        name="rwkv_readout",
    )(yf, yb, bonus, g, gn_g.reshape(1, W), gn_b.reshape(1, W), head_sum)


def _rope(y, cos, sin_signed, quarter):
    lane = lax.broadcasted_iota(jnp.int32, y.shape, 1)
    first = (lane & (2 * quarter - 1)) < quarter
    partner = jnp.where(first, pltpu.roll(y, LANES - quarter, 1), pltpu.roll(y, quarter, 1))
    return y * cos + partner * sin_signed


def _gqa_prep_body(p_ref, qg_ref, kg_ref, cos_ref, sin_ref, q_o, k_o, v_o):
    cos = cos_ref[...]
    sin = sin_ref[...]
    scale = GQA_HEAD ** -0.5 * LOG2E
    for h in range(GQA_Q_HEADS):
        sl = slice(h * GQA_HEAD, (h + 1) * GQA_HEAD)
        q = _rms(p_ref[:, sl], qg_ref[...])
        q_o[:, sl] = (_rope(q, cos, sin, GQA_HEAD // 4) * scale).astype(q_o.dtype)
    for h in range(GQA_KV_HEADS):
        sl = slice(h * GQA_HEAD, (h + 1) * GQA_HEAD)
        k = _rms(p_ref[:, GQA_Q_COLS + h * GQA_HEAD:GQA_Q_COLS + (h + 1) * GQA_HEAD], kg_ref[...])
        k_o[:, sl] = _rope(k, cos, sin, GQA_HEAD // 4).astype(k_o.dtype)
    v_o[...] = p_ref[:, GQA_Q_COLS + GQA_KV_COLS:GQA_COLS].astype(v_o.dtype)


def _gqa_proj_body(h_ref, w_ref, qg_ref, kg_ref, cos_ref, sin_ref, q_o, k_o, v_o):
    p = jnp.dot(h_ref[...], w_ref[...], preferred_element_type=F32)
    _gqa_prep_body(p, qg_ref, kg_ref, cos_ref, sin_ref, q_o, k_o, v_o)


def _gqa_proj(h, w, q_norm_g, k_norm_g, cos, sin):
    B, T, D = h.shape
    tt = PROJ_TILE
    tok = lambda w_: pl.BlockSpec((None, tt, w_), lambda b, i: (b, i, 0))
    row = pl.BlockSpec((1, GQA_HEAD), lambda b, i: (0, 0))
    tab = pl.BlockSpec((tt, LANES), lambda b, i: (i, 0))
    return pl.pallas_call(
        _gqa_proj_body,
        grid=(B, T // tt),
        in_specs=[tok(D), pl.BlockSpec(w.shape, lambda b, i: (0, 0)), row, row, tab, tab],
        out_specs=[tok(GQA_Q_COLS), tok(GQA_KV_COLS), tok(GQA_KV_COLS)],
        out_shape=[jax.ShapeDtypeStruct((B, T, GQA_Q_COLS), BF16),
                   jax.ShapeDtypeStruct((B, T, GQA_KV_COLS), BF16),
                   jax.ShapeDtypeStruct((B, T, GQA_KV_COLS), BF16)],
        compiler_params=_cp("parallel", "parallel"),
        name="gqa_proj",
    )(h, w, q_norm_g.reshape(1, -1), k_norm_g.reshape(1, -1), cos, sin)


def _mla_prep_body(p_ref, qg_ref, wq_ref, kvg_ref, wkv_ref, cos_ref, sin_ref, q_o, k_o, v_o):
    cos = cos_ref[...]
    sin = sin_ref[...]
    scale = (MLA_NOPE + MLA_ROPE) ** -0.5 * LOG2E
    cq = _rms(p_ref[:, 0:MLA_Q_RANK], qg_ref[...])
    q = jnp.dot(cq.astype(BF16), wq_ref[...], preferred_element_type=F32) * scale
    ckv = _rms(p_ref[:, MLA_Q_RANK:MLA_Q_RANK + MLA_KV_RANK], kvg_ref[...])
    kv = jnp.dot(ckv.astype(BF16), wkv_ref[...], preferred_element_type=F32)
    kr = _rope(p_ref[:, MLA_Q_RANK + MLA_KV_RANK:MLA_COLS_PAD], cos, sin, MLA_ROPE // 4).astype(k_o.dtype)
    for h in range(MLA_HEADS):
        lo = h * MLA_DK
        q_o[:, lo:lo + LANES] = q[:, lo:lo + LANES].astype(q_o.dtype)
        q_o[:, lo + LANES:lo + MLA_DK] = _rope(q[:, lo + LANES:lo + MLA_DK], cos, sin, MLA_ROPE // 4).astype(q_o.dtype)
        k_o[:, lo:lo + LANES] = kv[:, h * MLA_NOPE:(h + 1) * MLA_NOPE].astype(k_o.dtype)
        k_o[:, lo + LANES:lo + MLA_DK] = kr
    v_o[...] = kv[:, MLA_HEADS * MLA_NOPE:].astype(v_o.dtype)


def _mla_proj_body(h_ref, w_ref, qg_ref, wq_ref, kvg_ref, wkv_ref, cos_ref, sin_ref, q_o, k_o, v_o):
    p = jnp.dot(h_ref[...], w_ref[...], preferred_element_type=F32)
    _mla_prep_body(p, qg_ref, wq_ref, kvg_ref, wkv_ref, cos_ref, sin_ref, q_o, k_o, v_o)


def _mla_proj(h, w, q_norm_g, wq, kv_norm_g, wkv, cos, sin):
    B, T, D = h.shape
    tt = PROJ_TILE
    tok = lambda w_: pl.BlockSpec((None, tt, w_), lambda b, i: (b, i, 0))
    full = lambda a: pl.BlockSpec(a.shape, lambda b, i: (0,) * a.ndim)
    tab = pl.BlockSpec((tt, LANES), lambda b, i: (i, 0))
    qg = q_norm_g.reshape(1, -1)
    kvg = kv_norm_g.reshape(1, -1)
    return pl.pallas_call(
        _mla_proj_body,
        grid=(B, T // tt),
        in_specs=[tok(D), full(w), full(qg), full(wq), full(kvg), full(wkv), tab, tab],
        out_specs=[tok(MLA_HEADS * MLA_DK), tok(MLA_HEADS * MLA_DK), tok(MLA_HEADS * MLA_V)],
        out_shape=[jax.ShapeDtypeStruct((B, T, MLA_HEADS * MLA_DK), BF16),
                   jax.ShapeDtypeStruct((B, T, MLA_HEADS * MLA_DK), BF16),
                   jax.ShapeDtypeStruct((B, T, MLA_HEADS * MLA_V), BF16)],
        compiler_params=_cp("parallel", "parallel"),
        name="mla_proj",
    )(h, w, qg, wq, kvg, wkv, cos, sin)


def _attn_body(q_ref, k_ref, v_ref, o_ref, *, hq, hkv, dk, dv, n_ctx_tiles, n_ctx, n_all):
    rep = hq // hkv
    tq = q_ref.shape[0]

    def run(nk):
        for g in range(hkv):
            kg = k_ref[0:nk, g * dk:(g + 1) * dk]
            vg = v_ref[0:nk, g * dv:(g + 1) * dv]
            v_aug = jnp.concatenate([vg, jnp.ones_like(vg)], axis=1)
            q = jnp.concatenate([q_ref[:, h * dk:(h + 1) * dk] for h in range(g * rep, (g + 1) * rep)], axis=0)
            s = _dot_nt(q, kg)
            p = jnp.exp2(s - jnp.max(s, axis=-1, keepdims=True))
            o = jnp.dot(p.astype(BF16), v_aug, preferred_element_type=F32)
            o = o[:, 0:dv] / o[:, dv:dv + 1]
            for rr in range(rep):
                h = g * rep + rr
                o_ref[:, h * dv:(h + 1) * dv] = o[rr * tq:(rr + 1) * tq].astype(o_ref.dtype)

    @pl.when(pl.program_id(1) < n_ctx_tiles)
    def _():
        run(n_ctx)

    @pl.when(pl.program_id(1) >= n_ctx_tiles)
    def _():
        run(n_all)


def _attention(q, k, v, hq, hkv, dk, dv, n_ctx):
    B, T, _ = q.shape
    tq = TOK_TILE
    return pl.pallas_call(
        functools.partial(_attn_body, hq=hq, hkv=hkv, dk=dk, dv=dv, n_ctx_tiles=n_ctx // tq, n_ctx=n_ctx, n_all=T),
        grid=(B, T // tq),
        in_specs=[pl.BlockSpec((None, tq, hq * dk), lambda b, i: (b, i, 0)),
                  pl.BlockSpec((None, T, hkv * dk), lambda b, i: (b, 0, 0)),
                  pl.BlockSpec((None, T, hkv * dv), lambda b, i: (b, 0, 0))],
        out_specs=pl.BlockSpec((None, tq, hq * dv), lambda b, i: (b, i, 0)),
        out_shape=jax.ShapeDtypeStruct((B, T, hq * dv), BF16),
        compiler_params=_cp("parallel", "parallel"),
        name="attention",
    )(q, k, v)


def _rope_tables(T, n_ctx, n_rot):
    quarter = n_rot // 4
    t = jnp.arange(T - n_ctx)
    row = (t // GRID_W).astype(F32)
    col = (t % GRID_W).astype(F32)
    inv = ROPE_THETA ** (-jnp.arange(quarter, dtype=F32) / quarter)
    ar = row[:, None] * inv[None, :]
    ac = col[:, None] * inv[None, :]
    pad = LANES - n_rot
    cos = jnp.concatenate([jnp.cos(ar), jnp.cos(ar), jnp.cos(ac), jnp.cos(ac), jnp.ones((T - n_ctx, pad), F32)], axis=1)
    sin = jnp.concatenate([-jnp.sin(ar), jnp.sin(ar), -jnp.sin(ac), jnp.sin(ac), jnp.zeros((T - n_ctx, pad), F32)], axis=1)
    cos = jnp.concatenate([jnp.ones((n_ctx, LANES), F32), cos], axis=0)
    sin = jnp.concatenate([jnp.zeros((n_ctx, LANES), F32), sin], axis=0)
    return cos, sin


def _moe_body(te_ref, nv_ref, hs_ref, w1_ref, w3_ref, w2_ref, y_ref):
    i = pl.program_id(0)

    @pl.when(i < nv_ref[0])
    def _():
        hs = hs_ref[...].astype(BF16)
        a = jnp.dot(hs, w1_ref[...].astype(BF16), preferred_element_type=F32)
        b = jnp.dot(hs, w3_ref[...].astype(BF16), preferred_element_type=F32)
        act = a * _sigmoid(a) * b
        y = jnp.dot(act.astype(BF16), w2_ref[...].astype(BF16), preferred_element_type=F32)
        y_ref[...] = y.astype(y_ref.dtype)

    @pl.when(i >= nv_ref[0])
    def _():
        y_ref[...] = jnp.zeros_like(y_ref)


def _moe_experts(tile_expert, n_valid, hs, w1, w3, w2, layer):
    NP, D = hs.shape
    tm = MOE_TM
    DE = w1.shape[-1]
    grid_spec = pltpu.PrefetchScalarGridSpec(
        num_scalar_prefetch=2,
        grid=(NP // tm,),
        in_specs=[pl.BlockSpec((tm, D), lambda i, te, nv: (i, 0)),
                  pl.BlockSpec((None, None, D, DE), lambda i, te, nv: (layer, te[i], 0, 0)),
                  pl.BlockSpec((None, None, D, DE), lambda i, te, nv: (layer, te[i], 0, 0)),
                  pl.BlockSpec((None, None, DE, D), lambda i, te, nv: (layer, te[i], 0, 0))],
        out_specs=pl.BlockSpec((tm, D), lambda i, te, nv: (i, 0)),
    )
    return pl.pallas_call(
        _moe_body,
        grid_spec=grid_spec,
        out_shape=jax.ShapeDtypeStruct((NP, D), BF16),
        compiler_params=_cp("arbitrary"),
        name="moe_experts",
    )(tile_expert, n_valid, hs, w1, w3, w2)


def _moe(h2, re, w1, w3, w2, layer):
    N = h2.shape[0]
    tm = MOE_TM
    n_tiles = (2 * N) // tm + N_EXPERTS
    e_flat = re.reshape(2 * N)
    onehot = (e_flat[:, None] == jnp.arange(N_EXPERTS, dtype=jnp.int32)[None, :]).astype(jnp.int32)
    csum = jnp.cumsum(onehot, axis=0)
    rank = jnp.sum(onehot * (csum - 1), axis=1)
    counts = csum[-1]
    ptiles = (counts + tm - 1) // tm
    tile_end = jnp.cumsum(ptiles)
    tile_start = tile_end - ptiles
    pos = tile_start[e_flat] * tm + rank
    n_valid = tile_end[-1:].astype(jnp.int32)
    tile_ids = jnp.arange(n_tiles, dtype=jnp.int32)
    tile_expert = jnp.minimum(jnp.sum((tile_end[None, :] <= tile_ids[:, None]).astype(jnp.int32), axis=1), N_EXPERTS - 1)
    spread = jnp.arange(n_tiles * tm, dtype=jnp.int32) % N
    src = spread.at[pos].set(jnp.arange(2 * N, dtype=jnp.int32) // 2)
    rows = lambda a, idx: a.at[idx].get(mode="promise_in_bounds")
    hs = rows(h2, src)
    y = _moe_experts(tile_expert, n_valid, hs, w1, w3, w2, layer)
    pos2 = pos.reshape(N, 2)
    return rows(y, pos2[:, 0]), rows(y, pos2[:, 1])


def kernel(x, c, ctx, c_ctx, mod_w, mod_b, norm1_g, norm2_g, w_in, w_out, shift_prev, shift_next, decay_w0, decay_up, iclr_a0, iclr_up, gate_up, k_k, k_a, r_k, gn_g, gn_b, q_norm_g, k_norm_g, mla_q_norm_g, mla_w_uq, mla_kv_norm_g, mla_w_ukv, router_gw, router_gb, router_ew, router_eb, exp_w1, exp_w3, exp_w2, final_norm_g):
    B, S, D = x.shape
    C = ctx.shape[1]
    T = C + S
    depth = mod_w.shape[0]
    assert C == TOK_TILE and S % TOK_TILE == 0 and B <= CTX_ROW
    n_ctx_tiles = C // TOK_TILE

    X = jnp.concatenate([ctx, x], axis=1)
    cc = jnp.zeros((SUBLANES, D), F32).at[:B].set(c).at[CTX_ROW].set(c_ctx)
    mods_all = _modulation(cc, mod_w, mod_b).reshape(depth, SUBLANES * N_MOD, 1, D)

    cos_g, sin_g = _rope_tables(T, C, GQA_HEAD)
    cos_m, sin_m = _rope_tables(T, C, MLA_ROPE)
    hid = jnp.arange(RWKV_W) // RWKV_HEAD
    head_sum = (hid[:, None] == hid[None, :]).astype(F32)

    for l in range(depth):
        mods = mods_all[l]
        w_r = w_in[l][:, :RWKV_COLS].astype(BF16)
        w_g = w_in[l][:, RWKV_COLS:RWKV_COLS + GQA_COLS].astype(BF16)
        w_m = jnp.pad(w_in[l][:, RWKV_COLS + GQA_COLS:], ((0, 0), (0, MLA_COLS_PAD - MLA_COLS))).astype(BF16)

        if l == 0:
            h = _norm_mod(X, norm1_g[l], mods, 0, 1, n_ctx_tiles)
        pr = _matmul(h.reshape(B * T, D), w_r).reshape(B, T, RWKV_COLS)

        wup_pad = jnp.pad(decay_up[l], ((0, 0), (0, ICLR_RANK), (0, 0)))
        aup_pad = jnp.pad(iclr_up[l], ((0, 0), (DECAY_RANK, 0), (0, 0)))
        r, v, kh, lw, b, kt, g, bonus = _rwkv_prep(pr, shift_prev[l], shift_next[l], decay_w0[l], wup_pad,
                                                  iclr_a0[l], aup_pad, gate_up[l], k_k[l], k_a[l], r_k[l], head_sum)
        yf, yb = _rwkv_scan(r, v, kh, lw, b, kt, C)
        o_r = _rwkv_readout(yf, yb, bonus, g, gn_g[l], gn_b[l], head_sum)

        q, k, vv = _gqa_proj(h, w_g, q_norm_g[l], k_norm_g[l], cos_g, sin_g)
        o_g = _attention(q, k, vv, GQA_Q_HEADS, GQA_KV_HEADS, GQA_HEAD, GQA_HEAD, C)

        wq = mla_w_uq[l].reshape(MLA_Q_RANK, MLA_HEADS, MLA_NOPE + MLA_ROPE)
        wq = jnp.pad(wq, ((0, 0), (0, 0), (0, MLA_DK - MLA_NOPE - MLA_ROPE))).reshape(MLA_Q_RANK, MLA_HEADS * MLA_DK)
        wkv = mla_w_ukv[l].reshape(MLA_KV_RANK, MLA_HEADS, MLA_NOPE + MLA_V)
        wkv = jnp.concatenate([wkv[:, :, :MLA_NOPE].reshape(MLA_KV_RANK, -1), wkv[:, :, MLA_NOPE:].reshape(MLA_KV_RANK, -1)], axis=1)
        qm, km, vm = _mla_proj(h, w_m, mla_q_norm_g[l], wq.astype(BF16), mla_kv_norm_g[l], wkv.astype(BF16), cos_m, sin_m)
        o_m = _attention(qm, km, vm, MLA_HEADS, MLA_HEADS, MLA_DK, MLA_V, C)

        X = _matmul_gated_residual([o_r, o_g, o_m], w_out[l].astype(BF16), X, mods, 2, C)

        wr = jnp.pad(jnp.concatenate([router_gw[l], router_ew[l]], axis=1), ((0, 0), (0, LANES - N_GROUPS - N_EXPERTS)))
        br = jnp.pad(jnp.concatenate([router_gb[l], router_eb[l]]), (0, LANES - N_GROUPS - N_EXPERTS)).reshape(1, LANES)
        h2, rw, re = _norm_router(X, norm2_g[l], mods, 3, 4, n_ctx_tiles, wr, br)
        y0, y1 = _moe(h2.reshape(B * T, D), re.reshape(B * T, LANES)[:, :2], exp_w1, exp_w3, exp_w2, l)
        y0 = y0.reshape(B, T, D)
        y1 = y1.reshape(B, T, D)
        if l + 1 < depth:
            X, h = _moe_residual_norm(X, y0, y1, rw, mods, 5, n_ctx_tiles, norm1_g[l + 1], mods_all[l + 1])
        else:
            out = _moe_residual_final(X, y0, y1, rw, mods, 5, n_ctx_tiles, final_norm_g)
    return out
```

```python
import functools
import math

import jax
import jax.numpy as jnp
import numpy as np
from jax import lax
from jax.experimental import pallas as pl
from jax.experimental.pallas import tpu as pltpu

F32 = jnp.float32
BF16 = jnp.bfloat16
HI = lax.Precision.HIGHEST

V7X_VMEM_BYTES = 64 * 1024 * 1024
VMEM_LIMIT = V7X_VMEM_BYTES - 8 * 1024 * 1024
LANES = 128
SUBLANES = 8

GRID_W = 64
ROPE_THETA = 10000.0
NORM_EPS = 1e-6
GN_EPS = 64e-5
DECAY_SCALE = math.exp(-0.5)
LOG2E = math.log2(math.e)

RWKV_HEAD = 64
RWKV_W = 512
DECAY_RANK = 64
ICLR_RANK = 64
GATE_RANK = 128
RWKV_COLS = 3 * RWKV_W + DECAY_RANK + ICLR_RANK + GATE_RANK
LOWRANK_OFF = 3 * RWKV_W
GATE_OFF = LOWRANK_OFF + DECAY_RANK + ICLR_RANK
CHUNK = 64
SCAN_CHUNKS = 2

GQA_HEAD = 128
GQA_Q_HEADS = 8
GQA_KV_HEADS = 2
GQA_Q_COLS = GQA_Q_HEADS * GQA_HEAD
GQA_KV_COLS = GQA_KV_HEADS * GQA_HEAD
GQA_COLS = GQA_Q_COLS + 2 * GQA_KV_COLS

MLA_HEADS = 4
MLA_NOPE = 128
MLA_ROPE = 64
MLA_V = 128
MLA_Q_RANK = 384
MLA_KV_RANK = 256
MLA_COLS = MLA_Q_RANK + MLA_KV_RANK + MLA_ROPE
MLA_COLS_PAD = 768
MLA_DK = 2 * LANES

N_GROUPS = 4
EXPERTS_PER_GROUP = 8
N_EXPERTS = 32
D_EXPERT = 256
MOE_TM = 256

TOK_TILE = 256
PROJ_TILE = 768
N_MOD = 6
CTX_ROW = 4


def _cp(*sem):
    return pltpu.CompilerParams(dimension_semantics=sem, vmem_limit_bytes=VMEM_LIMIT)


def _sigmoid(x):
    return 1.0 / (1.0 + jnp.exp(-x))


def _bf16_terms(x, n):
    terms = []
    for _ in range(n):
        t = x.astype(BF16)
        terms.append(t)
        x = x - t.astype(F32)
    return terms


def _dot_terms(x, w, nx, nw):
    xs = _bf16_terms(x, nx)
    ws = _bf16_terms(w, nw)
    acc = None
    for i in range(nx):
        for j in range(nw):
            if i + j < max(nx, nw):
                p = jnp.dot(xs[i], ws[j], preferred_element_type=F32)
                acc = p if acc is None else acc + p
    return acc


def _mod_body(c_ref, w_ref, b_ref, o_ref):
    c = c_ref[...]
    s = c * _sigmoid(c)
    o_ref[...] = jnp.dot(s, w_ref[...], preferred_element_type=F32, precision=HI) + b_ref[...]


def _modulation(cc, mod_w, mod_b):
    L, D, N = mod_w.shape
    tn = 1024
    return pl.pallas_call(
        _mod_body,
        grid=(L, N // tn),
        in_specs=[pl.BlockSpec((SUBLANES, D), lambda l, j: (0, 0)),
                  pl.BlockSpec((None, D, tn), lambda l, j: (l, 0, j)),
                  pl.BlockSpec((None, 1, tn), lambda l, j: (l, 0, j))],
        out_specs=pl.BlockSpec((None, SUBLANES, tn), lambda l, j: (l, 0, j)),
        out_shape=jax.ShapeDtypeStruct((L, SUBLANES, N), F32),
        compiler_params=_cp("parallel", "parallel"),
        name="modulation",
    )(cc, mod_w, mod_b.reshape(L, 1, N))


def _mod_spec(which, n_ctx_tiles, D):
    return pl.BlockSpec((None, 1, D), lambda b, i: (jnp.where(i < n_ctx_tiles, CTX_ROW, b) * N_MOD + which, 0, 0))


def _rms(x, g):
    return x * lax.rsqrt(jnp.mean(x * x, axis=-1, keepdims=True) + NORM_EPS) * g


def _norm_mod_body(x_ref, g_ref, sh_ref, sc_ref, o_ref):
    y = _rms(x_ref[...], g_ref[...])
    o_ref[...] = (y * (1.0 + sc_ref[...]) + sh_ref[...]).astype(o_ref.dtype)


def _route(logits):
    lane = lax.broadcasted_iota(jnp.int32, logits.shape, 1)
    lane_f = lane.astype(F32)
    neg = jnp.float32(-1e30)
    far = jnp.float32(1e9)
    first_at = lambda hit: jnp.min(jnp.where(hit, lane_f, far), axis=-1, keepdims=True).astype(jnp.int32)
    gl = jnp.where(lane < N_GROUPS, logits, neg)
    gmax = jnp.max(gl, axis=-1, keepdims=True)
    gidx = first_at(gl == gmax)
    p_sel = 1.0 / jnp.sum(jnp.exp(gl - gmax), axis=-1, keepdims=True)
    lo = N_GROUPS + gidx * EXPERTS_PER_GROUP
    el = jnp.where((lane >= lo) & (lane < lo + EXPERTS_PER_GROUP), logits, neg)
    m1 = jnp.max(el, axis=-1, keepdims=True)
    i1 = first_at(el == m1)
    el2 = jnp.where(lane == i1, neg, el)
    m2 = jnp.max(el2, axis=-1, keepdims=True)
    i2 = first_at(el2 == m2)
    t = jnp.exp(m2 - m1)
    w1 = p_sel / (1.0 + t)
    w2 = p_sel * t / (1.0 + t)
    rw = jnp.where(lane == 0, w1, jnp.where(lane == 1, w2, 0.0))
    re = jnp.where(lane == 0, i1 - N_GROUPS, jnp.where(lane == 1, i2 - N_GROUPS, 0))
    return rw, re


def _norm_router_body(x_ref, g_ref, sh_ref, sc_ref, wr_ref, br_ref, h_ref, rw_ref, re_ref):
    y = _rms(x_ref[...], g_ref[...])
    h = y * (1.0 + sc_ref[...]) + sh_ref[...]
    h_ref[...] = h.astype(h_ref.dtype)
    logits = _dot_terms(h, wr_ref[...], 2, 2) + br_ref[...]
    rw, re = _route(logits)
    rw_ref[...] = rw
    re_ref[...] = re


def _norm_mod(X, g, mods, shift_i, scale_i, n_ctx_tiles):
    B, T, D = X.shape
    tt = TOK_TILE
    return pl.pallas_call(
        _norm_mod_body,
        grid=(B, T // tt),
        in_specs=[pl.BlockSpec((None, tt, D), lambda b, i: (b, i, 0)),
                  pl.BlockSpec((1, D), lambda b, i: (0, 0)),
                  _mod_spec(shift_i, n_ctx_tiles, D),
                  _mod_spec(scale_i, n_ctx_tiles, D)],
        out_specs=pl.BlockSpec((None, tt, D), lambda b, i: (b, i, 0)),
        out_shape=jax.ShapeDtypeStruct((B, T, D), BF16),
        compiler_params=_cp("parallel", "parallel"),
        name="norm_mod",
    )(X, g.reshape(1, D), mods, mods)


def _norm_router(X, g, mods, shift_i, scale_i, n_ctx_tiles, wr, br):
    B, T, D = X.shape
    tt = TOK_TILE
    tok = lambda w, dt: (pl.BlockSpec((None, tt, w), lambda b, i: (b, i, 0)), jax.ShapeDtypeStruct((B, T, w), dt))
    outs = [tok(D, F32), tok(LANES, F32), tok(LANES, jnp.int32)]
    return pl.pallas_call(
        _norm_router_body,
        grid=(B, T // tt),
        in_specs=[pl.BlockSpec((None, tt, D), lambda b, i: (b, i, 0)),
                  pl.BlockSpec((1, D), lambda b, i: (0, 0)),
                  _mod_spec(shift_i, n_ctx_tiles, D),
                  _mod_spec(scale_i, n_ctx_tiles, D),
                  pl.BlockSpec((D, LANES), lambda b, i: (0, 0)),
                  pl.BlockSpec((1, LANES), lambda b, i: (0, 0))],
        out_specs=[o[0] for o in outs],
        out_shape=[o[1] for o in outs],
        compiler_params=_cp("parallel", "parallel"),
        name="norm_router",
    )(X, g.reshape(1, D), mods, mods, wr, br)


def _mm_body(a_ref, w_ref, o_ref):
    o_ref[...] = jnp.dot(a_ref[...], w_ref[...], preferred_element_type=F32).astype(o_ref.dtype)


def _matmul(a, w, tm=512):
    M, K = a.shape
    N = w.shape[1]
    return pl.pallas_call(
        _mm_body,
        grid=(M // tm,),
        in_specs=[pl.BlockSpec((tm, K), lambda i: (i, 0)),
                  pl.BlockSpec((K, N), lambda i: (0, 0))],
        out_specs=pl.BlockSpec((tm, N), lambda i: (i, 0)),
        out_shape=jax.ShapeDtypeStruct((M, N), F32),
        compiler_params=_cp("parallel"),
        name="token_matmul",
    )(a, w)


def _mm_res_body(*refs, n_parts, n_ctx, tm):
    a_refs = refs[:n_parts]
    w_ref, x_ref, gl_ref, gc_ref, o_ref = refs[n_parts:]
    acc = None
    k0 = 0
    for a_ref in a_refs:
        k1 = k0 + a_ref.shape[-1]
        part = jnp.dot(a_ref[...], w_ref[k0:k1, :], preferred_element_type=F32)
        acc = part if acc is None else acc + part
        k0 = k1
    row = pl.program_id(1) * tm + lax.broadcasted_iota(jnp.int32, (tm, 1), 0)
    gate = jnp.where(row < n_ctx, gc_ref[...], gl_ref[...])
    o_ref[...] = x_ref[...] + gate * acc


def _matmul_gated_residual(parts, w, X, mods, gate_i, n_ctx, tm=384):
    B, T, D = X.shape
    assert T % tm == 0 and sum(p.shape[-1] for p in parts) == w.shape[0]
    return pl.pallas_call(
        functools.partial(_mm_res_body, n_parts=len(parts), n_ctx=n_ctx, tm=tm),
        grid=(B, T // tm),
        in_specs=[pl.BlockSpec((None, tm, p.shape[-1]), lambda b, i: (b, i, 0)) for p in parts] + [
                  pl.BlockSpec(w.shape, lambda b, i: (0, 0)),
                  pl.BlockSpec((None, tm, D), lambda b, i: (b, i, 0)),
                  pl.BlockSpec((None, 1, D), lambda b, i: (b * N_MOD + gate_i, 0, 0)),
                  pl.BlockSpec((None, 1, D), lambda b, i: (CTX_ROW * N_MOD + gate_i, 0, 0))],
        out_specs=pl.BlockSpec((None, tm, D), lambda b, i: (b, i, 0)),
        out_shape=jax.ShapeDtypeStruct((B, T, D), F32),
        compiler_params=_cp("parallel", "parallel"),
        name="out_proj_residual",
    )(*parts, w, X, mods, mods)


def _moe_residual(x_ref, y0_ref, y1_ref, rw_ref, gate_ref):
    rw = rw_ref[...]
    moe = rw[:, 0:1] * y0_ref[...].astype(F32) + rw[:, 1:2] * y1_ref[...].astype(F32)
    return x_ref[...] + gate_ref[...] * moe


def _moe_residual_norm_body(x_ref, y0_ref, y1_ref, rw_ref, gate_ref, g_ref, sh_ref, sc_ref, x_o, h_o):
    x = _moe_residual(x_ref, y0_ref, y1_ref, rw_ref, gate_ref)
    x_o[...] = x
    h_o[...] = (_rms(x, g_ref[...]) * (1.0 + sc_ref[...]) + sh_ref[...]).astype(h_o.dtype)


def _moe_residual_final_body(x_ref, y0_ref, y1_ref, rw_ref, gate_ref, g_ref, o_ref):
    o_ref[...] = _rms(_moe_residual(x_ref, y0_ref, y1_ref, rw_ref, gate_ref), g_ref[...])


def _moe_residual_norm(X, Y0, Y1, rw, mods, gate_i, n_ctx_tiles, g_next, mods_next):
    B, T, D = X.shape
    tt = TOK_TILE
    blk = pl.BlockSpec((None, tt, D), lambda b, i: (b, i, 0))
    return pl.pallas_call(
        _moe_residual_norm_body,
        grid=(B, T // tt),
        in_specs=[blk, blk, blk, pl.BlockSpec((None, tt, LANES), lambda b, i: (b, i, 0)),
                  _mod_spec(gate_i, n_ctx_tiles, D), pl.BlockSpec((1, D), lambda b, i: (0, 0)),
                  _mod_spec(0, n_ctx_tiles, D), _mod_spec(1, n_ctx_tiles, D)],
        out_specs=[blk, blk],
        out_shape=[jax.ShapeDtypeStruct((B, T, D), F32), jax.ShapeDtypeStruct((B, T, D), BF16)],
        compiler_params=_cp("parallel", "parallel"),
        name="moe_residual_norm",
    )(X, Y0, Y1, rw, mods, g_next.reshape(1, D), mods_next, mods_next)


def _moe_residual_final(X, Y0, Y1, rw, mods, gate_i, n_ctx_tiles, g_final):
    B, T, D = X.shape
    tt = TOK_TILE
    S = T - n_ctx_tiles * tt
    lat = lambda w: pl.BlockSpec((None, tt, w), lambda b, i: (b, i + n_ctx_tiles, 0))
    return pl.pallas_call(
        _moe_residual_final_body,
        grid=(B, S // tt),
        in_specs=[lat(D), lat(D), lat(D), lat(LANES),
                  pl.BlockSpec((None, 1, D), lambda b, i: (b * N_MOD + gate_i, 0, 0)),
                  pl.BlockSpec((1, D), lambda b, i: (0, 0))],
        out_specs=pl.BlockSpec((None, tt, D), lambda b, i: (b, i, 0)),
        out_shape=jax.ShapeDtypeStruct((B, S, D), F32),
        compiler_params=_cp("parallel", "parallel"),
        name="moe_residual_final",
    )(X, Y0, Y1, rw, mods, g_final.reshape(1, D))


def _rwkv_prep_body(p_ref, pv_ref, nx_ref, mup_ref, mun_ref, w0_ref, wup_ref, a0_ref, aup_ref, gup_ref,
                    kk_ref, ka_ref, rk_ref, e_ref,
                    r_o, v_o, kh_o, lw_o, b_o, kt_o, g_o, bon_o, *, n_tiles, tt):
    i = pl.program_id(1)
    p = p_ref[...]
    seq_first = i <= 1
    seq_last = (i == 0) | (i == n_tiles - 1)
    prow = jnp.where(seq_first, 0.0, pv_ref[SUBLANES - 1:SUBLANES, :])
    nrow = jnp.where(seq_last, 0.0, nx_ref[0:1, :])
    rid = lax.broadcasted_iota(jnp.int32, (tt, 1), 0)
    prev = jnp.where(rid == 0, prow, pltpu.roll(p, 1, 0))
    nxt = jnp.where(rid == tt - 1, nrow, pltpu.roll(p, tt - 1, 0))
    z = p + mup_ref[...] * (prev - p) + mun_ref[...] * (nxt - p)

    W = RWKV_W
    r = z[:, 0:W]
    k = z[:, W:2 * W]
    v = z[:, 2 * W:3 * W]
    lowrank = z[:, LOWRANK_OFF:LOWRANK_OFF + LANES]
    gd = z[:, GATE_OFF:GATE_OFF + GATE_RANK]
    head_sum = e_ref[...]

    kap = k * kk_ref[...]
    ss = _dot_terms(kap * kap, head_sum, 2, 1)
    khat = kap * lax.rsqrt(ss + 1e-12)
    wd_t = jnp.tanh(lowrank)
    g_o[...] = _dot_terms(_sigmoid(gd), gup_ref[...], 1, 1)
    r_o[...] = r
    v_o[...] = v
    kh_o[...] = khat
    kt_sum = None
    for d in range(2):
        dec = _dot_terms(wd_t, wup_ref[d], 2, 2)
        lw_o[d] = -DECAY_SCALE * _sigmoid(w0_ref[d:d + 1, :] + dec)
        a = _sigmoid(a0_ref[d:d + 1, :] + _dot_terms(lowrank, aup_ref[d], 1, 1))
        kt = k * (1.0 + (a - 1.0) * ka_ref[...])
        kt_o[d] = kt
        b_o[d] = a * khat
        kt_sum = kt if kt_sum is None else kt_sum + kt
    bsum = _dot_terms(r * kt_sum * rk_ref[...], head_sum, 2, 1)
    bon_o[...] = bsum * v


def _rwkv_prep(pr, mu_prev, mu_next, w0, wup_pad, a0, aup_pad, g_up, k_k, k_a, r_k, head_sum):
    B, T, _ = pr.shape
    tt = TOK_TILE
    W = RWKV_W
    n_tiles = T // tt
    n8 = tt // SUBLANES
    row = lambda v: v.reshape(1, -1)
    full = lambda a: pl.BlockSpec(a.shape, lambda b, i: (0,) * a.ndim)
    tok = pl.BlockSpec((None, tt, W), lambda b, i: (b, i, 0))
    tok2 = pl.BlockSpec((None, 2, tt, W), lambda b, i: (b, 0, i, 0))
    s1 = jax.ShapeDtypeStruct((B, T, W), F32)
    s2 = jax.ShapeDtypeStruct((B, 2, T, W), F32)
    consts = [row(mu_prev), row(mu_next), w0, wup_pad, a0, aup_pad, g_up, row(k_k), row(k_a), row(r_k), head_sum]
    return pl.pallas_call(
        functools.partial(_rwkv_prep_body, n_tiles=n_tiles, tt=tt),
        grid=(B, n_tiles),
        in_specs=[pl.BlockSpec((None, tt, RWKV_COLS), lambda b, i: (b, i, 0)),
                  pl.BlockSpec((None, SUBLANES, RWKV_COLS), lambda b, i: (b, jnp.maximum(i * n8 - 1, 0), 0)),
                  pl.BlockSpec((None, SUBLANES, RWKV_COLS), lambda b, i: (b, jnp.minimum((i + 1) * n8, T // SUBLANES - 1), 0)),
                  ] + [full(a) for a in consts],
        out_specs=[tok, tok, tok, tok2, tok2, tok2, tok, tok],
        out_shape=[s1, s1, s1, s2, s2, s2, s1, s1],
        compiler_params=_cp("parallel", "parallel"),
        name="rwkv_prep",
    )(pr, pr, pr, *consts)


def _stack_heads(x):
    lane = lax.broadcasted_iota(jnp.int32, x.shape, 1)
    first = lane < RWKV_HEAD
    return jnp.concatenate([jnp.where(first, x, 0.0), jnp.where(first, 0.0, x)], axis=0)


def _dot(a, b):
    return jnp.dot(a.astype(BF16), b.astype(BF16), preferred_element_type=F32)


def _dot_nt(a, b):
    return lax.dot_general(a.astype(BF16), b.astype(BF16), (((1,), (1,)), ((), ())), preferred_element_type=F32)


def _dot_tn(a, b):
    return jnp.dot(a.T.astype(BF16), b.astype(BF16), preferred_element_type=F32)


def _chunk_operands(r, v, kh, lw, b, kt, reverse):
    L = CHUNK
    ti = lax.broadcasted_iota(jnp.int32, (L, L), 0)
    tj = lax.broadcasted_iota(jnp.int32, (L, L), 1)
    tri = jnp.where((ti <= tj) if reverse else (ti >= tj), 1.0, 0.0)
    lam = _dot_terms(tri, lw, 1, 3)
    tot = lam[0:1, :] if reverse else lam[L - 1:L, :]
    e_n = jnp.exp(-lam)
    e_g = jnp.exp(tot - lam)
    full = dict(A=kh * jnp.exp(lam - lw), R=r * jnp.exp(lam), Kn=kt * e_n, Bn=b * e_n, Kg=kt * e_g, Bg=b * e_g, V=v)
    e_tot = jnp.exp(tot)
    pairs = []
    for p in range(RWKV_W // LANES):
        sl = slice(p * LANES, (p + 1) * LANES)
        ops = {k: a[:, sl] for k, a in full.items()}
        ops["e_tot"] = e_tot[:, sl]
        ops["reverse"] = reverse
        pairs.append(ops)
    return pairs


def _chunk_masks(reverse):
    L = CHUNK
    t = lax.broadcasted_iota(jnp.int32, (L, 2 * L), 0)
    i = lax.broadcasted_iota(jnp.int32, (L, 2 * L), 1) & (L - 1)
    before = (i > t) if reverse else (i < t)
    return before, before | (i == t), i == t


def _chunks_prepare(chains):
    L = CHUNK
    P2 = 2 * L
    n = len(chains)
    masks = {rev: _chunk_masks(rev) for rev in {c["reverse"] for c in chains}}
    strict = [masks[c["reverse"]][0] for c in chains]
    incl = [masks[c["reverse"]][1] for c in chains]
    eye = masks[chains[0]["reverse"]][2]
    bi = lax.broadcasted_iota(jnp.int32, (P2, P2), 0)
    bj = lax.broadcasted_iota(jnp.int32, (P2, P2), 1)
    same_head = (bi >= L) == (bj >= L)
    eye2 = bi == bj
    stack = _stack_heads
    cat0 = lambda *xs: jnp.concatenate(xs, axis=0)
    cat1 = lambda *xs: jnp.concatenate(xs, axis=1)

    big = [_dot_nt(cat0(c["A"], c["R"]), cat0(stack(c["Bn"]), stack(c["Kn"]))) for c in chains]
    Mb = [jnp.where(strict[i], big[i][0:L, 0:P2], 0.0) for i in range(n)]
    Mkv = [jnp.where(strict[i], big[i][0:L, P2:2 * P2], 0.0) for i in range(n)]
    Pb = [jnp.where(incl[i], big[i][L:P2, 0:P2], 0.0) for i in range(n)]
    Pkv = [jnp.where(incl[i], big[i][L:P2, P2:2 * P2], 0.0) for i in range(n)]

    Pw = [-m for m in Mb]
    Tm = [jnp.where(eye, 1.0, 0.0) + p for p in Pw]
    Pw = [_dot(p, stack(p)) for p in Pw]
    for _ in range(int(math.log2(L)) - 2):
        PT = [_dot(cat0(p, t), stack(p)) for p, t in zip(Pw, Tm)]
        Tm = [t + pt[L:P2] for t, pt in zip(Tm, PT)]
        Pw = [pt[0:L] for pt in PT]
    Tm = [t + _dot(t, stack(p)) for t, p in zip(Tm, Pw)]

    Vs = [stack(c["V"]) for c in chains]
    MPV = [_dot(cat0(Mkv[i], Pkv[i]), Vs[i]) for i in range(n)]
    TAM = [_dot(Tm[i], cat1(stack(chains[i]["A"]), stack(MPV[i][0:L]))) for i in range(n)]
    PB = [_dot(Pb[i], cat1(stack(TAM[i][:, 0:P2]), stack(TAM[i][:, P2:2 * P2]))) for i in range(n)]
    BG = [_dot_tn(chains[i]["Bg"], TAM[i]) for i in range(n)]
    KV = [_dot_tn(chains[i]["Kg"], chains[i]["V"]) for i in range(n)]
    prepared = []
    for i in range(n):
        RA = chains[i]["R"] - PB[i][:, 0:P2]
        G2 = jnp.where(eye2, chains[i]["e_tot"], 0.0) - jnp.where(same_head, BG[i][:, 0:P2], 0.0)
        H2 = jnp.where(same_head, KV[i] - BG[i][:, P2:2 * P2], 0.0)
        prepared.append((cat0(RA, G2), MPV[i][L:P2] - PB[i][:, P2:2 * P2], H2))
    return prepared


def _chunk_advance(prepared, state):
    lhs, y0, h2 = prepared
    out = _dot(lhs, state)
    return out[0:CHUNK] + y0, out[CHUNK:] + h2


def _rwkv_scan_body(rf_ref, vf_ref, khf_ref, rb_ref, vb_ref, khb_ref, lwf_ref, bf_ref, ktf_ref, lwb_ref, bb_ref, ktb_ref,
                    yf_ref, yb_ref, s_ref):
    @pl.when(pl.program_id(1) == 0)
    def _():
        s_ref[...] = jnp.zeros_like(s_ref)

    n_pairs = RWKV_W // LANES
    fwd_refs = (rf_ref, vf_ref, khf_ref, lwf_ref, bf_ref, ktf_ref)
    bwd_refs = (rb_ref, vb_ref, khb_ref, lwb_ref, bb_ref, ktb_ref)
    slots = []
    for c in range(SCAN_CHUNKS):
        lo_f = c * CHUNK
        lo_b = (SCAN_CHUNKS - 1 - c) * CHUNK
        slots.append((_chunk_operands(*(ref[lo_f:lo_f + CHUNK, :] for ref in fwd_refs), False), lo_f,
                      _chunk_operands(*(ref[lo_b:lo_b + CHUNK, :] for ref in bwd_refs), True), lo_b))
    prepared = _chunks_prepare([ch for s in slots for ch in s[0] + s[2]])
    states = [s_ref[i] for i in range(2 * n_pairs)]
    for c, (_, lo_f, _, lo_b) in enumerate(slots):
        ys = []
        for i in range(2 * n_pairs):
            y, states[i] = _chunk_advance(prepared[c * 2 * n_pairs + i], states[i])
            ys.append(y)
        yf_ref[lo_f:lo_f + CHUNK, :] = jnp.concatenate(ys[:n_pairs], axis=1)
        yb_ref[lo_b:lo_b + CHUNK, :] = jnp.concatenate(ys[n_pairs:], axis=1)
    for i in range(2 * n_pairs):
        s_ref[i] = states[i]


def _rwkv_scan(r, v, kh, lw, b, kt, n_ctx):
    B, T, W = r.shape
    blk = SCAN_CHUNKS * CHUNK
    assert T % blk == 0 and n_ctx % blk == 0
    nc = T // blk
    ncc = n_ctx // blk
    rev = lambda j: jnp.where(j < ncc, ncc - 1 - j, nc + ncc - 1 - j)
    fwd1 = pl.BlockSpec((None, blk, W), lambda bb, j: (bb, j, 0))
    bwd1 = pl.BlockSpec((None, blk, W), lambda bb, j: (bb, rev(j), 0))
    fwd2 = pl.BlockSpec((None, None, blk, W), lambda bb, j: (bb, 0, j, 0))
    bwd2 = pl.BlockSpec((None, None, blk, W), lambda bb, j: (bb, 1, rev(j), 0))
    out = jax.ShapeDtypeStruct((B, T, W), F32)
    return pl.pallas_call(
        _rwkv_scan_body,
        grid=(B, nc),
        in_specs=[fwd1, fwd1, fwd1, bwd1, bwd1, bwd1, fwd2, fwd2, fwd2, bwd2, bwd2, bwd2],
        out_specs=[fwd1, bwd1],
        out_shape=[out, out],
        scratch_shapes=[pltpu.VMEM((2 * W // LANES, LANES, LANES), F32)],
        compiler_params=_cp("parallel", "arbitrary"),
        name="rwkv_scan",
    )(r, v, kh, r, v, kh, lw, b, kt, lw, b, kt)


def _rwkv_readout_body(yf_ref, yb_ref, bon_ref, g_ref, gng_ref, gnb_ref, e_ref, o_ref):
    y = yf_ref[...] + yb_ref[...]
    head_mean = e_ref[...] * (1.0 / RWKV_HEAD)
    mu = _dot_terms(y, head_mean, 2, 1)
    yc = y - mu
    var = _dot_terms(yc * yc, head_mean, 2, 1)
    yn = yc * lax.rsqrt(var + GN_EPS) * gng_ref[...] + gnb_ref[...]
    o_ref[...] = ((yn + bon_ref[...]) * g_ref[...]).astype(o_ref.dtype)


def _rwkv_readout(yf, yb, bonus, g, gn_g, gn_b, head_sum):
    B, T, W = yf.shape
    tt = TOK_TILE
    tok = pl.BlockSpec((None, tt, W), lambda b, i: (b, i, 0))
    row = pl.BlockSpec((1, W), lambda b, i: (0, 0))
    return pl.pallas_call(
        _rwkv_readout_body,
        grid=(B, T // tt),
        in_specs=[tok, tok, tok, tok, row, row, pl.BlockSpec((W, W), lambda b, i: (0, 0))],
        out_specs=tok,
        out_shape=jax.ShapeDtypeStruct((B, T, W), BF16),
        compiler_params=_cp("parallel", "parallel"),
        name="rwkv_readout",
    )(yf, yb, bonus, g, gn_g.reshape(1, W), gn_b.reshape(1, W), head_sum)


def _rope(y, cos, sin_signed, quarter):
    lane = lax.broadcasted_iota(jnp.int32, y.shape, 1)
    first = (lane & (2 * quarter - 1)) < quarter
    partner = jnp.where(first, pltpu.roll(y, LANES - quarter, 1), pltpu.roll(y, quarter, 1))
    return y * cos + partner * sin_signed


def _gqa_prep_body(p_ref, qg_ref, kg_ref, cos_ref, sin_ref, q_o, k_o, v_o):
    cos = cos_ref[...]
    sin = sin_ref[...]
    scale = GQA_HEAD ** -0.5 * LOG2E
    for h in range(GQA_Q_HEADS):
        sl = slice(h * GQA_HEAD, (h + 1) * GQA_HEAD)
        q = _rms(p_ref[:, sl], qg_ref[...])
        q_o[:, sl] = (_rope(q, cos, sin, GQA_HEAD // 4) * scale).astype(q_o.dtype)
    for h in range(GQA_KV_HEADS):
        sl = slice(h * GQA_HEAD, (h + 1) * GQA_HEAD)
        k = _rms(p_ref[:, GQA_Q_COLS + h * GQA_HEAD:GQA_Q_COLS + (h + 1) * GQA_HEAD], kg_ref[...])
        k_o[:, sl] = _rope(k, cos, sin, GQA_HEAD // 4).astype(k_o.dtype)
    v_o[...] = p_ref[:, GQA_Q_COLS + GQA_KV_COLS:GQA_COLS].astype(v_o.dtype)


def _gqa_proj_body(h_ref, w_ref, qg_ref, kg_ref, cos_ref, sin_ref, q_o, k_o, v_o):
    p = jnp.dot(h_ref[...], w_ref[...], preferred_element_type=F32)
    _gqa_prep_body(p, qg_ref, kg_ref, cos_ref, sin_ref, q_o, k_o, v_o)


def _gqa_proj(h, w, q_norm_g, k_norm_g, cos, sin):
    B, T, D = h.shape
    tt = PROJ_TILE
    tok = lambda w_: pl.BlockSpec((None, tt, w_), lambda b, i: (b, i, 0))
    row = pl.BlockSpec((1, GQA_HEAD), lambda b, i: (0, 0))
    tab = pl.BlockSpec((tt, LANES), lambda b, i: (i, 0))
    return pl.pallas_call(
        _gqa_proj_body,
        grid=(B, T // tt),
        in_specs=[tok(D), pl.BlockSpec(w.shape, lambda b, i: (0, 0)), row, row, tab, tab],
        out_specs=[tok(GQA_Q_COLS), tok(GQA_KV_COLS), tok(GQA_KV_COLS)],
        out_shape=[jax.ShapeDtypeStruct((B, T, GQA_Q_COLS), BF16),
                   jax.ShapeDtypeStruct((B, T, GQA_KV_COLS), BF16),
                   jax.ShapeDtypeStruct((B, T, GQA_KV_COLS), BF16)],
        compiler_params=_cp("parallel", "parallel"),
        name="gqa_proj",
    )(h, w, q_norm_g.reshape(1, -1), k_norm_g.reshape(1, -1), cos, sin)


def _mla_prep_body(p_ref, qg_ref, wq_ref, kvg_ref, wkv_ref, cos_ref, sin_ref, q_o, k_o, v_o):
    cos = cos_ref[...]
    sin = sin_ref[...]
    scale = (MLA_NOPE + MLA_ROPE) ** -0.5 * LOG2E
    cq = _rms(p_ref[:, 0:MLA_Q_RANK], qg_ref[...])
    q = jnp.dot(cq.astype(BF16), wq_ref[...], preferred_element_type=F32) * scale
    ckv = _rms(p_ref[:, MLA_Q_RANK:MLA_Q_RANK + MLA_KV_RANK], kvg_ref[...])
    kv = jnp.dot(ckv.astype(BF16), wkv_ref[...], preferred_element_type=F32)
    kr = _rope(p_ref[:, MLA_Q_RANK + MLA_KV_RANK:MLA_COLS_PAD], cos, sin, MLA_ROPE // 4).astype(k_o.dtype)
    for h in range(MLA_HEADS):
        lo = h * MLA_DK
        q_o[:, lo:lo + LANES] = q[:, lo:lo + LANES].astype(q_o.dtype)
        q_o[:, lo + LANES:lo + MLA_DK] = _rope(q[:, lo + LANES:lo + MLA_DK], cos, sin, MLA_ROPE // 4).astype(q_o.dtype)
        k_o[:, lo:lo + LANES] = kv[:, h * MLA_NOPE:(h + 1) * MLA_NOPE].astype(k_o.dtype)
        k_o[:, lo + LANES:lo + MLA_DK] = kr
    v_o[...] = kv[:, MLA_HEADS * MLA_NOPE:].astype(v_o.dtype)


def _mla_proj_body(h_ref, w_ref, qg_ref, wq_ref, kvg_ref, wkv_ref, cos_ref, sin_ref, q_o, k_o, v_o):
    p = jnp.dot(h_ref[...], w_ref[...], preferred_element_type=F32)
    _mla_prep_body(p, qg_ref, wq_ref, kvg_ref, wkv_ref, cos_ref, sin_ref, q_o, k_o, v_o)


def _mla_proj(h, w, q_norm_g, wq, kv_norm_g, wkv, cos, sin):
    B, T, D = h.shape
    tt = PROJ_TILE
    tok = lambda w_: pl.BlockSpec((None, tt, w_), lambda b, i: (b, i, 0))
    full = lambda a: pl.BlockSpec(a.shape, lambda b, i: (0,) * a.ndim)
    tab = pl.BlockSpec((tt, LANES), lambda b, i: (i, 0))
    qg = q_norm_g.reshape(1, -1)
    kvg = kv_norm_g.reshape(1, -1)
    return pl.pallas_call(
        _mla_proj_body,
        grid=(B, T // tt),
        in_specs=[tok(D), full(w), full(qg), full(wq), full(kvg), full(wkv), tab, tab],
        out_specs=[tok(MLA_HEADS * MLA_DK), tok(MLA_HEADS * MLA_DK), tok(MLA_HEADS * MLA_V)],
        out_shape=[jax.ShapeDtypeStruct((B, T, MLA_HEADS * MLA_DK), BF16),
                   jax.ShapeDtypeStruct((B, T, MLA_HEADS * MLA_DK), BF16),
                   jax.ShapeDtypeStruct((B, T, MLA_HEADS * MLA_V), BF16)],
        compiler_params=_cp("parallel", "parallel"),
        name="mla_proj",
    )(h, w, qg, wq, kvg, wkv, cos, sin)


def _attn_body(q_ref, k_ref, v_ref, o_ref, *, hq, hkv, dk, dv, n_ctx_tiles, n_ctx, n_all):
    rep = hq // hkv
    tq = q_ref.shape[0]

    def run(nk):
        for g in range(hkv):
            kg = k_ref[0:nk, g * dk:(g + 1) * dk]
            vg = v_ref[0:nk, g * dv:(g + 1) * dv]
            v_aug = jnp.concatenate([vg, jnp.ones_like(vg)], axis=1)
            q = jnp.concatenate([q_ref[:, h * dk:(h + 1) * dk] for h in range(g * rep, (g + 1) * rep)], axis=0)
            s = _dot_nt(q, kg)
            p = jnp.exp2(s - jnp.max(s, axis=-1, keepdims=True))
            o = jnp.dot(p.astype(BF16), v_aug, preferred_element_type=F32)
            o = o[:, 0:dv] / o[:, dv:dv + 1]
            for rr in range(rep):
                h = g * rep + rr
                o_ref[:, h * dv:(h + 1) * dv] = o[rr * tq:(rr + 1) * tq].astype(o_ref.dtype)

    @pl.when(pl.program_id(1) < n_ctx_tiles)
    def _():
        run(n_ctx)

    @pl.when(pl.program_id(1) >= n_ctx_tiles)
    def _():
        run(n_all)


def _attention(q, k, v, hq, hkv, dk, dv, n_ctx):
    B, T, _ = q.shape
    tq = TOK_TILE
    return pl.pallas_call(
        functools.partial(_attn_body, hq=hq, hkv=hkv, dk=dk, dv=dv, n_ctx_tiles=n_ctx // tq, n_ctx=n_ctx, n_all=T),
        grid=(B, T // tq),
        in_specs=[pl.BlockSpec((None, tq, hq * dk), lambda b, i: (b, i, 0)),
                  pl.BlockSpec((None, T, hkv * dk), lambda b, i: (b, 0, 0)),
                  pl.BlockSpec((None, T, hkv * dv), lambda b, i: (b, 0, 0))],
        out_specs=pl.BlockSpec((None, tq, hq * dv), lambda b, i: (b, i, 0)),
        out_shape=jax.ShapeDtypeStruct((B, T, hq * dv), BF16),
        compiler_params=_cp("parallel", "parallel"),
        name="attention",
    )(q, k, v)


def _rope_tables(T, n_ctx, n_rot):
    quarter = n_rot // 4
    t = jnp.arange(T - n_ctx)
    row = (t // GRID_W).astype(F32)
    col = (t % GRID_W).astype(F32)
    inv = ROPE_THETA ** (-jnp.arange(quarter, dtype=F32) / quarter)
    ar = row[:, None] * inv[None, :]
    ac = col[:, None] * inv[None, :]
    pad = LANES - n_rot
    cos = jnp.concatenate([jnp.cos(ar), jnp.cos(ar), jnp.cos(ac), jnp.cos(ac), jnp.ones((T - n_ctx, pad), F32)], axis=1)
    sin = jnp.concatenate([-jnp.sin(ar), jnp.sin(ar), -jnp.sin(ac), jnp.sin(ac), jnp.zeros((T - n_ctx, pad), F32)], axis=1)
    cos = jnp.concatenate([jnp.ones((n_ctx, LANES), F32), cos], axis=0)
    sin = jnp.concatenate([jnp.zeros((n_ctx, LANES), F32), sin], axis=0)
    return cos, sin


def _moe_body(te_ref, nv_ref, hs_ref, w1_ref, w3_ref, w2_ref, y_ref):
    i = pl.program_id(0)

    @pl.when(i < nv_ref[0])
    def _():
        hs = hs_ref[...].astype(BF16)
        a = jnp.dot(hs, w1_ref[...].astype(BF16), preferred_element_type=F32)
        b = jnp.dot(hs, w3_ref[...].astype(BF16), preferred_element_type=F32)
        act = a * _sigmoid(a) * b
        y = jnp.dot(act.astype(BF16), w2_ref[...].astype(BF16), preferred_element_type=F32)
        y_ref[...] = y.astype(y_ref.dtype)

    @pl.when(i >= nv_ref[0])
    def _():
        y_ref[...] = jnp.zeros_like(y_ref)


def _moe_experts(tile_expert, n_valid, hs, w1, w3, w2, layer):
    NP, D = hs.shape
    tm = MOE_TM
    DE = w1.shape[-1]
    grid_spec = pltpu.PrefetchScalarGridSpec(
        num_scalar_prefetch=2,
        grid=(NP // tm,),
        in_specs=[pl.BlockSpec((tm, D), lambda i, te, nv: (i, 0)),
                  pl.BlockSpec((None, None, D, DE), lambda i, te, nv: (layer, te[i], 0, 0)),
                  pl.BlockSpec((None, None, D, DE), lambda i, te, nv: (layer, te[i], 0, 0)),
                  pl.BlockSpec((None, None, DE, D), lambda i, te, nv: (layer, te[i], 0, 0))],
        out_specs=pl.BlockSpec((tm, D), lambda i, te, nv: (i, 0)),
    )
    return pl.pallas_call(
        _moe_body,
        grid_spec=grid_spec,
        out_shape=jax.ShapeDtypeStruct((NP, D), BF16),
        compiler_params=_cp("arbitrary"),
        name="moe_experts",
    )(tile_expert, n_valid, hs, w1, w3, w2)


def _moe(h2, re, w1, w3, w2, layer):
    N = h2.shape[0]
    tm = MOE_TM
    n_tiles = (2 * N) // tm + N_EXPERTS
    e_flat = re.reshape(2 * N)
    onehot = (e_flat[:, None] == jnp.arange(N_EXPERTS, dtype=jnp.int32)[None, :]).astype(jnp.int32)
    csum = jnp.cumsum(onehot, axis=0)
    rank = jnp.sum(onehot * (csum - 1), axis=1)
    counts = csum[-1]
    ptiles = (counts + tm - 1) // tm
    tile_end = jnp.cumsum(ptiles)
    tile_start = tile_end - ptiles
    pos = tile_start[e_flat] * tm + rank
    n_valid = tile_end[-1:].astype(jnp.int32)
    tile_ids = jnp.arange(n_tiles, dtype=jnp.int32)
    tile_expert = jnp.minimum(jnp.sum((tile_end[None, :] <= tile_ids[:, None]).astype(jnp.int32), axis=1), N_EXPERTS - 1)
    spread = jnp.arange(n_tiles * tm, dtype=jnp.int32) % N
    src = spread.at[pos].set(jnp.arange(2 * N, dtype=jnp.int32) // 2)
    rows = lambda a, idx: a.at[idx].get(mode="promise_in_bounds")
    hs = rows(h2, src)
    y = _moe_experts(tile_expert, n_valid, hs, w1, w3, w2, layer)
    pos2 = pos.reshape(N, 2)
    return rows(y, pos2[:, 0]), rows(y, pos2[:, 1])


def kernel(x, c, ctx, c_ctx, mod_w, mod_b, norm1_g, norm2_g, w_in, w_out, shift_prev, shift_next, decay_w0, decay_up, iclr_a0, iclr_up, gate_up, k_k, k_a, r_k, gn_g, gn_b, q_norm_g, k_norm_g, mla_q_norm_g, mla_w_uq, mla_kv_norm_g, mla_w_ukv, router_gw, router_gb, router_ew, router_eb, exp_w1, exp_w3, exp_w2, final_norm_g):
    B, S, D = x.shape
    C = ctx.shape[1]
    T = C + S
    depth = mod_w.shape[0]
    assert C == TOK_TILE and S % TOK_TILE == 0 and B <= CTX_ROW
    n_ctx_tiles = C // TOK_TILE

    X = jnp.concatenate([ctx, x], axis=1)
    cc = jnp.zeros((SUBLANES, D), F32).at[:B].set(c).at[CTX_ROW].set(c_ctx)
    mods_all = _modulation(cc, mod_w, mod_b).reshape(depth, SUBLANES * N_MOD, 1, D)

    cos_g, sin_g = _rope_tables(T, C, GQA_HEAD)
    cos_m, sin_m = _rope_tables(T, C, MLA_ROPE)
    hid = jnp.arange(RWKV_W) // RWKV_HEAD
    head_sum = (hid[:, None] == hid[None, :]).astype(F32)

    for l in range(depth):
        mods = mods_all[l]
        w_r = w_in[l][:, :RWKV_COLS].astype(BF16)
        w_g = w_in[l][:, RWKV_COLS:RWKV_COLS + GQA_COLS].astype(BF16)
        w_m = jnp.pad(w_in[l][:, RWKV_COLS + GQA_COLS:], ((0, 0), (0, MLA_COLS_PAD - MLA_COLS))).astype(BF16)

        if l == 0:
            h = _norm_mod(X, norm1_g[l], mods, 0, 1, n_ctx_tiles)
        pr = _matmul(h.reshape(B * T, D), w_r).reshape(B, T, RWKV_COLS)

        wup_pad = jnp.pad(decay_up[l], ((0, 0), (0, ICLR_RANK), (0, 0)))
        aup_pad = jnp.pad(iclr_up[l], ((0, 0), (DECAY_RANK, 0), (0, 0)))
        r, v, kh, lw, b, kt, g, bonus = _rwkv_prep(pr, shift_prev[l], shift_next[l], decay_w0[l], wup_pad,
                                                  iclr_a0[l], aup_pad, gate_up[l], k_k[l], k_a[l], r_k[l], head_sum)
        yf, yb = _rwkv_scan(r, v, kh, lw, b, kt, C)
        o_r = _rwkv_readout(yf, yb, bonus, g, gn_g[l], gn_b[l], head_sum)

        q, k, vv = _gqa_proj(h, w_g, q_norm_g[l], k_norm_g[l], cos_g, sin_g)
        o_g = _attention(q, k, vv, GQA_Q_HEADS, GQA_KV_HEADS, GQA_HEAD, GQA_HEAD, C)

        wq = mla_w_uq[l].reshape(MLA_Q_RANK, MLA_HEADS, MLA_NOPE + MLA_ROPE)
        wq = jnp.pad(wq, ((0, 0), (0, 0), (0, MLA_DK - MLA_NOPE - MLA_ROPE))).reshape(MLA_Q_RANK, MLA_HEADS * MLA_DK)
        wkv = mla_w_ukv[l].reshape(MLA_KV_RANK, MLA_HEADS, MLA_NOPE + MLA_V)
        wkv = jnp.concatenate([wkv[:, :, :MLA_NOPE].reshape(MLA_KV_RANK, -1), wkv[:, :, MLA_NOPE:].reshape(MLA_KV_RANK, -1)], axis=1)
        qm, km, vm = _mla_proj(h, w_m, mla_q_norm_g[l], wq.astype(BF16), mla_kv_norm_g[l], wkv.astype(BF16), cos_m, sin_m)
        o_m = _attention(qm, km, vm, MLA_HEADS, MLA_HEADS, MLA_DK, MLA_V, C)

        X = _matmul_gated_residual([o_r, o_g, o_m], w_out[l].astype(BF16), X, mods, 2, C)

        wr = jnp.pad(jnp.concatenate([router_gw[l], router_ew[l]], axis=1), ((0, 0), (0, LANES - N_GROUPS - N_EXPERTS)))
        br = jnp.pad(jnp.concatenate([router_gb[l], router_eb[l]]), (0, LANES - N_GROUPS - N_EXPERTS)).reshape(1, LANES)
        h2, rw, re = _norm_router(X, norm2_g[l], mods, 3, 4, n_ctx_tiles, wr, br)
        y0, y1 = _moe(h2.reshape(B * T, D), re.reshape(B * T, LANES)[:, :2], exp_w1, exp_w3, exp_w2, l)
        y0 = y0.reshape(B, T, D)
        y1 = y1.reshape(B, T, D)
        if l + 1 < depth:
            X, h = _moe_residual_norm(X, y0, y1, rw, mods, 5, n_ctx_tiles, norm1_g[l + 1], mods_all[l + 1])
        else:
            out = _moe_residual_final(X, y0, y1, rw, mods, 5, n_ctx_tiles, final_norm_g)
    return out
```

```python
import functools
import math

import jax
import jax.numpy as jnp
import numpy as np
from jax import lax
from jax.experimental import pallas as pl
from jax.experimental.pallas import tpu as pltpu

F32 = jnp.float32
BF16 = jnp.bfloat16

V7X_VMEM_BYTES = 64 * 1024 * 1024
VMEM_LIMIT = V7X_VMEM_BYTES - 8 * 1024 * 1024
LANES = 128
SUBLANES = 8

GRID_W = 64
ROPE_THETA = 10000.0
NORM_EPS = 1e-6
GN_EPS = 64e-5
DECAY_SCALE = math.exp(-0.5)
LOG2E = math.log2(math.e)

RWKV_HEAD = 64
RWKV_W = 512
DECAY_RANK = 64
ICLR_RANK = 64
GATE_RANK = 128
RWKV_COLS = 3 * RWKV_W + DECAY_RANK + ICLR_RANK + GATE_RANK
LOWRANK_OFF = 3 * RWKV_W
GATE_OFF = LOWRANK_OFF + DECAY_RANK + ICLR_RANK
CHUNK = 64
SCAN_CHUNKS = 2

GQA_HEAD = 128
GQA_Q_HEADS = 8
GQA_KV_HEADS = 2
GQA_Q_COLS = GQA_Q_HEADS * GQA_HEAD
GQA_KV_COLS = GQA_KV_HEADS * GQA_HEAD
GQA_COLS = GQA_Q_COLS + 2 * GQA_KV_COLS

MLA_HEADS = 4
MLA_NOPE = 128
MLA_ROPE = 64
MLA_V = 128
MLA_Q_RANK = 384
MLA_KV_RANK = 256
MLA_COLS = MLA_Q_RANK + MLA_KV_RANK + MLA_ROPE
MLA_COLS_PAD = 768
MLA_DK = 2 * LANES

N_GROUPS = 4
EXPERTS_PER_GROUP = 8
N_EXPERTS = 32
D_EXPERT = 256
MOE_TM = 256

TOK_TILE = 256
PROJ_TILE = 768
N_MOD = 6
CTX_ROW = 4


def _cp(*sem):
    return pltpu.CompilerParams(dimension_semantics=sem, vmem_limit_bytes=VMEM_LIMIT)


def _sigmoid(x):
    return 1.0 / (1.0 + jnp.exp(-x))


def _bf16_terms(x, n):
    terms = []
    for _ in range(n):
        t = x.astype(BF16)
        terms.append(t)
        x = x - t.astype(F32)
    return terms


def _dot_terms(x, w, nx, nw):
    xs = _bf16_terms(x, nx)
    ws = _bf16_terms(w, nw)
    acc = None
    for i in range(nx):
        for j in range(nw):
            if i + j < max(nx, nw):
                p = jnp.dot(xs[i], ws[j], preferred_element_type=F32)
                acc = p if acc is None else acc + p
    return acc


def _mod_body(c_ref, w_ref, b_ref, o_ref):
    c = c_ref[...]
    s = c * _sigmoid(c)
    o_ref[...] = _dot_terms(s, w_ref[...], 2, 2) + b_ref[...]


def _modulation(cc, mod_w, mod_b):
    L, D, N = mod_w.shape
    tn = 1024
    return pl.pallas_call(
        _mod_body,
        grid=(L, N // tn),
        in_specs=[pl.BlockSpec((SUBLANES, D), lambda l, j: (0, 0)),
                  pl.BlockSpec((None, D, tn), lambda l, j: (l, 0, j)),
                  pl.BlockSpec((None, 1, tn), lambda l, j: (l, 0, j))],
        out_specs=pl.BlockSpec((None, SUBLANES, tn), lambda l, j: (l, 0, j)),
        out_shape=jax.ShapeDtypeStruct((L, SUBLANES, N), F32),
        compiler_params=_cp("parallel", "parallel"),
        name="modulation",
    )(cc, mod_w, mod_b.reshape(L, 1, N))


def _mod_spec(which, n_ctx_tiles, D):
    return pl.BlockSpec((None, 1, D), lambda b, i: (jnp.where(i < n_ctx_tiles, CTX_ROW, b) * N_MOD + which, 0, 0))


def _rms(x, g):
    return x * lax.rsqrt(jnp.mean(x * x, axis=-1, keepdims=True) + NORM_EPS) * g


def _norm_mod_body(x_ref, g_ref, sh_ref, sc_ref, o_ref):
    y = _rms(x_ref[...], g_ref[...])
    o_ref[...] = (y * (1.0 + sc_ref[...]) + sh_ref[...]).astype(o_ref.dtype)


def _route(logits):
    lane = lax.broadcasted_iota(jnp.int32, logits.shape, 1)
    lane_f = lane.astype(F32)
    neg = jnp.float32(-1e30)
    far = jnp.float32(1e9)
    first_at = lambda hit: jnp.min(jnp.where(hit, lane_f, far), axis=-1, keepdims=True).astype(jnp.int32)
    gl = jnp.where(lane < N_GROUPS, logits, neg)
    gmax = jnp.max(gl, axis=-1, keepdims=True)
    gidx = first_at(gl == gmax)
    p_sel = 1.0 / jnp.sum(jnp.exp(gl - gmax), axis=-1, keepdims=True)
    lo = N_GROUPS + gidx * EXPERTS_PER_GROUP
    el = jnp.where((lane >= lo) & (lane < lo + EXPERTS_PER_GROUP), logits, neg)
    m1 = jnp.max(el, axis=-1, keepdims=True)
    i1 = first_at(el == m1)
    el2 = jnp.where(lane == i1, neg, el)
    m2 = jnp.max(el2, axis=-1, keepdims=True)
    i2 = first_at(el2 == m2)
    t = jnp.exp(m2 - m1)
    w1 = p_sel / (1.0 + t)
    w2 = p_sel * t / (1.0 + t)
    rw = jnp.where(lane == 0, w1, jnp.where(lane == 1, w2, 0.0))
    re = jnp.where(lane == 0, i1 - N_GROUPS, jnp.where(lane == 1, i2 - N_GROUPS, 0))
    return rw, re


def _norm_router_body(x_ref, g_ref, sh_ref, sc_ref, wr_ref, br_ref, h_ref, rw_ref, re_ref):
    y = _rms(x_ref[...], g_ref[...])
    h = y * (1.0 + sc_ref[...]) + sh_ref[...]
    h_ref[...] = h.astype(h_ref.dtype)
    logits = _dot_terms(h, wr_ref[...], 2, 2) + br_ref[...]
    rw, re = _route(logits)
    rw_ref[...] = rw
    re_ref[...] = re


def _norm_mod(X, g, mods, shift_i, scale_i, n_ctx_tiles):
    B, T, D = X.shape
    tt = TOK_TILE
    return pl.pallas_call(
        _norm_mod_body,
        grid=(B, T // tt),
        in_specs=[pl.BlockSpec((None, tt, D), lambda b, i: (b, i, 0)),
                  pl.BlockSpec((1, D), lambda b, i: (0, 0)),
                  _mod_spec(shift_i, n_ctx_tiles, D),
                  _mod_spec(scale_i, n_ctx_tiles, D)],
        out_specs=pl.BlockSpec((None, tt, D), lambda b, i: (b, i, 0)),
        out_shape=jax.ShapeDtypeStruct((B, T, D), BF16),
        compiler_params=_cp("parallel", "parallel"),
        name="norm_mod",
    )(X, g.reshape(1, D), mods, mods)


def _norm_router(X, g, mods, shift_i, scale_i, n_ctx_tiles, wr, br):
    B, T, D = X.shape
    tt = TOK_TILE
    tok = lambda w, dt: (pl.BlockSpec((None, tt, w), lambda b, i: (b, i, 0)), jax.ShapeDtypeStruct((B, T, w), dt))
    outs = [tok(D, F32), tok(LANES, F32), tok(LANES, jnp.int32)]
    return pl.pallas_call(
        _norm_router_body,
        grid=(B, T // tt),
        in_specs=[pl.BlockSpec((None, tt, D), lambda b, i: (b, i, 0)),
                  pl.BlockSpec((1, D), lambda b, i: (0, 0)),
                  _mod_spec(shift_i, n_ctx_tiles, D),
                  _mod_spec(scale_i, n_ctx_tiles, D),
                  pl.BlockSpec((D, LANES), lambda b, i: (0, 0)),
                  pl.BlockSpec((1, LANES), lambda b, i: (0, 0))],
        out_specs=[o[0] for o in outs],
        out_shape=[o[1] for o in outs],
        compiler_params=_cp("parallel", "parallel"),
        name="norm_router",
    )(X, g.reshape(1, D), mods, mods, wr, br)


def _mm_body(a_ref, w_ref, o_ref):
    o_ref[...] = jnp.dot(a_ref[...], w_ref[...], preferred_element_type=F32).astype(o_ref.dtype)


def _matmul(a, w, tm=512):
    M, K = a.shape
    N = w.shape[1]
    return pl.pallas_call(
        _mm_body,
        grid=(M // tm,),
        in_specs=[pl.BlockSpec((tm, K), lambda i: (i, 0)),
                  pl.BlockSpec((K, N), lambda i: (0, 0))],
        out_specs=pl.BlockSpec((tm, N), lambda i: (i, 0)),
        out_shape=jax.ShapeDtypeStruct((M, N), F32),
        compiler_params=_cp("parallel"),
        name="token_matmul",
    )(a, w)


def _mm_res_body(*refs, n_parts, n_ctx, tm):
    a_refs = refs[:n_parts]
    w_ref, x_ref, gl_ref, gc_ref, o_ref = refs[n_parts:]
    acc = None
    k0 = 0
    for a_ref in a_refs:
        k1 = k0 + a_ref.shape[-1]
        part = jnp.dot(a_ref[...], w_ref[k0:k1, :], preferred_element_type=F32)
        acc = part if acc is None else acc + part
        k0 = k1
    row = pl.program_id(1) * tm + lax.broadcasted_iota(jnp.int32, (tm, 1), 0)
    gate = jnp.where(row < n_ctx, gc_ref[...], gl_ref[...])
    o_ref[...] = x_ref[...] + gate * acc


def _matmul_gated_residual(parts, w, X, mods, gate_i, n_ctx, tm=384):
    B, T, D = X.shape
    assert T % tm == 0 and sum(p.shape[-1] for p in parts) == w.shape[0]
    return pl.pallas_call(
        functools.partial(_mm_res_body, n_parts=len(parts), n_ctx=n_ctx, tm=tm),
        grid=(B, T // tm),
        in_specs=[pl.BlockSpec((None, tm, p.shape[-1]), lambda b, i: (b, i, 0)) for p in parts] + [
                  pl.BlockSpec(w.shape, lambda b, i: (0, 0)),
                  pl.BlockSpec((None, tm, D), lambda b, i: (b, i, 0)),
                  pl.BlockSpec((None, 1, D), lambda b, i: (b * N_MOD + gate_i, 0, 0)),
                  pl.BlockSpec((None, 1, D), lambda b, i: (CTX_ROW * N_MOD + gate_i, 0, 0))],
        out_specs=pl.BlockSpec((None, tm, D), lambda b, i: (b, i, 0)),
        out_shape=jax.ShapeDtypeStruct((B, T, D), F32),
        compiler_params=_cp("parallel", "parallel"),
        name="out_proj_residual",
    )(*parts, w, X, mods, mods)


def _moe_residual(x_ref, y0_ref, y1_ref, rw_ref, gate_ref):
    rw = rw_ref[...]
    moe = rw[:, 0:1] * y0_ref[...].astype(F32) + rw[:, 1:2] * y1_ref[...].astype(F32)
    return x_ref[...] + gate_ref[...] * moe


def _moe_residual_norm_body(x_ref, y0_ref, y1_ref, rw_ref, gate_ref, g_ref, sh_ref, sc_ref, x_o, h_o):
    x = _moe_residual(x_ref, y0_ref, y1_ref, rw_ref, gate_ref)
    x_o[...] = x
    h_o[...] = (_rms(x, g_ref[...]) * (1.0 + sc_ref[...]) + sh_ref[...]).astype(h_o.dtype)


def _moe_residual_final_body(x_ref, y0_ref, y1_ref, rw_ref, gate_ref, g_ref, o_ref):
    o_ref[...] = _rms(_moe_residual(x_ref, y0_ref, y1_ref, rw_ref, gate_ref), g_ref[...])


def _moe_residual_norm(X, Y0, Y1, rw, mods, gate_i, n_ctx_tiles, g_next, mods_next):
    B, T, D = X.shape
    tt = TOK_TILE
    blk = pl.BlockSpec((None, tt, D), lambda b, i: (b, i, 0))
    return pl.pallas_call(
        _moe_residual_norm_body,
        grid=(B, T // tt),
        in_specs=[blk, blk, blk, pl.BlockSpec((None, tt, LANES), lambda b, i: (b, i, 0)),
                  _mod_spec(gate_i, n_ctx_tiles, D), pl.BlockSpec((1, D), lambda b, i: (0, 0)),
                  _mod_spec(0, n_ctx_tiles, D), _mod_spec(1, n_ctx_tiles, D)],
        out_specs=[blk, blk],
        out_shape=[jax.ShapeDtypeStruct((B, T, D), F32), jax.ShapeDtypeStruct((B, T, D), BF16)],
        compiler_params=_cp("parallel", "parallel"),
        name="moe_residual_norm",
    )(X, Y0, Y1, rw, mods, g_next.reshape(1, D), mods_next, mods_next)


def _moe_residual_final(X, Y0, Y1, rw, mods, gate_i, n_ctx_tiles, g_final):
    B, T, D = X.shape
    tt = TOK_TILE
    S = T - n_ctx_tiles * tt
    lat = lambda w: pl.BlockSpec((None, tt, w), lambda b, i: (b, i + n_ctx_tiles, 0))
    return pl.pallas_call(
        _moe_residual_final_body,
        grid=(B, S // tt),
        in_specs=[lat(D), lat(D), lat(D), lat(LANES),
                  pl.BlockSpec((None, 1, D), lambda b, i: (b * N_MOD + gate_i, 0, 0)),
                  pl.BlockSpec((1, D), lambda b, i: (0, 0))],
        out_specs=pl.BlockSpec((None, tt, D), lambda b, i: (b, i, 0)),
        out_shape=jax.ShapeDtypeStruct((B, S, D), F32),
        compiler_params=_cp("parallel", "parallel"),
        name="moe_residual_final",
    )(X, Y0, Y1, rw, mods, g_final.reshape(1, D))


def _rwkv_prep_body(p_ref, pv_ref, nx_ref, mup_ref, mun_ref, w0_ref, wup_ref, a0_ref, aup_ref, gup_ref,
                    kk_ref, ka_ref, rk_ref, e_ref,
                    r_o, v_o, kh_o, lw_o, b_o, kt_o, g_o, bon_o, *, n_tiles, tt):
    i = pl.program_id(1)
    p = p_ref[...]
    seq_first = i <= 1
    seq_last = (i == 0) | (i == n_tiles - 1)
    prow = jnp.where(seq_first, 0.0, pv_ref[SUBLANES - 1:SUBLANES, :])
    nrow = jnp.where(seq_last, 0.0, nx_ref[0:1, :])
    rid = lax.broadcasted_iota(jnp.int32, (tt, 1), 0)
    prev = jnp.where(rid == 0, prow, pltpu.roll(p, 1, 0))
    nxt = jnp.where(rid == tt - 1, nrow, pltpu.roll(p, tt - 1, 0))
    z = p + mup_ref[...] * (prev - p) + mun_ref[...] * (nxt - p)

    W = RWKV_W
    r = z[:, 0:W]
    k = z[:, W:2 * W]
    v = z[:, 2 * W:3 * W]
    lowrank = z[:, LOWRANK_OFF:LOWRANK_OFF + LANES]
    gd = z[:, GATE_OFF:GATE_OFF + GATE_RANK]
    head_sum = e_ref[...]

    kap = k * kk_ref[...]
    ss = _dot_terms(kap * kap, head_sum, 2, 1)
    khat = kap * lax.rsqrt(ss + 1e-12)
    wd_t = jnp.tanh(lowrank)
    g_o[...] = _dot_terms(_sigmoid(gd), gup_ref[...], 1, 1)
    r_o[...] = r
    v_o[...] = v
    kh_o[...] = khat
    kt_sum = None
    for d in range(2):
        dec = _dot_terms(wd_t, wup_ref[d], 2, 2)
        lw_o[d] = -DECAY_SCALE * _sigmoid(w0_ref[d:d + 1, :] + dec)
        a = _sigmoid(a0_ref[d:d + 1, :] + _dot_terms(lowrank, aup_ref[d], 1, 1))
        kt = k * (1.0 + (a - 1.0) * ka_ref[...])
        kt_o[d] = kt
        b_o[d] = a * khat
        kt_sum = kt if kt_sum is None else kt_sum + kt
    bsum = _dot_terms(r * kt_sum * rk_ref[...], head_sum, 2, 1)
    bon_o[...] = bsum * v


def _rwkv_prep(pr, mu_prev, mu_next, w0, wup_pad, a0, aup_pad, g_up, k_k, k_a, r_k, head_sum):
    B, T, _ = pr.shape
    tt = TOK_TILE
    W = RWKV_W
    n_tiles = T // tt
    n8 = tt // SUBLANES
    row = lambda v: v.reshape(1, -1)
    full = lambda a: pl.BlockSpec(a.shape, lambda b, i: (0,) * a.ndim)
    tok = pl.BlockSpec((None, tt, W), lambda b, i: (b, i, 0))
    tok2 = pl.BlockSpec((None, 2, tt, W), lambda b, i: (b, 0, i, 0))
    s1 = jax.ShapeDtypeStruct((B, T, W), F32)
    s2 = jax.ShapeDtypeStruct((B, 2, T, W), F32)
    consts = [row(mu_prev), row(mu_next), w0, wup_pad, a0, aup_pad, g_up, row(k_k), row(k_a), row(r_k), head_sum]
    return pl.pallas_call(
        functools.partial(_rwkv_prep_body, n_tiles=n_tiles, tt=tt),
        grid=(B, n_tiles),
        in_specs=[pl.BlockSpec((None, tt, RWKV_COLS), lambda b, i: (b, i, 0)),
                  pl.BlockSpec((None, SUBLANES, RWKV_COLS), lambda b, i: (b, jnp.maximum(i * n8 - 1, 0), 0)),
                  pl.BlockSpec((None, SUBLANES, RWKV_COLS), lambda b, i: (b, jnp.minimum((i + 1) * n8, T // SUBLANES - 1), 0)),
                  ] + [full(a) for a in consts],
        out_specs=[tok, tok, tok, tok2, tok2, tok2, tok, tok],
        out_shape=[s1, s1, s1, s2, s2, s2, s1, s1],
        compiler_params=_cp("parallel", "parallel"),
        name="rwkv_prep",
    )(pr, pr, pr, *consts)


def _stack_heads(x):
    lane = lax.broadcasted_iota(jnp.int32, x.shape, 1)
    first = lane < RWKV_HEAD
    return jnp.concatenate([jnp.where(first, x, 0.0), jnp.where(first, 0.0, x)], axis=0)


def _dot(a, b):
    return jnp.dot(a.astype(BF16), b.astype(BF16), preferred_element_type=F32)


def _dot_nt(a, b):
    return lax.dot_general(a.astype(BF16), b.astype(BF16), (((1,), (1,)), ((), ())), preferred_element_type=F32)


def _dot_tn(a, b):
    return jnp.dot(a.T.astype(BF16), b.astype(BF16), preferred_element_type=F32)


def _chunk_operands(r, v, kh, lw, b, kt, reverse):
    L = CHUNK
    ti = lax.broadcasted_iota(jnp.int32, (L, L), 0)
    tj = lax.broadcasted_iota(jnp.int32, (L, L), 1)
    tri = jnp.where((ti <= tj) if reverse else (ti >= tj), 1.0, 0.0)
    lam = _dot_terms(tri, lw, 1, 3)
    tot = lam[0:1, :] if reverse else lam[L - 1:L, :]
    e_n = jnp.exp(-lam)
    e_g = jnp.exp(tot - lam)
    full = dict(A=kh * jnp.exp(lam - lw), R=r * jnp.exp(lam), Kn=kt * e_n, Bn=b * e_n, Kg=kt * e_g, Bg=b * e_g, V=v)
    e_tot = jnp.exp(tot)
    pairs = []
    for p in range(RWKV_W // LANES):
        sl = slice(p * LANES, (p + 1) * LANES)
        ops = {k: a[:, sl] for k, a in full.items()}
        ops["e_tot"] = e_tot[:, sl]
        ops["reverse"] = reverse
        pairs.append(ops)
    return pairs


def _chunk_masks(reverse):
    L = CHUNK
    t = lax.broadcasted_iota(jnp.int32, (L, 2 * L), 0)
    i = lax.broadcasted_iota(jnp.int32, (L, 2 * L), 1) & (L - 1)
    before = (i > t) if reverse else (i < t)
    return before, before | (i == t), i == t


def _chunks_prepare(chains):
    L = CHUNK
    P2 = 2 * L
    n = len(chains)
    masks = {rev: _chunk_masks(rev) for rev in {c["reverse"] for c in chains}}
    strict = [masks[c["reverse"]][0] for c in chains]
    incl = [masks[c["reverse"]][1] for c in chains]
    eye = masks[chains[0]["reverse"]][2]
    bi = lax.broadcasted_iota(jnp.int32, (P2, P2), 0)
    bj = lax.broadcasted_iota(jnp.int32, (P2, P2), 1)
    same_head = (bi >= L) == (bj >= L)
    eye2 = bi == bj
    stack = _stack_heads
    cat0 = lambda *xs: jnp.concatenate(xs, axis=0)
    cat1 = lambda *xs: jnp.concatenate(xs, axis=1)

    big = [_dot_nt(cat0(c["A"], c["R"]), cat0(stack(c["Bn"]), stack(c["Kn"]))) for c in chains]
    Mb = [jnp.where(strict[i], big[i][0:L, 0:P2], 0.0) for i in range(n)]
    Mkv = [jnp.where(strict[i], big[i][0:L, P2:2 * P2], 0.0) for i in range(n)]
    Pb = [jnp.where(incl[i], big[i][L:P2, 0:P2], 0.0) for i in range(n)]
    Pkv = [jnp.where(incl[i], big[i][L:P2, P2:2 * P2], 0.0) for i in range(n)]

    Pw = [-m for m in Mb]
    Tm = [jnp.where(eye, 1.0, 0.0) + p for p in Pw]
    Pw = [_dot(p, stack(p)) for p in Pw]
    for _ in range(int(math.log2(L)) - 2):
        PT = [_dot(cat0(p, t), stack(p)) for p, t in zip(Pw, Tm)]
        Tm = [t + pt[L:P2] for t, pt in zip(Tm, PT)]
        Pw = [pt[0:L] for pt in PT]
    Tm = [t + _dot(t, stack(p)) for t, p in zip(Tm, Pw)]

    Vs = [stack(c["V"]) for c in chains]
    MPV = [_dot(cat0(Mkv[i], Pkv[i]), Vs[i]) for i in range(n)]
    TAM = [_dot(Tm[i], cat1(stack(chains[i]["A"]), stack(MPV[i][0:L]))) for i in range(n)]
    PB = [_dot(Pb[i], cat1(stack(TAM[i][:, 0:P2]), stack(TAM[i][:, P2:2 * P2]))) for i in range(n)]
    BG = [_dot_tn(chains[i]["Bg"], TAM[i]) for i in range(n)]
    KV = [_dot_tn(chains[i]["Kg"], chains[i]["V"]) for i in range(n)]
    prepared = []
    for i in range(n):
        RA = chains[i]["R"] - PB[i][:, 0:P2]
        G2 = jnp.where(eye2, chains[i]["e_tot"], 0.0) - jnp.where(same_head, BG[i][:, 0:P2], 0.0)
        H2 = jnp.where(same_head, KV[i] - BG[i][:, P2:2 * P2], 0.0)
        prepared.append((cat0(RA, G2), MPV[i][L:P2] - PB[i][:, P2:2 * P2], H2))
    return prepared


def _chunk_advance(prepared, state):
    lhs, y0, h2 = prepared
    out = _dot(lhs, state)
    return out[0:CHUNK] + y0, out[CHUNK:] + h2


def _rwkv_scan_body(rf_ref, vf_ref, khf_ref, rb_ref, vb_ref, khb_ref, lwf_ref, bf_ref, ktf_ref, lwb_ref, bb_ref, ktb_ref,
                    yf_ref, yb_ref, s_ref):
    @pl.when(pl.program_id(1) == 0)
    def _():
        s_ref[...] = jnp.zeros_like(s_ref)

    n_pairs = RWKV_W // LANES
    fwd_refs = (rf_ref, vf_ref, khf_ref, lwf_ref, bf_ref, ktf_ref)
    bwd_refs = (rb_ref, vb_ref, khb_ref, lwb_ref, bb_ref, ktb_ref)
    slots = []
    for c in range(SCAN_CHUNKS):
        lo_f = c * CHUNK
        lo_b = (SCAN_CHUNKS - 1 - c) * CHUNK
        slots.append((_chunk_operands(*(ref[lo_f:lo_f + CHUNK, :] for ref in fwd_refs), False), lo_f,
                      _chunk_operands(*(ref[lo_b:lo_b + CHUNK, :] for ref in bwd_refs), True), lo_b))
    prepared = _chunks_prepare([ch for s in slots for ch in s[0] + s[2]])
    states = [s_ref[i] for i in range(2 * n_pairs)]
    for c, (_, lo_f, _, lo_b) in enumerate(slots):
        ys = []
        for i in range(2 * n_pairs):
            y, states[i] = _chunk_advance(prepared[c * 2 * n_pairs + i], states[i])
            ys.append(y)
        yf_ref[lo_f:lo_f + CHUNK, :] = jnp.concatenate(ys[:n_pairs], axis=1)
        yb_ref[lo_b:lo_b + CHUNK, :] = jnp.concatenate(ys[n_pairs:], axis=1)
    for i in range(2 * n_pairs):
        s_ref[i] = states[i]


def _rwkv_scan(r, v, kh, lw, b, kt, n_ctx):
    B, T, W = r.shape
    blk = SCAN_CHUNKS * CHUNK
    assert T % blk == 0 and n_ctx % blk == 0
    nc = T // blk
    ncc = n_ctx // blk
    rev = lambda j: jnp.where(j < ncc, ncc - 1 - j, nc + ncc - 1 - j)
    fwd1 = pl.BlockSpec((None, blk, W), lambda bb, j: (bb, j, 0))
    bwd1 = pl.BlockSpec((None, blk, W), lambda bb, j: (bb, rev(j), 0))
    fwd2 = pl.BlockSpec((None, None, blk, W), lambda bb, j: (bb, 0, j, 0))
    bwd2 = pl.BlockSpec((None, None, blk, W), lambda bb, j: (bb, 1, rev(j), 0))
    out = jax.ShapeDtypeStruct((B, T, W), F32)
    return pl.pallas_call(
        _rwkv_scan_body,
        grid=(B, nc),
        in_specs=[fwd1, fwd1, fwd1, bwd1, bwd1, bwd1, fwd2, fwd2, fwd2, bwd2, bwd2, bwd2],
        out_specs=[fwd1, bwd1],
        out_shape=[out, out],
        scratch_shapes=[pltpu.VMEM((2 * W // LANES, LANES, LANES), F32)],
        compiler_params=_cp("parallel", "arbitrary"),
        name="rwkv_scan",
    )(r, v, kh, r, v, kh, lw, b, kt, lw, b, kt)


def _rwkv_readout_body(yf_ref, yb_ref, bon_ref, g_ref, gng_ref, gnb_ref, e_ref, o_ref):
    y = yf_ref[...] + yb_ref[...]
    head_mean = e_ref[...] * (1.0 / RWKV_HEAD)
    mu = _dot_terms(y, head_mean, 2, 1)
    yc = y - mu
    var = _dot_terms(yc * yc, head_mean, 2, 1)
    yn = yc * lax.rsqrt(var + GN_EPS) * gng_ref[...] + gnb_ref[...]
    o_ref[...] = ((yn + bon_ref[...]) * g_ref[...]).astype(o_ref.dtype)


def _rwkv_readout(yf, yb, bonus, g, gn_g, gn_b, head_sum):
    B, T, W = yf.shape
    tt = TOK_TILE
    tok = pl.BlockSpec((None, tt, W), lambda b, i: (b, i, 0))
    row = pl.BlockSpec((1, W), lambda b, i: (0, 0))
    return pl.pallas_call(
        _rwkv_readout_body,
        grid=(B, T // tt),
        in_specs=[tok, tok, tok, tok, row, row, pl.BlockSpec((W, W), lambda b, i: (0, 0))],
        out_specs=tok,
        out_shape=jax.ShapeDtypeStruct((B, T, W), BF16),
        compiler_params=_cp("parallel", "parallel"),
        name="rwkv_readout",
    )(yf, yb, bonus, g, gn_g.reshape(1, W), gn_b.reshape(1, W), head_sum)


def _rope(y, cos, sin_signed, quarter):
    lane = lax.broadcasted_iota(jnp.int32, y.shape, 1)
    first = (lane & (2 * quarter - 1)) < quarter
    partner = jnp.where(first, pltpu.roll(y, LANES - quarter, 1), pltpu.roll(y, quarter, 1))
    return y * cos + partner * sin_signed


def _gqa_prep_body(p_ref, qg_ref, kg_ref, cos_ref, sin_ref, q_o, k_o, v_o):
    cos = cos_ref[...]
    sin = sin_ref[...]
    scale = GQA_HEAD ** -0.5 * LOG2E
    for h in range(GQA_Q_HEADS):
        sl = slice(h * GQA_HEAD, (h + 1) * GQA_HEAD)
        q = _rms(p_ref[:, sl], qg_ref[...])
        q_o[:, sl] = (_rope(q, cos, sin, GQA_HEAD // 4) * scale).astype(q_o.dtype)
    for h in range(GQA_KV_HEADS):
        sl = slice(h * GQA_HEAD, (h + 1) * GQA_HEAD)
        k = _rms(p_ref[:, GQA_Q_COLS + h * GQA_HEAD:GQA_Q_COLS + (h + 1) * GQA_HEAD], kg_ref[...])
        k_o[:, sl] = _rope(k, cos, sin, GQA_HEAD // 4).astype(k_o.dtype)
    v_o[...] = p_ref[:, GQA_Q_COLS + GQA_KV_COLS:GQA_COLS].astype(v_o.dtype)


def _gqa_proj_body(h_ref, w_ref, qg_ref, kg_ref, cos_ref, sin_ref, q_o, k_o, v_o):
    p = jnp.dot(h_ref[...], w_ref[...], preferred_element_type=F32)
    _gqa_prep_body(p, qg_ref, kg_ref, cos_ref, sin_ref, q_o, k_o, v_o)


def _gqa_proj(h, w, q_norm_g, k_norm_g, cos, sin):
    B, T, D = h.shape
    tt = PROJ_TILE
    tok = lambda w_: pl.BlockSpec((None, tt, w_), lambda b, i: (b, i, 0))
    row = pl.BlockSpec((1, GQA_HEAD), lambda b, i: (0, 0))
    tab = pl.BlockSpec((tt, LANES), lambda b, i: (i, 0))
    return pl.pallas_call(
        _gqa_proj_body,
        grid=(B, T // tt),
        in_specs=[tok(D), pl.BlockSpec(w.shape, lambda b, i: (0, 0)), row, row, tab, tab],
        out_specs=[tok(GQA_Q_COLS), tok(GQA_KV_COLS), tok(GQA_KV_COLS)],
        out_shape=[jax.ShapeDtypeStruct((B, T, GQA_Q_COLS), BF16),
                   jax.ShapeDtypeStruct((B, T, GQA_KV_COLS), BF16),
                   jax.ShapeDtypeStruct((B, T, GQA_KV_COLS), BF16)],
        compiler_params=_cp("parallel", "parallel"),
        name="gqa_proj",
    )(h, w, q_norm_g.reshape(1, -1), k_norm_g.reshape(1, -1), cos, sin)


def _mla_prep_body(p_ref, qg_ref, wq_ref, kvg_ref, wkv_ref, cos_ref, sin_ref, q_o, k_o, v_o):
    cos = cos_ref[...]
    sin = sin_ref[...]
    scale = (MLA_NOPE + MLA_ROPE) ** -0.5 * LOG2E
    cq = _rms(p_ref[:, 0:MLA_Q_RANK], qg_ref[...])
    q = jnp.dot(cq.astype(BF16), wq_ref[...], preferred_element_type=F32) * scale
    ckv = _rms(p_ref[:, MLA_Q_RANK:MLA_Q_RANK + MLA_KV_RANK], kvg_ref[...])
    kv = jnp.dot(ckv.astype(BF16), wkv_ref[...], preferred_element_type=F32)
    kr = _rope(p_ref[:, MLA_Q_RANK + MLA_KV_RANK:MLA_COLS_PAD], cos, sin, MLA_ROPE // 4).astype(k_o.dtype)
    for h in range(MLA_HEADS):
        lo = h * MLA_DK
        q_o[:, lo:lo + LANES] = q[:, lo:lo + LANES].astype(q_o.dtype)
        q_o[:, lo + LANES:lo + MLA_DK] = _rope(q[:, lo + LANES:lo + MLA_DK], cos, sin, MLA_ROPE // 4).astype(q_o.dtype)
        k_o[:, lo:lo + LANES] = kv[:, h * MLA_NOPE:(h + 1) * MLA_NOPE].astype(k_o.dtype)
        k_o[:, lo + LANES:lo + MLA_DK] = kr
    v_o[...] = kv[:, MLA_HEADS * MLA_NOPE:].astype(v_o.dtype)


def _mla_proj_body(h_ref, w_ref, qg_ref, wq_ref, kvg_ref, wkv_ref, cos_ref, sin_ref, q_o, k_o, v_o):
    p = jnp.dot(h_ref[...], w_ref[...], preferred_element_type=F32)
    _mla_prep_body(p, qg_ref, wq_ref, kvg_ref, wkv_ref, cos_ref, sin_ref, q_o, k_o, v_o)


def _mla_proj(h, w, q_norm_g, wq, kv_norm_g, wkv, cos, sin):
    B, T, D = h.shape
    tt = PROJ_TILE
    tok = lambda w_: pl.BlockSpec((None, tt, w_), lambda b, i: (b, i, 0))
    full = lambda a: pl.BlockSpec(a.shape, lambda b, i: (0,) * a.ndim)
    tab = pl.BlockSpec((tt, LANES), lambda b, i: (i, 0))
    qg = q_norm_g.reshape(1, -1)
    kvg = kv_norm_g.reshape(1, -1)
    return pl.pallas_call(
        _mla_proj_body,
        grid=(B, T // tt),
        in_specs=[tok(D), full(w), full(qg), full(wq), full(kvg), full(wkv), tab, tab],
        out_specs=[tok(MLA_HEADS * MLA_DK), tok(MLA_HEADS * MLA_DK), tok(MLA_HEADS * MLA_V)],
        out_shape=[jax.ShapeDtypeStruct((B, T, MLA_HEADS * MLA_DK), BF16),
                   jax.ShapeDtypeStruct((B, T, MLA_HEADS * MLA_DK), BF16),
                   jax.ShapeDtypeStruct((B, T, MLA_HEADS * MLA_V), BF16)],
        compiler_params=_cp("parallel", "parallel"),
        name="mla_proj",
    )(h, w, qg, wq, kvg, wkv, cos, sin)


def _attn_body(q_ref, k_ref, v_ref, o_ref, *, hq, hkv, dk, dv, n_ctx_tiles, n_ctx, n_all):
    rep = hq // hkv
    tq = q_ref.shape[0]

    def run(nk):
        for g in range(hkv):
            kg = k_ref[0:nk, g * dk:(g + 1) * dk]
            vg = v_ref[0:nk, g * dv:(g + 1) * dv]
            v_aug = jnp.concatenate([vg, jnp.ones_like(vg)], axis=1)
            q = jnp.concatenate([q_ref[:, h * dk:(h + 1) * dk] for h in range(g * rep, (g + 1) * rep)], axis=0)
            s = _dot_nt(q, kg)
            p = jnp.exp2(s - jnp.max(s, axis=-1, keepdims=True))
            o = jnp.dot(p.astype(BF16), v_aug, preferred_element_type=F32)
            o = o[:, 0:dv] / o[:, dv:dv + 1]
            for rr in range(rep):
                h = g * rep + rr
                o_ref[:, h * dv:(h + 1) * dv] = o[rr * tq:(rr + 1) * tq].astype(o_ref.dtype)

    @pl.when(pl.program_id(1) < n_ctx_tiles)
    def _():
        run(n_ctx)

    @pl.when(pl.program_id(1) >= n_ctx_tiles)
    def _():
        run(n_all)


def _attention(q, k, v, hq, hkv, dk, dv, n_ctx):
    B, T, _ = q.shape
    tq = TOK_TILE
    return pl.pallas_call(
        functools.partial(_attn_body, hq=hq, hkv=hkv, dk=dk, dv=dv, n_ctx_tiles=n_ctx // tq, n_ctx=n_ctx, n_all=T),
        grid=(B, T // tq),
        in_specs=[pl.BlockSpec((None, tq, hq * dk), lambda b, i: (b, i, 0)),
                  pl.BlockSpec((None, T, hkv * dk), lambda b, i: (b, 0, 0)),
                  pl.BlockSpec((None, T, hkv * dv), lambda b, i: (b, 0, 0))],
        out_specs=pl.BlockSpec((None, tq, hq * dv), lambda b, i: (b, i, 0)),
        out_shape=jax.ShapeDtypeStruct((B, T, hq * dv), BF16),
        compiler_params=_cp("parallel", "parallel"),
        name="attention",
    )(q, k, v)


def _rope_tables(T, n_ctx, n_rot):
    quarter = n_rot // 4
    t = jnp.arange(T - n_ctx)
    row = (t // GRID_W).astype(F32)
    col = (t % GRID_W).astype(F32)
    inv = ROPE_THETA ** (-jnp.arange(quarter, dtype=F32) / quarter)
    ar = row[:, None] * inv[None, :]
    ac = col[:, None] * inv[None, :]
    pad = LANES - n_rot
    cos = jnp.concatenate([jnp.cos(ar), jnp.cos(ar), jnp.cos(ac), jnp.cos(ac), jnp.ones((T - n_ctx, pad), F32)], axis=1)
    sin = jnp.concatenate([-jnp.sin(ar), jnp.sin(ar), -jnp.sin(ac), jnp.sin(ac), jnp.zeros((T - n_ctx, pad), F32)], axis=1)
    cos = jnp.concatenate([jnp.ones((n_ctx, LANES), F32), cos], axis=0)
    sin = jnp.concatenate([jnp.zeros((n_ctx, LANES), F32), sin], axis=0)
    return cos, sin


def _moe_body(te_ref, first_ref, slot_ref, nxt_ref, nv_ref, hs_ref, w1_hbm, w3_hbm, w2_hbm, y_ref,
              f1, f3, f2, c1, c3, c2, sem, *, layer):
    i = pl.program_id(0)

    def weight_copies(e, s):
        return (pltpu.make_async_copy(w1_hbm.at[layer, e], f1.at[s], sem.at[s, 0]),
                pltpu.make_async_copy(w3_hbm.at[layer, e], f3.at[s], sem.at[s, 1]),
                pltpu.make_async_copy(w2_hbm.at[layer, e], f2.at[s], sem.at[s, 2]))

    @pl.when(i == 0)
    def _():
        for cp in weight_copies(te_ref[0], 0):
            cp.start()

    @pl.when(first_ref[i] == 1)
    def _():
        s = slot_ref[i]
        for cp in weight_copies(te_ref[i], s):
            cp.wait()

        @pl.when(nxt_ref[i] >= 0)
        def _():
            for cp in weight_copies(nxt_ref[i], 1 - s):
                cp.start()

        c1[...] = f1[s].astype(BF16)
        c3[...] = f3[s].astype(BF16)
        c2[...] = f2[s].astype(BF16)

    @pl.when(i < nv_ref[0])
    def _():
        hs = hs_ref[...].astype(BF16)
        a = jnp.dot(hs, c1[...], preferred_element_type=F32)
        b = jnp.dot(hs, c3[...], preferred_element_type=F32)
        act = a * _sigmoid(a) * b
        y = jnp.dot(act.astype(BF16), c2[...], preferred_element_type=F32)
        y_ref[...] = y.astype(y_ref.dtype)

    @pl.when(i >= nv_ref[0])
    def _():
        y_ref[...] = jnp.zeros_like(y_ref)


def _moe_experts(tile_expert, run_first, run_slot, run_next, n_valid, hs, w1, w3, w2, layer):
    NP, D = hs.shape
    tm = MOE_TM
    DE = w1.shape[-1]
    grid_spec = pltpu.PrefetchScalarGridSpec(
        num_scalar_prefetch=5,
        grid=(NP // tm,),
        in_specs=[pl.BlockSpec((tm, D), lambda i, *_: (i, 0)),
                  pl.BlockSpec(memory_space=pl.ANY),
                  pl.BlockSpec(memory_space=pl.ANY),
                  pl.BlockSpec(memory_space=pl.ANY)],
        out_specs=pl.BlockSpec((tm, D), lambda i, *_: (i, 0)),
        scratch_shapes=[pltpu.VMEM((2, D, DE), F32), pltpu.VMEM((2, D, DE), F32), pltpu.VMEM((2, DE, D), F32),
                        pltpu.VMEM((D, DE), BF16), pltpu.VMEM((D, DE), BF16), pltpu.VMEM((DE, D), BF16),
                        pltpu.SemaphoreType.DMA((2, 3))],
    )
    return pl.pallas_call(
        functools.partial(_moe_body, layer=layer),
        grid_spec=grid_spec,
        out_shape=jax.ShapeDtypeStruct((NP, D), BF16),
        compiler_params=_cp("arbitrary"),
        name="moe_experts",
    )(tile_expert, run_first, run_slot, run_next, n_valid, hs, w1, w3, w2)


def _moe(h2, re, w1, w3, w2, layer):
    N = h2.shape[0]
    tm = MOE_TM
    n_tiles = (2 * N) // tm + N_EXPERTS
    e_flat = re.reshape(2 * N)
    onehot = (e_flat[:, None] == jnp.arange(N_EXPERTS, dtype=jnp.int32)[None, :]).astype(jnp.int32)
    csum = jnp.cumsum(onehot, axis=0)
    rank = jnp.sum(onehot * (csum - 1), axis=1)
    counts = csum[-1]
    ptiles = (counts + tm - 1) // tm
    tile_end = jnp.cumsum(ptiles)
    tile_start = tile_end - ptiles
    pos = tile_start[e_flat] * tm + rank
    n_valid = tile_end[-1:].astype(jnp.int32)
    tile_ids = jnp.arange(n_tiles, dtype=jnp.int32)
    tile_expert = jnp.minimum(jnp.sum((tile_end[None, :] <= tile_ids[:, None]).astype(jnp.int32), axis=1), N_EXPERTS - 1)
    experts = jnp.arange(N_EXPERTS, dtype=jnp.int32)
    owns = ptiles > 0
    run_no = jnp.cumsum(owns.astype(jnp.int32)) - 1
    later = jnp.where(owns[None, :] & (experts[None, :] > experts[:, None]), experts[None, :], N_EXPERTS)
    next_run = jnp.min(later, axis=1)
    next_run = jnp.where(next_run < N_EXPERTS, next_run, -1).astype(jnp.int32)
    run_first = ((tile_ids == tile_start[tile_expert]) & (tile_ids < n_valid[0])).astype(jnp.int32)
    run_slot = (run_no[tile_expert] % 2).astype(jnp.int32)
    run_next = next_run[tile_expert]
    spread = jnp.arange(n_tiles * tm, dtype=jnp.int32) % N
    src = spread.at[pos].set(jnp.arange(2 * N, dtype=jnp.int32) // 2)
    rows = lambda a, idx: a.at[idx].get(mode="promise_in_bounds")
    hs = rows(h2, src)
    y = _moe_experts(tile_expert, run_first, run_slot, run_next, n_valid, hs, w1, w3, w2, layer)
    pos2 = pos.reshape(N, 2)
    return rows(y, pos2[:, 0]), rows(y, pos2[:, 1])


def kernel(x, c, ctx, c_ctx, mod_w, mod_b, norm1_g, norm2_g, w_in, w_out, shift_prev, shift_next, decay_w0, decay_up, iclr_a0, iclr_up, gate_up, k_k, k_a, r_k, gn_g, gn_b, q_norm_g, k_norm_g, mla_q_norm_g, mla_w_uq, mla_kv_norm_g, mla_w_ukv, router_gw, router_gb, router_ew, router_eb, exp_w1, exp_w3, exp_w2, final_norm_g):
    B, S, D = x.shape
    C = ctx.shape[1]
    T = C + S
    depth = mod_w.shape[0]
    assert C == TOK_TILE and S % TOK_TILE == 0 and B <= CTX_ROW
    n_ctx_tiles = C // TOK_TILE

    X = jnp.concatenate([ctx, x], axis=1)
    cc = jnp.zeros((SUBLANES, D), F32).at[:B].set(c).at[CTX_ROW].set(c_ctx)
    mods_all = _modulation(cc, mod_w, mod_b).reshape(depth, SUBLANES * N_MOD, 1, D)

    cos_g, sin_g = _rope_tables(T, C, GQA_HEAD)
    cos_m, sin_m = _rope_tables(T, C, MLA_ROPE)
    hid = jnp.arange(RWKV_W) // RWKV_HEAD
    head_sum = (hid[:, None] == hid[None, :]).astype(F32)

    for l in range(depth):
        mods = mods_all[l]
        w_r = w_in[l][:, :RWKV_COLS].astype(BF16)
        w_g = w_in[l][:, RWKV_COLS:RWKV_COLS + GQA_COLS].astype(BF16)
        w_m = jnp.pad(w_in[l][:, RWKV_COLS + GQA_COLS:], ((0, 0), (0, MLA_COLS_PAD - MLA_COLS))).astype(BF16)

        if l == 0:
            h = _norm_mod(X, norm1_g[l], mods, 0, 1, n_ctx_tiles)
        pr = _matmul(h.reshape(B * T, D), w_r).reshape(B, T, RWKV_COLS)

        wup_pad = jnp.pad(decay_up[l], ((0, 0), (0, ICLR_RANK), (0, 0)))
        aup_pad = jnp.pad(iclr_up[l], ((0, 0), (DECAY_RANK, 0), (0, 0)))
        r, v, kh, lw, b, kt, g, bonus = _rwkv_prep(pr, shift_prev[l], shift_next[l], decay_w0[l], wup_pad,
                                                  iclr_a0[l], aup_pad, gate_up[l], k_k[l], k_a[l], r_k[l], head_sum)
        yf, yb = _rwkv_scan(r, v, kh, lw, b, kt, C)
        o_r = _rwkv_readout(yf, yb, bonus, g, gn_g[l], gn_b[l], head_sum)

        q, k, vv = _gqa_proj(h, w_g, q_norm_g[l], k_norm_g[l], cos_g, sin_g)
        o_g = _attention(q, k, vv, GQA_Q_HEADS, GQA_KV_HEADS, GQA_HEAD, GQA_HEAD, C)

        wq = mla_w_uq[l].reshape(MLA_Q_RANK, MLA_HEADS, MLA_NOPE + MLA_ROPE)
        wq = jnp.pad(wq, ((0, 0), (0, 0), (0, MLA_DK - MLA_NOPE - MLA_ROPE))).reshape(MLA_Q_RANK, MLA_HEADS * MLA_DK)
        wkv = mla_w_ukv[l].reshape(MLA_KV_RANK, MLA_HEADS, MLA_NOPE + MLA_V)
        wkv = jnp.concatenate([wkv[:, :, :MLA_NOPE].reshape(MLA_KV_RANK, -1), wkv[:, :, MLA_NOPE:].reshape(MLA_KV_RANK, -1)], axis=1)
        qm, km, vm = _mla_proj(h, w_m, mla_q_norm_g[l], wq.astype(BF16), mla_kv_norm_g[l], wkv.astype(BF16), cos_m, sin_m)
        o_m = _attention(qm, km, vm, MLA_HEADS, MLA_HEADS, MLA_DK, MLA_V, C)

        X = _matmul_gated_residual([o_r, o_g, o_m], w_out[l].astype(BF16), X, mods, 2, C)

        wr = jnp.pad(jnp.concatenate([router_gw[l], router_ew[l]], axis=1), ((0, 0), (0, LANES - N_GROUPS - N_EXPERTS)))
        br = jnp.pad(jnp.concatenate([router_gb[l], router_eb[l]]), (0, LANES - N_GROUPS - N_EXPERTS)).reshape(1, LANES)
        h2, rw, re = _norm_router(X, norm2_g[l], mods, 3, 4, n_ctx_tiles, wr, br)
        y0, y1 = _moe(h2.reshape(B * T, D), re.reshape(B * T, LANES)[:, :2], exp_w1, exp_w3, exp_w2, l)
        y0 = y0.reshape(B, T, D)
        y1 = y1.reshape(B, T, D)
        if l + 1 < depth:
            X, h = _moe_residual_norm(X, y0, y1, rw, mods, 5, n_ctx_tiles, norm1_g[l + 1], mods_all[l + 1])
        else:
            out = _moe_residual_final(X, y0, y1, rw, mods, 5, n_ctx_tiles, final_norm_g)
    return out
```

```python
import functools
import math

import jax
import jax.numpy as jnp
import numpy as np
from jax import lax
from jax.experimental import pallas as pl
from jax.experimental.pallas import tpu as pltpu

F32 = jnp.float32
BF16 = jnp.bfloat16

V7X_VMEM_BYTES = 64 * 1024 * 1024
VMEM_LIMIT = V7X_VMEM_BYTES - 8 * 1024 * 1024
LANES = 128
SUBLANES = 8

GRID_W = 64
ROPE_THETA = 10000.0
NORM_EPS = 1e-6
GN_EPS = 64e-5
DECAY_SCALE = math.exp(-0.5)
LOG2E = math.log2(math.e)

RWKV_HEAD = 64
RWKV_W = 512
DECAY_RANK = 64
ICLR_RANK = 64
GATE_RANK = 128
RWKV_COLS = 3 * RWKV_W + DECAY_RANK + ICLR_RANK + GATE_RANK
LOWRANK_OFF = 3 * RWKV_W
GATE_OFF = LOWRANK_OFF + DECAY_RANK + ICLR_RANK
CHUNK = 64
SCAN_CHUNKS = 2

GQA_HEAD = 128
GQA_Q_HEADS = 8
GQA_KV_HEADS = 2
GQA_Q_COLS = GQA_Q_HEADS * GQA_HEAD
GQA_KV_COLS = GQA_KV_HEADS * GQA_HEAD
GQA_COLS = GQA_Q_COLS + 2 * GQA_KV_COLS

MLA_HEADS = 4
MLA_NOPE = 128
MLA_ROPE = 64
MLA_V = 128
MLA_Q_RANK = 384
MLA_KV_RANK = 256
MLA_COLS = MLA_Q_RANK + MLA_KV_RANK + MLA_ROPE
MLA_COLS_PAD = 768
MLA_DK = 2 * LANES

N_GROUPS = 4
EXPERTS_PER_GROUP = 8
N_EXPERTS = 32
D_EXPERT = 256
MOE_TM = 256

TOK_TILE = 256
PROJ_TILE = 768
OUT_TILE = 384
ATTN_STACK = 1
N_MOD = 6
CTX_ROW = 4


def _cp(*sem):
    return pltpu.CompilerParams(dimension_semantics=sem, vmem_limit_bytes=VMEM_LIMIT)


def _sigmoid(x):
    return 1.0 / (1.0 + jnp.exp(-x))


def _bf16_terms(x, n):
    terms = []
    for _ in range(n):
        t = x.astype(BF16)
        terms.append(t)
        x = x - t.astype(F32)
    return terms


def _dot_terms(x, w, nx, nw):
    xs = _bf16_terms(x, nx)
    ws = _bf16_terms(w, nw)
    acc = None
    for i in range(nx):
        for j in range(nw):
            if i + j < max(nx, nw):
                p = jnp.dot(xs[i], ws[j], preferred_element_type=F32)
                acc = p if acc is None else acc + p
    return acc


def _mod_body(c_ref, w_ref, b_ref, o_ref):
    c = c_ref[...]
    s = c * _sigmoid(c)
    o_ref[...] = _dot_terms(s, w_ref[...], 2, 2) + b_ref[...]


def _modulation(cc, mod_w, mod_b):
    L, D, N = mod_w.shape
    tn = 1024
    return pl.pallas_call(
        _mod_body,
        grid=(L, N // tn),
        in_specs=[pl.BlockSpec((SUBLANES, D), lambda l, j: (0, 0)),
                  pl.BlockSpec((None, D, tn), lambda l, j: (l, 0, j)),
                  pl.BlockSpec((None, 1, tn), lambda l, j: (l, 0, j))],
        out_specs=pl.BlockSpec((None, SUBLANES, tn), lambda l, j: (l, 0, j)),
        out_shape=jax.ShapeDtypeStruct((L, SUBLANES, N), F32),
        compiler_params=_cp("parallel", "parallel"),
        name="modulation",
    )(cc, mod_w, mod_b.reshape(L, 1, N))


def _mod_spec(which, n_ctx_tiles, D):
    return pl.BlockSpec((None, 1, D), lambda b, i: (jnp.where(i < n_ctx_tiles, CTX_ROW, b) * N_MOD + which, 0, 0))


def _rms(x, g):
    return x * lax.rsqrt(jnp.mean(x * x, axis=-1, keepdims=True) + NORM_EPS) * g


def _norm_mod_body(x_ref, g_ref, sh_ref, sc_ref, o_ref):
    y = _rms(x_ref[...], g_ref[...])
    o_ref[...] = (y * (1.0 + sc_ref[...]) + sh_ref[...]).astype(o_ref.dtype)


def _route(logits):
    lane = lax.broadcasted_iota(jnp.int32, logits.shape, 1)
    lane_f = lane.astype(F32)
    neg = jnp.float32(-1e30)
    far = jnp.float32(1e9)
    first_at = lambda hit: jnp.min(jnp.where(hit, lane_f, far), axis=-1, keepdims=True).astype(jnp.int32)
    gl = jnp.where(lane < N_GROUPS, logits, neg)
    gmax = jnp.max(gl, axis=-1, keepdims=True)
    gidx = first_at(gl == gmax)
    p_sel = 1.0 / jnp.sum(jnp.exp(gl - gmax), axis=-1, keepdims=True)
    lo = N_GROUPS + gidx * EXPERTS_PER_GROUP
    el = jnp.where((lane >= lo) & (lane < lo + EXPERTS_PER_GROUP), logits, neg)
    m1 = jnp.max(el, axis=-1, keepdims=True)
    i1 = first_at(el == m1)
    el2 = jnp.where(lane == i1, neg, el)
    m2 = jnp.max(el2, axis=-1, keepdims=True)
    i2 = first_at(el2 == m2)
    t = jnp.exp(m2 - m1)
    w1 = p_sel / (1.0 + t)
    w2 = p_sel * t / (1.0 + t)
    rw = jnp.where(lane == 0, w1, jnp.where(lane == 1, w2, 0.0))
    re = jnp.where(lane == 0, i1 - N_GROUPS, jnp.where(lane == 1, i2 - N_GROUPS, 0))
    return rw, re


def _norm_router_body(x_ref, g_ref, sh_ref, sc_ref, wr_ref, br_ref, h_ref, rw_ref, re_ref):
    y = _rms(x_ref[...], g_ref[...])
    h = y * (1.0 + sc_ref[...]) + sh_ref[...]
    h_ref[...] = h.astype(h_ref.dtype)
    logits = _dot_terms(h, wr_ref[...], 2, 2) + br_ref[...]
    rw, re = _route(logits)
    rw_ref[...] = rw
    re_ref[...] = re


def _norm_mod(X, g, mods, shift_i, scale_i, n_ctx_tiles):
    B, T, D = X.shape
    tt = TOK_TILE
    return pl.pallas_call(
        _norm_mod_body,
        grid=(B, T // tt),
        in_specs=[pl.BlockSpec((None, tt, D), lambda b, i: (b, i, 0)),
                  pl.BlockSpec((1, D), lambda b, i: (0, 0)),
                  _mod_spec(shift_i, n_ctx_tiles, D),
                  _mod_spec(scale_i, n_ctx_tiles, D)],
        out_specs=pl.BlockSpec((None, tt, D), lambda b, i: (b, i, 0)),
        out_shape=jax.ShapeDtypeStruct((B, T, D), BF16),
        compiler_params=_cp("parallel", "parallel"),
        name="norm_mod",
    )(X, g.reshape(1, D), mods, mods)


def _norm_router(X, g, mods, shift_i, scale_i, n_ctx_tiles, wr, br):
    B, T, D = X.shape
    tt = TOK_TILE
    tok = lambda w, dt: (pl.BlockSpec((None, tt, w), lambda b, i: (b, i, 0)), jax.ShapeDtypeStruct((B, T, w), dt))
    outs = [tok(D, F32), tok(LANES, F32), tok(LANES, jnp.int32)]
    return pl.pallas_call(
        _norm_router_body,
        grid=(B, T // tt),
        in_specs=[pl.BlockSpec((None, tt, D), lambda b, i: (b, i, 0)),
                  pl.BlockSpec((1, D), lambda b, i: (0, 0)),
                  _mod_spec(shift_i, n_ctx_tiles, D),
                  _mod_spec(scale_i, n_ctx_tiles, D),
                  pl.BlockSpec((D, LANES), lambda b, i: (0, 0)),
                  pl.BlockSpec((1, LANES), lambda b, i: (0, 0))],
        out_specs=[o[0] for o in outs],
        out_shape=[o[1] for o in outs],
        compiler_params=_cp("parallel", "parallel"),
        name="norm_router",
    )(X, g.reshape(1, D), mods, mods, wr, br)


def _mm_body(a_ref, w_ref, o_ref):
    o_ref[...] = jnp.dot(a_ref[...], w_ref[...], preferred_element_type=F32).astype(o_ref.dtype)


def _matmul(a, w, tm=512):
    M, K = a.shape
    N = w.shape[1]
    return pl.pallas_call(
        _mm_body,
        grid=(M // tm,),
        in_specs=[pl.BlockSpec((tm, K), lambda i: (i, 0)),
                  pl.BlockSpec((K, N), lambda i: (0, 0))],
        out_specs=pl.BlockSpec((tm, N), lambda i: (i, 0)),
        out_shape=jax.ShapeDtypeStruct((M, N), F32),
        compiler_params=_cp("parallel"),
        name="token_matmul",
    )(a, w)


def _mm_res_body(*refs, n_parts, n_ctx, tm, tile0):
    a_refs = refs[:n_parts]
    w_ref, x_ref, gl_ref, gc_ref, o_ref = refs[n_parts:]
    acc = None
    k0 = 0
    for a_ref in a_refs:
        k1 = k0 + a_ref.shape[-1]
        part = jnp.dot(a_ref[...], w_ref[k0:k1, :], preferred_element_type=F32)
        acc = part if acc is None else acc + part
        k0 = k1
    row = (pl.program_id(1) + tile0) * tm + lax.broadcasted_iota(jnp.int32, (tm, 1), 0)
    gate = jnp.where(row < n_ctx, gc_ref[...], gl_ref[...])
    o_ref[...] = x_ref[...] + gate * acc


def _matmul_gated_residual(parts, w, X, mods, gate_i, n_ctx, tm, latents_only):
    B, T, D = X.shape
    tile0 = n_ctx // tm if latents_only else 0
    assert T % tm == 0 and (n_ctx % tm == 0 or not latents_only)
    assert sum(p.shape[-1] for p in parts) == w.shape[0]
    rows = pl.BlockSpec((None, tm, D), lambda b, i: (b, i + tile0, 0))
    return pl.pallas_call(
        functools.partial(_mm_res_body, n_parts=len(parts), n_ctx=n_ctx, tm=tm, tile0=tile0),
        grid=(B, T // tm - tile0),
        in_specs=[pl.BlockSpec((None, tm, p.shape[-1]), lambda b, i: (b, i + tile0, 0)) for p in parts] + [
                  pl.BlockSpec(w.shape, lambda b, i: (0, 0)),
                  rows,
                  pl.BlockSpec((None, 1, D), lambda b, i: (b * N_MOD + gate_i, 0, 0)),
                  pl.BlockSpec((None, 1, D), lambda b, i: (CTX_ROW * N_MOD + gate_i, 0, 0))],
        out_specs=pl.BlockSpec((None, tm, D), lambda b, i: (b, i, 0)),
        out_shape=jax.ShapeDtypeStruct((B, T - tile0 * tm, D), F32),
        compiler_params=_cp("parallel", "parallel"),
        name="out_proj_residual",
    )(*parts, w, X, mods, mods)


def _moe_residual(x_ref, y0_ref, y1_ref, rw_ref, gate_ref):
    rw = rw_ref[...]
    moe = rw[:, 0:1] * y0_ref[...].astype(F32) + rw[:, 1:2] * y1_ref[...].astype(F32)
    return x_ref[...] + gate_ref[...] * moe


def _moe_residual_norm_body(x_ref, y0_ref, y1_ref, rw_ref, gate_ref, g_ref, sh_ref, sc_ref, x_o, h_o):
    x = _moe_residual(x_ref, y0_ref, y1_ref, rw_ref, gate_ref)
    x_o[...] = x
    h_o[...] = (_rms(x, g_ref[...]) * (1.0 + sc_ref[...]) + sh_ref[...]).astype(h_o.dtype)


def _moe_residual_final_body(x_ref, y0_ref, y1_ref, rw_ref, gate_ref, g_ref, o_ref):
    o_ref[...] = _rms(_moe_residual(x_ref, y0_ref, y1_ref, rw_ref, gate_ref), g_ref[...])


def _moe_residual_norm(X, Y0, Y1, rw, mods, gate_i, n_ctx_tiles, g_next, mods_next):
    B, T, D = X.shape
    tt = TOK_TILE
    blk = pl.BlockSpec((None, tt, D), lambda b, i: (b, i, 0))
    return pl.pallas_call(
        _moe_residual_norm_body,
        grid=(B, T // tt),
        in_specs=[blk, blk, blk, pl.BlockSpec((None, tt, LANES), lambda b, i: (b, i, 0)),
                  _mod_spec(gate_i, n_ctx_tiles, D), pl.BlockSpec((1, D), lambda b, i: (0, 0)),
                  _mod_spec(0, n_ctx_tiles, D), _mod_spec(1, n_ctx_tiles, D)],
        out_specs=[blk, blk],
        out_shape=[jax.ShapeDtypeStruct((B, T, D), F32), jax.ShapeDtypeStruct((B, T, D), BF16)],
        compiler_params=_cp("parallel", "parallel"),
        name="moe_residual_norm",
    )(X, Y0, Y1, rw, mods, g_next.reshape(1, D), mods_next, mods_next)


def _moe_residual_final(X, Y0, Y1, rw, mods, gate_i, n_ctx_tiles, g_final):
    B, T, D = X.shape
    tt = TOK_TILE
    S = T - n_ctx_tiles * tt
    lat = lambda w: pl.BlockSpec((None, tt, w), lambda b, i: (b, i + n_ctx_tiles, 0))
    return pl.pallas_call(
        _moe_residual_final_body,
        grid=(B, S // tt),
        in_specs=[lat(D), lat(D), lat(D), lat(LANES),
                  pl.BlockSpec((None, 1, D), lambda b, i: (b * N_MOD + gate_i, 0, 0)),
                  pl.BlockSpec((1, D), lambda b, i: (0, 0))],
        out_specs=pl.BlockSpec((None, tt, D), lambda b, i: (b, i, 0)),
        out_shape=jax.ShapeDtypeStruct((B, S, D), F32),
        compiler_params=_cp("parallel", "parallel"),
        name="moe_residual_final",
    )(X, Y0, Y1, rw, mods, g_final.reshape(1, D))


def _rwkv_prep_body(p_ref, pv_ref, nx_ref, mup_ref, mun_ref, w0_ref, wup_ref, a0_ref, aup_ref, gup_ref,
                    kk_ref, ka_ref, rk_ref, e_ref,
                    r_o, v_o, kh_o, lw_o, b_o, kt_o, g_o, bon_o, *, n_tiles, tt):
    i = pl.program_id(1)
    p = p_ref[...]
    seq_first = i <= 1
    seq_last = (i == 0) | (i == n_tiles - 1)
    prow = jnp.where(seq_first, 0.0, pv_ref[SUBLANES - 1:SUBLANES, :])
    nrow = jnp.where(seq_last, 0.0, nx_ref[0:1, :])
    rid = lax.broadcasted_iota(jnp.int32, (tt, 1), 0)
    prev = jnp.where(rid == 0, prow, pltpu.roll(p, 1, 0))
    nxt = jnp.where(rid == tt - 1, nrow, pltpu.roll(p, tt - 1, 0))
    z = p + mup_ref[...] * (prev - p) + mun_ref[...] * (nxt - p)

    W = RWKV_W
    r = z[:, 0:W]
    k = z[:, W:2 * W]
    v = z[:, 2 * W:3 * W]
    lowrank = z[:, LOWRANK_OFF:LOWRANK_OFF + LANES]
    gd = z[:, GATE_OFF:GATE_OFF + GATE_RANK]
    head_sum = e_ref[...]

    kap = k * kk_ref[...]
    ss = _dot_terms(kap * kap, head_sum, 2, 1)
    khat = kap * lax.rsqrt(ss + 1e-12)
    wd_t = jnp.tanh(lowrank)
    g_o[...] = _dot_terms(_sigmoid(gd), gup_ref[...], 1, 1)
    r_o[...] = r
    v_o[...] = v
    kh_o[...] = khat
    kt_sum = None
    for d in range(2):
        dec = _dot_terms(wd_t, wup_ref[d], 2, 2)
        lw_o[d] = -DECAY_SCALE * _sigmoid(w0_ref[d:d + 1, :] + dec)
        a = _sigmoid(a0_ref[d:d + 1, :] + _dot_terms(lowrank, aup_ref[d], 1, 1))
        kt = k * (1.0 + (a - 1.0) * ka_ref[...])
        kt_o[d] = kt
        b_o[d] = a * khat
        kt_sum = kt if kt_sum is None else kt_sum + kt
    bsum = _dot_terms(r * kt_sum * rk_ref[...], head_sum, 2, 1)
    bon_o[...] = bsum * v


def _rwkv_prep(pr, mu_prev, mu_next, w0, wup_pad, a0, aup_pad, g_up, k_k, k_a, r_k, head_sum):
    B, T, _ = pr.shape
    tt = TOK_TILE
    W = RWKV_W
    n_tiles = T // tt
    n8 = tt // SUBLANES
    row = lambda v: v.reshape(1, -1)
    full = lambda a: pl.BlockSpec(a.shape, lambda b, i: (0,) * a.ndim)
    tok = pl.BlockSpec((None, tt, W), lambda b, i: (b, i, 0))
    tok2 = pl.BlockSpec((None, 2, tt, W), lambda b, i: (b, 0, i, 0))
    s1 = jax.ShapeDtypeStruct((B, T, W), F32)
    s2 = jax.ShapeDtypeStruct((B, 2, T, W), F32)
    consts = [row(mu_prev), row(mu_next), w0, wup_pad, a0, aup_pad, g_up, row(k_k), row(k_a), row(r_k), head_sum]
    return pl.pallas_call(
        functools.partial(_rwkv_prep_body, n_tiles=n_tiles, tt=tt),
        grid=(B, n_tiles),
        in_specs=[pl.BlockSpec((None, tt, RWKV_COLS), lambda b, i: (b, i, 0)),
                  pl.BlockSpec((None, SUBLANES, RWKV_COLS), lambda b, i: (b, jnp.maximum(i * n8 - 1, 0), 0)),
                  pl.BlockSpec((None, SUBLANES, RWKV_COLS), lambda b, i: (b, jnp.minimum((i + 1) * n8, T // SUBLANES - 1), 0)),
                  ] + [full(a) for a in consts],
        out_specs=[tok, tok, tok, tok2, tok2, tok2, tok, tok],
        out_shape=[s1, s1, s1, s2, s2, s2, s1, s1],
        compiler_params=_cp("parallel", "parallel"),
        name="rwkv_prep",
    )(pr, pr, pr, *consts)


def _stack_heads(x):
    lane = lax.broadcasted_iota(jnp.int32, x.shape, 1)
    first = lane < RWKV_HEAD
    return jnp.concatenate([jnp.where(first, x, 0.0), jnp.where(first, 0.0, x)], axis=0)


def _dot(a, b):
    return jnp.dot(a.astype(BF16), b.astype(BF16), preferred_element_type=F32)


def _dot_nt(a, b):
    return lax.dot_general(a.astype(BF16), b.astype(BF16), (((1,), (1,)), ((), ())), preferred_element_type=F32)


def _dot_tn(a, b):
    return jnp.dot(a.T.astype(BF16), b.astype(BF16), preferred_element_type=F32)


def _chunk_operands(r, v, kh, lw, b, kt, reverse):
    L = CHUNK
    ti = lax.broadcasted_iota(jnp.int32, (L, L), 0)
    tj = lax.broadcasted_iota(jnp.int32, (L, L), 1)
    tri = jnp.where((ti <= tj) if reverse else (ti >= tj), 1.0, 0.0)
    lam = _dot_terms(tri, lw, 1, 3)
    tot = lam[0:1, :] if reverse else lam[L - 1:L, :]
    e_n = jnp.exp(-lam)
    e_g = jnp.exp(tot - lam)
    full = dict(A=kh * jnp.exp(lam - lw), R=r * jnp.exp(lam), Kn=kt * e_n, Bn=b * e_n, Kg=kt * e_g, Bg=b * e_g, V=v)
    e_tot = jnp.exp(tot)
    pairs = []
    for p in range(RWKV_W // LANES):
        sl = slice(p * LANES, (p + 1) * LANES)
        ops = {k: a[:, sl] for k, a in full.items()}
        ops["e_tot"] = e_tot[:, sl]
        ops["reverse"] = reverse
        pairs.append(ops)
    return pairs


def _chunk_masks(reverse):
    L = CHUNK
    t = lax.broadcasted_iota(jnp.int32, (L, 2 * L), 0)
    i = lax.broadcasted_iota(jnp.int32, (L, 2 * L), 1) & (L - 1)
    before = (i > t) if reverse else (i < t)
    return before, before | (i == t), i == t


def _chunks_prepare(chains):
    L = CHUNK
    P2 = 2 * L
    n = len(chains)
    masks = {rev: _chunk_masks(rev) for rev in {c["reverse"] for c in chains}}
    strict = [masks[c["reverse"]][0] for c in chains]
    incl = [masks[c["reverse"]][1] for c in chains]
    eye = masks[chains[0]["reverse"]][2]
    bi = lax.broadcasted_iota(jnp.int32, (P2, P2), 0)
    bj = lax.broadcasted_iota(jnp.int32, (P2, P2), 1)
    same_head = (bi >= L) == (bj >= L)
    eye2 = bi == bj
    stack = _stack_heads
    cat0 = lambda *xs: jnp.concatenate(xs, axis=0)
    cat1 = lambda *xs: jnp.concatenate(xs, axis=1)

    big = [_dot_nt(cat0(c["A"], c["R"]), cat0(stack(c["Bn"]), stack(c["Kn"]))) for c in chains]
    Mb = [jnp.where(strict[i], big[i][0:L, 0:P2], 0.0) for i in range(n)]
    Mkv = [jnp.where(strict[i], big[i][0:L, P2:2 * P2], 0.0) for i in range(n)]
    Pb = [jnp.where(incl[i], big[i][L:P2, 0:P2], 0.0) for i in range(n)]
    Pkv = [jnp.where(incl[i], big[i][L:P2, P2:2 * P2], 0.0) for i in range(n)]

    Pw = [-m for m in Mb]
    Tm = [jnp.where(eye, 1.0, 0.0) + p for p in Pw]
    Pw = [_dot(p, stack(p)) for p in Pw]
    for _ in range(int(math.log2(L)) - 2):
        PT = [_dot(cat0(p, t), stack(p)) for p, t in zip(Pw, Tm)]
        Tm = [t + pt[L:P2] for t, pt in zip(Tm, PT)]
        Pw = [pt[0:L] for pt in PT]
    Tm = [t + _dot(t, stack(p)) for t, p in zip(Tm, Pw)]

    Vs = [stack(c["V"]) for c in chains]
    MPV = [_dot(cat0(Mkv[i], Pkv[i]), Vs[i]) for i in range(n)]
    TAM = [_dot(Tm[i], cat1(stack(chains[i]["A"]), stack(MPV[i][0:L]))) for i in range(n)]
    PB = [_dot(Pb[i], cat1(stack(TAM[i][:, 0:P2]), stack(TAM[i][:, P2:2 * P2]))) for i in range(n)]
    BG = [_dot_tn(chains[i]["Bg"], TAM[i]) for i in range(n)]
    KV = [_dot_tn(chains[i]["Kg"], chains[i]["V"]) for i in range(n)]
    prepared = []
    for i in range(n):
        RA = chains[i]["R"] - PB[i][:, 0:P2]
        G2 = jnp.where(eye2, chains[i]["e_tot"], 0.0) - jnp.where(same_head, BG[i][:, 0:P2], 0.0)
        H2 = jnp.where(same_head, KV[i] - BG[i][:, P2:2 * P2], 0.0)
        prepared.append((cat0(RA, G2), MPV[i][L:P2] - PB[i][:, P2:2 * P2], H2))
    return prepared


def _chunk_advance(prepared, state):
    lhs, y0, h2 = prepared
    out = _dot(lhs, state)
    return out[0:CHUNK] + y0, out[CHUNK:] + h2


def _rwkv_scan_body(rf_ref, vf_ref, khf_ref, rb_ref, vb_ref, khb_ref, lwf_ref, bf_ref, ktf_ref, lwb_ref, bb_ref, ktb_ref,
                    yf_ref, yb_ref, s_ref):
    @pl.when(pl.program_id(1) == 0)
    def _():
        s_ref[...] = jnp.zeros_like(s_ref)

    n_pairs = RWKV_W // LANES
    fwd_refs = (rf_ref, vf_ref, khf_ref, lwf_ref, bf_ref, ktf_ref)
    bwd_refs = (rb_ref, vb_ref, khb_ref, lwb_ref, bb_ref, ktb_ref)
    slots = []
    for c in range(SCAN_CHUNKS):
        lo_f = c * CHUNK
        lo_b = (SCAN_CHUNKS - 1 - c) * CHUNK
        slots.append((_chunk_operands(*(ref[lo_f:lo_f + CHUNK, :] for ref in fwd_refs), False), lo_f,
                      _chunk_operands(*(ref[lo_b:lo_b + CHUNK, :] for ref in bwd_refs), True), lo_b))
    prepared = _chunks_prepare([ch for s in slots for ch in s[0] + s[2]])
    states = [s_ref[i] for i in range(2 * n_pairs)]
    for c, (_, lo_f, _, lo_b) in enumerate(slots):
        ys = []
        for i in range(2 * n_pairs):
            y, states[i] = _chunk_advance(prepared[c * 2 * n_pairs + i], states[i])
            ys.append(y)
        yf_ref[lo_f:lo_f + CHUNK, :] = jnp.concatenate(ys[:n_pairs], axis=1)
        yb_ref[lo_b:lo_b + CHUNK, :] = jnp.concatenate(ys[n_pairs:], axis=1)
    for i in range(2 * n_pairs):
        s_ref[i] = states[i]


def _rwkv_scan(r, v, kh, lw, b, kt, n_ctx):
    B, T, W = r.shape
    blk = SCAN_CHUNKS * CHUNK
    assert T % blk == 0 and n_ctx % blk == 0
    nc = T // blk
    ncc = n_ctx // blk
    rev = lambda j: jnp.where(j < ncc, ncc - 1 - j, nc + ncc - 1 - j)
    fwd1 = pl.BlockSpec((None, blk, W), lambda bb, j: (bb, j, 0))
    bwd1 = pl.BlockSpec((None, blk, W), lambda bb, j: (bb, rev(j), 0))
    fwd2 = pl.BlockSpec((None, None, blk, W), lambda bb, j: (bb, 0, j, 0))
    bwd2 = pl.BlockSpec((None, None, blk, W), lambda bb, j: (bb, 1, rev(j), 0))
    out = jax.ShapeDtypeStruct((B, T, W), F32)
    return pl.pallas_call(
        _rwkv_scan_body,
        grid=(B, nc),
        in_specs=[fwd1, fwd1, fwd1, bwd1, bwd1, bwd1, fwd2, fwd2, fwd2, bwd2, bwd2, bwd2],
        out_specs=[fwd1, bwd1],
        out_shape=[out, out],
        scratch_shapes=[pltpu.VMEM((2 * W // LANES, LANES, LANES), F32)],
        compiler_params=_cp("parallel", "arbitrary"),
        name="rwkv_scan",
    )(r, v, kh, r, v, kh, lw, b, kt, lw, b, kt)


def _rwkv_readout_body(yf_ref, yb_ref, bon_ref, g_ref, gng_ref, gnb_ref, e_ref, o_ref):
    y = yf_ref[...] + yb_ref[...]
    head_mean = e_ref[...] * (1.0 / RWKV_HEAD)
    mu = _dot_terms(y, head_mean, 2, 1)
    yc = y - mu
    var = _dot_terms(yc * yc, head_mean, 2, 1)
    yn = yc * lax.rsqrt(var + GN_EPS) * gng_ref[...] + gnb_ref[...]
    o_ref[...] = ((yn + bon_ref[...]) * g_ref[...]).astype(o_ref.dtype)


def _rwkv_readout(yf, yb, bonus, g, gn_g, gn_b, head_sum):
    B, T, W = yf.shape
    tt = TOK_TILE
    tok = pl.BlockSpec((None, tt, W), lambda b, i: (b, i, 0))
    row = pl.BlockSpec((1, W), lambda b, i: (0, 0))
    return pl.pallas_call(
        _rwkv_readout_body,
        grid=(B, T // tt),
        in_specs=[tok, tok, tok, tok, row, row, pl.BlockSpec((W, W), lambda b, i: (0, 0))],
        out_specs=tok,
        out_shape=jax.ShapeDtypeStruct((B, T, W), BF16),
        compiler_params=_cp("parallel", "parallel"),
        name="rwkv_readout",
    )(yf, yb, bonus, g, gn_g.reshape(1, W), gn_b.reshape(1, W), head_sum)


def _rope(y, cos, sin_signed, quarter):
    lane = lax.broadcasted_iota(jnp.int32, y.shape, 1)
    first = (lane & (2 * quarter - 1)) < quarter
    partner = jnp.where(first, pltpu.roll(y, LANES - quarter, 1), pltpu.roll(y, quarter, 1))
    return y * cos + partner * sin_signed


def _gqa_prep_body(p_ref, qg_ref, kg_ref, cos_ref, sin_ref, q_o, k_o, v_o):
    cos = cos_ref[...]
    sin = sin_ref[...]
    scale = GQA_HEAD ** -0.5 * LOG2E
    for h in range(GQA_Q_HEADS):
        sl = slice(h * GQA_HEAD, (h + 1) * GQA_HEAD)
        q = _rms(p_ref[:, sl], qg_ref[...])
        q_o[:, sl] = (_rope(q, cos, sin, GQA_HEAD // 4) * scale).astype(q_o.dtype)
    for h in range(GQA_KV_HEADS):
        sl = slice(h * GQA_HEAD, (h + 1) * GQA_HEAD)
        k = _rms(p_ref[:, GQA_Q_COLS + h * GQA_HEAD:GQA_Q_COLS + (h + 1) * GQA_HEAD], kg_ref[...])
        k_o[:, sl] = _rope(k, cos, sin, GQA_HEAD // 4).astype(k_o.dtype)
    v_o[...] = p_ref[:, GQA_Q_COLS + GQA_KV_COLS:GQA_COLS].astype(v_o.dtype)


def _row_blocks(n_rows):
    return [slice(r, r + TOK_TILE) for r in range(0, n_rows, TOK_TILE)]


def _gqa_proj_body(h_ref, w_ref, qg_ref, kg_ref, cos_ref, sin_ref, q_o, k_o, v_o):
    for rows in _row_blocks(h_ref.shape[0]):
        p = jnp.dot(h_ref[rows, :], w_ref[...], preferred_element_type=F32)
        _gqa_prep_body(p, qg_ref, kg_ref, cos_ref.at[rows, :], sin_ref.at[rows, :],
                       q_o.at[rows, :], k_o.at[rows, :], v_o.at[rows, :])


def _gqa_proj(h, w, q_norm_g, k_norm_g, cos, sin):
    B, T, D = h.shape
    tt = PROJ_TILE
    tok = lambda w_: pl.BlockSpec((None, tt, w_), lambda b, i: (b, i, 0))
    row = pl.BlockSpec((1, GQA_HEAD), lambda b, i: (0, 0))
    tab = pl.BlockSpec((tt, LANES), lambda b, i: (i, 0))
    return pl.pallas_call(
        _gqa_proj_body,
        grid=(B, T // tt),
        in_specs=[tok(D), pl.BlockSpec(w.shape, lambda b, i: (0, 0)), row, row, tab, tab],
        out_specs=[tok(GQA_Q_COLS), tok(GQA_KV_COLS), tok(GQA_KV_COLS)],
        out_shape=[jax.ShapeDtypeStruct((B, T, GQA_Q_COLS), BF16),
                   jax.ShapeDtypeStruct((B, T, GQA_KV_COLS), BF16),
                   jax.ShapeDtypeStruct((B, T, GQA_KV_COLS), BF16)],
        compiler_params=_cp("parallel", "parallel"),
        name="gqa_proj",
    )(h, w, q_norm_g.reshape(1, -1), k_norm_g.reshape(1, -1), cos, sin)


def _mla_prep_body(p_ref, qg_ref, wq_ref, kvg_ref, wkv_ref, cos_ref, sin_ref, q_o, k_o, v_o):
    cos = cos_ref[...]
    sin = sin_ref[...]
    scale = (MLA_NOPE + MLA_ROPE) ** -0.5 * LOG2E
    cq = _rms(p_ref[:, 0:MLA_Q_RANK], qg_ref[...])
    q = jnp.dot(cq.astype(BF16), wq_ref[...], preferred_element_type=F32) * scale
    ckv = _rms(p_ref[:, MLA_Q_RANK:MLA_Q_RANK + MLA_KV_RANK], kvg_ref[...])
    kv = jnp.dot(ckv.astype(BF16), wkv_ref[...], preferred_element_type=F32)
    kr = _rope(p_ref[:, MLA_Q_RANK + MLA_KV_RANK:MLA_COLS_PAD], cos, sin, MLA_ROPE // 4).astype(k_o.dtype)
    for h in range(MLA_HEADS):
        lo = h * MLA_DK
        q_o[:, lo:lo + LANES] = q[:, lo:lo + LANES].astype(q_o.dtype)
        q_o[:, lo + LANES:lo + MLA_DK] = _rope(q[:, lo + LANES:lo + MLA_DK], cos, sin, MLA_ROPE // 4).astype(q_o.dtype)
        k_o[:, lo:lo + LANES] = kv[:, h * MLA_NOPE:(h + 1) * MLA_NOPE].astype(k_o.dtype)
        k_o[:, lo + LANES:lo + MLA_DK] = kr
    v_o[...] = kv[:, MLA_HEADS * MLA_NOPE:].astype(v_o.dtype)


def _mla_proj_body(h_ref, w_ref, qg_ref, wq_ref, kvg_ref, wkv_ref, cos_ref, sin_ref, q_o, k_o, v_o):
    for rows in _row_blocks(h_ref.shape[0]):
        p = jnp.dot(h_ref[rows, :], w_ref[...], preferred_element_type=F32)
        _mla_prep_body(p, qg_ref, wq_ref, kvg_ref, wkv_ref, cos_ref.at[rows, :], sin_ref.at[rows, :],
                       q_o.at[rows, :], k_o.at[rows, :], v_o.at[rows, :])


def _mla_proj(h, w, q_norm_g, wq, kv_norm_g, wkv, cos, sin):
    B, T, D = h.shape
    tt = PROJ_TILE
    tok = lambda w_: pl.BlockSpec((None, tt, w_), lambda b, i: (b, i, 0))
    full = lambda a: pl.BlockSpec(a.shape, lambda b, i: (0,) * a.ndim)
    tab = pl.BlockSpec((tt, LANES), lambda b, i: (i, 0))
    qg = q_norm_g.reshape(1, -1)
    kvg = kv_norm_g.reshape(1, -1)
    return pl.pallas_call(
        _mla_proj_body,
        grid=(B, T // tt),
        in_specs=[tok(D), full(w), full(qg), full(wq), full(kvg), full(wkv), tab, tab],
        out_specs=[tok(MLA_HEADS * MLA_DK), tok(MLA_HEADS * MLA_DK), tok(MLA_HEADS * MLA_V)],
        out_shape=[jax.ShapeDtypeStruct((B, T, MLA_HEADS * MLA_DK), BF16),
                   jax.ShapeDtypeStruct((B, T, MLA_HEADS * MLA_DK), BF16),
                   jax.ShapeDtypeStruct((B, T, MLA_HEADS * MLA_V), BF16)],
        compiler_params=_cp("parallel", "parallel"),
        name="mla_proj",
    )(h, w, qg, wq, kvg, wkv, cos, sin)


def _attn_body(q_ref, k_ref, v_ref, o_ref, *, hq, hkv, dk, dv, n_ctx_tiles, n_ctx, n_all):
    rep = hq // hkv
    tq = q_ref.shape[0]

    def run(nk):
        for g in range(hkv):
            kg = k_ref[0:nk, g * dk:(g + 1) * dk]
            vg = v_ref[0:nk, g * dv:(g + 1) * dv]
            v_aug = jnp.concatenate([vg, jnp.ones_like(vg)], axis=1)
            for h0 in range(g * rep, (g + 1) * rep, ATTN_STACK):
                heads = range(h0, min(h0 + ATTN_STACK, (g + 1) * rep))
                q = jnp.concatenate([q_ref[:, h * dk:(h + 1) * dk] for h in heads], axis=0)
                s = _dot_nt(q, kg)
                p = jnp.exp2(s - jnp.max(s, axis=-1, keepdims=True))
                o = jnp.dot(p.astype(BF16), v_aug, preferred_element_type=F32)
                o = o[:, 0:dv] / o[:, dv:dv + 1]
                for j, h in enumerate(heads):
                    o_ref[:, h * dv:(h + 1) * dv] = o[j * tq:(j + 1) * tq].astype(o_ref.dtype)

    @pl.when(pl.program_id(1) < n_ctx_tiles)
    def _():
        run(n_ctx)

    @pl.when(pl.program_id(1) >= n_ctx_tiles)
    def _():
        run(n_all)


def _attention(q, k, v, hq, hkv, dk, dv, n_ctx):
    B, T, _ = q.shape
    tq = TOK_TILE
    return pl.pallas_call(
        functools.partial(_attn_body, hq=hq, hkv=hkv, dk=dk, dv=dv, n_ctx_tiles=n_ctx // tq, n_ctx=n_ctx, n_all=T),
        grid=(B, T // tq),
        in_specs=[pl.BlockSpec((None, tq, hq * dk), lambda b, i: (b, i, 0)),
                  pl.BlockSpec((None, T, hkv * dk), lambda b, i: (b, 0, 0)),
                  pl.BlockSpec((None, T, hkv * dv), lambda b, i: (b, 0, 0))],
        out_specs=pl.BlockSpec((None, tq, hq * dv), lambda b, i: (b, i, 0)),
        out_shape=jax.ShapeDtypeStruct((B, T, hq * dv), BF16),
        compiler_params=_cp("parallel", "parallel"),
        name="attention",
    )(q, k, v)


def _rope_tables(T, n_ctx, n_rot):
    quarter = n_rot // 4
    t = jnp.arange(T - n_ctx)
    row = (t // GRID_W).astype(F32)
    col = (t % GRID_W).astype(F32)
    inv = ROPE_THETA ** (-jnp.arange(quarter, dtype=F32) / quarter)
    ar = row[:, None] * inv[None, :]
    ac = col[:, None] * inv[None, :]
    pad = LANES - n_rot
    cos = jnp.concatenate([jnp.cos(ar), jnp.cos(ar), jnp.cos(ac), jnp.cos(ac), jnp.ones((T - n_ctx, pad), F32)], axis=1)
    sin = jnp.concatenate([-jnp.sin(ar), jnp.sin(ar), -jnp.sin(ac), jnp.sin(ac), jnp.zeros((T - n_ctx, pad), F32)], axis=1)
    cos = jnp.concatenate([jnp.ones((n_ctx, LANES), F32), cos], axis=0)
    sin = jnp.concatenate([jnp.zeros((n_ctx, LANES), F32), sin], axis=0)
    return cos, sin


def _moe_body(te_ref, first_ref, slot_ref, nxt_ref, nv_ref, hs_ref, w1_hbm, w3_hbm, w2_hbm, y_ref,
              f1, f3, f2, c1, c3, c2, sem, *, layer):
    i = pl.program_id(0)

    def weight_copies(e, s):
        return (pltpu.make_async_copy(w1_hbm.at[layer, e], f1.at[s], sem.at[s, 0]),
                pltpu.make_async_copy(w3_hbm.at[layer, e], f3.at[s], sem.at[s, 1]),
                pltpu.make_async_copy(w2_hbm.at[layer, e], f2.at[s], sem.at[s, 2]))

    @pl.when(i == 0)
    def _():
        for cp in weight_copies(te_ref[0], 0):
            cp.start()

    @pl.when(first_ref[i] == 1)
    def _():
        s = slot_ref[i]
        for cp in weight_copies(te_ref[i], s):
            cp.wait()

        @pl.when(nxt_ref[i] >= 0)
        def _():
            for cp in weight_copies(nxt_ref[i], 1 - s):
                cp.start()

        c1[...] = f1[s].astype(BF16)
        c3[...] = f3[s].astype(BF16)
        c2[...] = f2[s].astype(BF16)

    @pl.when(i < nv_ref[0])
    def _():
        hs = hs_ref[...].astype(BF16)
        a = jnp.dot(hs, c1[...], preferred_element_type=F32)
        b = jnp.dot(hs, c3[...], preferred_element_type=F32)
        act = a * _sigmoid(a) * b
        y = jnp.dot(act.astype(BF16), c2[...], preferred_element_type=F32)
        y_ref[...] = y.astype(y_ref.dtype)

    @pl.when(i >= nv_ref[0])
    def _():
        y_ref[...] = jnp.zeros_like(y_ref)


def _moe_experts(tile_expert, run_first, run_slot, run_next, n_valid, hs, w1, w3, w2, layer):
    NP, D = hs.shape
    tm = MOE_TM
    DE = w1.shape[-1]
    grid_spec = pltpu.PrefetchScalarGridSpec(
        num_scalar_prefetch=5,
        grid=(NP // tm,),
        in_specs=[pl.BlockSpec((tm, D), lambda i, *_: (i, 0)),
                  pl.BlockSpec(memory_space=pl.ANY),
                  pl.BlockSpec(memory_space=pl.ANY),
                  pl.BlockSpec(memory_space=pl.ANY)],
        out_specs=pl.BlockSpec((tm, D), lambda i, *_: (i, 0)),
        scratch_shapes=[pltpu.VMEM((2, D, DE), F32), pltpu.VMEM((2, D, DE), F32), pltpu.VMEM((2, DE, D), F32),
                        pltpu.VMEM((D, DE), BF16), pltpu.VMEM((D, DE), BF16), pltpu.VMEM((DE, D), BF16),
                        pltpu.SemaphoreType.DMA((2, 3))],
    )
    return pl.pallas_call(
        functools.partial(_moe_body, layer=layer),
        grid_spec=grid_spec,
        out_shape=jax.ShapeDtypeStruct((NP, D), BF16),
        compiler_params=_cp("arbitrary"),
        name="moe_experts",
    )(tile_expert, run_first, run_slot, run_next, n_valid, hs, w1, w3, w2)


def _moe(h2, re, w1, w3, w2, layer):
    N = h2.shape[0]
    tm = MOE_TM
    n_tiles = (2 * N) // tm + N_EXPERTS
    e_flat = re.reshape(2 * N)
    onehot = (e_flat[:, None] == jnp.arange(N_EXPERTS, dtype=jnp.int32)[None, :]).astype(jnp.int32)
    csum = jnp.cumsum(onehot, axis=0)
    rank = jnp.sum(onehot * (csum - 1), axis=1)
    counts = csum[-1]
    ptiles = (counts + tm - 1) // tm
    tile_end = jnp.cumsum(ptiles)
    tile_start = tile_end - ptiles
    pos = tile_start[e_flat] * tm + rank
    n_valid = tile_end[-1:].astype(jnp.int32)
    tile_ids = jnp.arange(n_tiles, dtype=jnp.int32)
    tile_expert = jnp.minimum(jnp.sum((tile_end[None, :] <= tile_ids[:, None]).astype(jnp.int32), axis=1), N_EXPERTS - 1)
    experts = jnp.arange(N_EXPERTS, dtype=jnp.int32)
    owns = ptiles > 0
    run_no = jnp.cumsum(owns.astype(jnp.int32)) - 1
    later = jnp.where(owns[None, :] & (experts[None, :] > experts[:, None]), experts[None, :], N_EXPERTS)
    next_run = jnp.min(later, axis=1)
    next_run = jnp.where(next_run < N_EXPERTS, next_run, -1).astype(jnp.int32)
    run_first = ((tile_ids == tile_start[tile_expert]) & (tile_ids < n_valid[0])).astype(jnp.int32)
    run_slot = (run_no[tile_expert] % 2).astype(jnp.int32)
    run_next = next_run[tile_expert]
    spread = jnp.arange(n_tiles * tm, dtype=jnp.int32) % N
    src = spread.at[pos].set(jnp.arange(2 * N, dtype=jnp.int32) // 2)
    rows = lambda a, idx: a.at[idx].get(mode="promise_in_bounds")
    hs = rows(h2, src)
    y = _moe_experts(tile_expert, run_first, run_slot, run_next, n_valid, hs, w1, w3, w2, layer)
    pos2 = pos.reshape(N, 2)
    return rows(y, pos2[:, 0]), rows(y, pos2[:, 1])


def kernel(x, c, ctx, c_ctx, mod_w, mod_b, norm1_g, norm2_g, w_in, w_out, shift_prev, shift_next, decay_w0, decay_up, iclr_a0, iclr_up, gate_up, k_k, k_a, r_k, gn_g, gn_b, q_norm_g, k_norm_g, mla_q_norm_g, mla_w_uq, mla_kv_norm_g, mla_w_ukv, router_gw, router_gb, router_ew, router_eb, exp_w1, exp_w3, exp_w2, final_norm_g):
    B, S, D = x.shape
    C = ctx.shape[1]
    T = C + S
    depth = mod_w.shape[0]
    assert C == TOK_TILE and S % TOK_TILE == 0 and B <= CTX_ROW
    n_ctx_tiles = C // TOK_TILE

    X = jnp.concatenate([ctx, x], axis=1)
    cc = jnp.zeros((SUBLANES, D), F32).at[:B].set(c).at[CTX_ROW].set(c_ctx)
    mods_all = _modulation(cc, mod_w, mod_b).reshape(depth, SUBLANES * N_MOD, 1, D)

    cos_g, sin_g = _rope_tables(T, C, GQA_HEAD)
    cos_m, sin_m = _rope_tables(T, C, MLA_ROPE)
    hid = jnp.arange(RWKV_W) // RWKV_HEAD
    head_sum = (hid[:, None] == hid[None, :]).astype(F32)

    for l in range(depth):
        mods = mods_all[l]
        w_r = w_in[l][:, :RWKV_COLS].astype(BF16)
        w_g = w_in[l][:, RWKV_COLS:RWKV_COLS + GQA_COLS].astype(BF16)
        w_m = jnp.pad(w_in[l][:, RWKV_COLS + GQA_COLS:], ((0, 0), (0, MLA_COLS_PAD - MLA_COLS))).astype(BF16)

        if l == 0:
            h = _norm_mod(X, norm1_g[l], mods, 0, 1, n_ctx_tiles)
        pr = _matmul(h.reshape(B * T, D), w_r).reshape(B, T, RWKV_COLS)

        wup_pad = jnp.pad(decay_up[l], ((0, 0), (0, ICLR_RANK), (0, 0)))
        aup_pad = jnp.pad(iclr_up[l], ((0, 0), (DECAY_RANK, 0), (0, 0)))
        r, v, kh, lw, b, kt, g, bonus = _rwkv_prep(pr, shift_prev[l], shift_next[l], decay_w0[l], wup_pad,
                                                  iclr_a0[l], aup_pad, gate_up[l], k_k[l], k_a[l], r_k[l], head_sum)
        yf, yb = _rwkv_scan(r, v, kh, lw, b, kt, C)
        o_r = _rwkv_readout(yf, yb, bonus, g, gn_g[l], gn_b[l], head_sum)

        q, k, vv = _gqa_proj(h, w_g, q_norm_g[l], k_norm_g[l], cos_g, sin_g)
        o_g = _attention(q, k, vv, GQA_Q_HEADS, GQA_KV_HEADS, GQA_HEAD, GQA_HEAD, C)

        wq = mla_w_uq[l].reshape(MLA_Q_RANK, MLA_HEADS, MLA_NOPE + MLA_ROPE)
        wq = jnp.pad(wq, ((0, 0), (0, 0), (0, MLA_DK - MLA_NOPE - MLA_ROPE))).reshape(MLA_Q_RANK, MLA_HEADS * MLA_DK)
        wkv = mla_w_ukv[l].reshape(MLA_KV_RANK, MLA_HEADS, MLA_NOPE + MLA_V)
        wkv = jnp.concatenate([wkv[:, :, :MLA_NOPE].reshape(MLA_KV_RANK, -1), wkv[:, :, MLA_NOPE:].reshape(MLA_KV_RANK, -1)], axis=1)
        qm, km, vm = _mla_proj(h, w_m, mla_q_norm_g[l], wq.astype(BF16), mla_kv_norm_g[l], wkv.astype(BF16), cos_m, sin_m)
        o_m = _attention(qm, km, vm, MLA_HEADS, MLA_HEADS, MLA_DK, MLA_V, C)

        last = l + 1 == depth
        X = _matmul_gated_residual([o_r, o_g, o_m], w_out[l].astype(BF16), X, mods, 2, C,
                                   tm=TOK_TILE if last else OUT_TILE, latents_only=last)
        rows = X.shape[1]
        ctx_tiles = 0 if last else n_ctx_tiles

        wr = jnp.pad(jnp.concatenate([router_gw[l], router_ew[l]], axis=1), ((0, 0), (0, LANES - N_GROUPS - N_EXPERTS)))
        br = jnp.pad(jnp.concatenate([router_gb[l], router_eb[l]]), (0, LANES - N_GROUPS - N_EXPERTS)).reshape(1, LANES)
        h2, rw, re = _norm_router(X, norm2_g[l], mods, 3, 4, ctx_tiles, wr, br)
        y0, y1 = _moe(h2.reshape(B * rows, D), re.reshape(B * rows, LANES)[:, :2], exp_w1, exp_w3, exp_w2, l)
        y0 = y0.reshape(B, rows, D)
        y1 = y1.reshape(B, rows, D)
        if last:
            out = _moe_residual_final(X, y0, y1, rw, mods, 5, ctx_tiles, final_norm_g)
        else:
            X, h = _moe_residual_norm(X, y0, y1, rw, mods, 5, ctx_tiles, norm1_g[l + 1], mods_all[l + 1])
    return out
```

```python
import functools
import math

import jax
import jax.numpy as jnp
import numpy as np
from jax import lax
from jax.experimental import pallas as pl
from jax.experimental.pallas import tpu as pltpu

F32 = jnp.float32
BF16 = jnp.bfloat16

V7X_VMEM_BYTES = 64 * 1024 * 1024
VMEM_LIMIT = V7X_VMEM_BYTES - 8 * 1024 * 1024
LANES = 128
SUBLANES = 8

GRID_W = 64
ROPE_THETA = 10000.0
NORM_EPS = 1e-6
GN_EPS = 64e-5
DECAY_SCALE = math.exp(-0.5)
LOG2E = math.log2(math.e)

RWKV_HEAD = 64
RWKV_W = 512
DECAY_RANK = 64
ICLR_RANK = 64
GATE_RANK = 128
RWKV_COLS = 3 * RWKV_W + DECAY_RANK + ICLR_RANK + GATE_RANK
LOWRANK_OFF = 3 * RWKV_W
GATE_OFF = LOWRANK_OFF + DECAY_RANK + ICLR_RANK
CHUNK = 64
SCAN_CHUNKS = 2

GQA_HEAD = 128
GQA_Q_HEADS = 8
GQA_KV_HEADS = 2
GQA_Q_COLS = GQA_Q_HEADS * GQA_HEAD
GQA_KV_COLS = GQA_KV_HEADS * GQA_HEAD
GQA_COLS = GQA_Q_COLS + 2 * GQA_KV_COLS

MLA_HEADS = 4
MLA_NOPE = 128
MLA_ROPE = 64
MLA_V = 128
MLA_Q_RANK = 384
MLA_KV_RANK = 256
MLA_COLS = MLA_Q_RANK + MLA_KV_RANK + MLA_ROPE
MLA_COLS_PAD = 768
MLA_DK = 2 * LANES

N_GROUPS = 4
EXPERTS_PER_GROUP = 8
N_EXPERTS = 32
D_EXPERT = 256
MOE_TM = 256

TOK_TILE = 256
PROJ_TILE = 768
OUT_TILE = 384
ATTN_STACK = 1
N_MOD = 6
CTX_ROW = 4


def _cp(*sem):
    return pltpu.CompilerParams(dimension_semantics=sem, vmem_limit_bytes=VMEM_LIMIT)


def _sigmoid(x):
    return 1.0 / (1.0 + jnp.exp(-x))


def _bf16_terms(x, n):
    terms = []
    for _ in range(n):
        t = x.astype(BF16)
        terms.append(t)
        x = x - t.astype(F32)
    return terms


def _dot_terms(x, w, nx, nw):
    xs = _bf16_terms(x, nx)
    ws = _bf16_terms(w, nw)
    acc = None
    for i in range(nx):
        for j in range(nw):
            if i + j < max(nx, nw):
                p = jnp.dot(xs[i], ws[j], preferred_element_type=F32)
                acc = p if acc is None else acc + p
    return acc


def _mod_body(c_ref, w_ref, b_ref, o_ref):
    c = c_ref[...]
    s = c * _sigmoid(c)
    o_ref[...] = _dot_terms(s, w_ref[...], 2, 2) + b_ref[...]


def _modulation(cc, mod_w, mod_b):
    L, D, N = mod_w.shape
    tn = 1024
    return pl.pallas_call(
        _mod_body,
        grid=(L, N // tn),
        in_specs=[pl.BlockSpec((SUBLANES, D), lambda l, j: (0, 0)),
                  pl.BlockSpec((None, D, tn), lambda l, j: (l, 0, j)),
                  pl.BlockSpec((None, 1, tn), lambda l, j: (l, 0, j))],
        out_specs=pl.BlockSpec((None, SUBLANES, tn), lambda l, j: (l, 0, j)),
        out_shape=jax.ShapeDtypeStruct((L, SUBLANES, N), F32),
        compiler_params=_cp("parallel", "parallel"),
        name="modulation",
    )(cc, mod_w, mod_b.reshape(L, 1, N))


def _mod_spec(which, n_ctx_tiles, D):
    return pl.BlockSpec((None, 1, D), lambda b, i: (jnp.where(i < n_ctx_tiles, CTX_ROW, b) * N_MOD + which, 0, 0))


def _rms(x, g):
    return x * lax.rsqrt(jnp.mean(x * x, axis=-1, keepdims=True) + NORM_EPS) * g


def _norm_mod_body(x_ref, g_ref, sh_ref, sc_ref, o_ref):
    y = _rms(x_ref[...], g_ref[...])
    o_ref[...] = (y * (1.0 + sc_ref[...]) + sh_ref[...]).astype(o_ref.dtype)


def _route(logits):
    lane = lax.broadcasted_iota(jnp.int32, logits.shape, 1)
    lane_f = lane.astype(F32)
    neg = jnp.float32(-1e30)
    far = jnp.float32(1e9)
    first_at = lambda hit: jnp.min(jnp.where(hit, lane_f, far), axis=-1, keepdims=True).astype(jnp.int32)
    gl = jnp.where(lane < N_GROUPS, logits, neg)
    gmax = jnp.max(gl, axis=-1, keepdims=True)
    gidx = first_at(gl == gmax)
    p_sel = 1.0 / jnp.sum(jnp.exp(gl - gmax), axis=-1, keepdims=True)
    lo = N_GROUPS + gidx * EXPERTS_PER_GROUP
    el = jnp.where((lane >= lo) & (lane < lo + EXPERTS_PER_GROUP), logits, neg)
    m1 = jnp.max(el, axis=-1, keepdims=True)
    i1 = first_at(el == m1)
    el2 = jnp.where(lane == i1, neg, el)
    m2 = jnp.max(el2, axis=-1, keepdims=True)
    i2 = first_at(el2 == m2)
    t = jnp.exp(m2 - m1)
    w1 = p_sel / (1.0 + t)
    w2 = p_sel * t / (1.0 + t)
    rw = jnp.where(lane == 0, w1, jnp.where(lane == 1, w2, 0.0))
    re = jnp.where(lane == 0, i1 - N_GROUPS, jnp.where(lane == 1, i2 - N_GROUPS, 0))
    return rw, re


def _norm_mod(X, g, mods, shift_i, scale_i, n_ctx_tiles):
    B, T, D = X.shape
    tt = TOK_TILE
    return pl.pallas_call(
        _norm_mod_body,
        grid=(B, T // tt),
        in_specs=[pl.BlockSpec((None, tt, D), lambda b, i: (b, i, 0)),
                  pl.BlockSpec((1, D), lambda b, i: (0, 0)),
                  _mod_spec(shift_i, n_ctx_tiles, D),
                  _mod_spec(scale_i, n_ctx_tiles, D)],
        out_specs=pl.BlockSpec((None, tt, D), lambda b, i: (b, i, 0)),
        out_shape=jax.ShapeDtypeStruct((B, T, D), BF16),
        compiler_params=_cp("parallel", "parallel"),
        name="norm_mod",
    )(X, g.reshape(1, D), mods, mods)


def _mm_body(a_ref, w_ref, o_ref):
    o_ref[...] = jnp.dot(a_ref[...], w_ref[...], preferred_element_type=F32).astype(o_ref.dtype)


def _matmul(a, w, tm=512):
    M, K = a.shape
    N = w.shape[1]
    return pl.pallas_call(
        _mm_body,
        grid=(M // tm,),
        in_specs=[pl.BlockSpec((tm, K), lambda i: (i, 0)),
                  pl.BlockSpec((K, N), lambda i: (0, 0))],
        out_specs=pl.BlockSpec((tm, N), lambda i: (i, 0)),
        out_shape=jax.ShapeDtypeStruct((M, N), F32),
        compiler_params=_cp("parallel"),
        name="token_matmul",
    )(a, w)


def _out_proj_router_body(*refs, n_parts, n_ctx, tm, tile0):
    a_refs = refs[:n_parts]
    w_ref, x_ref, ml_ref, mc_ref, g_ref, wr_ref, br_ref, x_o, h_o, rw_o, re_o = refs[n_parts:]
    acc = None
    k0 = 0
    for a_ref in a_refs:
        k1 = k0 + a_ref.shape[-1]
        part = jnp.dot(a_ref[...], w_ref[k0:k1, :], preferred_element_type=F32)
        acc = part if acc is None else acc + part
        k0 = k1
    row = (pl.program_id(1) + tile0) * tm + lax.broadcasted_iota(jnp.int32, (tm, 1), 0)
    is_ctx = row < n_ctx
    mod = lambda which: jnp.where(is_ctx, mc_ref[which], ml_ref[which])
    x = x_ref[...] + mod(2) * acc
    x_o[...] = x
    h = _rms(x, g_ref[...]) * (1.0 + mod(4)) + mod(3)
    h_o[...] = h
    logits = _dot_terms(h, wr_ref[...], 2, 2) + br_ref[...]
    rw, re = _route(logits)
    rw_o[...] = rw
    re_o[...] = re


def _out_proj_router(parts, w, X, mods, n_ctx, tm, latents_only, g2, wr, br):
    B, T, D = X.shape
    tile0 = n_ctx // tm if latents_only else 0
    assert T % tm == 0 and (n_ctx % tm == 0 or not latents_only)
    assert sum(p.shape[-1] for p in parts) == w.shape[0]
    rows_out = T - tile0 * tm
    mods6 = mods.reshape(SUBLANES, N_MOD, 1, D)
    tok_in = lambda wd: pl.BlockSpec((None, tm, wd), lambda b, i: (b, i + tile0, 0))
    tok_out = lambda wd: pl.BlockSpec((None, tm, wd), lambda b, i: (b, i, 0))
    shape = lambda wd, dt: jax.ShapeDtypeStruct((B, rows_out, wd), dt)
    return pl.pallas_call(
        functools.partial(_out_proj_router_body, n_parts=len(parts), n_ctx=n_ctx, tm=tm, tile0=tile0),
        grid=(B, T // tm - tile0),
        in_specs=[tok_in(p.shape[-1]) for p in parts] + [
                  pl.BlockSpec(w.shape, lambda b, i: (0, 0)),
                  tok_in(D),
                  pl.BlockSpec((None, N_MOD, 1, D), lambda b, i: (b, 0, 0, 0)),
                  pl.BlockSpec((None, N_MOD, 1, D), lambda b, i: (CTX_ROW, 0, 0, 0)),
                  pl.BlockSpec((1, D), lambda b, i: (0, 0)),
                  pl.BlockSpec((D, LANES), lambda b, i: (0, 0)),
                  pl.BlockSpec((1, LANES), lambda b, i: (0, 0))],
        out_specs=[tok_out(D), tok_out(D), tok_out(LANES), tok_out(LANES)],
        out_shape=[shape(D, F32), shape(D, F32), shape(LANES, F32), shape(LANES, jnp.int32)],
        compiler_params=_cp("parallel", "parallel"),
        name="out_proj_router",
    )(*parts, w, X, mods6, mods6, g2.reshape(1, D), wr, br)


def _moe_residual(x_ref, y0_ref, y1_ref, rw_ref, gate_ref):
    rw = rw_ref[...]
    moe = rw[:, 0:1] * y0_ref[...].astype(F32) + rw[:, 1:2] * y1_ref[...].astype(F32)
    return x_ref[...] + gate_ref[...] * moe


def _moe_residual_norm_body(x_ref, y0_ref, y1_ref, rw_ref, gate_ref, g_ref, sh_ref, sc_ref, x_o, h_o):
    x = _moe_residual(x_ref, y0_ref, y1_ref, rw_ref, gate_ref)
    x_o[...] = x
    h_o[...] = (_rms(x, g_ref[...]) * (1.0 + sc_ref[...]) + sh_ref[...]).astype(h_o.dtype)


def _moe_residual_final_body(x_ref, y0_ref, y1_ref, rw_ref, gate_ref, g_ref, o_ref):
    o_ref[...] = _rms(_moe_residual(x_ref, y0_ref, y1_ref, rw_ref, gate_ref), g_ref[...])


def _moe_residual_norm(X, Y0, Y1, rw, mods, gate_i, n_ctx_tiles, g_next, mods_next):
    B, T, D = X.shape
    tt = TOK_TILE
    blk = pl.BlockSpec((None, tt, D), lambda b, i: (b, i, 0))
    return pl.pallas_call(
        _moe_residual_norm_body,
        grid=(B, T // tt),
        in_specs=[blk, blk, blk, pl.BlockSpec((None, tt, LANES), lambda b, i: (b, i, 0)),
                  _mod_spec(gate_i, n_ctx_tiles, D), pl.BlockSpec((1, D), lambda b, i: (0, 0)),
                  _mod_spec(0, n_ctx_tiles, D), _mod_spec(1, n_ctx_tiles, D)],
        out_specs=[blk, blk],
        out_shape=[jax.ShapeDtypeStruct((B, T, D), F32), jax.ShapeDtypeStruct((B, T, D), BF16)],
        compiler_params=_cp("parallel", "parallel"),
        name="moe_residual_norm",
    )(X, Y0, Y1, rw, mods, g_next.reshape(1, D), mods_next, mods_next)


def _moe_residual_final(X, Y0, Y1, rw, mods, gate_i, n_ctx_tiles, g_final):
    B, T, D = X.shape
    tt = TOK_TILE
    S = T - n_ctx_tiles * tt
    lat = lambda w: pl.BlockSpec((None, tt, w), lambda b, i: (b, i + n_ctx_tiles, 0))
    return pl.pallas_call(
        _moe_residual_final_body,
        grid=(B, S // tt),
        in_specs=[lat(D), lat(D), lat(D), lat(LANES),
                  pl.BlockSpec((None, 1, D), lambda b, i: (b * N_MOD + gate_i, 0, 0)),
                  pl.BlockSpec((1, D), lambda b, i: (0, 0))],
        out_specs=pl.BlockSpec((None, tt, D), lambda b, i: (b, i, 0)),
        out_shape=jax.ShapeDtypeStruct((B, S, D), F32),
        compiler_params=_cp("parallel", "parallel"),
        name="moe_residual_final",
    )(X, Y0, Y1, rw, mods, g_final.reshape(1, D))


def _rwkv_prep_body(p_ref, pv_ref, nx_ref, mup_ref, mun_ref, w0_ref, wup_ref, a0_ref, aup_ref, gup_ref,
                    kk_ref, ka_ref, rk_ref, e_ref,
                    r_o, v_o, kh_o, lw_o, b_o, kt_o, g_o, bon_o, *, n_tiles, tt):
    i = pl.program_id(1)
    p = p_ref[...]
    seq_first = i <= 1
    seq_last = (i == 0) | (i == n_tiles - 1)
    prow = jnp.where(seq_first, 0.0, pv_ref[SUBLANES - 1:SUBLANES, :])
    nrow = jnp.where(seq_last, 0.0, nx_ref[0:1, :])
    rid = lax.broadcasted_iota(jnp.int32, (tt, 1), 0)
    prev = jnp.where(rid == 0, prow, pltpu.roll(p, 1, 0))
    nxt = jnp.where(rid == tt - 1, nrow, pltpu.roll(p, tt - 1, 0))
    z = p + mup_ref[...] * (prev - p) + mun_ref[...] * (nxt - p)

    W = RWKV_W
    r = z[:, 0:W]
    k = z[:, W:2 * W]
    v = z[:, 2 * W:3 * W]
    lowrank = z[:, LOWRANK_OFF:LOWRANK_OFF + LANES]
    gd = z[:, GATE_OFF:GATE_OFF + GATE_RANK]
    head_sum = e_ref[...]

    kap = k * kk_ref[...]
    ss = _dot_terms(kap * kap, head_sum, 2, 1)
    khat = kap * lax.rsqrt(ss + 1e-12)
    wd_t = jnp.tanh(lowrank)
    g_o[...] = _dot_terms(_sigmoid(gd), gup_ref[...], 1, 1)
    r_o[...] = r
    v_o[...] = v
    kh_o[...] = khat
    kt_sum = None
    for d in range(2):
        dec = _dot_terms(wd_t, wup_ref[d], 2, 2)
        lw_o[d] = -DECAY_SCALE * _sigmoid(w0_ref[d:d + 1, :] + dec)
        a = _sigmoid(a0_ref[d:d + 1, :] + _dot_terms(lowrank, aup_ref[d], 1, 1))
        kt = k * (1.0 + (a - 1.0) * ka_ref[...])
        kt_o[d] = kt
        b_o[d] = a * khat
        kt_sum = kt if kt_sum is None else kt_sum + kt
    bsum = _dot_terms(r * kt_sum * rk_ref[...], head_sum, 2, 1)
    bon_o[...] = bsum * v


def _rwkv_prep(pr, mu_prev, mu_next, w0, wup_pad, a0, aup_pad, g_up, k_k, k_a, r_k, head_sum):
    B, T, _ = pr.shape
    tt = TOK_TILE
    W = RWKV_W
    n_tiles = T // tt
    n8 = tt // SUBLANES
    row = lambda v: v.reshape(1, -1)
    full = lambda a: pl.BlockSpec(a.shape, lambda b, i: (0,) * a.ndim)
    tok = pl.BlockSpec((None, tt, W), lambda b, i: (b, i, 0))
    tok2 = pl.BlockSpec((None, 2, tt, W), lambda b, i: (b, 0, i, 0))
    s1 = jax.ShapeDtypeStruct((B, T, W), F32)
    s2 = jax.ShapeDtypeStruct((B, 2, T, W), F32)
    consts = [row(mu_prev), row(mu_next), w0, wup_pad, a0, aup_pad, g_up, row(k_k), row(k_a), row(r_k), head_sum]
    return pl.pallas_call(
        functools.partial(_rwkv_prep_body, n_tiles=n_tiles, tt=tt),
        grid=(B, n_tiles),
        in_specs=[pl.BlockSpec((None, tt, RWKV_COLS), lambda b, i: (b, i, 0)),
                  pl.BlockSpec((None, SUBLANES, RWKV_COLS), lambda b, i: (b, jnp.maximum(i * n8 - 1, 0), 0)),
                  pl.BlockSpec((None, SUBLANES, RWKV_COLS), lambda b, i: (b, jnp.minimum((i + 1) * n8, T // SUBLANES - 1), 0)),
                  ] + [full(a) for a in consts],
        out_specs=[tok, tok, tok, tok2, tok2, tok2, tok, tok],
        out_shape=[s1, s1, s1, s2, s2, s2, s1, s1],
        compiler_params=_cp("parallel", "parallel"),
        name="rwkv_prep",
    )(pr, pr, pr, *consts)


def _stack_heads(x):
    lane = lax.broadcasted_iota(jnp.int32, x.shape, 1)
    first = lane < RWKV_HEAD
    return jnp.concatenate([jnp.where(first, x, 0.0), jnp.where(first, 0.0, x)], axis=0)


def _dot(a, b):
    return jnp.dot(a.astype(BF16), b.astype(BF16), preferred_element_type=F32)


def _dot_nt(a, b):
    return lax.dot_general(a.astype(BF16), b.astype(BF16), (((1,), (1,)), ((), ())), preferred_element_type=F32)


def _dot_tn(a, b):
    return jnp.dot(a.T.astype(BF16), b.astype(BF16), preferred_element_type=F32)


def _chunk_operands(r, v, kh, lw, b, kt, reverse):
    L = CHUNK
    ti = lax.broadcasted_iota(jnp.int32, (L, L), 0)
    tj = lax.broadcasted_iota(jnp.int32, (L, L), 1)
    tri = jnp.where((ti <= tj) if reverse else (ti >= tj), 1.0, 0.0)
    lam = _dot_terms(tri, lw, 1, 3)
    tot = lam[0:1, :] if reverse else lam[L - 1:L, :]
    e_n = jnp.exp(-lam)
    e_g = jnp.exp(tot - lam)
    full = dict(A=kh * jnp.exp(lam - lw), R=r * jnp.exp(lam), Kn=kt * e_n, Bn=b * e_n, Kg=kt * e_g, Bg=b * e_g, V=v)
    e_tot = jnp.exp(tot)
    pairs = []
    for p in range(RWKV_W // LANES):
        sl = slice(p * LANES, (p + 1) * LANES)
        ops = {k: a[:, sl] for k, a in full.items()}
        ops["e_tot"] = e_tot[:, sl]
        ops["reverse"] = reverse
        pairs.append(ops)
    return pairs


def _chunk_masks(reverse):
    L = CHUNK
    t = lax.broadcasted_iota(jnp.int32, (L, 2 * L), 0)
    i = lax.broadcasted_iota(jnp.int32, (L, 2 * L), 1) & (L - 1)
    before = (i > t) if reverse else (i < t)
    return before, before | (i == t), i == t


def _chunks_prepare(chains):
    L = CHUNK
    P2 = 2 * L
    n = len(chains)
    masks = {rev: _chunk_masks(rev) for rev in {c["reverse"] for c in chains}}
    strict = [masks[c["reverse"]][0] for c in chains]
    incl = [masks[c["reverse"]][1] for c in chains]
    eye = masks[chains[0]["reverse"]][2]
    bi = lax.broadcasted_iota(jnp.int32, (P2, P2), 0)
    bj = lax.broadcasted_iota(jnp.int32, (P2, P2), 1)
    same_head = (bi >= L) == (bj >= L)
    eye2 = bi == bj
    stack = _stack_heads
    cat0 = lambda *xs: jnp.concatenate(xs, axis=0)
    cat1 = lambda *xs: jnp.concatenate(xs, axis=1)

    big = [_dot_nt(cat0(c["A"], c["R"]), cat0(stack(c["Bn"]), stack(c["Kn"]))) for c in chains]
    Mb = [jnp.where(strict[i], big[i][0:L, 0:P2], 0.0) for i in range(n)]
    Mkv = [jnp.where(strict[i], big[i][0:L, P2:2 * P2], 0.0) for i in range(n)]
    Pb = [jnp.where(incl[i], big[i][L:P2, 0:P2], 0.0) for i in range(n)]
    Pkv = [jnp.where(incl[i], big[i][L:P2, P2:2 * P2], 0.0) for i in range(n)]

    Pw = [-m for m in Mb]
    Tm = [jnp.where(eye, 1.0, 0.0) + p for p in Pw]
    Pw = [_dot(p, stack(p)) for p in Pw]
    for _ in range(int(math.log2(L)) - 2):
        PT = [_dot(cat0(p, t), stack(p)) for p, t in zip(Pw, Tm)]
        Tm = [t + pt[L:P2] for t, pt in zip(Tm, PT)]
        Pw = [pt[0:L] for pt in PT]
    Tm = [t + _dot(t, stack(p)) for t, p in zip(Tm, Pw)]

    Vs = [stack(c["V"]) for c in chains]
    MPV = [_dot(cat0(Mkv[i], Pkv[i]), Vs[i]) for i in range(n)]
    TAM = [_dot(Tm[i], cat1(stack(chains[i]["A"]), stack(MPV[i][0:L]))) for i in range(n)]
    PB = [_dot(Pb[i], cat1(stack(TAM[i][:, 0:P2]), stack(TAM[i][:, P2:2 * P2]))) for i in range(n)]
    BG = [_dot_tn(chains[i]["Bg"], TAM[i]) for i in range(n)]
    KV = [_dot_tn(chains[i]["Kg"], chains[i]["V"]) for i in range(n)]
    prepared = []
    for i in range(n):
        RA = chains[i]["R"] - PB[i][:, 0:P2]
        G2 = jnp.where(eye2, chains[i]["e_tot"], 0.0) - jnp.where(same_head, BG[i][:, 0:P2], 0.0)
        H2 = jnp.where(same_head, KV[i] - BG[i][:, P2:2 * P2], 0.0)
        prepared.append((cat0(RA, G2), MPV[i][L:P2] - PB[i][:, P2:2 * P2], H2))
    return prepared


def _chunk_advance(prepared, state):
    lhs, y0, h2 = prepared
    out = _dot(lhs, state)
    return out[0:CHUNK] + y0, out[CHUNK:] + h2


def _rwkv_scan_body(rf_ref, vf_ref, khf_ref, rb_ref, vb_ref, khb_ref, lwf_ref, bf_ref, ktf_ref, lwb_ref, bb_ref, ktb_ref,
                    yf_ref, yb_ref, s_ref):
    @pl.when(pl.program_id(1) == 0)
    def _():
        s_ref[...] = jnp.zeros_like(s_ref)

    n_pairs = RWKV_W // LANES
    fwd_refs = (rf_ref, vf_ref, khf_ref, lwf_ref, bf_ref, ktf_ref)
    bwd_refs = (rb_ref, vb_ref, khb_ref, lwb_ref, bb_ref, ktb_ref)
    slots = []
    for c in range(SCAN_CHUNKS):
        lo_f = c * CHUNK
        lo_b = (SCAN_CHUNKS - 1 - c) * CHUNK
        slots.append((_chunk_operands(*(ref[lo_f:lo_f + CHUNK, :] for ref in fwd_refs), False), lo_f,
                      _chunk_operands(*(ref[lo_b:lo_b + CHUNK, :] for ref in bwd_refs), True), lo_b))
    prepared = _chunks_prepare([ch for s in slots for ch in s[0] + s[2]])
    states = [s_ref[i] for i in range(2 * n_pairs)]
    for c, (_, lo_f, _, lo_b) in enumerate(slots):
        ys = []
        for i in range(2 * n_pairs):
            y, states[i] = _chunk_advance(prepared[c * 2 * n_pairs + i], states[i])
            ys.append(y)
        yf_ref[lo_f:lo_f + CHUNK, :] = jnp.concatenate(ys[:n_pairs], axis=1)
        yb_ref[lo_b:lo_b + CHUNK, :] = jnp.concatenate(ys[n_pairs:], axis=1)
    for i in range(2 * n_pairs):
        s_ref[i] = states[i]


def _rwkv_scan(r, v, kh, lw, b, kt, n_ctx):
    B, T, W = r.shape
    blk = SCAN_CHUNKS * CHUNK
    assert T % blk == 0 and n_ctx % blk == 0
    nc = T // blk
    ncc = n_ctx // blk
    rev = lambda j: jnp.where(j < ncc, ncc - 1 - j, nc + ncc - 1 - j)
    fwd1 = pl.BlockSpec((None, blk, W), lambda bb, j: (bb, j, 0))
    bwd1 = pl.BlockSpec((None, blk, W), lambda bb, j: (bb, rev(j), 0))
    fwd2 = pl.BlockSpec((None, None, blk, W), lambda bb, j: (bb, 0, j, 0))
    bwd2 = pl.BlockSpec((None, None, blk, W), lambda bb, j: (bb, 1, rev(j), 0))
    out = jax.ShapeDtypeStruct((B, T, W), F32)
    return pl.pallas_call(
        _rwkv_scan_body,
        grid=(B, nc),
        in_specs=[fwd1, fwd1, fwd1, bwd1, bwd1, bwd1, fwd2, fwd2, fwd2, bwd2, bwd2, bwd2],
        out_specs=[fwd1, bwd1],
        out_shape=[out, out],
        scratch_shapes=[pltpu.VMEM((2 * W // LANES, LANES, LANES), F32)],
        compiler_params=_cp("parallel", "arbitrary"),
        name="rwkv_scan",
    )(r, v, kh, r, v, kh, lw, b, kt, lw, b, kt)


def _rwkv_readout_body(yf_ref, yb_ref, bon_ref, g_ref, gng_ref, gnb_ref, e_ref, o_ref):
    y = yf_ref[...] + yb_ref[...]
    head_mean = e_ref[...] * (1.0 / RWKV_HEAD)
    mu = _dot_terms(y, head_mean, 2, 1)
    yc = y - mu
    var = _dot_terms(yc * yc, head_mean, 2, 1)
    yn = yc * lax.rsqrt(var + GN_EPS) * gng_ref[...] + gnb_ref[...]
    o_ref[...] = ((yn + bon_ref[...]) * g_ref[...]).astype(o_ref.dtype)


def _rwkv_readout(yf, yb, bonus, g, gn_g, gn_b, head_sum):
    B, T, W = yf.shape
    tt = TOK_TILE
    tok = pl.BlockSpec((None, tt, W), lambda b, i: (b, i, 0))
    row = pl.BlockSpec((1, W), lambda b, i: (0, 0))
    return pl.pallas_call(
        _rwkv_readout_body,
        grid=(B, T // tt),
        in_specs=[tok, tok, tok, tok, row, row, pl.BlockSpec((W, W), lambda b, i: (0, 0))],
        out_specs=tok,
        out_shape=jax.ShapeDtypeStruct((B, T, W), BF16),
        compiler_params=_cp("parallel", "parallel"),
        name="rwkv_readout",
    )(yf, yb, bonus, g, gn_g.reshape(1, W), gn_b.reshape(1, W), head_sum)


def _rope(y, cos, sin_signed, quarter):
    lane = lax.broadcasted_iota(jnp.int32, y.shape, 1)
    first = (lane & (2 * quarter - 1)) < quarter
    partner = jnp.where(first, pltpu.roll(y, LANES - quarter, 1), pltpu.roll(y, quarter, 1))
    return y * cos + partner * sin_signed


def _gqa_prep_body(p_ref, qg_ref, kg_ref, cos_ref, sin_ref, q_o, k_o, v_o):
    cos = cos_ref[...]
    sin = sin_ref[...]
    scale = GQA_HEAD ** -0.5 * LOG2E
    for h in range(GQA_Q_HEADS):
        sl = slice(h * GQA_HEAD, (h + 1) * GQA_HEAD)
        q = _rms(p_ref[:, sl], qg_ref[...])
        q_o[:, sl] = (_rope(q, cos, sin, GQA_HEAD // 4) * scale).astype(q_o.dtype)
    for h in range(GQA_KV_HEADS):
        sl = slice(h * GQA_HEAD, (h + 1) * GQA_HEAD)
        k = _rms(p_ref[:, GQA_Q_COLS + h * GQA_HEAD:GQA_Q_COLS + (h + 1) * GQA_HEAD], kg_ref[...])
        k_o[:, sl] = _rope(k, cos, sin, GQA_HEAD // 4).astype(k_o.dtype)
    v_o[...] = p_ref[:, GQA_Q_COLS + GQA_KV_COLS:GQA_COLS].astype(v_o.dtype)


def _row_blocks(n_rows):
    return [slice(r, r + TOK_TILE) for r in range(0, n_rows, TOK_TILE)]


def _gqa_proj_body(h_ref, w_ref, qg_ref, kg_ref, cos_ref, sin_ref, q_o, k_o, v_o):
    for rows in _row_blocks(h_ref.shape[0]):
        p = jnp.dot(h_ref[rows, :], w_ref[...], preferred_element_type=F32)
        _gqa_prep_body(p, qg_ref, kg_ref, cos_ref.at[rows, :], sin_ref.at[rows, :],
                       q_o.at[rows, :], k_o.at[rows, :], v_o.at[rows, :])


def _gqa_proj(h, w, q_norm_g, k_norm_g, cos, sin):
    B, T, D = h.shape
    tt = PROJ_TILE
    tok = lambda w_: pl.BlockSpec((None, tt, w_), lambda b, i: (b, i, 0))
    row = pl.BlockSpec((1, GQA_HEAD), lambda b, i: (0, 0))
    tab = pl.BlockSpec((tt, LANES), lambda b, i: (i, 0))
    return pl.pallas_call(
        _gqa_proj_body,
        grid=(B, T // tt),
        in_specs=[tok(D), pl.BlockSpec(w.shape, lambda b, i: (0, 0)), row, row, tab, tab],
        out_specs=[tok(GQA_Q_COLS), tok(GQA_KV_COLS), tok(GQA_KV_COLS)],
        out_shape=[jax.ShapeDtypeStruct((B, T, GQA_Q_COLS), BF16),
                   jax.ShapeDtypeStruct((B, T, GQA_KV_COLS), BF16),
                   jax.ShapeDtypeStruct((B, T, GQA_KV_COLS), BF16)],
        compiler_params=_cp("parallel", "parallel"),
        name="gqa_proj",
    )(h, w, q_norm_g.reshape(1, -1), k_norm_g.reshape(1, -1), cos, sin)


def _mla_prep_body(p_ref, qg_ref, wq_ref, kvg_ref, wkv_ref, cos_ref, sin_ref, q_o, k_o, v_o):
    cos = cos_ref[...]
    sin = sin_ref[...]
    scale = (MLA_NOPE + MLA_ROPE) ** -0.5 * LOG2E
    cq = _rms(p_ref[:, 0:MLA_Q_RANK], qg_ref[...])
    q = jnp.dot(cq.astype(BF16), wq_ref[...], preferred_element_type=F32) * scale
    ckv = _rms(p_ref[:, MLA_Q_RANK:MLA_Q_RANK + MLA_KV_RANK], kvg_ref[...])
    kv = jnp.dot(ckv.astype(BF16), wkv_ref[...], preferred_element_type=F32)
    kr = _rope(p_ref[:, MLA_Q_RANK + MLA_KV_RANK:MLA_COLS_PAD], cos, sin, MLA_ROPE // 4).astype(k_o.dtype)
    for h in range(MLA_HEADS):
        lo = h * MLA_DK
        q_o[:, lo:lo + LANES] = q[:, lo:lo + LANES].astype(q_o.dtype)
        q_o[:, lo + LANES:lo + MLA_DK] = _rope(q[:, lo + LANES:lo + MLA_DK], cos, sin, MLA_ROPE // 4).astype(q_o.dtype)
        k_o[:, lo:lo + LANES] = kv[:, h * MLA_NOPE:(h + 1) * MLA_NOPE].astype(k_o.dtype)
        k_o[:, lo + LANES:lo + MLA_DK] = kr
    v_o[...] = kv[:, MLA_HEADS * MLA_NOPE:].astype(v_o.dtype)


def _mla_proj_body(h_ref, w_ref, qg_ref, wq_ref, kvg_ref, wkv_ref, cos_ref, sin_ref, q_o, k_o, v_o):
    for rows in _row_blocks(h_ref.shape[0]):
        p = jnp.dot(h_ref[rows, :], w_ref[...], preferred_element_type=F32)
        _mla_prep_body(p, qg_ref, wq_ref, kvg_ref, wkv_ref, cos_ref.at[rows, :], sin_ref.at[rows, :],
                       q_o.at[rows, :], k_o.at[rows, :], v_o.at[rows, :])


def _mla_proj(h, w, q_norm_g, wq, kv_norm_g, wkv, cos, sin):
    B, T, D = h.shape
    tt = PROJ_TILE
    tok = lambda w_: pl.BlockSpec((None, tt, w_), lambda b, i: (b, i, 0))
    full = lambda a: pl.BlockSpec(a.shape, lambda b, i: (0,) * a.ndim)
    tab = pl.BlockSpec((tt, LANES), lambda b, i: (i, 0))
    qg = q_norm_g.reshape(1, -1)
    kvg = kv_norm_g.reshape(1, -1)
    return pl.pallas_call(
        _mla_proj_body,
        grid=(B, T // tt),
        in_specs=[tok(D), full(w), full(qg), full(wq), full(kvg), full(wkv), tab, tab],
        out_specs=[tok(MLA_HEADS * MLA_DK), tok(MLA_HEADS * MLA_DK), tok(MLA_HEADS * MLA_V)],
        out_shape=[jax.ShapeDtypeStruct((B, T, MLA_HEADS * MLA_DK), BF16),
                   jax.ShapeDtypeStruct((B, T, MLA_HEADS * MLA_DK), BF16),
                   jax.ShapeDtypeStruct((B, T, MLA_HEADS * MLA_V), BF16)],
        compiler_params=_cp("parallel", "parallel"),
        name="mla_proj",
    )(h, w, qg, wq, kvg, wkv, cos, sin)


def _attn_body(q_ref, k_ref, v_ref, o_ref, *, hq, hkv, dk, dv, n_ctx_tiles, n_ctx, n_all):
    rep = hq // hkv
    tq = q_ref.shape[0]

    def run(nk):
        for g in range(hkv):
            kg = k_ref[0:nk, g * dk:(g + 1) * dk]
            vg = v_ref[0:nk, g * dv:(g + 1) * dv]
            v_aug = jnp.concatenate([vg, jnp.ones_like(vg)], axis=1)
            for h0 in range(g * rep, (g + 1) * rep, ATTN_STACK):
                heads = range(h0, min(h0 + ATTN_STACK, (g + 1) * rep))
                q = jnp.concatenate([q_ref[:, h * dk:(h + 1) * dk] for h in heads], axis=0)
                s = _dot_nt(q, kg)
                p = jnp.exp2(s - jnp.max(s, axis=-1, keepdims=True))
                o = jnp.dot(p.astype(BF16), v_aug, preferred_element_type=F32)
                o = o[:, 0:dv] / o[:, dv:dv + 1]
                for j, h in enumerate(heads):
                    o_ref[:, h * dv:(h + 1) * dv] = o[j * tq:(j + 1) * tq].astype(o_ref.dtype)

    @pl.when(pl.program_id(1) < n_ctx_tiles)
    def _():
        run(n_ctx)

    @pl.when(pl.program_id(1) >= n_ctx_tiles)
    def _():
        run(n_all)


def _attention(q, k, v, hq, hkv, dk, dv, n_ctx):
    B, T, _ = q.shape
    tq = TOK_TILE
    return pl.pallas_call(
        functools.partial(_attn_body, hq=hq, hkv=hkv, dk=dk, dv=dv, n_ctx_tiles=n_ctx // tq, n_ctx=n_ctx, n_all=T),
        grid=(B, T // tq),
        in_specs=[pl.BlockSpec((None, tq, hq * dk), lambda b, i: (b, i, 0)),
                  pl.BlockSpec((None, T, hkv * dk), lambda b, i: (b, 0, 0)),
                  pl.BlockSpec((None, T, hkv * dv), lambda b, i: (b, 0, 0))],
        out_specs=pl.BlockSpec((None, tq, hq * dv), lambda b, i: (b, i, 0)),
        out_shape=jax.ShapeDtypeStruct((B, T, hq * dv), BF16),
        compiler_params=_cp("parallel", "parallel"),
        name="attention",
    )(q, k, v)


def _rope_tables(T, n_ctx, n_rot):
    quarter = n_rot // 4
    t = jnp.arange(T - n_ctx)
    row = (t // GRID_W).astype(F32)
    col = (t % GRID_W).astype(F32)
    inv = ROPE_THETA ** (-jnp.arange(quarter, dtype=F32) / quarter)
    ar = row[:, None] * inv[None, :]
    ac = col[:, None] * inv[None, :]
    pad = LANES - n_rot
    cos = jnp.concatenate([jnp.cos(ar), jnp.cos(ar), jnp.cos(ac), jnp.cos(ac), jnp.ones((T - n_ctx, pad), F32)], axis=1)
    sin = jnp.concatenate([-jnp.sin(ar), jnp.sin(ar), -jnp.sin(ac), jnp.sin(ac), jnp.zeros((T - n_ctx, pad), F32)], axis=1)
    cos = jnp.concatenate([jnp.ones((n_ctx, LANES), F32), cos], axis=0)
    sin = jnp.concatenate([jnp.zeros((n_ctx, LANES), F32), sin], axis=0)
    return cos, sin


def _moe_body(te_ref, first_ref, slot_ref, nxt_ref, nv_ref, hs_ref, w1_hbm, w3_hbm, w2_hbm, y_ref,
              f1, f3, f2, c1, c3, c2, sem, *, layer):
    i = pl.program_id(0)

    def weight_copies(e, s):
        return (pltpu.make_async_copy(w1_hbm.at[layer, e], f1.at[s], sem.at[s, 0]),
                pltpu.make_async_copy(w3_hbm.at[layer, e], f3.at[s], sem.at[s, 1]),
                pltpu.make_async_copy(w2_hbm.at[layer, e], f2.at[s], sem.at[s, 2]))

    @pl.when(i == 0)
    def _():
        for cp in weight_copies(te_ref[0], 0):
            cp.start()

    @pl.when(first_ref[i] == 1)
    def _():
        s = slot_ref[i]
        for cp in weight_copies(te_ref[i], s):
            cp.wait()

        @pl.when(nxt_ref[i] >= 0)
        def _():
            for cp in weight_copies(nxt_ref[i], 1 - s):
                cp.start()

        c1[...] = f1[s].astype(BF16)
        c3[...] = f3[s].astype(BF16)
        c2[...] = f2[s].astype(BF16)

    @pl.when(i < nv_ref[0])
    def _():
        hs = hs_ref[...].astype(BF16)
        a = jnp.dot(hs, c1[...], preferred_element_type=F32)
        b = jnp.dot(hs, c3[...], preferred_element_type=F32)
        act = a * _sigmoid(a) * b
        y = jnp.dot(act.astype(BF16), c2[...], preferred_element_type=F32)
        y_ref[...] = y.astype(y_ref.dtype)

    @pl.when(i >= nv_ref[0])
    def _():
        y_ref[...] = jnp.zeros_like(y_ref)


def _moe_experts(tile_expert, run_first, run_slot, run_next, n_valid, hs, w1, w3, w2, layer):
    NP, D = hs.shape
    tm = MOE_TM
    DE = w1.shape[-1]
    grid_spec = pltpu.PrefetchScalarGridSpec(
        num_scalar_prefetch=5,
        grid=(NP // tm,),
        in_specs=[pl.BlockSpec((tm, D), lambda i, *_: (i, 0)),
                  pl.BlockSpec(memory_space=pl.ANY),
                  pl.BlockSpec(memory_space=pl.ANY),
                  pl.BlockSpec(memory_space=pl.ANY)],
        out_specs=pl.BlockSpec((tm, D), lambda i, *_: (i, 0)),
        scratch_shapes=[pltpu.VMEM((2, D, DE), F32), pltpu.VMEM((2, D, DE), F32), pltpu.VMEM((2, DE, D), F32),
                        pltpu.VMEM((D, DE), BF16), pltpu.VMEM((D, DE), BF16), pltpu.VMEM((DE, D), BF16),
                        pltpu.SemaphoreType.DMA((2, 3))],
    )
    return pl.pallas_call(
        functools.partial(_moe_body, layer=layer),
        grid_spec=grid_spec,
        out_shape=jax.ShapeDtypeStruct((NP, D), BF16),
        compiler_params=_cp("arbitrary"),
        name="moe_experts",
    )(tile_expert, run_first, run_slot, run_next, n_valid, hs, w1, w3, w2)


def _moe(h2, re, w1, w3, w2, layer):
    N = h2.shape[0]
    tm = MOE_TM
    n_tiles = (2 * N) // tm + N_EXPERTS
    e_flat = re.reshape(2 * N)
    onehot = (e_flat[:, None] == jnp.arange(N_EXPERTS, dtype=jnp.int32)[None, :]).astype(jnp.int32)
    csum = jnp.cumsum(onehot, axis=0)
    rank = jnp.sum(onehot * (csum - 1), axis=1)
    counts = csum[-1]
    ptiles = (counts + tm - 1) // tm
    tile_end = jnp.cumsum(ptiles)
    tile_start = tile_end - ptiles
    pos = tile_start[e_flat] * tm + rank
    n_valid = tile_end[-1:].astype(jnp.int32)
    tile_ids = jnp.arange(n_tiles, dtype=jnp.int32)
    tile_expert = jnp.minimum(jnp.sum((tile_end[None, :] <= tile_ids[:, None]).astype(jnp.int32), axis=1), N_EXPERTS - 1)
    experts = jnp.arange(N_EXPERTS, dtype=jnp.int32)
    owns = ptiles > 0
    run_no = jnp.cumsum(owns.astype(jnp.int32)) - 1
    later = jnp.where(owns[None, :] & (experts[None, :] > experts[:, None]), experts[None, :], N_EXPERTS)
    next_run = jnp.min(later, axis=1)
    next_run = jnp.where(next_run < N_EXPERTS, next_run, -1).astype(jnp.int32)
    run_first = ((tile_ids == tile_start[tile_expert]) & (tile_ids < n_valid[0])).astype(jnp.int32)
    run_slot = (run_no[tile_expert] % 2).astype(jnp.int32)
    run_next = next_run[tile_expert]
    spread = jnp.arange(n_tiles * tm, dtype=jnp.int32) % N
    src = spread.at[pos].set(jnp.arange(2 * N, dtype=jnp.int32) // 2)
    rows = lambda a, idx: a.at[idx].get(mode="promise_in_bounds")
    hs = rows(h2, src)
    y = _moe_experts(tile_expert, run_first, run_slot, run_next, n_valid, hs, w1, w3, w2, layer)
    pos2 = pos.reshape(N, 2)
    return rows(y, pos2[:, 0]), rows(y, pos2[:, 1])


def kernel(x, c, ctx, c_ctx, mod_w, mod_b, norm1_g, norm2_g, w_in, w_out, shift_prev, shift_next, decay_w0, decay_up, iclr_a0, iclr_up, gate_up, k_k, k_a, r_k, gn_g, gn_b, q_norm_g, k_norm_g, mla_q_norm_g, mla_w_uq, mla_kv_norm_g, mla_w_ukv, router_gw, router_gb, router_ew, router_eb, exp_w1, exp_w3, exp_w2, final_norm_g):
    B, S, D = x.shape
    C = ctx.shape[1]
    T = C + S
    depth = mod_w.shape[0]
    assert C == TOK_TILE and S % TOK_TILE == 0 and B <= CTX_ROW
    n_ctx_tiles = C // TOK_TILE

    X = jnp.concatenate([ctx, x], axis=1)
    cc = jnp.zeros((SUBLANES, D), F32).at[:B].set(c).at[CTX_ROW].set(c_ctx)
    mods_all = _modulation(cc, mod_w, mod_b).reshape(depth, SUBLANES * N_MOD, 1, D)

    cos_g, sin_g = _rope_tables(T, C, GQA_HEAD)
    cos_m, sin_m = _rope_tables(T, C, MLA_ROPE)
    hid = jnp.arange(RWKV_W) // RWKV_HEAD
    head_sum = (hid[:, None] == hid[None, :]).astype(F32)

    for l in range(depth):
        mods = mods_all[l]
        w_r = w_in[l][:, :RWKV_COLS].astype(BF16)
        w_g = w_in[l][:, RWKV_COLS:RWKV_COLS + GQA_COLS].astype(BF16)
        w_m = jnp.pad(w_in[l][:, RWKV_COLS + GQA_COLS:], ((0, 0), (0, MLA_COLS_PAD - MLA_COLS))).astype(BF16)

        if l == 0:
            h = _norm_mod(X, norm1_g[l], mods, 0, 1, n_ctx_tiles)
        pr = _matmul(h.reshape(B * T, D), w_r).reshape(B, T, RWKV_COLS)

        wup_pad = jnp.pad(decay_up[l], ((0, 0), (0, ICLR_RANK), (0, 0)))
        aup_pad = jnp.pad(iclr_up[l], ((0, 0), (DECAY_RANK, 0), (0, 0)))
        r, v, kh, lw, b, kt, g, bonus = _rwkv_prep(pr, shift_prev[l], shift_next[l], decay_w0[l], wup_pad,
                                                  iclr_a0[l], aup_pad, gate_up[l], k_k[l], k_a[l], r_k[l], head_sum)
        yf, yb = _rwkv_scan(r, v, kh, lw, b, kt, C)
        o_r = _rwkv_readout(yf, yb, bonus, g, gn_g[l], gn_b[l], head_sum)

        q, k, vv = _gqa_proj(h, w_g, q_norm_g[l], k_norm_g[l], cos_g, sin_g)
        o_g = _attention(q, k, vv, GQA_Q_HEADS, GQA_KV_HEADS, GQA_HEAD, GQA_HEAD, C)

        wq = mla_w_uq[l].reshape(MLA_Q_RANK, MLA_HEADS, MLA_NOPE + MLA_ROPE)
        wq = jnp.pad(wq, ((0, 0), (0, 0), (0, MLA_DK - MLA_NOPE - MLA_ROPE))).reshape(MLA_Q_RANK, MLA_HEADS * MLA_DK)
        wkv = mla_w_ukv[l].reshape(MLA_KV_RANK, MLA_HEADS, MLA_NOPE + MLA_V)
        wkv = jnp.concatenate([wkv[:, :, :MLA_NOPE].reshape(MLA_KV_RANK, -1), wkv[:, :, MLA_NOPE:].reshape(MLA_KV_RANK, -1)], axis=1)
        qm, km, vm = _mla_proj(h, w_m, mla_q_norm_g[l], wq.astype(BF16), mla_kv_norm_g[l], wkv.astype(BF16), cos_m, sin_m)
        o_m = _attention(qm, km, vm, MLA_HEADS, MLA_HEADS, MLA_DK, MLA_V, C)

        last = l + 1 == depth
        wr = jnp.pad(jnp.concatenate([router_gw[l], router_ew[l]], axis=1), ((0, 0), (0, LANES - N_GROUPS - N_EXPERTS)))
        br = jnp.pad(jnp.concatenate([router_gb[l], router_eb[l]]), (0, LANES - N_GROUPS - N_EXPERTS)).reshape(1, LANES)
        X, h2, rw, re = _out_proj_router([o_r, o_g, o_m], w_out[l].astype(BF16), X, mods, C,
                                         TOK_TILE if last else OUT_TILE, last, norm2_g[l], wr, br)
        rows = X.shape[1]
        ctx_tiles = 0 if last else n_ctx_tiles

        y0, y1 = _moe(h2.reshape(B * rows, D), re.reshape(B * rows, LANES)[:, :2], exp_w1, exp_w3, exp_w2, l)
        y0 = y0.reshape(B, rows, D)
        y1 = y1.reshape(B, rows, D)
        if last:
            out = _moe_residual_final(X, y0, y1, rw, mods, 5, ctx_tiles, final_norm_g)
        else:
            X, h = _moe_residual_norm(X, y0, y1, rw, mods, 5, ctx_tiles, norm1_g[l + 1], mods_all[l + 1])
    return out
```

```python
import functools
import math

import jax
import jax.numpy as jnp
import numpy as np
from jax import lax
from jax.experimental import pallas as pl
from jax.experimental.pallas import tpu as pltpu

F32 = jnp.float32
BF16 = jnp.bfloat16

V7X_VMEM_BYTES = 64 * 1024 * 1024
VMEM_LIMIT = V7X_VMEM_BYTES - 8 * 1024 * 1024
LANES = 128
SUBLANES = 8

GRID_W = 64
ROPE_THETA = 10000.0
NORM_EPS = 1e-6
GN_EPS = 64e-5
DECAY_SCALE = math.exp(-0.5)
LOG2E = math.log2(math.e)

RWKV_HEAD = 64
RWKV_W = 512
DECAY_RANK = 64
ICLR_RANK = 64
GATE_RANK = 128
RWKV_COLS = 3 * RWKV_W + DECAY_RANK + ICLR_RANK + GATE_RANK
LOWRANK_OFF = 3 * RWKV_W
GATE_OFF = LOWRANK_OFF + DECAY_RANK + ICLR_RANK
CHUNK = 64
SCAN_CHUNKS = 2

GQA_HEAD = 128
GQA_Q_HEADS = 8
GQA_KV_HEADS = 2
GQA_Q_COLS = GQA_Q_HEADS * GQA_HEAD
GQA_KV_COLS = GQA_KV_HEADS * GQA_HEAD
GQA_COLS = GQA_Q_COLS + 2 * GQA_KV_COLS

MLA_HEADS = 4
MLA_NOPE = 128
MLA_ROPE = 64
MLA_V = 128
MLA_Q_RANK = 384
MLA_KV_RANK = 256
MLA_COLS = MLA_Q_RANK + MLA_KV_RANK + MLA_ROPE
MLA_COLS_PAD = 768
MLA_DK = 2 * LANES

N_GROUPS = 4
EXPERTS_PER_GROUP = 8
N_EXPERTS = 32
D_EXPERT = 256
MOE_TM = 256

TOK_TILE = 256
PROJ_TILE = 768
OUT_TILE = 384
OUT_SUBBLOCKS = 1
ATTN_STACK = 1
N_MOD = 6
CTX_ROW = 4


def _cp(*sem):
    return pltpu.CompilerParams(dimension_semantics=sem, vmem_limit_bytes=VMEM_LIMIT)


def _sigmoid(x):
    return 1.0 / (1.0 + jnp.exp(-x))


def _bf16_terms(x, n):
    terms = []
    for _ in range(n):
        t = x.astype(BF16)
        terms.append(t)
        x = x - t.astype(F32)
    return terms


def _dot_terms(x, w, nx, nw):
    xs = _bf16_terms(x, nx)
    ws = _bf16_terms(w, nw)
    acc = None
    for i in range(nx):
        for j in range(nw):
            if i + j < max(nx, nw):
                p = jnp.dot(xs[i], ws[j], preferred_element_type=F32)
                acc = p if acc is None else acc + p
    return acc


def _mod_body(c_ref, w_ref, b_ref, o_ref):
    c = c_ref[...]
    s = c * _sigmoid(c)
    o_ref[...] = _dot_terms(s, w_ref[...], 2, 2) + b_ref[...]


def _modulation(cc, mod_w, mod_b):
    L, D, N = mod_w.shape
    tn = 1024
    return pl.pallas_call(
        _mod_body,
        grid=(L, N // tn),
        in_specs=[pl.BlockSpec((SUBLANES, D), lambda l, j: (0, 0)),
                  pl.BlockSpec((None, D, tn), lambda l, j: (l, 0, j)),
                  pl.BlockSpec((None, 1, tn), lambda l, j: (l, 0, j))],
        out_specs=pl.BlockSpec((None, SUBLANES, tn), lambda l, j: (l, 0, j)),
        out_shape=jax.ShapeDtypeStruct((L, SUBLANES, N), F32),
        compiler_params=_cp("parallel", "parallel"),
        name="modulation",
    )(cc, mod_w, mod_b.reshape(L, 1, N))


def _mod_spec(which, n_ctx_tiles, D):
    return pl.BlockSpec((None, 1, D), lambda b, i: (jnp.where(i < n_ctx_tiles, CTX_ROW, b) * N_MOD + which, 0, 0))


def _rms(x, g):
    return x * lax.rsqrt(jnp.mean(x * x, axis=-1, keepdims=True) + NORM_EPS) * g


def _join_norm_body(ctx_ref, x_ref, g_ref, sh_ref, sc_ref, x_o, h_o, *, n_ctx_tiles):
    i = pl.program_id(1)

    def emit(src_ref):
        x = src_ref[...]
        x_o[...] = x
        h_o[...] = (_rms(x, g_ref[...]) * (1.0 + sc_ref[...]) + sh_ref[...]).astype(h_o.dtype)

    @pl.when(i < n_ctx_tiles)
    def _():
        emit(ctx_ref)

    @pl.when(i >= n_ctx_tiles)
    def _():
        emit(x_ref)


def _route(logits):
    lane = lax.broadcasted_iota(jnp.int32, logits.shape, 1)
    lane_f = lane.astype(F32)
    neg = jnp.float32(-1e30)
    far = jnp.float32(1e9)
    first_at = lambda hit: jnp.min(jnp.where(hit, lane_f, far), axis=-1, keepdims=True).astype(jnp.int32)
    gl = jnp.where(lane < N_GROUPS, logits, neg)
    gmax = jnp.max(gl, axis=-1, keepdims=True)
    gidx = first_at(gl == gmax)
    p_sel = 1.0 / jnp.sum(jnp.exp(gl - gmax), axis=-1, keepdims=True)
    lo = N_GROUPS + gidx * EXPERTS_PER_GROUP
    el = jnp.where((lane >= lo) & (lane < lo + EXPERTS_PER_GROUP), logits, neg)
    m1 = jnp.max(el, axis=-1, keepdims=True)
    i1 = first_at(el == m1)
    el2 = jnp.where(lane == i1, neg, el)
    m2 = jnp.max(el2, axis=-1, keepdims=True)
    i2 = first_at(el2 == m2)
    t = jnp.exp(m2 - m1)
    w1 = p_sel / (1.0 + t)
    w2 = p_sel * t / (1.0 + t)
    rw = jnp.where(lane == 0, w1, jnp.where(lane == 1, w2, 0.0))
    re = jnp.where(lane == 0, i1 - N_GROUPS, jnp.where(lane == 1, i2 - N_GROUPS, 0))
    return rw, re


def _join_norm(ctx, x, g, mods, shift_i, scale_i):
    B, C, D = ctx.shape
    tt = TOK_TILE
    n_ctx_tiles = C // tt
    T = C + x.shape[1]
    tok = pl.BlockSpec((None, tt, D), lambda b, i: (b, i, 0))
    return pl.pallas_call(
        functools.partial(_join_norm_body, n_ctx_tiles=n_ctx_tiles),
        grid=(B, T // tt),
        in_specs=[pl.BlockSpec((None, tt, D), lambda b, i: (b, jnp.minimum(i, n_ctx_tiles - 1), 0)),
                  pl.BlockSpec((None, tt, D), lambda b, i: (b, jnp.maximum(i - n_ctx_tiles, 0), 0)),
                  pl.BlockSpec((1, D), lambda b, i: (0, 0)),
                  _mod_spec(shift_i, n_ctx_tiles, D),
                  _mod_spec(scale_i, n_ctx_tiles, D)],
        out_specs=[tok, tok],
        out_shape=[jax.ShapeDtypeStruct((B, T, D), F32), jax.ShapeDtypeStruct((B, T, D), BF16)],
        compiler_params=_cp("parallel", "parallel"),
        name="join_norm",
    )(ctx, x, g.reshape(1, D), mods, mods)


def _mm_body(a_ref, w_ref, o_ref):
    o_ref[...] = jnp.dot(a_ref[...], w_ref[...], preferred_element_type=F32).astype(o_ref.dtype)


def _matmul(a, w, tm=512):
    M, K = a.shape
    N = w.shape[1]
    return pl.pallas_call(
        _mm_body,
        grid=(M // tm,),
        in_specs=[pl.BlockSpec((tm, K), lambda i: (i, 0)),
                  pl.BlockSpec((K, N), lambda i: (0, 0))],
        out_specs=pl.BlockSpec((tm, N), lambda i: (i, 0)),
        out_shape=jax.ShapeDtypeStruct((M, N), F32),
        compiler_params=_cp("parallel"),
        name="token_matmul",
    )(a, w)


def _out_proj_router_body(*refs, n_parts, n_ctx, tm, tile0):
    a_refs = refs[:n_parts]
    w_ref, x_ref, ml_ref, mc_ref, g_ref, wr_ref, br_ref, x_o, h_o, rw_o, re_o = refs[n_parts:]
    sub = tm // OUT_SUBBLOCKS
    for r0 in range(0, tm, sub):
        rows = slice(r0, r0 + sub)
        acc = None
        k0 = 0
        for a_ref in a_refs:
            k1 = k0 + a_ref.shape[-1]
            part = jnp.dot(a_ref[rows, :], w_ref[k0:k1, :], preferred_element_type=F32)
            acc = part if acc is None else acc + part
            k0 = k1
        row = (pl.program_id(1) + tile0) * tm + r0 + lax.broadcasted_iota(jnp.int32, (sub, 1), 0)
        is_ctx = row < n_ctx
        mod = lambda which: jnp.where(is_ctx, mc_ref[which], ml_ref[which])
        x = x_ref[rows, :] + mod(2) * acc
        x_o[rows, :] = x
        h = _rms(x, g_ref[...]) * (1.0 + mod(4)) + mod(3)
        h_o[rows, :] = h
        h_hi, h_lo = _bf16_terms(h, 2)
        w_hi, w_lo = _bf16_terms(wr_ref[...], 2)
        both = jnp.dot(h_hi, jnp.concatenate([w_hi, w_lo], axis=1), preferred_element_type=F32)
        logits = (both[:, 0:LANES] + both[:, LANES:2 * LANES]
                  + jnp.dot(h_lo, w_hi, preferred_element_type=F32) + br_ref[...])
        rw, re = _route(logits)
        rw_o[rows, :] = rw
        re_o[rows, :] = re


def _out_proj_router(parts, w, X, mods, n_ctx, tm, latents_only, g2, wr, br):
    B, T, D = X.shape
    tile0 = n_ctx // tm if latents_only else 0
    assert T % tm == 0 and (n_ctx % tm == 0 or not latents_only)
    assert sum(p.shape[-1] for p in parts) == w.shape[0]
    rows_out = T - tile0 * tm
    mods6 = mods.reshape(SUBLANES, N_MOD, 1, D)
    tok_in = lambda wd: pl.BlockSpec((None, tm, wd), lambda b, i: (b, i + tile0, 0))
    tok_out = lambda wd: pl.BlockSpec((None, tm, wd), lambda b, i: (b, i, 0))
    shape = lambda wd, dt: jax.ShapeDtypeStruct((B, rows_out, wd), dt)
    return pl.pallas_call(
        functools.partial(_out_proj_router_body, n_parts=len(parts), n_ctx=n_ctx, tm=tm, tile0=tile0),
        grid=(B, T // tm - tile0),
        in_specs=[tok_in(p.shape[-1]) for p in parts] + [
                  pl.BlockSpec(w.shape, lambda b, i: (0, 0)),
                  tok_in(D),
                  pl.BlockSpec((None, N_MOD, 1, D), lambda b, i: (b, 0, 0, 0)),
                  pl.BlockSpec((None, N_MOD, 1, D), lambda b, i: (CTX_ROW, 0, 0, 0)),
                  pl.BlockSpec((1, D), lambda b, i: (0, 0)),
                  pl.BlockSpec((D, LANES), lambda b, i: (0, 0)),
                  pl.BlockSpec((1, LANES), lambda b, i: (0, 0))],
        out_specs=[tok_out(D), tok_out(D), tok_out(LANES), tok_out(LANES)],
        out_shape=[shape(D, F32), shape(D, F32), shape(LANES, F32), shape(LANES, jnp.int32)],
        compiler_params=_cp("parallel", "parallel"),
        name="out_proj_router",
    )(*parts, w, X, mods6, mods6, g2.reshape(1, D), wr, br)


def _moe_residual(x_ref, y0_ref, y1_ref, rw_ref, gate_ref):
    rw = rw_ref[...]
    moe = rw[:, 0:1] * y0_ref[...].astype(F32) + rw[:, 1:2] * y1_ref[...].astype(F32)
    return x_ref[...] + gate_ref[...] * moe


def _moe_residual_norm_body(x_ref, y0_ref, y1_ref, rw_ref, gate_ref, g_ref, sh_ref, sc_ref, x_o, h_o):
    x = _moe_residual(x_ref, y0_ref, y1_ref, rw_ref, gate_ref)
    x_o[...] = x
    h_o[...] = (_rms(x, g_ref[...]) * (1.0 + sc_ref[...]) + sh_ref[...]).astype(h_o.dtype)


def _moe_residual_final_body(x_ref, y0_ref, y1_ref, rw_ref, gate_ref, g_ref, o_ref):
    o_ref[...] = _rms(_moe_residual(x_ref, y0_ref, y1_ref, rw_ref, gate_ref), g_ref[...])


def _moe_residual_norm(X, Y0, Y1, rw, mods, gate_i, n_ctx_tiles, g_next, mods_next):
    B, T, D = X.shape
    tt = TOK_TILE
    blk = pl.BlockSpec((None, tt, D), lambda b, i: (b, i, 0))
    return pl.pallas_call(
        _moe_residual_norm_body,
        grid=(B, T // tt),
        in_specs=[blk, blk, blk, pl.BlockSpec((None, tt, LANES), lambda b, i: (b, i, 0)),
                  _mod_spec(gate_i, n_ctx_tiles, D), pl.BlockSpec((1, D), lambda b, i: (0, 0)),
                  _mod_spec(0, n_ctx_tiles, D), _mod_spec(1, n_ctx_tiles, D)],
        out_specs=[blk, blk],
        out_shape=[jax.ShapeDtypeStruct((B, T, D), F32), jax.ShapeDtypeStruct((B, T, D), BF16)],
        compiler_params=_cp("parallel", "parallel"),
        name="moe_residual_norm",
    )(X, Y0, Y1, rw, mods, g_next.reshape(1, D), mods_next, mods_next)


def _moe_residual_final(X, Y0, Y1, rw, mods, gate_i, n_ctx_tiles, g_final):
    B, T, D = X.shape
    tt = TOK_TILE
    S = T - n_ctx_tiles * tt
    lat = lambda w: pl.BlockSpec((None, tt, w), lambda b, i: (b, i + n_ctx_tiles, 0))
    return pl.pallas_call(
        _moe_residual_final_body,
        grid=(B, S // tt),
        in_specs=[lat(D), lat(D), lat(D), lat(LANES),
                  pl.BlockSpec((None, 1, D), lambda b, i: (b * N_MOD + gate_i, 0, 0)),
                  pl.BlockSpec((1, D), lambda b, i: (0, 0))],
        out_specs=pl.BlockSpec((None, tt, D), lambda b, i: (b, i, 0)),
        out_shape=jax.ShapeDtypeStruct((B, S, D), F32),
        compiler_params=_cp("parallel", "parallel"),
        name="moe_residual_final",
    )(X, Y0, Y1, rw, mods, g_final.reshape(1, D))


def _rwkv_prep_body(p_ref, pv_ref, nx_ref, mup_ref, mun_ref, w0_ref, wup_ref, a0_ref, aup_ref, gup_ref,
                    kk_ref, ka_ref, rk_ref, e_ref,
                    r_o, v_o, kh_o, lw_o, b_o, kt_o, g_o, bon_o, *, n_tiles, tt):
    i = pl.program_id(1)
    p = p_ref[...]
    seq_first = i <= 1
    seq_last = (i == 0) | (i == n_tiles - 1)
    prow = jnp.where(seq_first, 0.0, pv_ref[SUBLANES - 1:SUBLANES, :])
    nrow = jnp.where(seq_last, 0.0, nx_ref[0:1, :])
    rid = lax.broadcasted_iota(jnp.int32, (tt, 1), 0)
    prev = jnp.where(rid == 0, prow, pltpu.roll(p, 1, 0))
    nxt = jnp.where(rid == tt - 1, nrow, pltpu.roll(p, tt - 1, 0))
    z = p + mup_ref[...] * (prev - p) + mun_ref[...] * (nxt - p)

    W = RWKV_W
    r = z[:, 0:W]
    k = z[:, W:2 * W]
    v = z[:, 2 * W:3 * W]
    lowrank = z[:, LOWRANK_OFF:LOWRANK_OFF + LANES]
    gd = z[:, GATE_OFF:GATE_OFF + GATE_RANK]
    head_sum = e_ref[...]

    kap = k * kk_ref[...]
    ss = _dot_terms(kap * kap, head_sum, 2, 1)
    khat = kap * lax.rsqrt(ss + 1e-12)
    wd_t = jnp.tanh(lowrank)
    g_o[...] = _dot_terms(_sigmoid(gd), gup_ref[...], 1, 1)
    r_o[...] = r
    v_o[...] = v
    kh_o[...] = khat
    kt_sum = None
    for d in range(2):
        dec = _dot_terms(wd_t, wup_ref[d], 2, 2)
        lw_o[d] = -DECAY_SCALE * _sigmoid(w0_ref[d:d + 1, :] + dec)
        a = _sigmoid(a0_ref[d:d + 1, :] + _dot_terms(lowrank, aup_ref[d], 1, 1))
        kt = k * (1.0 + (a - 1.0) * ka_ref[...])
        kt_o[d] = kt
        b_o[d] = a * khat
        kt_sum = kt if kt_sum is None else kt_sum + kt
    bsum = _dot_terms(r * kt_sum * rk_ref[...], head_sum, 2, 1)
    bon_o[...] = bsum * v


def _rwkv_prep(pr, mu_prev, mu_next, w0, wup_pad, a0, aup_pad, g_up, k_k, k_a, r_k, head_sum):
    B, T, _ = pr.shape
    tt = TOK_TILE
    W = RWKV_W
    n_tiles = T // tt
    n8 = tt // SUBLANES
    row = lambda v: v.reshape(1, -1)
    full = lambda a: pl.BlockSpec(a.shape, lambda b, i: (0,) * a.ndim)
    tok = pl.BlockSpec((None, tt, W), lambda b, i: (b, i, 0))
    tok2 = pl.BlockSpec((None, 2, tt, W), lambda b, i: (b, 0, i, 0))
    s1 = jax.ShapeDtypeStruct((B, T, W), F32)
    s2 = jax.ShapeDtypeStruct((B, 2, T, W), F32)
    consts = [row(mu_prev), row(mu_next), w0, wup_pad, a0, aup_pad, g_up, row(k_k), row(k_a), row(r_k), head_sum]
    return pl.pallas_call(
        functools.partial(_rwkv_prep_body, n_tiles=n_tiles, tt=tt),
        grid=(B, n_tiles),
        in_specs=[pl.BlockSpec((None, tt, RWKV_COLS), lambda b, i: (b, i, 0)),
                  pl.BlockSpec((None, SUBLANES, RWKV_COLS), lambda b, i: (b, jnp.maximum(i * n8 - 1, 0), 0)),
                  pl.BlockSpec((None, SUBLANES, RWKV_COLS), lambda b, i: (b, jnp.minimum((i + 1) * n8, T // SUBLANES - 1), 0)),
                  ] + [full(a) for a in consts],
        out_specs=[tok, tok, tok, tok2, tok2, tok2, tok, tok],
        out_shape=[s1, s1, s1, s2, s2, s2, s1, s1],
        compiler_params=_cp("parallel", "parallel"),
        name="rwkv_prep",
    )(pr, pr, pr, *consts)


def _stack_heads(x):
    lane = lax.broadcasted_iota(jnp.int32, x.shape, 1)
    first = lane < RWKV_HEAD
    return jnp.concatenate([jnp.where(first, x, 0.0), jnp.where(first, 0.0, x)], axis=0)


def _dot(a, b):
    return jnp.dot(a.astype(BF16), b.astype(BF16), preferred_element_type=F32)


def _dot_nt(a, b):
    return lax.dot_general(a.astype(BF16), b.astype(BF16), (((1,), (1,)), ((), ())), preferred_element_type=F32)


def _dot_tn(a, b):
    return jnp.dot(a.T.astype(BF16), b.astype(BF16), preferred_element_type=F32)


def _chunk_operands(r, v, kh, lw, b, kt, reverse):
    L = CHUNK
    ti = lax.broadcasted_iota(jnp.int32, (L, L), 0)
    tj = lax.broadcasted_iota(jnp.int32, (L, L), 1)
    tri = jnp.where((ti <= tj) if reverse else (ti >= tj), 1.0, 0.0)
    lam = _dot_terms(tri, lw, 1, 3)
    tot = lam[0:1, :] if reverse else lam[L - 1:L, :]
    e_n = jnp.exp(-lam)
    e_g = jnp.exp(tot - lam)
    full = dict(A=kh * jnp.exp(lam - lw), R=r * jnp.exp(lam), Kn=kt * e_n, Bn=b * e_n, Kg=kt * e_g, Bg=b * e_g, V=v)
    e_tot = jnp.exp(tot)
    pairs = []
    for p in range(RWKV_W // LANES):
        sl = slice(p * LANES, (p + 1) * LANES)
        ops = {k: a[:, sl] for k, a in full.items()}
        ops["e_tot"] = e_tot[:, sl]
        ops["reverse"] = reverse
        pairs.append(ops)
    return pairs


def _chunk_masks(reverse):
    L = CHUNK
    t = lax.broadcasted_iota(jnp.int32, (L, 2 * L), 0)
    i = lax.broadcasted_iota(jnp.int32, (L, 2 * L), 1) & (L - 1)
    before = (i > t) if reverse else (i < t)
    return before, before | (i == t), i == t


def _chunks_prepare(chains):
    L = CHUNK
    P2 = 2 * L
    n = len(chains)
    masks = {rev: _chunk_masks(rev) for rev in {c["reverse"] for c in chains}}
    strict = [masks[c["reverse"]][0] for c in chains]
    incl = [masks[c["reverse"]][1] for c in chains]
    eye = masks[chains[0]["reverse"]][2]
    bi = lax.broadcasted_iota(jnp.int32, (P2, P2), 0)
    bj = lax.broadcasted_iota(jnp.int32, (P2, P2), 1)
    same_head = (bi >= L) == (bj >= L)
    eye2 = bi == bj
    stack = _stack_heads
    cat0 = lambda *xs: jnp.concatenate(xs, axis=0)
    cat1 = lambda *xs: jnp.concatenate(xs, axis=1)

    big = [_dot_nt(cat0(c["A"], c["R"]), cat0(stack(c["Bn"]), stack(c["Kn"]))) for c in chains]
    Mb = [jnp.where(strict[i], big[i][0:L, 0:P2], 0.0) for i in range(n)]
    Mkv = [jnp.where(strict[i], big[i][0:L, P2:2 * P2], 0.0) for i in range(n)]
    Pb = [jnp.where(incl[i], big[i][L:P2, 0:P2], 0.0) for i in range(n)]
    Pkv = [jnp.where(incl[i], big[i][L:P2, P2:2 * P2], 0.0) for i in range(n)]

    Pw = [-m for m in Mb]
    Tm = [jnp.where(eye, 1.0, 0.0) + p for p in Pw]
    Pw = [_dot(p, stack(p)) for p in Pw]
    for _ in range(int(math.log2(L)) - 2):
        PT = [_dot(cat0(p, t), stack(p)) for p, t in zip(Pw, Tm)]
        Tm = [t + pt[L:P2] for t, pt in zip(Tm, PT)]
        Pw = [pt[0:L] for pt in PT]
    Tm = [t + _dot(t, stack(p)) for t, p in zip(Tm, Pw)]

    Vs = [stack(c["V"]) for c in chains]
    MPV = [_dot(cat0(Mkv[i], Pkv[i]), Vs[i]) for i in range(n)]
    TAM = [_dot(Tm[i], cat1(stack(chains[i]["A"]), stack(MPV[i][0:L]))) for i in range(n)]
    PB = [_dot(Pb[i], cat1(stack(TAM[i][:, 0:P2]), stack(TAM[i][:, P2:2 * P2]))) for i in range(n)]
    BG = [_dot_tn(chains[i]["Bg"], TAM[i]) for i in range(n)]
    KV = [_dot_tn(chains[i]["Kg"], chains[i]["V"]) for i in range(n)]
    prepared = []
    for i in range(n):
        RA = chains[i]["R"] - PB[i][:, 0:P2]
        G2 = jnp.where(eye2, chains[i]["e_tot"], 0.0) - jnp.where(same_head, BG[i][:, 0:P2], 0.0)
        H2 = jnp.where(same_head, KV[i] - BG[i][:, P2:2 * P2], 0.0)
        prepared.append((cat0(RA, G2), MPV[i][L:P2] - PB[i][:, P2:2 * P2], H2))
    return prepared


def _chunk_advance(prepared, state):
    lhs, y0, h2 = prepared
    out = _dot(lhs, state)
    return out[0:CHUNK] + y0, out[CHUNK:] + h2


def _rwkv_scan_body(rf_ref, vf_ref, khf_ref, rb_ref, vb_ref, khb_ref, lwf_ref, bf_ref, ktf_ref, lwb_ref, bb_ref, ktb_ref,
                    yf_ref, yb_ref, s_ref):
    @pl.when(pl.program_id(1) == 0)
    def _():
        s_ref[...] = jnp.zeros_like(s_ref)

    n_pairs = RWKV_W // LANES
    fwd_refs = (rf_ref, vf_ref, khf_ref, lwf_ref, bf_ref, ktf_ref)
    bwd_refs = (rb_ref, vb_ref, khb_ref, lwb_ref, bb_ref, ktb_ref)
    slots = []
    for c in range(SCAN_CHUNKS):
        lo_f = c * CHUNK
        lo_b = (SCAN_CHUNKS - 1 - c) * CHUNK
        slots.append((_chunk_operands(*(ref[lo_f:lo_f + CHUNK, :] for ref in fwd_refs), False), lo_f,
                      _chunk_operands(*(ref[lo_b:lo_b + CHUNK, :] for ref in bwd_refs), True), lo_b))
    prepared = _chunks_prepare([ch for s in slots for ch in s[0] + s[2]])
    states = [s_ref[i] for i in range(2 * n_pairs)]
    for c, (_, lo_f, _, lo_b) in enumerate(slots):
        ys = []
        for i in range(2 * n_pairs):
            y, states[i] = _chunk_advance(prepared[c * 2 * n_pairs + i], states[i])
            ys.append(y)
        yf_ref[lo_f:lo_f + CHUNK, :] = jnp.concatenate(ys[:n_pairs], axis=1)
        yb_ref[lo_b:lo_b + CHUNK, :] = jnp.concatenate(ys[n_pairs:], axis=1)
    for i in range(2 * n_pairs):
        s_ref[i] = states[i]


def _rwkv_scan(r, v, kh, lw, b, kt, n_ctx):
    B, T, W = r.shape
    blk = SCAN_CHUNKS * CHUNK
    assert T % blk == 0 and n_ctx % blk == 0
    nc = T // blk
    ncc = n_ctx // blk
    rev = lambda j: jnp.where(j < ncc, ncc - 1 - j, nc + ncc - 1 - j)
    fwd1 = pl.BlockSpec((None, blk, W), lambda bb, j: (bb, j, 0))
    bwd1 = pl.BlockSpec((None, blk, W), lambda bb, j: (bb, rev(j), 0))
    fwd2 = pl.BlockSpec((None, None, blk, W), lambda bb, j: (bb, 0, j, 0))
    bwd2 = pl.BlockSpec((None, None, blk, W), lambda bb, j: (bb, 1, rev(j), 0))
    out = jax.ShapeDtypeStruct((B, T, W), F32)
    return pl.pallas_call(
        _rwkv_scan_body,
        grid=(B, nc),
        in_specs=[fwd1, fwd1, fwd1, bwd1, bwd1, bwd1, fwd2, fwd2, fwd2, bwd2, bwd2, bwd2],
        out_specs=[fwd1, bwd1],
        out_shape=[out, out],
        scratch_shapes=[pltpu.VMEM((2 * W // LANES, LANES, LANES), F32)],
        compiler_params=_cp("parallel", "arbitrary"),
        name="rwkv_scan",
    )(r, v, kh, r, v, kh, lw, b, kt, lw, b, kt)


def _rwkv_readout_body(yf_ref, yb_ref, bon_ref, g_ref, gng_ref, gnb_ref, e_ref, o_ref):
    y = yf_ref[...] + yb_ref[...]
    head_mean = e_ref[...] * (1.0 / RWKV_HEAD)
    mu = _dot_terms(y, head_mean, 2, 1)
    yc = y - mu
    var = _dot_terms(yc * yc, head_mean, 2, 1)
    yn = yc * lax.rsqrt(var + GN_EPS) * gng_ref[...] + gnb_ref[...]
    o_ref[...] = ((yn + bon_ref[...]) * g_ref[...]).astype(o_ref.dtype)


def _rwkv_readout(yf, yb, bonus, g, gn_g, gn_b, head_sum):
    B, T, W = yf.shape
    tt = TOK_TILE
    tok = pl.BlockSpec((None, tt, W), lambda b, i: (b, i, 0))
    row = pl.BlockSpec((1, W), lambda b, i: (0, 0))
    return pl.pallas_call(
        _rwkv_readout_body,
        grid=(B, T // tt),
        in_specs=[tok, tok, tok, tok, row, row, pl.BlockSpec((W, W), lambda b, i: (0, 0))],
        out_specs=tok,
        out_shape=jax.ShapeDtypeStruct((B, T, W), BF16),
        compiler_params=_cp("parallel", "parallel"),
        name="rwkv_readout",
    )(yf, yb, bonus, g, gn_g.reshape(1, W), gn_b.reshape(1, W), head_sum)


def _rope(y, cos, sin_signed, quarter):
    lane = lax.broadcasted_iota(jnp.int32, y.shape, 1)
    first = (lane & (2 * quarter - 1)) < quarter
    partner = jnp.where(first, pltpu.roll(y, LANES - quarter, 1), pltpu.roll(y, quarter, 1))
    return y * cos + partner * sin_signed


def _gqa_prep_body(p_ref, qg_ref, kg_ref, cos_ref, sin_ref, q_o, k_o, v_o):
    cos = cos_ref[...]
    sin = sin_ref[...]
    scale = GQA_HEAD ** -0.5 * LOG2E
    for h in range(GQA_Q_HEADS):
        sl = slice(h * GQA_HEAD, (h + 1) * GQA_HEAD)
        q = _rms(p_ref[:, sl], qg_ref[...])
        q_o[:, sl] = (_rope(q, cos, sin, GQA_HEAD // 4) * scale).astype(q_o.dtype)
    for h in range(GQA_KV_HEADS):
        sl = slice(h * GQA_HEAD, (h + 1) * GQA_HEAD)
        k = _rms(p_ref[:, GQA_Q_COLS + h * GQA_HEAD:GQA_Q_COLS + (h + 1) * GQA_HEAD], kg_ref[...])
        k_o[:, sl] = _rope(k, cos, sin, GQA_HEAD // 4).astype(k_o.dtype)
    v_o[...] = p_ref[:, GQA_Q_COLS + GQA_KV_COLS:GQA_COLS].astype(v_o.dtype)


def _row_blocks(n_rows):
    return [slice(r, r + TOK_TILE) for r in range(0, n_rows, TOK_TILE)]


def _gqa_proj_body(h_ref, w_ref, qg_ref, kg_ref, cos_ref, sin_ref, q_o, k_o, v_o):
    for rows in _row_blocks(h_ref.shape[0]):
        p = jnp.dot(h_ref[rows, :], w_ref[...], preferred_element_type=F32)
        _gqa_prep_body(p, qg_ref, kg_ref, cos_ref.at[rows, :], sin_ref.at[rows, :],
                       q_o.at[rows, :], k_o.at[rows, :], v_o.at[rows, :])


def _gqa_proj(h, w, q_norm_g, k_norm_g, cos, sin):
    B, T, D = h.shape
    tt = PROJ_TILE
    tok = lambda w_: pl.BlockSpec((None, tt, w_), lambda b, i: (b, i, 0))
    row = pl.BlockSpec((1, GQA_HEAD), lambda b, i: (0, 0))
    tab = pl.BlockSpec((tt, LANES), lambda b, i: (i, 0))
    return pl.pallas_call(
        _gqa_proj_body,
        grid=(B, T // tt),
        in_specs=[tok(D), pl.BlockSpec(w.shape, lambda b, i: (0, 0)), row, row, tab, tab],
        out_specs=[tok(GQA_Q_COLS), tok(GQA_KV_COLS), tok(GQA_KV_COLS)],
        out_shape=[jax.ShapeDtypeStruct((B, T, GQA_Q_COLS), BF16),
                   jax.ShapeDtypeStruct((B, T, GQA_KV_COLS), BF16),
                   jax.ShapeDtypeStruct((B, T, GQA_KV_COLS), BF16)],
        compiler_params=_cp("parallel", "parallel"),
        name="gqa_proj",
    )(h, w, q_norm_g.reshape(1, -1), k_norm_g.reshape(1, -1), cos, sin)


def _mla_prep_body(p_ref, qg_ref, wq_ref, kvg_ref, wkv_ref, cos_ref, sin_ref, q_o, k_o, v_o):
    cos = cos_ref[...]
    sin = sin_ref[...]
    scale = (MLA_NOPE + MLA_ROPE) ** -0.5 * LOG2E
    cq = _rms(p_ref[:, 0:MLA_Q_RANK], qg_ref[...])
    q = jnp.dot(cq.astype(BF16), wq_ref[...], preferred_element_type=F32) * scale
    ckv = _rms(p_ref[:, MLA_Q_RANK:MLA_Q_RANK + MLA_KV_RANK], kvg_ref[...])
    kv = jnp.dot(ckv.astype(BF16), wkv_ref[...], preferred_element_type=F32)
    kr = _rope(p_ref[:, MLA_Q_RANK + MLA_KV_RANK:MLA_COLS_PAD], cos, sin, MLA_ROPE // 4).astype(k_o.dtype)
    for h in range(MLA_HEADS):
        lo = h * MLA_DK
        q_o[:, lo:lo + LANES] = q[:, lo:lo + LANES].astype(q_o.dtype)
        q_o[:, lo + LANES:lo + MLA_DK] = _rope(q[:, lo + LANES:lo + MLA_DK], cos, sin, MLA_ROPE // 4).astype(q_o.dtype)
        k_o[:, lo:lo + LANES] = kv[:, h * MLA_NOPE:(h + 1) * MLA_NOPE].astype(k_o.dtype)
        k_o[:, lo + LANES:lo + MLA_DK] = kr
    v_o[...] = kv[:, MLA_HEADS * MLA_NOPE:].astype(v_o.dtype)


def _mla_proj_body(h_ref, w_ref, qg_ref, wq_ref, kvg_ref, wkv_ref, cos_ref, sin_ref, q_o, k_o, v_o):
    for rows in _row_blocks(h_ref.shape[0]):
        p = jnp.dot(h_ref[rows, :], w_ref[...], preferred_element_type=F32)
        _mla_prep_body(p, qg_ref, wq_ref, kvg_ref, wkv_ref, cos_ref.at[rows, :], sin_ref.at[rows, :],
                       q_o.at[rows, :], k_o.at[rows, :], v_o.at[rows, :])


def _mla_proj(h, w, q_norm_g, wq, kv_norm_g, wkv, cos, sin):
    B, T, D = h.shape
    tt = PROJ_TILE
    tok = lambda w_: pl.BlockSpec((None, tt, w_), lambda b, i: (b, i, 0))
    full = lambda a: pl.BlockSpec(a.shape, lambda b, i: (0,) * a.ndim)
    tab = pl.BlockSpec((tt, LANES), lambda b, i: (i, 0))
    qg = q_norm_g.reshape(1, -1)
    kvg = kv_norm_g.reshape(1, -1)
    return pl.pallas_call(
        _mla_proj_body,
        grid=(B, T // tt),
        in_specs=[tok(D), full(w), full(qg), full(wq), full(kvg), full(wkv), tab, tab],
        out_specs=[tok(MLA_HEADS * MLA_DK), tok(MLA_HEADS * MLA_DK), tok(MLA_HEADS * MLA_V)],
        out_shape=[jax.ShapeDtypeStruct((B, T, MLA_HEADS * MLA_DK), BF16),
                   jax.ShapeDtypeStruct((B, T, MLA_HEADS * MLA_DK), BF16),
                   jax.ShapeDtypeStruct((B, T, MLA_HEADS * MLA_V), BF16)],
        compiler_params=_cp("parallel", "parallel"),
        name="mla_proj",
    )(h, w, qg, wq, kvg, wkv, cos, sin)


def _attn_body(q_ref, k_ref, v_ref, o_ref, *, hq, hkv, dk, dv, n_ctx_tiles, n_ctx, n_all):
    rep = hq // hkv
    tq = q_ref.shape[0]

    def run(nk):
        for g in range(hkv):
            kg = k_ref[0:nk, g * dk:(g + 1) * dk]
            vg = v_ref[0:nk, g * dv:(g + 1) * dv]
            v_aug = jnp.concatenate([vg, jnp.ones_like(vg)], axis=1)
            for h0 in range(g * rep, (g + 1) * rep, ATTN_STACK):
                heads = range(h0, min(h0 + ATTN_STACK, (g + 1) * rep))
                q = jnp.concatenate([q_ref[:, h * dk:(h + 1) * dk] for h in heads], axis=0)
                s = _dot_nt(q, kg)
                p = jnp.exp2(s - jnp.max(s, axis=-1, keepdims=True))
                o = jnp.dot(p.astype(BF16), v_aug, preferred_element_type=F32)
                o = o[:, 0:dv] / o[:, dv:dv + 1]
                for j, h in enumerate(heads):
                    o_ref[:, h * dv:(h + 1) * dv] = o[j * tq:(j + 1) * tq].astype(o_ref.dtype)

    @pl.when(pl.program_id(1) < n_ctx_tiles)
    def _():
        run(n_ctx)

    @pl.when(pl.program_id(1) >= n_ctx_tiles)
    def _():
        run(n_all)


def _attention(q, k, v, hq, hkv, dk, dv, n_ctx):
    B, T, _ = q.shape
    tq = TOK_TILE
    return pl.pallas_call(
        functools.partial(_attn_body, hq=hq, hkv=hkv, dk=dk, dv=dv, n_ctx_tiles=n_ctx // tq, n_ctx=n_ctx, n_all=T),
        grid=(B, T // tq),
        in_specs=[pl.BlockSpec((None, tq, hq * dk), lambda b, i: (b, i, 0)),
                  pl.BlockSpec((None, T, hkv * dk), lambda b, i: (b, 0, 0)),
                  pl.BlockSpec((None, T, hkv * dv), lambda b, i: (b, 0, 0))],
        out_specs=pl.BlockSpec((None, tq, hq * dv), lambda b, i: (b, i, 0)),
        out_shape=jax.ShapeDtypeStruct((B, T, hq * dv), BF16),
        compiler_params=_cp("parallel", "parallel"),
        name="attention",
    )(q, k, v)


def _rope_tables(T, n_ctx, n_rot):
    quarter = n_rot // 4
    t = jnp.arange(T - n_ctx)
    row = (t // GRID_W).astype(F32)
    col = (t % GRID_W).astype(F32)
    inv = ROPE_THETA ** (-jnp.arange(quarter, dtype=F32) / quarter)
    ar = row[:, None] * inv[None, :]
    ac = col[:, None] * inv[None, :]
    pad = LANES - n_rot
    cos = jnp.concatenate([jnp.cos(ar), jnp.cos(ar), jnp.cos(ac), jnp.cos(ac), jnp.ones((T - n_ctx, pad), F32)], axis=1)
    sin = jnp.concatenate([-jnp.sin(ar), jnp.sin(ar), -jnp.sin(ac), jnp.sin(ac), jnp.zeros((T - n_ctx, pad), F32)], axis=1)
    cos = jnp.concatenate([jnp.ones((n_ctx, LANES), F32), cos], axis=0)
    sin = jnp.concatenate([jnp.zeros((n_ctx, LANES), F32), sin], axis=0)
    return cos, sin


def _moe_body(te_ref, first_ref, slot_ref, nxt_ref, nv_ref, hs_ref, w1_hbm, w3_hbm, w2_hbm, y_ref,
              f1, f3, f2, c1, c3, c2, sem, *, layer):
    i = pl.program_id(0)

    def weight_copies(e, s):
        return (pltpu.make_async_copy(w1_hbm.at[layer, e], f1.at[s], sem.at[s, 0]),
                pltpu.make_async_copy(w3_hbm.at[layer, e], f3.at[s], sem.at[s, 1]),
                pltpu.make_async_copy(w2_hbm.at[layer, e], f2.at[s], sem.at[s, 2]))

    @pl.when(i == 0)
    def _():
        for cp in weight_copies(te_ref[0], 0):
            cp.start()

    @pl.when(first_ref[i] == 1)
    def _():
        s = slot_ref[i]
        for cp in weight_copies(te_ref[i], s):
            cp.wait()

        @pl.when(nxt_ref[i] >= 0)
        def _():
            for cp in weight_copies(nxt_ref[i], 1 - s):
                cp.start()

        c1[...] = f1[s].astype(BF16)
        c3[...] = f3[s].astype(BF16)
        c2[...] = f2[s].astype(BF16)

    @pl.when(i < nv_ref[0])
    def _():
        hs = hs_ref[...].astype(BF16)
        a = jnp.dot(hs, c1[...], preferred_element_type=F32)
        b = jnp.dot(hs, c3[...], preferred_element_type=F32)
        act = a * _sigmoid(a) * b
        y = jnp.dot(act.astype(BF16), c2[...], preferred_element_type=F32)
        y_ref[...] = y.astype(y_ref.dtype)

    @pl.when(i >= nv_ref[0])
    def _():
        y_ref[...] = jnp.zeros_like(y_ref)


def _moe_experts(tile_expert, run_first, run_slot, run_next, n_valid, hs, w1, w3, w2, layer):
    NP, D = hs.shape
    tm = MOE_TM
    DE = w1.shape[-1]
    grid_spec = pltpu.PrefetchScalarGridSpec(
        num_scalar_prefetch=5,
        grid=(NP // tm,),
        in_specs=[pl.BlockSpec((tm, D), lambda i, te, fi, sl, nx, nv: (jnp.minimum(i, nv[0] - 1), 0)),
                  pl.BlockSpec(memory_space=pl.ANY),
                  pl.BlockSpec(memory_space=pl.ANY),
                  pl.BlockSpec(memory_space=pl.ANY)],
        out_specs=pl.BlockSpec((tm, D), lambda i, *_: (i, 0)),
        scratch_shapes=[pltpu.VMEM((2, D, DE), F32), pltpu.VMEM((2, D, DE), F32), pltpu.VMEM((2, DE, D), F32),
                        pltpu.VMEM((D, DE), BF16), pltpu.VMEM((D, DE), BF16), pltpu.VMEM((DE, D), BF16),
                        pltpu.SemaphoreType.DMA((2, 3))],
    )
    return pl.pallas_call(
        functools.partial(_moe_body, layer=layer),
        grid_spec=grid_spec,
        out_shape=jax.ShapeDtypeStruct((NP, D), BF16),
        compiler_params=_cp("arbitrary"),
        name="moe_experts",
    )(tile_expert, run_first, run_slot, run_next, n_valid, hs, w1, w3, w2)


def _moe(h2, re, w1, w3, w2, layer):
    N = h2.shape[0]
    tm = MOE_TM
    n_tiles = (2 * N) // tm + N_EXPERTS
    e_flat = re.reshape(2 * N)
    onehot = (e_flat[:, None] == jnp.arange(N_EXPERTS, dtype=jnp.int32)[None, :]).astype(jnp.int32)
    csum = jnp.cumsum(onehot, axis=0)
    rank = jnp.sum(onehot * (csum - 1), axis=1)
    counts = csum[-1]
    ptiles = (counts + tm - 1) // tm
    tile_end = jnp.cumsum(ptiles)
    tile_start = tile_end - ptiles
    pos = tile_start[e_flat] * tm + rank
    n_valid = tile_end[-1:].astype(jnp.int32)
    tile_ids = jnp.arange(n_tiles, dtype=jnp.int32)
    tile_expert = jnp.minimum(jnp.sum((tile_end[None, :] <= tile_ids[:, None]).astype(jnp.int32), axis=1), N_EXPERTS - 1)
    experts = jnp.arange(N_EXPERTS, dtype=jnp.int32)
    owns = ptiles > 0
    run_no = jnp.cumsum(owns.astype(jnp.int32)) - 1
    later = jnp.where(owns[None, :] & (experts[None, :] > experts[:, None]), experts[None, :], N_EXPERTS)
    next_run = jnp.min(later, axis=1)
    next_run = jnp.where(next_run < N_EXPERTS, next_run, -1).astype(jnp.int32)
    run_first = ((tile_ids == tile_start[tile_expert]) & (tile_ids < n_valid[0])).astype(jnp.int32)
    run_slot = (run_no[tile_expert] % 2).astype(jnp.int32)
    run_next = next_run[tile_expert]
    spread = jnp.arange(n_tiles * tm, dtype=jnp.int32) % N
    src = spread.at[pos].set(jnp.arange(2 * N, dtype=jnp.int32) // 2)
    rows = lambda a, idx: a.at[idx].get(mode="promise_in_bounds")
    hs = rows(h2, src)
    y = _moe_experts(tile_expert, run_first, run_slot, run_next, n_valid, hs, w1, w3, w2, layer)
    pos2 = pos.reshape(N, 2)
    return rows(y, pos2[:, 0]), rows(y, pos2[:, 1])


def kernel(x, c, ctx, c_ctx, mod_w, mod_b, norm1_g, norm2_g, w_in, w_out, shift_prev, shift_next, decay_w0, decay_up, iclr_a0, iclr_up, gate_up, k_k, k_a, r_k, gn_g, gn_b, q_norm_g, k_norm_g, mla_q_norm_g, mla_w_uq, mla_kv_norm_g, mla_w_ukv, router_gw, router_gb, router_ew, router_eb, exp_w1, exp_w3, exp_w2, final_norm_g):
    B, S, D = x.shape
    C = ctx.shape[1]
    T = C + S
    depth = mod_w.shape[0]
    assert C == TOK_TILE and S % TOK_TILE == 0 and B <= CTX_ROW
    n_ctx_tiles = C // TOK_TILE

    cc = jnp.zeros((SUBLANES, D), F32).at[:B].set(c).at[CTX_ROW].set(c_ctx)
    mods_all = _modulation(cc, mod_w, mod_b).reshape(depth, SUBLANES * N_MOD, 1, D)
    X, h = _join_norm(ctx, x, norm1_g[0], mods_all[0], 0, 1)

    cos_g, sin_g = _rope_tables(T, C, GQA_HEAD)
    cos_m, sin_m = _rope_tables(T, C, MLA_ROPE)
    hid = jnp.arange(RWKV_W) // RWKV_HEAD
    head_sum = (hid[:, None] == hid[None, :]).astype(F32)

    for l in range(depth):
        mods = mods_all[l]
        w_r = w_in[l][:, :RWKV_COLS].astype(BF16)
        w_g = w_in[l][:, RWKV_COLS:RWKV_COLS + GQA_COLS].astype(BF16)
        w_m = jnp.pad(w_in[l][:, RWKV_COLS + GQA_COLS:], ((0, 0), (0, MLA_COLS_PAD - MLA_COLS))).astype(BF16)

        pr = _matmul(h.reshape(B * T, D), w_r).reshape(B, T, RWKV_COLS)

        wup_pad = jnp.pad(decay_up[l], ((0, 0), (0, ICLR_RANK), (0, 0)))
        aup_pad = jnp.pad(iclr_up[l], ((0, 0), (DECAY_RANK, 0), (0, 0)))
        r, v, kh, lw, b, kt, g, bonus = _rwkv_prep(pr, shift_prev[l], shift_next[l], decay_w0[l], wup_pad,
                                                  iclr_a0[l], aup_pad, gate_up[l], k_k[l], k_a[l], r_k[l], head_sum)
        yf, yb = _rwkv_scan(r, v, kh, lw, b, kt, C)
        o_r = _rwkv_readout(yf, yb, bonus, g, gn_g[l], gn_b[l], head_sum)

        q, k, vv = _gqa_proj(h, w_g, q_norm_g[l], k_norm_g[l], cos_g, sin_g)
        o_g = _attention(q, k, vv, GQA_Q_HEADS, GQA_KV_HEADS, GQA_HEAD, GQA_HEAD, C)

        wq = mla_w_uq[l].reshape(MLA_Q_RANK, MLA_HEADS, MLA_NOPE + MLA_ROPE)
        wq = jnp.pad(wq, ((0, 0), (0, 0), (0, MLA_DK - MLA_NOPE - MLA_ROPE))).reshape(MLA_Q_RANK, MLA_HEADS * MLA_DK)
        wkv = mla_w_ukv[l].reshape(MLA_KV_RANK, MLA_HEADS, MLA_NOPE + MLA_V)
        wkv = jnp.concatenate([wkv[:, :, :MLA_NOPE].reshape(MLA_KV_RANK, -1), wkv[:, :, MLA_NOPE:].reshape(MLA_KV_RANK, -1)], axis=1)
        qm, km, vm = _mla_proj(h, w_m, mla_q_norm_g[l], wq.astype(BF16), mla_kv_norm_g[l], wkv.astype(BF16), cos_m, sin_m)
        o_m = _attention(qm, km, vm, MLA_HEADS, MLA_HEADS, MLA_DK, MLA_V, C)

        last = l + 1 == depth
        wr = jnp.pad(jnp.concatenate([router_gw[l], router_ew[l]], axis=1), ((0, 0), (0, LANES - N_GROUPS - N_EXPERTS)))
        br = jnp.pad(jnp.concatenate([router_gb[l], router_eb[l]]), (0, LANES - N_GROUPS - N_EXPERTS)).reshape(1, LANES)
        X, h2, rw, re = _out_proj_router([o_r, o_g, o_m], w_out[l].astype(BF16), X, mods, C,
                                         TOK_TILE if last else OUT_TILE, last, norm2_g[l], wr, br)
        rows = X.shape[1]
        ctx_tiles = 0 if last else n_ctx_tiles

        y0, y1 = _moe(h2.reshape(B * rows, D), re.reshape(B * rows, LANES)[:, :2], exp_w1, exp_w3, exp_w2, l)
        y0 = y0.reshape(B, rows, D)
        y1 = y1.reshape(B, rows, D)
        if last:
            out = _moe_residual_final(X, y0, y1, rw, mods, 5, ctx_tiles, final_norm_g)
        else:
            X, h = _moe_residual_norm(X, y0, y1, rw, mods, 5, ctx_tiles, norm1_g[l + 1], mods_all[l + 1])
    return out
```

```python
import functools
import math

import jax
import jax.numpy as jnp
import numpy as np
from jax import lax
from jax.experimental import pallas as pl
from jax.experimental.pallas import tpu as pltpu

F32 = jnp.float32
BF16 = jnp.bfloat16

V7X_VMEM_BYTES = 64 * 1024 * 1024
VMEM_LIMIT = V7X_VMEM_BYTES - 8 * 1024 * 1024
LANES = 128
SUBLANES = 8

GRID_W = 64
ROPE_THETA = 10000.0
NORM_EPS = 1e-6
GN_EPS = 64e-5
DECAY_SCALE = math.exp(-0.5)
LOG2E = math.log2(math.e)

RWKV_HEAD = 64
RWKV_W = 512
DECAY_RANK = 64
ICLR_RANK = 64
GATE_RANK = 128
RWKV_COLS = 3 * RWKV_W + DECAY_RANK + ICLR_RANK + GATE_RANK
LOWRANK_OFF = 3 * RWKV_W
GATE_OFF = LOWRANK_OFF + DECAY_RANK + ICLR_RANK
CHUNK = 64
SCAN_CHUNKS = 2

GQA_HEAD = 128
GQA_Q_HEADS = 8
GQA_KV_HEADS = 2
GQA_Q_COLS = GQA_Q_HEADS * GQA_HEAD
GQA_KV_COLS = GQA_KV_HEADS * GQA_HEAD
GQA_COLS = GQA_Q_COLS + 2 * GQA_KV_COLS

MLA_HEADS = 4
MLA_NOPE = 128
MLA_ROPE = 64
MLA_V = 128
MLA_Q_RANK = 384
MLA_KV_RANK = 256
MLA_COLS = MLA_Q_RANK + MLA_KV_RANK + MLA_ROPE
MLA_COLS_PAD = 768
MLA_DK = 2 * LANES

N_GROUPS = 4
EXPERTS_PER_GROUP = 8
N_EXPERTS = 32
D_EXPERT = 256
MOE_TM = 256

TOK_TILE = 256
PROJ_TILE = 768
OUT_TILE = 384
OUT_SUBBLOCKS = 1
ATTN_STACK = 1
N_MOD = 6
CTX_ROW = 4


def _cp(*sem):
    return pltpu.CompilerParams(dimension_semantics=sem, vmem_limit_bytes=VMEM_LIMIT)


def _sigmoid(x):
    return 1.0 / (1.0 + jnp.exp(-x))


def _bf16_terms(x, n):
    terms = []
    for _ in range(n):
        t = x.astype(BF16)
        terms.append(t)
        x = x - t.astype(F32)
    return terms


def _dot_terms(x, w, nx, nw):
    xs = _bf16_terms(x, nx)
    ws = _bf16_terms(w, nw)
    acc = None
    for i in range(nx):
        for j in range(nw):
            if i + j < max(nx, nw):
                p = jnp.dot(xs[i], ws[j], preferred_element_type=F32)
                acc = p if acc is None else acc + p
    return acc


def _mod_body(c_ref, w_ref, b_ref, o_ref):
    c = c_ref[...]
    s = c * _sigmoid(c)
    o_ref[...] = _dot_terms(s, w_ref[...], 2, 1) + b_ref[...]


def _modulation(cc, mod_w, mod_b):
    L, D, N = mod_w.shape
    tn = 1024
    return pl.pallas_call(
        _mod_body,
        grid=(L, N // tn),
        in_specs=[pl.BlockSpec((SUBLANES, D), lambda l, j: (0, 0)),
                  pl.BlockSpec((None, D, tn), lambda l, j: (l, 0, j)),
                  pl.BlockSpec((None, 1, tn), lambda l, j: (l, 0, j))],
        out_specs=pl.BlockSpec((None, SUBLANES, tn), lambda l, j: (l, 0, j)),
        out_shape=jax.ShapeDtypeStruct((L, SUBLANES, N), F32),
        compiler_params=_cp("parallel", "parallel"),
        name="modulation",
    )(cc, mod_w, mod_b.reshape(L, 1, N))


def _mod_spec(which, n_ctx_tiles, D):
    return pl.BlockSpec((None, 1, D), lambda b, i: (jnp.where(i < n_ctx_tiles, CTX_ROW, b) * N_MOD + which, 0, 0))


def _rms(x, g):
    return x * lax.rsqrt(jnp.mean(x * x, axis=-1, keepdims=True) + NORM_EPS) * g


def _join_norm_body(ctx_ref, x_ref, g_ref, sh_ref, sc_ref, x_o, h_o, *, n_ctx_tiles):
    i = pl.program_id(1)

    def emit(src_ref):
        x = src_ref[...]
        x_o[...] = x
        h_o[...] = (_rms(x, g_ref[...]) * (1.0 + sc_ref[...]) + sh_ref[...]).astype(h_o.dtype)

    @pl.when(i < n_ctx_tiles)
    def _():
        emit(ctx_ref)

    @pl.when(i >= n_ctx_tiles)
    def _():
        emit(x_ref)


def _route(logits):
    lane = lax.broadcasted_iota(jnp.int32, logits.shape, 1)
    lane_f = lane.astype(F32)
    neg = jnp.float32(-1e30)
    far = jnp.float32(1e9)
    first_at = lambda hit: jnp.min(jnp.where(hit, lane_f, far), axis=-1, keepdims=True).astype(jnp.int32)
    gl = jnp.where(lane < N_GROUPS, logits, neg)
    gmax = jnp.max(gl, axis=-1, keepdims=True)
    gidx = first_at(gl == gmax)
    p_sel = 1.0 / jnp.sum(jnp.exp(gl - gmax), axis=-1, keepdims=True)
    lo = N_GROUPS + gidx * EXPERTS_PER_GROUP
    el = jnp.where((lane >= lo) & (lane < lo + EXPERTS_PER_GROUP), logits, neg)
    m1 = jnp.max(el, axis=-1, keepdims=True)
    i1 = first_at(el == m1)
    el2 = jnp.where(lane == i1, neg, el)
    m2 = jnp.max(el2, axis=-1, keepdims=True)
    i2 = first_at(el2 == m2)
    t = jnp.exp(m2 - m1)
    w1 = p_sel / (1.0 + t)
    w2 = p_sel * t / (1.0 + t)
    rw = jnp.where(lane == 0, w1, jnp.where(lane == 1, w2, 0.0))
    re = jnp.where(lane == 0, i1 - N_GROUPS, jnp.where(lane == 1, i2 - N_GROUPS, 0))
    return rw, re


def _join_norm(ctx, x, g, mods, shift_i, scale_i):
    B, C, D = ctx.shape
    tt = TOK_TILE
    n_ctx_tiles = C // tt
    T = C + x.shape[1]
    tok = pl.BlockSpec((None, tt, D), lambda b, i: (b, i, 0))
    return pl.pallas_call(
        functools.partial(_join_norm_body, n_ctx_tiles=n_ctx_tiles),
        grid=(B, T // tt),
        in_specs=[pl.BlockSpec((None, tt, D), lambda b, i: (b, jnp.minimum(i, n_ctx_tiles - 1), 0)),
                  pl.BlockSpec((None, tt, D), lambda b, i: (b, jnp.maximum(i - n_ctx_tiles, 0), 0)),
                  pl.BlockSpec((1, D), lambda b, i: (0, 0)),
                  _mod_spec(shift_i, n_ctx_tiles, D),
                  _mod_spec(scale_i, n_ctx_tiles, D)],
        out_specs=[tok, tok],
        out_shape=[jax.ShapeDtypeStruct((B, T, D), F32), jax.ShapeDtypeStruct((B, T, D), BF16)],
        compiler_params=_cp("parallel", "parallel"),
        name="join_norm",
    )(ctx, x, g.reshape(1, D), mods, mods)


def _mm_body(a_ref, w_ref, o_ref):
    o_ref[...] = jnp.dot(a_ref[...], w_ref[...], preferred_element_type=F32).astype(o_ref.dtype)


def _matmul(a, w, tm=512):
    M, K = a.shape
    N = w.shape[1]
    return pl.pallas_call(
        _mm_body,
        grid=(M // tm,),
        in_specs=[pl.BlockSpec((tm, K), lambda i: (i, 0)),
                  pl.BlockSpec((K, N), lambda i: (0, 0))],
        out_specs=pl.BlockSpec((tm, N), lambda i: (i, 0)),
        out_shape=jax.ShapeDtypeStruct((M, N), F32),
        compiler_params=_cp("parallel"),
        name="token_matmul",
    )(a, w)


def _out_proj_router_body(*refs, n_parts, n_ctx, tm, tile0):
    a_refs = refs[:n_parts]
    w_ref, x_ref, ml_ref, mc_ref, g_ref, wr_ref, br_ref, x_o, h_o, rw_o, re_o = refs[n_parts:]
    sub = tm // OUT_SUBBLOCKS
    for r0 in range(0, tm, sub):
        rows = slice(r0, r0 + sub)
        acc = None
        k0 = 0
        for a_ref in a_refs:
            k1 = k0 + a_ref.shape[-1]
            part = jnp.dot(a_ref[rows, :], w_ref[k0:k1, :], preferred_element_type=F32)
            acc = part if acc is None else acc + part
            k0 = k1
        row = (pl.program_id(1) + tile0) * tm + r0 + lax.broadcasted_iota(jnp.int32, (sub, 1), 0)
        is_ctx = row < n_ctx
        mod = lambda which: jnp.where(is_ctx, mc_ref[which], ml_ref[which])
        x = x_ref[rows, :] + mod(2) * acc
        x_o[rows, :] = x
        h = _rms(x, g_ref[...]) * (1.0 + mod(4)) + mod(3)
        h_o[rows, :] = h
        h_hi, h_lo = _bf16_terms(h, 2)
        w_hi, w_lo = _bf16_terms(wr_ref[...], 2)
        both = jnp.dot(h_hi, jnp.concatenate([w_hi, w_lo], axis=1), preferred_element_type=F32)
        logits = (both[:, 0:LANES] + both[:, LANES:2 * LANES]
                  + jnp.dot(h_lo, w_hi, preferred_element_type=F32) + br_ref[...])
        rw, re = _route(logits)
        rw_o[rows, :] = rw
        re_o[rows, :] = re


def _out_proj_router(parts, w, X, mods, n_ctx, tm, latents_only, g2, wr, br):
    B, T, D = X.shape
    tile0 = n_ctx // tm if latents_only else 0
    assert T % tm == 0 and (n_ctx % tm == 0 or not latents_only)
    assert sum(p.shape[-1] for p in parts) == w.shape[0]
    rows_out = T - tile0 * tm
    mods6 = mods.reshape(SUBLANES, N_MOD, 1, D)
    tok_in = lambda wd: pl.BlockSpec((None, tm, wd), lambda b, i: (b, i + tile0, 0))
    tok_out = lambda wd: pl.BlockSpec((None, tm, wd), lambda b, i: (b, i, 0))
    shape = lambda wd, dt: jax.ShapeDtypeStruct((B, rows_out, wd), dt)
    return pl.pallas_call(
        functools.partial(_out_proj_router_body, n_parts=len(parts), n_ctx=n_ctx, tm=tm, tile0=tile0),
        grid=(B, T // tm - tile0),
        in_specs=[tok_in(p.shape[-1]) for p in parts] + [
                  pl.BlockSpec(w.shape, lambda b, i: (0, 0)),
                  tok_in(D),
                  pl.BlockSpec((None, N_MOD, 1, D), lambda b, i: (b, 0, 0, 0)),
                  pl.BlockSpec((None, N_MOD, 1, D), lambda b, i: (CTX_ROW, 0, 0, 0)),
                  pl.BlockSpec((1, D), lambda b, i: (0, 0)),
                  pl.BlockSpec((D, LANES), lambda b, i: (0, 0)),
                  pl.BlockSpec((1, LANES), lambda b, i: (0, 0))],
        out_specs=[tok_out(D), tok_out(D), tok_out(LANES), tok_out(LANES)],
        out_shape=[shape(D, F32), shape(D, F32), shape(LANES, F32), shape(LANES, jnp.int32)],
        compiler_params=_cp("parallel", "parallel"),
        name="out_proj_router",
    )(*parts, w, X, mods6, mods6, g2.reshape(1, D), wr, br)


def _moe_residual(x_ref, y0_ref, y1_ref, rw_ref, gate_ref):
    rw = rw_ref[...]
    moe = rw[:, 0:1] * y0_ref[...].astype(F32) + rw[:, 1:2] * y1_ref[...].astype(F32)
    return x_ref[...] + gate_ref[...] * moe


def _moe_residual_norm_body(x_ref, y0_ref, y1_ref, rw_ref, gate_ref, g_ref, sh_ref, sc_ref, x_o, h_o):
    x = _moe_residual(x_ref, y0_ref, y1_ref, rw_ref, gate_ref)
    x_o[...] = x
    h_o[...] = (_rms(x, g_ref[...]) * (1.0 + sc_ref[...]) + sh_ref[...]).astype(h_o.dtype)


def _moe_residual_final_body(x_ref, y0_ref, y1_ref, rw_ref, gate_ref, g_ref, o_ref):
    o_ref[...] = _rms(_moe_residual(x_ref, y0_ref, y1_ref, rw_ref, gate_ref), g_ref[...])


def _moe_residual_norm(X, Y0, Y1, rw, mods, gate_i, n_ctx_tiles, g_next, mods_next):
    B, T, D = X.shape
    tt = TOK_TILE
    blk = pl.BlockSpec((None, tt, D), lambda b, i: (b, i, 0))
    return pl.pallas_call(
        _moe_residual_norm_body,
        grid=(B, T // tt),
        in_specs=[blk, blk, blk, pl.BlockSpec((None, tt, LANES), lambda b, i: (b, i, 0)),
                  _mod_spec(gate_i, n_ctx_tiles, D), pl.BlockSpec((1, D), lambda b, i: (0, 0)),
                  _mod_spec(0, n_ctx_tiles, D), _mod_spec(1, n_ctx_tiles, D)],
        out_specs=[blk, blk],
        out_shape=[jax.ShapeDtypeStruct((B, T, D), F32), jax.ShapeDtypeStruct((B, T, D), BF16)],
        compiler_params=_cp("parallel", "parallel"),
        name="moe_residual_norm",
    )(X, Y0, Y1, rw, mods, g_next.reshape(1, D), mods_next, mods_next)


def _moe_residual_final(X, Y0, Y1, rw, mods, gate_i, n_ctx_tiles, g_final):
    B, T, D = X.shape
    tt = TOK_TILE
    S = T - n_ctx_tiles * tt
    lat = lambda w: pl.BlockSpec((None, tt, w), lambda b, i: (b, i + n_ctx_tiles, 0))
    return pl.pallas_call(
        _moe_residual_final_body,
        grid=(B, S // tt),
        in_specs=[lat(D), lat(D), lat(D), lat(LANES),
                  pl.BlockSpec((None, 1, D), lambda b, i: (b * N_MOD + gate_i, 0, 0)),
                  pl.BlockSpec((1, D), lambda b, i: (0, 0))],
        out_specs=pl.BlockSpec((None, tt, D), lambda b, i: (b, i, 0)),
        out_shape=jax.ShapeDtypeStruct((B, S, D), F32),
        compiler_params=_cp("parallel", "parallel"),
        name="moe_residual_final",
    )(X, Y0, Y1, rw, mods, g_final.reshape(1, D))


def _rwkv_prep_body(p_ref, pv_ref, nx_ref, mup_ref, mun_ref, w0_ref, wup_ref, a0_ref, aup_ref, gup_ref,
                    kk_ref, ka_ref, rk_ref, e_ref,
                    r_o, v_o, kh_o, lw_o, b_o, kt_o, g_o, bon_o, *, n_tiles, tt):
    i = pl.program_id(1)
    p = p_ref[...]
    seq_first = i <= 1
    seq_last = (i == 0) | (i == n_tiles - 1)
    prow = jnp.where(seq_first, 0.0, pv_ref[SUBLANES - 1:SUBLANES, :])
    nrow = jnp.where(seq_last, 0.0, nx_ref[0:1, :])
    rid = lax.broadcasted_iota(jnp.int32, (tt, 1), 0)
    prev = jnp.where(rid == 0, prow, pltpu.roll(p, 1, 0))
    nxt = jnp.where(rid == tt - 1, nrow, pltpu.roll(p, tt - 1, 0))
    z = p + mup_ref[...] * (prev - p) + mun_ref[...] * (nxt - p)

    W = RWKV_W
    r = z[:, 0:W]
    k = z[:, W:2 * W]
    v = z[:, 2 * W:3 * W]
    lowrank = z[:, LOWRANK_OFF:LOWRANK_OFF + LANES]
    gd = z[:, GATE_OFF:GATE_OFF + GATE_RANK]
    head_sum = e_ref[...]

    kap = k * kk_ref[...]
    ss = _dot_terms(kap * kap, head_sum, 2, 1)
    khat = kap * lax.rsqrt(ss + 1e-12)
    wd_t = jnp.tanh(lowrank)
    g_o[...] = _dot_terms(_sigmoid(gd), gup_ref[...], 1, 1)
    r_o[...] = r.astype(r_o.dtype)
    v_o[...] = v.astype(v_o.dtype)
    kh_o[...] = khat.astype(kh_o.dtype)
    kt_sum = None
    for d in range(2):
        dec = _dot_terms(wd_t, wup_ref[d], 2, 2)
        lw_o[d] = -DECAY_SCALE * _sigmoid(w0_ref[d:d + 1, :] + dec)
        a = _sigmoid(a0_ref[d:d + 1, :] + _dot_terms(lowrank, aup_ref[d], 1, 1))
        kt = k * (1.0 + (a - 1.0) * ka_ref[...])
        kt_o[d] = kt.astype(kt_o.dtype)
        b_o[d] = (a * khat).astype(b_o.dtype)
        kt_sum = kt if kt_sum is None else kt_sum + kt
    bsum = _dot_terms(r * kt_sum * rk_ref[...], head_sum, 2, 1)
    bon_o[...] = bsum * v


def _rwkv_prep(pr, mu_prev, mu_next, w0, wup_pad, a0, aup_pad, g_up, k_k, k_a, r_k, head_sum):
    B, T, _ = pr.shape
    tt = TOK_TILE
    W = RWKV_W
    n_tiles = T // tt
    n8 = tt // SUBLANES
    row = lambda v: v.reshape(1, -1)
    full = lambda a: pl.BlockSpec(a.shape, lambda b, i: (0,) * a.ndim)
    tok = pl.BlockSpec((None, tt, W), lambda b, i: (b, i, 0))
    tok2 = pl.BlockSpec((None, 2, tt, W), lambda b, i: (b, 0, i, 0))
    s1 = jax.ShapeDtypeStruct((B, T, W), F32)
    s2 = jax.ShapeDtypeStruct((B, 2, T, W), F32)
    h1 = jax.ShapeDtypeStruct((B, T, W), BF16)
    h2 = jax.ShapeDtypeStruct((B, 2, T, W), BF16)
    consts = [row(mu_prev), row(mu_next), w0, wup_pad, a0, aup_pad, g_up, row(k_k), row(k_a), row(r_k), head_sum]
    return pl.pallas_call(
        functools.partial(_rwkv_prep_body, n_tiles=n_tiles, tt=tt),
        grid=(B, n_tiles),
        in_specs=[pl.BlockSpec((None, tt, RWKV_COLS), lambda b, i: (b, i, 0)),
                  pl.BlockSpec((None, SUBLANES, RWKV_COLS), lambda b, i: (b, jnp.maximum(i * n8 - 1, 0), 0)),
                  pl.BlockSpec((None, SUBLANES, RWKV_COLS), lambda b, i: (b, jnp.minimum((i + 1) * n8, T // SUBLANES - 1), 0)),
                  ] + [full(a) for a in consts],
        out_specs=[tok, tok, tok, tok2, tok2, tok2, tok, tok],
        out_shape=[h1, h1, h1, s2, h2, h2, s1, s1],
        compiler_params=_cp("parallel", "parallel"),
        name="rwkv_prep",
    )(pr, pr, pr, *consts)


def _stack_heads(x):
    lane = lax.broadcasted_iota(jnp.int32, x.shape, 1)
    first = lane < RWKV_HEAD
    return jnp.concatenate([jnp.where(first, x, 0.0), jnp.where(first, 0.0, x)], axis=0)


def _dot(a, b):
    return jnp.dot(a.astype(BF16), b.astype(BF16), preferred_element_type=F32)


def _dot_nt(a, b):
    return lax.dot_general(a.astype(BF16), b.astype(BF16), (((1,), (1,)), ((), ())), preferred_element_type=F32)


def _dot_tn(a, b):
    return jnp.dot(a.T.astype(BF16), b.astype(BF16), preferred_element_type=F32)


def _chunk_operands(r, v, kh, lw, b, kt, reverse):
    L = CHUNK
    ti = lax.broadcasted_iota(jnp.int32, (L, L), 0)
    tj = lax.broadcasted_iota(jnp.int32, (L, L), 1)
    tri = jnp.where((ti <= tj) if reverse else (ti >= tj), 1.0, 0.0)
    lam = _dot_terms(tri, lw, 1, 3)
    tot = lam[0:1, :] if reverse else lam[L - 1:L, :]
    e_n = jnp.exp(-lam)
    e_g = jnp.exp(tot - lam)
    full = dict(A=kh * jnp.exp(lam - lw), R=r * jnp.exp(lam), Kn=kt * e_n, Bn=b * e_n, Kg=kt * e_g, Bg=b * e_g, V=v)
    e_tot = jnp.exp(tot)
    pairs = []
    for p in range(RWKV_W // LANES):
        sl = slice(p * LANES, (p + 1) * LANES)
        ops = {k: a[:, sl] for k, a in full.items()}
        ops["e_tot"] = e_tot[:, sl]
        ops["reverse"] = reverse
        pairs.append(ops)
    return pairs


def _chunk_masks(reverse):
    L = CHUNK
    t = lax.broadcasted_iota(jnp.int32, (L, 2 * L), 0)
    i = lax.broadcasted_iota(jnp.int32, (L, 2 * L), 1) & (L - 1)
    before = (i > t) if reverse else (i < t)
    return before, before | (i == t), i == t


def _chunks_prepare(chains):
    L = CHUNK
    P2 = 2 * L
    n = len(chains)
    masks = {rev: _chunk_masks(rev) for rev in {c["reverse"] for c in chains}}
    strict = [masks[c["reverse"]][0] for c in chains]
    incl = [masks[c["reverse"]][1] for c in chains]
    eye = masks[chains[0]["reverse"]][2]
    bi = lax.broadcasted_iota(jnp.int32, (P2, P2), 0)
    bj = lax.broadcasted_iota(jnp.int32, (P2, P2), 1)
    same_head = (bi >= L) == (bj >= L)
    eye2 = bi == bj
    stack = _stack_heads
    cat0 = lambda *xs: jnp.concatenate(xs, axis=0)
    cat1 = lambda *xs: jnp.concatenate(xs, axis=1)

    big = [_dot_nt(cat0(c["A"], c["R"]), cat0(stack(c["Bn"]), stack(c["Kn"]))) for c in chains]
    Mb = [jnp.where(strict[i], big[i][0:L, 0:P2], 0.0) for i in range(n)]
    Mkv = [jnp.where(strict[i], big[i][0:L, P2:2 * P2], 0.0) for i in range(n)]
    Pb = [jnp.where(incl[i], big[i][L:P2, 0:P2], 0.0) for i in range(n)]
    Pkv = [jnp.where(incl[i], big[i][L:P2, P2:2 * P2], 0.0) for i in range(n)]

    Pw = [-m for m in Mb]
    Tm = [jnp.where(eye, 1.0, 0.0) + p for p in Pw]
    Pw = [_dot(p, stack(p)) for p in Pw]
    for _ in range(int(math.log2(L)) - 2):
        PT = [_dot(cat0(p, t), stack(p)) for p, t in zip(Pw, Tm)]
        Tm = [t + pt[L:P2] for t, pt in zip(Tm, PT)]
        Pw = [pt[0:L] for pt in PT]
    Tm = [t + _dot(t, stack(p)) for t, p in zip(Tm, Pw)]

    Vs = [stack(c["V"]) for c in chains]
    MPV = [_dot(cat0(Mkv[i], Pkv[i]), Vs[i]) for i in range(n)]
    TAM = [_dot(Tm[i], cat1(stack(chains[i]["A"]), stack(MPV[i][0:L]))) for i in range(n)]
    PB = [_dot(Pb[i], cat1(stack(TAM[i][:, 0:P2]), stack(TAM[i][:, P2:2 * P2]))) for i in range(n)]
    BG = [_dot_tn(chains[i]["Bg"], TAM[i]) for i in range(n)]
    KV = [_dot_tn(chains[i]["Kg"], chains[i]["V"]) for i in range(n)]
    prepared = []
    for i in range(n):
        RA = chains[i]["R"] - PB[i][:, 0:P2]
        G2 = jnp.where(eye2, chains[i]["e_tot"], 0.0) - jnp.where(same_head, BG[i][:, 0:P2], 0.0)
        H2 = jnp.where(same_head, KV[i] - BG[i][:, P2:2 * P2], 0.0)
        prepared.append((cat0(RA, G2), MPV[i][L:P2] - PB[i][:, P2:2 * P2], H2))
    return prepared


def _chunk_advance(prepared, state):
    lhs, y0, h2 = prepared
    out = _dot(lhs, state)
    return out[0:CHUNK] + y0, out[CHUNK:] + h2


def _rwkv_scan_body(rf_ref, vf_ref, khf_ref, rb_ref, vb_ref, khb_ref, lwf_ref, bf_ref, ktf_ref, lwb_ref, bb_ref, ktb_ref,
                    yf_ref, yb_ref, s_ref):
    @pl.when(pl.program_id(1) == 0)
    def _():
        s_ref[...] = jnp.zeros_like(s_ref)

    n_pairs = RWKV_W // LANES
    fwd_refs = (rf_ref, vf_ref, khf_ref, lwf_ref, bf_ref, ktf_ref)
    bwd_refs = (rb_ref, vb_ref, khb_ref, lwb_ref, bb_ref, ktb_ref)
    slots = []
    for c in range(SCAN_CHUNKS):
        lo_f = c * CHUNK
        lo_b = (SCAN_CHUNKS - 1 - c) * CHUNK
        slots.append((_chunk_operands(*(ref[lo_f:lo_f + CHUNK, :] for ref in fwd_refs), False), lo_f,
                      _chunk_operands(*(ref[lo_b:lo_b + CHUNK, :] for ref in bwd_refs), True), lo_b))
    prepared = _chunks_prepare([ch for s in slots for ch in s[0] + s[2]])
    states = [s_ref[i] for i in range(2 * n_pairs)]
    for c, (_, lo_f, _, lo_b) in enumerate(slots):
        ys = []
        for i in range(2 * n_pairs):
            y, states[i] = _chunk_advance(prepared[c * 2 * n_pairs + i], states[i])
            ys.append(y)
        yf_ref[lo_f:lo_f + CHUNK, :] = jnp.concatenate(ys[:n_pairs], axis=1)
        yb_ref[lo_b:lo_b + CHUNK, :] = jnp.concatenate(ys[n_pairs:], axis=1)
    for i in range(2 * n_pairs):
        s_ref[i] = states[i]


def _rwkv_scan(r, v, kh, lw, b, kt, n_ctx):
    B, T, W = r.shape
    blk = SCAN_CHUNKS * CHUNK
    assert T % blk == 0 and n_ctx % blk == 0
    nc = T // blk
    ncc = n_ctx // blk
    rev = lambda j: jnp.where(j < ncc, ncc - 1 - j, nc + ncc - 1 - j)
    fwd1 = pl.BlockSpec((None, blk, W), lambda bb, j: (bb, j, 0))
    bwd1 = pl.BlockSpec((None, blk, W), lambda bb, j: (bb, rev(j), 0))
    fwd2 = pl.BlockSpec((None, None, blk, W), lambda bb, j: (bb, 0, j, 0))
    bwd2 = pl.BlockSpec((None, None, blk, W), lambda bb, j: (bb, 1, rev(j), 0))
    out = jax.ShapeDtypeStruct((B, T, W), F32)
    return pl.pallas_call(
        _rwkv_scan_body,
        grid=(B, nc),
        in_specs=[fwd1, fwd1, fwd1, bwd1, bwd1, bwd1, fwd2, fwd2, fwd2, bwd2, bwd2, bwd2],
        out_specs=[fwd1, bwd1],
        out_shape=[out, out],
        scratch_shapes=[pltpu.VMEM((2 * W // LANES, LANES, LANES), F32)],
        compiler_params=_cp("parallel", "arbitrary"),
        name="rwkv_scan",
    )(r, v, kh, r, v, kh, lw, b, kt, lw, b, kt)


def _rwkv_readout_body(yf_ref, yb_ref, bon_ref, g_ref, gng_ref, gnb_ref, e_ref, o_ref):
    y = yf_ref[...] + yb_ref[...]
    head_mean = e_ref[...] * (1.0 / RWKV_HEAD)
    mu = _dot_terms(y, head_mean, 2, 1)
    yc = y - mu
    var = _dot_terms(yc * yc, head_mean, 2, 1)
    yn = yc * lax.rsqrt(var + GN_EPS) * gng_ref[...] + gnb_ref[...]
    o_ref[...] = ((yn + bon_ref[...]) * g_ref[...]).astype(o_ref.dtype)


def _rwkv_readout(yf, yb, bonus, g, gn_g, gn_b, head_sum):
    B, T, W = yf.shape
    tt = TOK_TILE
    tok = pl.BlockSpec((None, tt, W), lambda b, i: (b, i, 0))
    row = pl.BlockSpec((1, W), lambda b, i: (0, 0))
    return pl.pallas_call(
        _rwkv_readout_body,
        grid=(B, T // tt),
        in_specs=[tok, tok, tok, tok, row, row, pl.BlockSpec((W, W), lambda b, i: (0, 0))],
        out_specs=tok,
        out_shape=jax.ShapeDtypeStruct((B, T, W), BF16),
        compiler_params=_cp("parallel", "parallel"),
        name="rwkv_readout",
    )(yf, yb, bonus, g, gn_g.reshape(1, W), gn_b.reshape(1, W), head_sum)


def _rope(y, cos, sin_signed, quarter):
    lane = lax.broadcasted_iota(jnp.int32, y.shape, 1)
    first = (lane & (2 * quarter - 1)) < quarter
    partner = jnp.where(first, pltpu.roll(y, LANES - quarter, 1), pltpu.roll(y, quarter, 1))
    return y * cos + partner * sin_signed


def _gqa_prep_body(p_ref, qg_ref, kg_ref, cos_ref, sin_ref, q_o, k_o, v_o):
    cos = cos_ref[...]
    sin = sin_ref[...]
    scale = GQA_HEAD ** -0.5 * LOG2E
    for h in range(GQA_Q_HEADS):
        sl = slice(h * GQA_HEAD, (h + 1) * GQA_HEAD)
        q = _rms(p_ref[:, sl], qg_ref[...])
        q_o[:, sl] = (_rope(q, cos, sin, GQA_HEAD // 4) * scale).astype(q_o.dtype)
    for h in range(GQA_KV_HEADS):
        sl = slice(h * GQA_HEAD, (h + 1) * GQA_HEAD)
        k = _rms(p_ref[:, GQA_Q_COLS + h * GQA_HEAD:GQA_Q_COLS + (h + 1) * GQA_HEAD], kg_ref[...])
        k_o[:, sl] = _rope(k, cos, sin, GQA_HEAD // 4).astype(k_o.dtype)
    v_o[...] = p_ref[:, GQA_Q_COLS + GQA_KV_COLS:GQA_COLS].astype(v_o.dtype)


def _row_blocks(n_rows):
    return [slice(r, r + TOK_TILE) for r in range(0, n_rows, TOK_TILE)]


def _gqa_proj_body(h_ref, w_ref, qg_ref, kg_ref, cos_ref, sin_ref, q_o, k_o, v_o):
    for rows in _row_blocks(h_ref.shape[0]):
        p = jnp.dot(h_ref[rows, :], w_ref[...], preferred_element_type=F32)
        _gqa_prep_body(p, qg_ref, kg_ref, cos_ref.at[rows, :], sin_ref.at[rows, :],
                       q_o.at[rows, :], k_o.at[rows, :], v_o.at[rows, :])


def _gqa_proj(h, w, q_norm_g, k_norm_g, cos, sin):
    B, T, D = h.shape
    tt = PROJ_TILE
    tok = lambda w_: pl.BlockSpec((None, tt, w_), lambda b, i: (b, i, 0))
    row = pl.BlockSpec((1, GQA_HEAD), lambda b, i: (0, 0))
    tab = pl.BlockSpec((tt, LANES), lambda b, i: (i, 0))
    return pl.pallas_call(
        _gqa_proj_body,
        grid=(B, T // tt),
        in_specs=[tok(D), pl.BlockSpec(w.shape, lambda b, i: (0, 0)), row, row, tab, tab],
        out_specs=[tok(GQA_Q_COLS), tok(GQA_KV_COLS), tok(GQA_KV_COLS)],
        out_shape=[jax.ShapeDtypeStruct((B, T, GQA_Q_COLS), BF16),
                   jax.ShapeDtypeStruct((B, T, GQA_KV_COLS), BF16),
                   jax.ShapeDtypeStruct((B, T, GQA_KV_COLS), BF16)],
        compiler_params=_cp("parallel", "parallel"),
        name="gqa_proj",
    )(h, w, q_norm_g.reshape(1, -1), k_norm_g.reshape(1, -1), cos, sin)


def _mla_prep_body(p_ref, qg_ref, wq_ref, kvg_ref, wkv_ref, cos_ref, sin_ref, q_o, k_o, v_o):
    cos = cos_ref[...]
    sin = sin_ref[...]
    scale = (MLA_NOPE + MLA_ROPE) ** -0.5 * LOG2E
    cq = _rms(p_ref[:, 0:MLA_Q_RANK], qg_ref[...])
    q = jnp.dot(cq.astype(BF16), wq_ref[...], preferred_element_type=F32) * scale
    ckv = _rms(p_ref[:, MLA_Q_RANK:MLA_Q_RANK + MLA_KV_RANK], kvg_ref[...])
    kv = jnp.dot(ckv.astype(BF16), wkv_ref[...], preferred_element_type=F32)
    kr = _rope(p_ref[:, MLA_Q_RANK + MLA_KV_RANK:MLA_COLS_PAD], cos, sin, MLA_ROPE // 4).astype(k_o.dtype)
    for h in range(MLA_HEADS):
        lo = h * MLA_DK
        q_o[:, lo:lo + LANES] = q[:, lo:lo + LANES].astype(q_o.dtype)
        q_o[:, lo + LANES:lo + MLA_DK] = _rope(q[:, lo + LANES:lo + MLA_DK], cos, sin, MLA_ROPE // 4).astype(q_o.dtype)
        k_o[:, lo:lo + LANES] = kv[:, h * MLA_NOPE:(h + 1) * MLA_NOPE].astype(k_o.dtype)
        k_o[:, lo + LANES:lo + MLA_DK] = kr
    v_o[...] = kv[:, MLA_HEADS * MLA_NOPE:].astype(v_o.dtype)


def _mla_proj_body(h_ref, w_ref, qg_ref, wq_ref, kvg_ref, wkv_ref, cos_ref, sin_ref, q_o, k_o, v_o):
    for rows in _row_blocks(h_ref.shape[0]):
        p = jnp.dot(h_ref[rows, :], w_ref[...], preferred_element_type=F32)
        _mla_prep_body(p, qg_ref, wq_ref, kvg_ref, wkv_ref, cos_ref.at[rows, :], sin_ref.at[rows, :],
                       q_o.at[rows, :], k_o.at[rows, :], v_o.at[rows, :])


def _mla_proj(h, w, q_norm_g, wq, kv_norm_g, wkv, cos, sin):
    B, T, D = h.shape
    tt = PROJ_TILE
    tok = lambda w_: pl.BlockSpec((None, tt, w_), lambda b, i: (b, i, 0))
    full = lambda a: pl.BlockSpec(a.shape, lambda b, i: (0,) * a.ndim)
    tab = pl.BlockSpec((tt, LANES), lambda b, i: (i, 0))
    qg = q_norm_g.reshape(1, -1)
    kvg = kv_norm_g.reshape(1, -1)
    return pl.pallas_call(
        _mla_proj_body,
        grid=(B, T // tt),
        in_specs=[tok(D), full(w), full(qg), full(wq), full(kvg), full(wkv), tab, tab],
        out_specs=[tok(MLA_HEADS * MLA_DK), tok(MLA_HEADS * MLA_DK), tok(MLA_HEADS * MLA_V)],
        out_shape=[jax.ShapeDtypeStruct((B, T, MLA_HEADS * MLA_DK), BF16),
                   jax.ShapeDtypeStruct((B, T, MLA_HEADS * MLA_DK), BF16),
                   jax.ShapeDtypeStruct((B, T, MLA_HEADS * MLA_V), BF16)],
        compiler_params=_cp("parallel", "parallel"),
        name="mla_proj",
    )(h, w, qg, wq, kvg, wkv, cos, sin)


def _attn_body(q_ref, k_ref, v_ref, o_ref, *, hq, hkv, dk, dv, n_ctx_tiles, n_ctx, n_all):
    rep = hq // hkv
    tq = q_ref.shape[0]

    def run(nk):
        for g in range(hkv):
            kg = k_ref[0:nk, g * dk:(g + 1) * dk]
            vg = v_ref[0:nk, g * dv:(g + 1) * dv]
            v_aug = jnp.concatenate([vg, jnp.ones_like(vg)], axis=1)
            for h0 in range(g * rep, (g + 1) * rep, ATTN_STACK):
                heads = range(h0, min(h0 + ATTN_STACK, (g + 1) * rep))
                q = jnp.concatenate([q_ref[:, h * dk:(h + 1) * dk] for h in heads], axis=0)
                s = _dot_nt(q, kg)
                p = jnp.exp2(s - jnp.max(s, axis=-1, keepdims=True))
                o = jnp.dot(p.astype(BF16), v_aug, preferred_element_type=F32)
                o = o[:, 0:dv] / o[:, dv:dv + 1]
                for j, h in enumerate(heads):
                    o_ref[:, h * dv:(h + 1) * dv] = o[j * tq:(j + 1) * tq].astype(o_ref.dtype)

    @pl.when(pl.program_id(1) < n_ctx_tiles)
    def _():
        run(n_ctx)

    @pl.when(pl.program_id(1) >= n_ctx_tiles)
    def _():
        run(n_all)


def _attention(q, k, v, hq, hkv, dk, dv, n_ctx):
    B, T, _ = q.shape
    tq = TOK_TILE
    return pl.pallas_call(
        functools.partial(_attn_body, hq=hq, hkv=hkv, dk=dk, dv=dv, n_ctx_tiles=n_ctx // tq, n_ctx=n_ctx, n_all=T),
        grid=(B, T // tq),
        in_specs=[pl.BlockSpec((None, tq, hq * dk), lambda b, i: (b, i, 0)),
                  pl.BlockSpec((None, T, hkv * dk), lambda b, i: (b, 0, 0)),
                  pl.BlockSpec((None, T, hkv * dv), lambda b, i: (b, 0, 0))],
        out_specs=pl.BlockSpec((None, tq, hq * dv), lambda b, i: (b, i, 0)),
        out_shape=jax.ShapeDtypeStruct((B, T, hq * dv), BF16),
        compiler_params=_cp("parallel", "parallel"),
        name="attention",
    )(q, k, v)


def _rope_tables(T, n_ctx, n_rot):
    quarter = n_rot // 4
    t = jnp.arange(T - n_ctx)
    row = (t // GRID_W).astype(F32)
    col = (t % GRID_W).astype(F32)
    inv = ROPE_THETA ** (-jnp.arange(quarter, dtype=F32) / quarter)
    ar = row[:, None] * inv[None, :]
    ac = col[:, None] * inv[None, :]
    pad = LANES - n_rot
    cos = jnp.concatenate([jnp.cos(ar), jnp.cos(ar), jnp.cos(ac), jnp.cos(ac), jnp.ones((T - n_ctx, pad), F32)], axis=1)
    sin = jnp.concatenate([-jnp.sin(ar), jnp.sin(ar), -jnp.sin(ac), jnp.sin(ac), jnp.zeros((T - n_ctx, pad), F32)], axis=1)
    cos = jnp.concatenate([jnp.ones((n_ctx, LANES), F32), cos], axis=0)
    sin = jnp.concatenate([jnp.zeros((n_ctx, LANES), F32), sin], axis=0)
    return cos, sin


def _moe_body(te_ref, first_ref, slot_ref, nxt_ref, nv_ref, hs_ref, w1_hbm, w3_hbm, w2_hbm, y_ref,
              f1, f3, f2, c1, c3, c2, sem, *, layer):
    i = pl.program_id(0)

    def weight_copies(e, s):
        return (pltpu.make_async_copy(w1_hbm.at[layer, e], f1.at[s], sem.at[s, 0]),
                pltpu.make_async_copy(w3_hbm.at[layer, e], f3.at[s], sem.at[s, 1]),
                pltpu.make_async_copy(w2_hbm.at[layer, e], f2.at[s], sem.at[s, 2]))

    @pl.when(i == 0)
    def _():
        for cp in weight_copies(te_ref[0], 0):
            cp.start()

    @pl.when(first_ref[i] == 1)
    def _():
        s = slot_ref[i]
        for cp in weight_copies(te_ref[i], s):
            cp.wait()

        @pl.when(nxt_ref[i] >= 0)
        def _():
            for cp in weight_copies(nxt_ref[i], 1 - s):
                cp.start()

        c1[...] = f1[s].astype(BF16)
        c3[...] = f3[s].astype(BF16)
        c2[...] = f2[s].astype(BF16)

    @pl.when(i < nv_ref[0])
    def _():
        hs = hs_ref[...].astype(BF16)
        a = jnp.dot(hs, c1[...], preferred_element_type=F32)
        b = jnp.dot(hs, c3[...], preferred_element_type=F32)
        act = a * _sigmoid(a) * b
        y = jnp.dot(act.astype(BF16), c2[...], preferred_element_type=F32)
        y_ref[...] = y.astype(y_ref.dtype)

    @pl.when(i >= nv_ref[0])
    def _():
        y_ref[...] = jnp.zeros_like(y_ref)


def _moe_experts(tile_expert, run_first, run_slot, run_next, n_valid, hs, w1, w3, w2, layer):
    NP, D = hs.shape
    tm = MOE_TM
    DE = w1.shape[-1]
    grid_spec = pltpu.PrefetchScalarGridSpec(
        num_scalar_prefetch=5,
        grid=(NP // tm,),
        in_specs=[pl.BlockSpec((tm, D), lambda i, te, fi, sl, nx, nv: (jnp.minimum(i, nv[0] - 1), 0)),
                  pl.BlockSpec(memory_space=pl.ANY),
                  pl.BlockSpec(memory_space=pl.ANY),
                  pl.BlockSpec(memory_space=pl.ANY)],
        out_specs=pl.BlockSpec((tm, D), lambda i, *_: (i, 0)),
        scratch_shapes=[pltpu.VMEM((2, D, DE), F32), pltpu.VMEM((2, D, DE), F32), pltpu.VMEM((2, DE, D), F32),
                        pltpu.VMEM((D, DE), BF16), pltpu.VMEM((D, DE), BF16), pltpu.VMEM((DE, D), BF16),
                        pltpu.SemaphoreType.DMA((2, 3))],
    )
    return pl.pallas_call(
        functools.partial(_moe_body, layer=layer),
        grid_spec=grid_spec,
        out_shape=jax.ShapeDtypeStruct((NP, D), BF16),
        compiler_params=_cp("arbitrary"),
        name="moe_experts",
    )(tile_expert, run_first, run_slot, run_next, n_valid, hs, w1, w3, w2)


def _moe(h2, re, w1, w3, w2, layer):
    N = h2.shape[0]
    tm = MOE_TM
    n_tiles = (2 * N) // tm + N_EXPERTS
    e_flat = re.reshape(2 * N)
    onehot = (e_flat[:, None] == jnp.arange(N_EXPERTS, dtype=jnp.int32)[None, :]).astype(jnp.int32)
    csum = jnp.cumsum(onehot, axis=0)
    rank = jnp.sum(onehot * (csum - 1), axis=1)
    counts = csum[-1]
    ptiles = (counts + tm - 1) // tm
    tile_end = jnp.cumsum(ptiles)
    tile_start = tile_end - ptiles
    pos = tile_start[e_flat] * tm + rank
    n_valid = tile_end[-1:].astype(jnp.int32)
    tile_ids = jnp.arange(n_tiles, dtype=jnp.int32)
    tile_expert = jnp.minimum(jnp.sum((tile_end[None, :] <= tile_ids[:, None]).astype(jnp.int32), axis=1), N_EXPERTS - 1)
    experts = jnp.arange(N_EXPERTS, dtype=jnp.int32)
    owns = ptiles > 0
    run_no = jnp.cumsum(owns.astype(jnp.int32)) - 1
    later = jnp.where(owns[None, :] & (experts[None, :] > experts[:, None]), experts[None, :], N_EXPERTS)
    next_run = jnp.min(later, axis=1)
    next_run = jnp.where(next_run < N_EXPERTS, next_run, -1).astype(jnp.int32)
    run_first = ((tile_ids == tile_start[tile_expert]) & (tile_ids < n_valid[0])).astype(jnp.int32)
    run_slot = (run_no[tile_expert] % 2).astype(jnp.int32)
    run_next = next_run[tile_expert]
    spread = jnp.arange(n_tiles * tm, dtype=jnp.int32) % N
    src = spread.at[pos].set(jnp.arange(2 * N, dtype=jnp.int32) // 2)
    rows = lambda a, idx: a.at[idx].get(mode="promise_in_bounds")
    hs = rows(h2, src)
    y = _moe_experts(tile_expert, run_first, run_slot, run_next, n_valid, hs, w1, w3, w2, layer)
    pos2 = pos.reshape(N, 2)
    return rows(y, pos2[:, 0]), rows(y, pos2[:, 1])


def kernel(x, c, ctx, c_ctx, mod_w, mod_b, norm1_g, norm2_g, w_in, w_out, shift_prev, shift_next, decay_w0, decay_up, iclr_a0, iclr_up, gate_up, k_k, k_a, r_k, gn_g, gn_b, q_norm_g, k_norm_g, mla_q_norm_g, mla_w_uq, mla_kv_norm_g, mla_w_ukv, router_gw, router_gb, router_ew, router_eb, exp_w1, exp_w3, exp_w2, final_norm_g):
    B, S, D = x.shape
    C = ctx.shape[1]
    T = C + S
    depth = mod_w.shape[0]
    assert C == TOK_TILE and S % TOK_TILE == 0 and B <= CTX_ROW
    n_ctx_tiles = C // TOK_TILE

    cc = jnp.zeros((SUBLANES, D), F32).at[:B].set(c).at[CTX_ROW].set(c_ctx)
    mods_all = _modulation(cc, mod_w, mod_b).reshape(depth, SUBLANES * N_MOD, 1, D)
    X, h = _join_norm(ctx, x, norm1_g[0], mods_all[0], 0, 1)

    cos_g, sin_g = _rope_tables(T, C, GQA_HEAD)
    cos_m, sin_m = _rope_tables(T, C, MLA_ROPE)
    hid = jnp.arange(RWKV_W) // RWKV_HEAD
    head_sum = (hid[:, None] == hid[None, :]).astype(F32)

    for l in range(depth):
        mods = mods_all[l]
        w_r = w_in[l][:, :RWKV_COLS].astype(BF16)
        w_g = w_in[l][:, RWKV_COLS:RWKV_COLS + GQA_COLS].astype(BF16)
        w_m = jnp.pad(w_in[l][:, RWKV_COLS + GQA_COLS:], ((0, 0), (0, MLA_COLS_PAD - MLA_COLS))).astype(BF16)

        pr = _matmul(h.reshape(B * T, D), w_r).reshape(B, T, RWKV_COLS)

        wup_pad = jnp.pad(decay_up[l], ((0, 0), (0, ICLR_RANK), (0, 0)))
        aup_pad = jnp.pad(iclr_up[l], ((0, 0), (DECAY_RANK, 0), (0, 0)))
        r, v, kh, lw, b, kt, g, bonus = _rwkv_prep(pr, shift_prev[l], shift_next[l], decay_w0[l], wup_pad,
                                                  iclr_a0[l], aup_pad, gate_up[l], k_k[l], k_a[l], r_k[l], head_sum)
        yf, yb = _rwkv_scan(r, v, kh, lw, b, kt, C)
        o_r = _rwkv_readout(yf, yb, bonus, g, gn_g[l], gn_b[l], head_sum)

        q, k, vv = _gqa_proj(h, w_g, q_norm_g[l], k_norm_g[l], cos_g, sin_g)
        o_g = _attention(q, k, vv, GQA_Q_HEADS, GQA_KV_HEADS, GQA_HEAD, GQA_HEAD, C)

        wq = mla_w_uq[l].reshape(MLA_Q_RANK, MLA_HEADS, MLA_NOPE + MLA_ROPE)
        wq = jnp.pad(wq, ((0, 0), (0, 0), (0, MLA_DK - MLA_NOPE - MLA_ROPE))).reshape(MLA_Q_RANK, MLA_HEADS * MLA_DK)
        wkv = mla_w_ukv[l].reshape(MLA_KV_RANK, MLA_HEADS, MLA_NOPE + MLA_V)
        wkv = jnp.concatenate([wkv[:, :, :MLA_NOPE].reshape(MLA_KV_RANK, -1), wkv[:, :, MLA_NOPE:].reshape(MLA_KV_RANK, -1)], axis=1)
        qm, km, vm = _mla_proj(h, w_m, mla_q_norm_g[l], wq.astype(BF16), mla_kv_norm_g[l], wkv.astype(BF16), cos_m, sin_m)
        o_m = _attention(qm, km, vm, MLA_HEADS, MLA_HEADS, MLA_DK, MLA_V, C)

        last = l + 1 == depth
        wr = jnp.pad(jnp.concatenate([router_gw[l], router_ew[l]], axis=1), ((0, 0), (0, LANES - N_GROUPS - N_EXPERTS)))
        br = jnp.pad(jnp.concatenate([router_gb[l], router_eb[l]]), (0, LANES - N_GROUPS - N_EXPERTS)).reshape(1, LANES)
        X, h2, rw, re = _out_proj_router([o_r, o_g, o_m], w_out[l].astype(BF16), X, mods, C,
                                         TOK_TILE if last else OUT_TILE, last, norm2_g[l], wr, br)
        rows = X.shape[1]
        ctx_tiles = 0 if last else n_ctx_tiles

        y0, y1 = _moe(h2.reshape(B * rows, D), re.reshape(B * rows, LANES)[:, :2], exp_w1, exp_w3, exp_w2, l)
        y0 = y0.reshape(B, rows, D)
        y1 = y1.reshape(B, rows, D)
        if last:
            out = _moe_residual_final(X, y0, y1, rw, mods, 5, ctx_tiles, final_norm_g)
        else:
            X, h = _moe_residual_norm(X, y0, y1, rw, mods, 5, ctx_tiles, norm1_g[l + 1], mods_all[l + 1])
    return out
```

```python
import functools
import math

import jax
import jax.numpy as jnp
from jax import lax
from jax.experimental import pallas as pl
from jax.experimental.pallas import tpu as pltpu

F32 = jnp.float32
BF16 = jnp.bfloat16

V7X_VMEM_BYTES = 64 * 1024 * 1024
VMEM_LIMIT = V7X_VMEM_BYTES - 8 * 1024 * 1024
LANES = 128
SUBLANES = 8

GRID_W = 64
ROPE_THETA = 10000.0
NORM_EPS = 1e-6
GN_EPS = 64e-5
DECAY_SCALE = math.exp(-0.5)
LOG2E = math.log2(math.e)

RWKV_HEAD = 64
RWKV_W = 512
DECAY_RANK = 64
ICLR_RANK = 64
GATE_RANK = 128
RWKV_COLS = 3 * RWKV_W + DECAY_RANK + ICLR_RANK + GATE_RANK
LOWRANK_OFF = 3 * RWKV_W
GATE_OFF = LOWRANK_OFF + DECAY_RANK + ICLR_RANK
CHUNK = 64
SCAN_CHUNKS = 2

GQA_HEAD = 128
GQA_Q_HEADS = 8
GQA_KV_HEADS = 2
GQA_Q_COLS = GQA_Q_HEADS * GQA_HEAD
GQA_KV_COLS = GQA_KV_HEADS * GQA_HEAD
GQA_COLS = GQA_Q_COLS + 2 * GQA_KV_COLS

MLA_HEADS = 4
MLA_NOPE = 128
MLA_ROPE = 64
MLA_V = 128
MLA_Q_RANK = 384
MLA_KV_RANK = 256
MLA_COLS = MLA_Q_RANK + MLA_KV_RANK + MLA_ROPE
MLA_COLS_PAD = 768
MLA_DK = 2 * LANES

N_GROUPS = 4
EXPERTS_PER_GROUP = 8
N_EXPERTS = 32
D_EXPERT = 256
MOE_TM = 256

TOK_TILE = 256
PROJ_TILE = 768
OUT_TILE = 384
N_MOD = 6
CTX_ROW = 4


def _cp(*sem):
    return pltpu.CompilerParams(dimension_semantics=sem, vmem_limit_bytes=VMEM_LIMIT)


def _sigmoid(x):
    return 1.0 / (1.0 + jnp.exp(-x))


def _bf16_terms(x, n):
    terms = []
    for _ in range(n):
        t = x.astype(BF16)
        terms.append(t)
        x = x - t.astype(F32)
    return terms


def _dot_terms(x, w, nx, nw):
    xs = _bf16_terms(x, nx)
    ws = _bf16_terms(w, nw)
    acc = None
    for i in range(nx):
        for j in range(nw):
            if i + j < max(nx, nw):
                p = jnp.dot(xs[i], ws[j], preferred_element_type=F32)
                acc = p if acc is None else acc + p
    return acc


def _mod_body(c_ref, w_ref, b_ref, o_ref):
    c = c_ref[...]
    s = c * _sigmoid(c)
    o_ref[...] = _dot_terms(s, w_ref[...], 2, 2) + b_ref[...]


def _modulation(cc, mod_w, mod_b):
    L, D, N = mod_w.shape
    tn = 1024
    return pl.pallas_call(
        _mod_body,
        grid=(L, N // tn),
        in_specs=[pl.BlockSpec((SUBLANES, D), lambda l, j: (0, 0)),
                  pl.BlockSpec((None, D, tn), lambda l, j: (l, 0, j)),
                  pl.BlockSpec((None, 1, tn), lambda l, j: (l, 0, j))],
        out_specs=pl.BlockSpec((None, SUBLANES, tn), lambda l, j: (l, 0, j)),
        out_shape=jax.ShapeDtypeStruct((L, SUBLANES, N), F32),
        compiler_params=_cp("parallel", "parallel"),
        name="modulation",
    )(cc, mod_w, mod_b.reshape(L, 1, N))


def _mod_spec(which, n_ctx_tiles, D):
    return pl.BlockSpec((None, 1, D), lambda b, i: (jnp.where(i < n_ctx_tiles, CTX_ROW, b) * N_MOD + which, 0, 0))


def _rms(x, g):
    return x * lax.rsqrt(jnp.mean(x * x, axis=-1, keepdims=True) + NORM_EPS) * g


def _join_norm_body(ctx_ref, x_ref, g_ref, sh_ref, sc_ref, x_o, h_o, *, n_ctx_tiles):
    i = pl.program_id(1)

    def emit(src_ref):
        x = src_ref[...]
        x_o[...] = x
        h_o[...] = (_rms(x, g_ref[...]) * (1.0 + sc_ref[...]) + sh_ref[...]).astype(h_o.dtype)

    @pl.when(i < n_ctx_tiles)
    def _():
        emit(ctx_ref)

    @pl.when(i >= n_ctx_tiles)
    def _():
        emit(x_ref)


def _route(logits):
    lane = lax.broadcasted_iota(jnp.int32, logits.shape, 1)
    lane_f = lane.astype(F32)
    neg = jnp.float32(-1e30)
    far = jnp.float32(1e9)
    first_at = lambda hit: jnp.min(jnp.where(hit, lane_f, far), axis=-1, keepdims=True).astype(jnp.int32)
    gl = jnp.where(lane < N_GROUPS, logits, neg)
    gmax = jnp.max(gl, axis=-1, keepdims=True)
    gidx = first_at(gl == gmax)
    p_sel = 1.0 / jnp.sum(jnp.exp(gl - gmax), axis=-1, keepdims=True)
    lo = N_GROUPS + gidx * EXPERTS_PER_GROUP
    el = jnp.where((lane >= lo) & (lane < lo + EXPERTS_PER_GROUP), logits, neg)
    m1 = jnp.max(el, axis=-1, keepdims=True)
    i1 = first_at(el == m1)
    el2 = jnp.where(lane == i1, neg, el)
    m2 = jnp.max(el2, axis=-1, keepdims=True)
    i2 = first_at(el2 == m2)
    t = jnp.exp(m2 - m1)
    w1 = p_sel / (1.0 + t)
    w2 = p_sel * t / (1.0 + t)
    rw = jnp.where(lane == 0, w1, jnp.where(lane == 1, w2, 0.0))
    re = jnp.where(lane == 0, i1 - N_GROUPS, jnp.where(lane == 1, i2 - N_GROUPS, 0))
    return rw, re


def _join_norm(ctx, x, g, mods, shift_i, scale_i):
    B, C, D = ctx.shape
    tt = TOK_TILE
    n_ctx_tiles = C // tt
    T = C + x.shape[1]
    tok = pl.BlockSpec((None, tt, D), lambda b, i: (b, i, 0))
    return pl.pallas_call(
        functools.partial(_join_norm_body, n_ctx_tiles=n_ctx_tiles),
        grid=(B, T // tt),
        in_specs=[pl.BlockSpec((None, tt, D), lambda b, i: (b, jnp.minimum(i, n_ctx_tiles - 1), 0)),
                  pl.BlockSpec((None, tt, D), lambda b, i: (b, jnp.maximum(i - n_ctx_tiles, 0), 0)),
                  pl.BlockSpec((1, D), lambda b, i: (0, 0)),
                  _mod_spec(shift_i, n_ctx_tiles, D),
                  _mod_spec(scale_i, n_ctx_tiles, D)],
        out_specs=[tok, tok],
        out_shape=[jax.ShapeDtypeStruct((B, T, D), F32), jax.ShapeDtypeStruct((B, T, D), BF16)],
        compiler_params=_cp("parallel", "parallel"),
        name="join_norm",
    )(ctx, x, g.reshape(1, D), mods, mods)


def _mm_body(a_ref, w_ref, o_ref):
    o_ref[...] = jnp.dot(a_ref[...], w_ref[...], preferred_element_type=F32).astype(o_ref.dtype)


def _matmul(a, w, tm=PROJ_TILE):
    M, K = a.shape
    N = w.shape[1]
    assert M % tm == 0
    return pl.pallas_call(
        _mm_body,
        grid=(M // tm,),
        in_specs=[pl.BlockSpec((tm, K), lambda i: (i, 0)),
                  pl.BlockSpec((K, N), lambda i: (0, 0))],
        out_specs=pl.BlockSpec((tm, N), lambda i: (i, 0)),
        out_shape=jax.ShapeDtypeStruct((M, N), F32),
        compiler_params=_cp("parallel"),
        name="token_matmul",
    )(a, w)


def _out_proj_router_body(*refs, n_parts, n_ctx, tm, tile0):
    a_refs = refs[:n_parts]
    w_ref, x_ref, ml_ref, mc_ref, g_ref, wr_ref, br_ref, x_o, h_o, rw_o, re_o = refs[n_parts:]
    acc = None
    k0 = 0
    for a_ref in a_refs:
        k1 = k0 + a_ref.shape[-1]
        part = jnp.dot(a_ref[...], w_ref[k0:k1, :], preferred_element_type=F32)
        acc = part if acc is None else acc + part
        k0 = k1
    row = (pl.program_id(1) + tile0) * tm + lax.broadcasted_iota(jnp.int32, (tm, 1), 0)
    is_ctx = row < n_ctx
    mod = lambda which: jnp.where(is_ctx, mc_ref[which], ml_ref[which])
    x = x_ref[...] + mod(2) * acc
    x_o[...] = x
    h = _rms(x, g_ref[...]) * (1.0 + mod(4)) + mod(3)
    h_o[...] = h
    h_hi, h_lo = _bf16_terms(h, 2)
    w_hi, w_lo = _bf16_terms(wr_ref[...], 2)
    both = jnp.dot(h_hi, jnp.concatenate([w_hi, w_lo], axis=1), preferred_element_type=F32)
    logits = (both[:, 0:LANES] + both[:, LANES:2 * LANES]
              + jnp.dot(h_lo, w_hi, preferred_element_type=F32) + br_ref[...])
    rw, re = _route(logits)
    rw_o[...] = rw
    re_o[...] = re


def _out_proj_router(parts, w, X, mods, n_ctx, tm, latents_only, g2, wr, br):
    B, T, D = X.shape
    tile0 = n_ctx // tm if latents_only else 0
    assert T % tm == 0 and (n_ctx % tm == 0 or not latents_only)
    assert sum(p.shape[-1] for p in parts) == w.shape[0]
    rows_out = T - tile0 * tm
    mods6 = mods.reshape(SUBLANES, N_MOD, 1, D)
    tok_in = lambda wd: pl.BlockSpec((None, tm, wd), lambda b, i: (b, i + tile0, 0))
    tok_out = lambda wd: pl.BlockSpec((None, tm, wd), lambda b, i: (b, i, 0))
    shape = lambda wd, dt: jax.ShapeDtypeStruct((B, rows_out, wd), dt)
    return pl.pallas_call(
        functools.partial(_out_proj_router_body, n_parts=len(parts), n_ctx=n_ctx, tm=tm, tile0=tile0),
        grid=(B, T // tm - tile0),
        in_specs=[tok_in(p.shape[-1]) for p in parts] + [
                  pl.BlockSpec(w.shape, lambda b, i: (0, 0)),
                  tok_in(D),
                  pl.BlockSpec((None, N_MOD, 1, D), lambda b, i: (b, 0, 0, 0)),
                  pl.BlockSpec((None, N_MOD, 1, D), lambda b, i: (CTX_ROW, 0, 0, 0)),
                  pl.BlockSpec((1, D), lambda b, i: (0, 0)),
                  pl.BlockSpec((D, LANES), lambda b, i: (0, 0)),
                  pl.BlockSpec((1, LANES), lambda b, i: (0, 0))],
        out_specs=[tok_out(D), tok_out(D), tok_out(LANES), tok_out(LANES)],
        out_shape=[shape(D, F32), shape(D, F32), shape(LANES, F32), shape(LANES, jnp.int32)],
        compiler_params=_cp("parallel", "parallel"),
        name="out_proj_router",
    )(*parts, w, X, mods6, mods6, g2.reshape(1, D), wr, br)


def _moe_residual(x_ref, y0_ref, y1_ref, rw_ref, gate_ref):
    rw = rw_ref[...]
    moe = rw[:, 0:1] * y0_ref[...].astype(F32) + rw[:, 1:2] * y1_ref[...].astype(F32)
    return x_ref[...] + gate_ref[...] * moe


def _moe_residual_norm_body(x_ref, y0_ref, y1_ref, rw_ref, gate_ref, g_ref, sh_ref, sc_ref, x_o, h_o):
    x = _moe_residual(x_ref, y0_ref, y1_ref, rw_ref, gate_ref)
    x_o[...] = x
    h_o[...] = (_rms(x, g_ref[...]) * (1.0 + sc_ref[...]) + sh_ref[...]).astype(h_o.dtype)


def _moe_residual_final_body(x_ref, y0_ref, y1_ref, rw_ref, gate_ref, g_ref, o_ref):
    o_ref[...] = _rms(_moe_residual(x_ref, y0_ref, y1_ref, rw_ref, gate_ref), g_ref[...])


def _moe_residual_norm(X, Y0, Y1, rw, mods, gate_i, n_ctx_tiles, g_next, mods_next):
    B, T, D = X.shape
    tt = TOK_TILE
    blk = pl.BlockSpec((None, tt, D), lambda b, i: (b, i, 0))
    return pl.pallas_call(
        _moe_residual_norm_body,
        grid=(B, T // tt),
        in_specs=[blk, blk, blk, pl.BlockSpec((None, tt, LANES), lambda b, i: (b, i, 0)),
                  _mod_spec(gate_i, n_ctx_tiles, D), pl.BlockSpec((1, D), lambda b, i: (0, 0)),
                  _mod_spec(0, n_ctx_tiles, D), _mod_spec(1, n_ctx_tiles, D)],
        out_specs=[blk, blk],
        out_shape=[jax.ShapeDtypeStruct((B, T, D), F32), jax.ShapeDtypeStruct((B, T, D), BF16)],
        compiler_params=_cp("parallel", "parallel"),
        name="moe_residual_norm",
    )(X, Y0, Y1, rw, mods, g_next.reshape(1, D), mods_next, mods_next)


def _moe_residual_final(X, Y0, Y1, rw, mods, gate_i, n_ctx_tiles, g_final):
    B, T, D = X.shape
    tt = TOK_TILE
    S = T - n_ctx_tiles * tt
    lat = lambda w: pl.BlockSpec((None, tt, w), lambda b, i: (b, i + n_ctx_tiles, 0))
    return pl.pallas_call(
        _moe_residual_final_body,
        grid=(B, S // tt),
        in_specs=[lat(D), lat(D), lat(D), lat(LANES),
                  pl.BlockSpec((None, 1, D), lambda b, i: (b * N_MOD + gate_i, 0, 0)),
                  pl.BlockSpec((1, D), lambda b, i: (0, 0))],
        out_specs=pl.BlockSpec((None, tt, D), lambda b, i: (b, i, 0)),
        out_shape=jax.ShapeDtypeStruct((B, S, D), F32),
        compiler_params=_cp("parallel", "parallel"),
        name="moe_residual_final",
    )(X, Y0, Y1, rw, mods, g_final.reshape(1, D))


def _rwkv_prep_body(p_ref, pv_ref, nx_ref, mup_ref, mun_ref, w0_ref, wup_ref, a0_ref, aup_ref, gup_ref,
                    kk_ref, ka_ref, rk_ref, e_ref,
                    r_o, v_o, kh_o, lw_o, b_o, kt_o, g_o, bon_o, *, n_tiles, tt):
    i = pl.program_id(1)
    p = p_ref[...]
    seq_first = i <= 1
    seq_last = (i == 0) | (i == n_tiles - 1)
    prow = jnp.where(seq_first, 0.0, pv_ref[SUBLANES - 1:SUBLANES, :])
    nrow = jnp.where(seq_last, 0.0, nx_ref[0:1, :])
    rid = lax.broadcasted_iota(jnp.int32, (tt, 1), 0)
    prev = jnp.where(rid == 0, prow, pltpu.roll(p, 1, 0))
    nxt = jnp.where(rid == tt - 1, nrow, pltpu.roll(p, tt - 1, 0))
    z = p + mup_ref[...] * (prev - p) + mun_ref[...] * (nxt - p)

    W = RWKV_W
    r = z[:, 0:W]
    k = z[:, W:2 * W]
    v = z[:, 2 * W:3 * W]
    lowrank = z[:, LOWRANK_OFF:LOWRANK_OFF + LANES]
    gd = z[:, GATE_OFF:GATE_OFF + GATE_RANK]
    head_sum = e_ref[...]

    kap = k * kk_ref[...]
    ss = _dot_terms(kap * kap, head_sum, 2, 1)
    khat = kap * lax.rsqrt(ss + 1e-12)
    wd_t = jnp.tanh(lowrank)
    g_o[...] = _dot_terms(_sigmoid(gd), gup_ref[...], 1, 1)
    r_o[...] = r
    v_o[...] = v
    kh_o[...] = khat
    kt_sum = None
    for d in range(2):
        dec = _dot_terms(wd_t, wup_ref[d], 2, 2)
        lw_o[d] = -DECAY_SCALE * _sigmoid(w0_ref[d:d + 1, :] + dec)
        a = _sigmoid(a0_ref[d:d + 1, :] + _dot_terms(lowrank, aup_ref[d], 1, 1))
        kt = k * (1.0 + (a - 1.0) * ka_ref[...])
        kt_o[d] = kt
        b_o[d] = a * khat
        kt_sum = kt if kt_sum is None else kt_sum + kt
    bsum = _dot_terms(r * kt_sum * rk_ref[...], head_sum, 2, 1)
    bon_o[...] = bsum * v


def _rwkv_prep(pr, mu_prev, mu_next, w0, wup_pad, a0, aup_pad, g_up, k_k, k_a, r_k, head_sum):
    B, T, _ = pr.shape
    tt = TOK_TILE
    W = RWKV_W
    n_tiles = T // tt
    n8 = tt // SUBLANES
    row = lambda v: v.reshape(1, -1)
    full = lambda a: pl.BlockSpec(a.shape, lambda b, i: (0,) * a.ndim)
    tok = pl.BlockSpec((None, tt, W), lambda b, i: (b, i, 0))
    tok2 = pl.BlockSpec((None, 2, tt, W), lambda b, i: (b, 0, i, 0))
    s1 = jax.ShapeDtypeStruct((B, T, W), F32)
    s2 = jax.ShapeDtypeStruct((B, 2, T, W), F32)
    consts = [row(mu_prev), row(mu_next), w0, wup_pad, a0, aup_pad, g_up, row(k_k), row(k_a), row(r_k), head_sum]
    return pl.pallas_call(
        functools.partial(_rwkv_prep_body, n_tiles=n_tiles, tt=tt),
        grid=(B, n_tiles),
        in_specs=[pl.BlockSpec((None, tt, RWKV_COLS), lambda b, i: (b, i, 0)),
                  pl.BlockSpec((None, SUBLANES, RWKV_COLS), lambda b, i: (b, jnp.maximum(i * n8 - 1, 0), 0)),
                  pl.BlockSpec((None, SUBLANES, RWKV_COLS), lambda b, i: (b, jnp.minimum((i + 1) * n8, T // SUBLANES - 1), 0)),
                  ] + [full(a) for a in consts],
        out_specs=[tok, tok, tok, tok2, tok2, tok2, tok, tok],
        out_shape=[s1, s1, s1, s2, s2, s2, s1, s1],
        compiler_params=_cp("parallel", "parallel"),
        name="rwkv_prep",
    )(pr, pr, pr, *consts)


def _stack_heads(x):
    lane = lax.broadcasted_iota(jnp.int32, x.shape, 1)
    first = lane < RWKV_HEAD
    return jnp.concatenate([jnp.where(first, x, 0.0), jnp.where(first, 0.0, x)], axis=0)


def _dot(a, b):
    return jnp.dot(a.astype(BF16), b.astype(BF16), preferred_element_type=F32)


def _dot_nt(a, b):
    return lax.dot_general(a.astype(BF16), b.astype(BF16), (((1,), (1,)), ((), ())), preferred_element_type=F32)


def _dot_tn(a, b):
    return jnp.dot(a.T.astype(BF16), b.astype(BF16), preferred_element_type=F32)


def _chunk_operands(r, v, kh, lw, b, kt, reverse):
    L = CHUNK
    ti = lax.broadcasted_iota(jnp.int32, (L, L), 0)
    tj = lax.broadcasted_iota(jnp.int32, (L, L), 1)
    tri = jnp.where((ti <= tj) if reverse else (ti >= tj), 1.0, 0.0)
    lam = _dot_terms(tri, lw, 1, 3)
    tot = lam[0:1, :] if reverse else lam[L - 1:L, :]
    e_n = jnp.exp(-lam)
    e_g = jnp.exp(tot - lam)
    full = dict(A=kh * jnp.exp(lam - lw), R=r * jnp.exp(lam), Kn=kt * e_n, Bn=b * e_n, Kg=kt * e_g, Bg=b * e_g, V=v)
    e_tot = jnp.exp(tot)
    pairs = []
    for p in range(RWKV_W // LANES):
        sl = slice(p * LANES, (p + 1) * LANES)
        ops = {k: a[:, sl] for k, a in full.items()}
        ops["e_tot"] = e_tot[:, sl]
        ops["reverse"] = reverse
        pairs.append(ops)
    return pairs


def _chunk_masks(reverse):
    L = CHUNK
    t = lax.broadcasted_iota(jnp.int32, (L, 2 * L), 0)
    i = lax.broadcasted_iota(jnp.int32, (L, 2 * L), 1) & (L - 1)
    before = (i > t) if reverse else (i < t)
    return before, before | (i == t), i == t


def _chunks_prepare(chains):
    L = CHUNK
    P2 = 2 * L
    n = len(chains)
    masks = {rev: _chunk_masks(rev) for rev in {c["reverse"] for c in chains}}
    strict = [masks[c["reverse"]][0] for c in chains]
    incl = [masks[c["reverse"]][1] for c in chains]
    eye = masks[chains[0]["reverse"]][2]
    bi = lax.broadcasted_iota(jnp.int32, (P2, P2), 0)
    bj = lax.broadcasted_iota(jnp.int32, (P2, P2), 1)
    same_head = (bi >= L) == (bj >= L)
    eye2 = bi == bj
    stack = _stack_heads
    cat0 = lambda *xs: jnp.concatenate(xs, axis=0)
    cat1 = lambda *xs: jnp.concatenate(xs, axis=1)

    big = [_dot_nt(cat0(c["A"], c["R"]), cat0(stack(c["Bn"]), stack(c["Kn"]))) for c in chains]
    Mb = [jnp.where(strict[i], big[i][0:L, 0:P2], 0.0) for i in range(n)]
    Mkv = [jnp.where(strict[i], big[i][0:L, P2:2 * P2], 0.0) for i in range(n)]
    Pb = [jnp.where(incl[i], big[i][L:P2, 0:P2], 0.0) for i in range(n)]
    Pkv = [jnp.where(incl[i], big[i][L:P2, P2:2 * P2], 0.0) for i in range(n)]

    Pw = [-m for m in Mb]
    Tm = [jnp.where(eye, 1.0, 0.0) + p for p in Pw]
    Pw = [_dot(p, stack(p)) for p in Pw]
    for _ in range(int(math.log2(L)) - 2):
        PT = [_dot(cat0(p, t), stack(p)) for p, t in zip(Pw, Tm)]
        Tm = [t + pt[L:P2] for t, pt in zip(Tm, PT)]
        Pw = [pt[0:L] for pt in PT]
    Tm = [t + _dot(t, stack(p)) for t, p in zip(Tm, Pw)]

    Vs = [stack(c["V"]) for c in chains]
    MPV = [_dot(cat0(Mkv[i], Pkv[i]), Vs[i]) for i in range(n)]
    TAM = [_dot(Tm[i], cat1(stack(chains[i]["A"]), stack(MPV[i][0:L]))) for i in range(n)]
    PB = [_dot(Pb[i], cat1(stack(TAM[i][:, 0:P2]), stack(TAM[i][:, P2:2 * P2]))) for i in range(n)]
    BG = [_dot_tn(chains[i]["Bg"], TAM[i]) for i in range(n)]
    KV = [_dot_tn(chains[i]["Kg"], chains[i]["V"]) for i in range(n)]
    prepared = []
    for i in range(n):
        RA = chains[i]["R"] - PB[i][:, 0:P2]
        G2 = jnp.where(eye2, chains[i]["e_tot"], 0.0) - jnp.where(same_head, BG[i][:, 0:P2], 0.0)
        H2 = jnp.where(same_head, KV[i] - BG[i][:, P2:2 * P2], 0.0)
        prepared.append((cat0(RA, G2), MPV[i][L:P2] - PB[i][:, P2:2 * P2], H2))
    return prepared


def _chunk_advance(prepared, state):
    lhs, y0, h2 = prepared
    out = _dot(lhs, state)
    return out[0:CHUNK] + y0, out[CHUNK:] + h2


def _rwkv_scan_body(rf_ref, vf_ref, khf_ref, rb_ref, vb_ref, khb_ref, lwf_ref, bf_ref, ktf_ref, lwb_ref, bb_ref, ktb_ref,
                    yf_ref, yb_ref, s_ref):
    @pl.when(pl.program_id(1) == 0)
    def _():
        s_ref[...] = jnp.zeros_like(s_ref)

    n_pairs = RWKV_W // LANES
    fwd_refs = (rf_ref, vf_ref, khf_ref, lwf_ref, bf_ref, ktf_ref)
    bwd_refs = (rb_ref, vb_ref, khb_ref, lwb_ref, bb_ref, ktb_ref)
    slots = []
    for c in range(SCAN_CHUNKS):
        lo_f = c * CHUNK
        lo_b = (SCAN_CHUNKS - 1 - c) * CHUNK
        slots.append((_chunk_operands(*(ref[lo_f:lo_f + CHUNK, :] for ref in fwd_refs), False), lo_f,
                      _chunk_operands(*(ref[lo_b:lo_b + CHUNK, :] for ref in bwd_refs), True), lo_b))
    prepared = _chunks_prepare([ch for s in slots for ch in s[0] + s[2]])
    states = [s_ref[i] for i in range(2 * n_pairs)]
    for c, (_, lo_f, _, lo_b) in enumerate(slots):
        ys = []
        for i in range(2 * n_pairs):
            y, states[i] = _chunk_advance(prepared[c * 2 * n_pairs + i], states[i])
            ys.append(y)
        yf_ref[lo_f:lo_f + CHUNK, :] = jnp.concatenate(ys[:n_pairs], axis=1)
        yb_ref[lo_b:lo_b + CHUNK, :] = jnp.concatenate(ys[n_pairs:], axis=1)
    for i in range(2 * n_pairs):
        s_ref[i] = states[i]


def _rwkv_scan(r, v, kh, lw, b, kt, n_ctx):
    B, T, W = r.shape
    blk = SCAN_CHUNKS * CHUNK
    assert T % blk == 0 and n_ctx % blk == 0
    nc = T // blk
    ncc = n_ctx // blk
    rev = lambda j: jnp.where(j < ncc, ncc - 1 - j, nc + ncc - 1 - j)
    fwd1 = pl.BlockSpec((None, blk, W), lambda bb, j: (bb, j, 0))
    bwd1 = pl.BlockSpec((None, blk, W), lambda bb, j: (bb, rev(j), 0))
    fwd2 = pl.BlockSpec((None, None, blk, W), lambda bb, j: (bb, 0, j, 0))
    bwd2 = pl.BlockSpec((None, None, blk, W), lambda bb, j: (bb, 1, rev(j), 0))
    out = jax.ShapeDtypeStruct((B, T, W), F32)
    return pl.pallas_call(
        _rwkv_scan_body,
        grid=(B, nc),
        in_specs=[fwd1, fwd1, fwd1, bwd1, bwd1, bwd1, fwd2, fwd2, fwd2, bwd2, bwd2, bwd2],
        out_specs=[fwd1, bwd1],
        out_shape=[out, out],
        scratch_shapes=[pltpu.VMEM((2 * W // LANES, LANES, LANES), F32)],
        compiler_params=_cp("parallel", "arbitrary"),
        name="rwkv_scan",
    )(r, v, kh, r, v, kh, lw, b, kt, lw, b, kt)


def _rwkv_readout_body(yf_ref, yb_ref, bon_ref, g_ref, gng_ref, gnb_ref, e_ref, o_ref):
    y = yf_ref[...] + yb_ref[...]
    head_mean = e_ref[...] * (1.0 / RWKV_HEAD)
    mu = _dot_terms(y, head_mean, 2, 1)
    yc = y - mu
    var = _dot_terms(yc * yc, head_mean, 2, 1)
    yn = yc * lax.rsqrt(var + GN_EPS) * gng_ref[...] + gnb_ref[...]
    o_ref[...] = ((yn + bon_ref[...]) * g_ref[...]).astype(o_ref.dtype)


def _rwkv_readout(yf, yb, bonus, g, gn_g, gn_b, head_sum):
    B, T, W = yf.shape
    tt = TOK_TILE
    tok = pl.BlockSpec((None, tt, W), lambda b, i: (b, i, 0))
    row = pl.BlockSpec((1, W), lambda b, i: (0, 0))
    return pl.pallas_call(
        _rwkv_readout_body,
        grid=(B, T // tt),
        in_specs=[tok, tok, tok, tok, row, row, pl.BlockSpec((W, W), lambda b, i: (0, 0))],
        out_specs=tok,
        out_shape=jax.ShapeDtypeStruct((B, T, W), BF16),
        compiler_params=_cp("parallel", "parallel"),
        name="rwkv_readout",
    )(yf, yb, bonus, g, gn_g.reshape(1, W), gn_b.reshape(1, W), head_sum)


def _rope(y, cos, sin_signed, quarter):
    lane = lax.broadcasted_iota(jnp.int32, y.shape, 1)
    first = (lane & (2 * quarter - 1)) < quarter
    partner = jnp.where(first, pltpu.roll(y, LANES - quarter, 1), pltpu.roll(y, quarter, 1))
    return y * cos + partner * sin_signed


def _gqa_prep_body(p, qg_ref, kg_ref, cos_ref, sin_ref, q_o, k_o, v_o):
    cos = cos_ref[...]
    sin = sin_ref[...]
    scale = GQA_HEAD ** -0.5 * LOG2E
    for h in range(GQA_Q_HEADS):
        sl = slice(h * GQA_HEAD, (h + 1) * GQA_HEAD)
        q = _rms(p[:, sl], qg_ref[...])
        q_o[:, sl] = (_rope(q, cos, sin, GQA_HEAD // 4) * scale).astype(q_o.dtype)
    for h in range(GQA_KV_HEADS):
        sl = slice(h * GQA_HEAD, (h + 1) * GQA_HEAD)
        k = _rms(p[:, GQA_Q_COLS + h * GQA_HEAD:GQA_Q_COLS + (h + 1) * GQA_HEAD], kg_ref[...])
        k_o[:, sl] = _rope(k, cos, sin, GQA_HEAD // 4).astype(k_o.dtype)
    v_o[...] = p[:, GQA_Q_COLS + GQA_KV_COLS:GQA_COLS].astype(v_o.dtype)


def _row_blocks(n_rows):
    return [slice(r, r + TOK_TILE) for r in range(0, n_rows, TOK_TILE)]


def _gqa_proj_body(h_ref, w_ref, qg_ref, kg_ref, cos_ref, sin_ref, q_o, k_o, v_o):
    for rows in _row_blocks(h_ref.shape[0]):
        p = jnp.dot(h_ref[rows, :], w_ref[...], preferred_element_type=F32)
        _gqa_prep_body(p, qg_ref, kg_ref, cos_ref.at[rows, :], sin_ref.at[rows, :],
                       q_o.at[rows, :], k_o.at[rows, :], v_o.at[rows, :])


def _gqa_proj(h, w, q_norm_g, k_norm_g, cos, sin):
    B, T, D = h.shape
    tt = PROJ_TILE
    tok = lambda w_: pl.BlockSpec((None, tt, w_), lambda b, i: (b, i, 0))
    row = pl.BlockSpec((1, GQA_HEAD), lambda b, i: (0, 0))
    tab = pl.BlockSpec((tt, LANES), lambda b, i: (i, 0))
    return pl.pallas_call(
        _gqa_proj_body,
        grid=(B, T // tt),
        in_specs=[tok(D), pl.BlockSpec(w.shape, lambda b, i: (0, 0)), row, row, tab, tab],
        out_specs=[tok(GQA_Q_COLS), tok(GQA_KV_COLS), tok(GQA_KV_COLS)],
        out_shape=[jax.ShapeDtypeStruct((B, T, GQA_Q_COLS), BF16),
                   jax.ShapeDtypeStruct((B, T, GQA_KV_COLS), BF16),
                   jax.ShapeDtypeStruct((B, T, GQA_KV_COLS), BF16)],
        compiler_params=_cp("parallel", "parallel"),
        name="gqa_proj",
    )(h, w, q_norm_g.reshape(1, -1), k_norm_g.reshape(1, -1), cos, sin)


def _mla_prep_body(p, qg_ref, wq_ref, kvg_ref, wkv_ref, cos_ref, sin_ref, q_o, k_o, v_o):
    cos = cos_ref[...]
    sin = sin_ref[...]
    scale = (MLA_NOPE + MLA_ROPE) ** -0.5 * LOG2E
    cq = _rms(p[:, 0:MLA_Q_RANK], qg_ref[...])
    q = jnp.dot(cq.astype(BF16), wq_ref[...], preferred_element_type=F32) * scale
    ckv = _rms(p[:, MLA_Q_RANK:MLA_Q_RANK + MLA_KV_RANK], kvg_ref[...])
    kv = jnp.dot(ckv.astype(BF16), wkv_ref[...], preferred_element_type=F32)
    kr = _rope(p[:, MLA_Q_RANK + MLA_KV_RANK:MLA_COLS_PAD], cos, sin, MLA_ROPE // 4).astype(k_o.dtype)
    for h in range(MLA_HEADS):
        lo = h * MLA_DK
        q_o[:, lo:lo + LANES] = q[:, lo:lo + LANES].astype(q_o.dtype)
        q_o[:, lo + LANES:lo + MLA_DK] = _rope(q[:, lo + LANES:lo + MLA_DK], cos, sin, MLA_ROPE // 4).astype(q_o.dtype)
        k_o[:, lo:lo + LANES] = kv[:, h * MLA_NOPE:(h + 1) * MLA_NOPE].astype(k_o.dtype)
        k_o[:, lo + LANES:lo + MLA_DK] = kr
    v_o[...] = kv[:, MLA_HEADS * MLA_NOPE:].astype(v_o.dtype)


def _mla_proj_body(h_ref, w_ref, qg_ref, wq_ref, kvg_ref, wkv_ref, cos_ref, sin_ref, q_o, k_o, v_o):
    for rows in _row_blocks(h_ref.shape[0]):
        p = jnp.dot(h_ref[rows, :], w_ref[...], preferred_element_type=F32)
        _mla_prep_body(p, qg_ref, wq_ref, kvg_ref, wkv_ref, cos_ref.at[rows, :], sin_ref.at[rows, :],
                       q_o.at[rows, :], k_o.at[rows, :], v_o.at[rows, :])


def _mla_proj(h, w, q_norm_g, wq, kv_norm_g, wkv, cos, sin):
    B, T, D = h.shape
    tt = PROJ_TILE
    tok = lambda w_: pl.BlockSpec((None, tt, w_), lambda b, i: (b, i, 0))
    full = lambda a: pl.BlockSpec(a.shape, lambda b, i: (0,) * a.ndim)
    tab = pl.BlockSpec((tt, LANES), lambda b, i: (i, 0))
    qg = q_norm_g.reshape(1, -1)
    kvg = kv_norm_g.reshape(1, -1)
    return pl.pallas_call(
        _mla_proj_body,
        grid=(B, T // tt),
        in_specs=[tok(D), full(w), full(qg), full(wq), full(kvg), full(wkv), tab, tab],
        out_specs=[tok(MLA_HEADS * MLA_DK), tok(MLA_HEADS * MLA_DK), tok(MLA_HEADS * MLA_V)],
        out_shape=[jax.ShapeDtypeStruct((B, T, MLA_HEADS * MLA_DK), BF16),
                   jax.ShapeDtypeStruct((B, T, MLA_HEADS * MLA_DK), BF16),
                   jax.ShapeDtypeStruct((B, T, MLA_HEADS * MLA_V), BF16)],
        compiler_params=_cp("parallel", "parallel"),
        name="mla_proj",
    )(h, w, qg, wq, kvg, wkv, cos, sin)


def _attn_body(q_ref, k_ref, v_ref, o_ref, *, hq, hkv, dk, dv, n_ctx_tiles, n_ctx, n_all):
    rep = hq // hkv

    def run(nk):
        for g in range(hkv):
            kg = k_ref[0:nk, g * dk:(g + 1) * dk]
            vg = v_ref[0:nk, g * dv:(g + 1) * dv]
            v_aug = jnp.concatenate([vg, jnp.ones_like(vg)], axis=1)
            for h in range(g * rep, (g + 1) * rep):
                s = _dot_nt(q_ref[:, h * dk:(h + 1) * dk], kg)
                p = jnp.exp2(s - jnp.max(s, axis=-1, keepdims=True))
                o = jnp.dot(p.astype(BF16), v_aug, preferred_element_type=F32)
                o_ref[:, h * dv:(h + 1) * dv] = (o[:, 0:dv] / o[:, dv:dv + 1]).astype(o_ref.dtype)

    @pl.when(pl.program_id(1) < n_ctx_tiles)
    def _():
        run(n_ctx)

    @pl.when(pl.program_id(1) >= n_ctx_tiles)
    def _():
        run(n_all)


def _attention(q, k, v, hq, hkv, dk, dv, n_ctx):
    B, T, _ = q.shape
    tq = TOK_TILE
    return pl.pallas_call(
        functools.partial(_attn_body, hq=hq, hkv=hkv, dk=dk, dv=dv, n_ctx_tiles=n_ctx // tq, n_ctx=n_ctx, n_all=T),
        grid=(B, T // tq),
        in_specs=[pl.BlockSpec((None, tq, hq * dk), lambda b, i: (b, i, 0)),
                  pl.BlockSpec((None, T, hkv * dk), lambda b, i: (b, 0, 0)),
                  pl.BlockSpec((None, T, hkv * dv), lambda b, i: (b, 0, 0))],
        out_specs=pl.BlockSpec((None, tq, hq * dv), lambda b, i: (b, i, 0)),
        out_shape=jax.ShapeDtypeStruct((B, T, hq * dv), BF16),
        compiler_params=_cp("parallel", "parallel"),
        name="attention",
    )(q, k, v)


def _rope_tables(T, n_ctx, n_rot):
    quarter = n_rot // 4
    t = jnp.arange(T - n_ctx)
    row = (t // GRID_W).astype(F32)
    col = (t % GRID_W).astype(F32)
    inv = ROPE_THETA ** (-jnp.arange(quarter, dtype=F32) / quarter)
    ar = row[:, None] * inv[None, :]
    ac = col[:, None] * inv[None, :]
    pad = LANES - n_rot
    cos = jnp.concatenate([jnp.cos(ar), jnp.cos(ar), jnp.cos(ac), jnp.cos(ac), jnp.ones((T - n_ctx, pad), F32)], axis=1)
    sin = jnp.concatenate([-jnp.sin(ar), jnp.sin(ar), -jnp.sin(ac), jnp.sin(ac), jnp.zeros((T - n_ctx, pad), F32)], axis=1)
    cos = jnp.concatenate([jnp.ones((n_ctx, LANES), F32), cos], axis=0)
    sin = jnp.concatenate([jnp.zeros((n_ctx, LANES), F32), sin], axis=0)
    return cos, sin


def _moe_body(te_ref, first_ref, slot_ref, nxt_ref, nv_ref, hs_ref, w1_hbm, w3_hbm, w2_hbm, y_ref,
              f1, f3, f2, c1, c3, c2, sem, *, layer):
    i = pl.program_id(0)

    def weight_copies(e, s):
        return (pltpu.make_async_copy(w1_hbm.at[layer, e], f1.at[s], sem.at[s, 0]),
                pltpu.make_async_copy(w3_hbm.at[layer, e], f3.at[s], sem.at[s, 1]),
                pltpu.make_async_copy(w2_hbm.at[layer, e], f2.at[s], sem.at[s, 2]))

    @pl.when(i == 0)
    def _():
        for cp in weight_copies(te_ref[0], 0):
            cp.start()

    @pl.when(first_ref[i] == 1)
    def _():
        s = slot_ref[i]
        for cp in weight_copies(te_ref[i], s):
            cp.wait()

        @pl.when(nxt_ref[i] >= 0)
        def _():
            for cp in weight_copies(nxt_ref[i], 1 - s):
                cp.start()

        c1[...] = f1[s].astype(BF16)
        c3[...] = f3[s].astype(BF16)
        c2[...] = f2[s].astype(BF16)

    @pl.when(i < nv_ref[0])
    def _():
        hs = hs_ref[...].astype(BF16)
        a = jnp.dot(hs, c1[...], preferred_element_type=F32)
        b = jnp.dot(hs, c3[...], preferred_element_type=F32)
        act = a * _sigmoid(a) * b
        y = jnp.dot(act.astype(BF16), c2[...], preferred_element_type=F32)
        y_ref[...] = y.astype(y_ref.dtype)

    @pl.when(i >= nv_ref[0])
    def _():
        y_ref[...] = jnp.zeros_like(y_ref)


def _moe_experts(tile_expert, run_first, run_slot, run_next, n_valid, hs, w1, w3, w2, layer):
    NP, D = hs.shape
    tm = MOE_TM
    DE = w1.shape[-1]
    grid_spec = pltpu.PrefetchScalarGridSpec(
        num_scalar_prefetch=5,
        grid=(NP // tm,),
        in_specs=[pl.BlockSpec((tm, D), lambda i, te, fi, sl, nx, nv: (jnp.minimum(i, nv[0] - 1), 0)),
                  pl.BlockSpec(memory_space=pl.ANY),
                  pl.BlockSpec(memory_space=pl.ANY),
                  pl.BlockSpec(memory_space=pl.ANY)],
        out_specs=pl.BlockSpec((tm, D), lambda i, *_: (i, 0)),
        scratch_shapes=[pltpu.VMEM((2, D, DE), F32), pltpu.VMEM((2, D, DE), F32), pltpu.VMEM((2, DE, D), F32),
                        pltpu.VMEM((D, DE), BF16), pltpu.VMEM((D, DE), BF16), pltpu.VMEM((DE, D), BF16),
                        pltpu.SemaphoreType.DMA((2, 3))],
    )
    return pl.pallas_call(
        functools.partial(_moe_body, layer=layer),
        grid_spec=grid_spec,
        out_shape=jax.ShapeDtypeStruct((NP, D), BF16),
        compiler_params=_cp("arbitrary"),
        name="moe_experts",
    )(tile_expert, run_first, run_slot, run_next, n_valid, hs, w1, w3, w2)


def _moe(h2, re, w1, w3, w2, layer):
    N = h2.shape[0]
    tm = MOE_TM
    n_tiles = (2 * N) // tm + N_EXPERTS
    e_flat = re.reshape(2 * N)
    onehot = (e_flat[:, None] == jnp.arange(N_EXPERTS, dtype=jnp.int32)[None, :]).astype(jnp.int32)
    csum = jnp.cumsum(onehot, axis=0)
    rank = jnp.sum(onehot * (csum - 1), axis=1)
    counts = csum[-1]
    ptiles = (counts + tm - 1) // tm
    tile_end = jnp.cumsum(ptiles)
    tile_start = tile_end - ptiles
    pos = tile_start[e_flat] * tm + rank
    n_valid = tile_end[-1:].astype(jnp.int32)
    tile_ids = jnp.arange(n_tiles, dtype=jnp.int32)
    tile_expert = jnp.minimum(jnp.sum((tile_end[None, :] <= tile_ids[:, None]).astype(jnp.int32), axis=1), N_EXPERTS - 1)
    experts = jnp.arange(N_EXPERTS, dtype=jnp.int32)
    owns = ptiles > 0
    run_no = jnp.cumsum(owns.astype(jnp.int32)) - 1
    later = jnp.where(owns[None, :] & (experts[None, :] > experts[:, None]), experts[None, :], N_EXPERTS)
    next_run = jnp.min(later, axis=1)
    next_run = jnp.where(next_run < N_EXPERTS, next_run, -1).astype(jnp.int32)
    run_first = ((tile_ids == tile_start[tile_expert]) & (tile_ids < n_valid[0])).astype(jnp.int32)
    run_slot = (run_no[tile_expert] % 2).astype(jnp.int32)
    run_next = next_run[tile_expert]
    spread = jnp.arange(n_tiles * tm, dtype=jnp.int32) % N
    src = spread.at[pos].set(jnp.arange(2 * N, dtype=jnp.int32) // 2)
    rows = lambda a, idx: a.at[idx].get(mode="promise_in_bounds")
    hs = rows(h2, src)
    y = _moe_experts(tile_expert, run_first, run_slot, run_next, n_valid, hs, w1, w3, w2, layer)
    pos2 = pos.reshape(N, 2)
    return rows(y, pos2[:, 0]), rows(y, pos2[:, 1])


def kernel(x, c, ctx, c_ctx, mod_w, mod_b, norm1_g, norm2_g, w_in, w_out, shift_prev, shift_next, decay_w0, decay_up, iclr_a0, iclr_up, gate_up, k_k, k_a, r_k, gn_g, gn_b, q_norm_g, k_norm_g, mla_q_norm_g, mla_w_uq, mla_kv_norm_g, mla_w_ukv, router_gw, router_gb, router_ew, router_eb, exp_w1, exp_w3, exp_w2, final_norm_g):
    B, S, D = x.shape
    C = ctx.shape[1]
    T = C + S
    depth = mod_w.shape[0]
    assert C == TOK_TILE and S % TOK_TILE == 0 and B <= CTX_ROW
    n_ctx_tiles = C // TOK_TILE

    cc = jnp.zeros((SUBLANES, D), F32).at[:B].set(c).at[CTX_ROW].set(c_ctx)
    mods_all = _modulation(cc, mod_w, mod_b).reshape(depth, SUBLANES * N_MOD, 1, D)
    X, h = _join_norm(ctx, x, norm1_g[0], mods_all[0], 0, 1)

    cos_g, sin_g = _rope_tables(T, C, GQA_HEAD)
    cos_m, sin_m = _rope_tables(T, C, MLA_ROPE)
    hid = jnp.arange(RWKV_W) // RWKV_HEAD
    head_sum = (hid[:, None] == hid[None, :]).astype(F32)

    for l in range(depth):
        mods = mods_all[l]
        w_r = w_in[l][:, :RWKV_COLS].astype(BF16)
        w_g = w_in[l][:, RWKV_COLS:RWKV_COLS + GQA_COLS].astype(BF16)
        w_m = jnp.pad(w_in[l][:, RWKV_COLS + GQA_COLS:], ((0, 0), (0, MLA_COLS_PAD - MLA_COLS))).astype(BF16)

        pr = _matmul(h.reshape(B * T, D), w_r).reshape(B, T, RWKV_COLS)

        wup_pad = jnp.pad(decay_up[l], ((0, 0), (0, ICLR_RANK), (0, 0)))
        aup_pad = jnp.pad(iclr_up[l], ((0, 0), (DECAY_RANK, 0), (0, 0)))
        r, v, kh, lw, b, kt, g, bonus = _rwkv_prep(pr, shift_prev[l], shift_next[l], decay_w0[l], wup_pad,
                                                  iclr_a0[l], aup_pad, gate_up[l], k_k[l], k_a[l], r_k[l], head_sum)
        yf, yb = _rwkv_scan(r, v, kh, lw, b, kt, C)
        o_r = _rwkv_readout(yf, yb, bonus, g, gn_g[l], gn_b[l], head_sum)

        q, k, vv = _gqa_proj(h, w_g, q_norm_g[l], k_norm_g[l], cos_g, sin_g)
        o_g = _attention(q, k, vv, GQA_Q_HEADS, GQA_KV_HEADS, GQA_HEAD, GQA_HEAD, C)

        wq = mla_w_uq[l].reshape(MLA_Q_RANK, MLA_HEADS, MLA_NOPE + MLA_ROPE)
        wq = jnp.pad(wq, ((0, 0), (0, 0), (0, MLA_DK - MLA_NOPE - MLA_ROPE))).reshape(MLA_Q_RANK, MLA_HEADS * MLA_DK)
        wkv = mla_w_ukv[l].reshape(MLA_KV_RANK, MLA_HEADS, MLA_NOPE + MLA_V)
        wkv = jnp.concatenate([wkv[:, :, :MLA_NOPE].reshape(MLA_KV_RANK, -1), wkv[:, :, MLA_NOPE:].reshape(MLA_KV_RANK, -1)], axis=1)
        qm, km, vm = _mla_proj(h, w_m, mla_q_norm_g[l], wq.astype(BF16), mla_kv_norm_g[l], wkv.astype(BF16), cos_m, sin_m)
        o_m = _attention(qm, km, vm, MLA_HEADS, MLA_HEADS, MLA_DK, MLA_V, C)

        last = l + 1 == depth
        wr = jnp.pad(jnp.concatenate([router_gw[l], router_ew[l]], axis=1), ((0, 0), (0, LANES - N_GROUPS - N_EXPERTS)))
        br = jnp.pad(jnp.concatenate([router_gb[l], router_eb[l]]), (0, LANES - N_GROUPS - N_EXPERTS)).reshape(1, LANES)
        X, h2, rw, re = _out_proj_router([o_r, o_g, o_m], w_out[l].astype(BF16), X, mods, C,
                                         TOK_TILE if last else OUT_TILE, last, norm2_g[l], wr, br)
        rows = X.shape[1]
        ctx_tiles = 0 if last else n_ctx_tiles

        y0, y1 = _moe(h2.reshape(B * rows, D), re.reshape(B * rows, LANES)[:, :2], exp_w1, exp_w3, exp_w2, l)
        y0 = y0.reshape(B, rows, D)
        y1 = y1.reshape(B, rows, D)
        if last:
            out = _moe_residual_final(X, y0, y1, rw, mods, 5, ctx_tiles, final_norm_g)
        else:
            X, h = _moe_residual_norm(X, y0, y1, rw, mods, 5, ctx_tiles, norm1_g[l + 1], mods_all[l + 1])
    return out
```

```python
import functools
import math

import jax
import jax.numpy as jnp
from jax import lax
from jax.experimental import pallas as pl
from jax.experimental.pallas import tpu as pltpu

F32 = jnp.float32
BF16 = jnp.bfloat16

V7X_VMEM_BYTES = 64 * 1024 * 1024
VMEM_LIMIT = V7X_VMEM_BYTES - 8 * 1024 * 1024
LANES = 128
SUBLANES = 8

GRID_W = 64
ROPE_THETA = 10000.0
NORM_EPS = 1e-6
GN_EPS = 64e-5
DECAY_SCALE = math.exp(-0.5)
LOG2E = math.log2(math.e)

RWKV_HEAD = 64
RWKV_W = 512
DECAY_RANK = 64
ICLR_RANK = 64
GATE_RANK = 128
RWKV_COLS = 3 * RWKV_W + DECAY_RANK + ICLR_RANK + GATE_RANK
LOWRANK_OFF = 3 * RWKV_W
GATE_OFF = LOWRANK_OFF + DECAY_RANK + ICLR_RANK
CHUNK = 64
SCAN_CHUNKS = 2

GQA_HEAD = 128
GQA_Q_HEADS = 8
GQA_KV_HEADS = 2
GQA_Q_COLS = GQA_Q_HEADS * GQA_HEAD
GQA_KV_COLS = GQA_KV_HEADS * GQA_HEAD
GQA_COLS = GQA_Q_COLS + 2 * GQA_KV_COLS

MLA_HEADS = 4
MLA_NOPE = 128
MLA_ROPE = 64
MLA_V = 128
MLA_Q_RANK = 384
MLA_KV_RANK = 256
MLA_COLS = MLA_Q_RANK + MLA_KV_RANK + MLA_ROPE
MLA_COLS_PAD = 768
MLA_DK = 2 * LANES

N_GROUPS = 4
EXPERTS_PER_GROUP = 8
N_EXPERTS = 32
D_EXPERT = 256
MOE_TM = 256

TOK_TILE = 256
PROJ_TILE = 768
OUT_TILE = 384
N_MOD = 6
CTX_ROW = 4


def _cp(*sem):
    return pltpu.CompilerParams(dimension_semantics=sem, vmem_limit_bytes=VMEM_LIMIT)


def _sigmoid(x):
    return 1.0 / (1.0 + jnp.exp(-x))


def _bf16_terms(x, n):
    terms = []
    for _ in range(n):
        t = x.astype(BF16)
        terms.append(t)
        x = x - t.astype(F32)
    return terms


def _dot_terms(x, w, nx, nw):
    xs = _bf16_terms(x, nx)
    ws = _bf16_terms(w, nw)
    acc = None
    for i in range(nx):
        for j in range(nw):
            if i + j < max(nx, nw):
                p = jnp.dot(xs[i], ws[j], preferred_element_type=F32)
                acc = p if acc is None else acc + p
    return acc


def _mod_body(c_ref, w_ref, b_ref, o_ref):
    c = c_ref[...]
    s = c * _sigmoid(c)
    o_ref[...] = _dot_terms(s, w_ref[...], 2, 2) + b_ref[...]


def _modulation(cc, mod_w, mod_b, layer):
    L, D, N = mod_w.shape
    tn = 1024
    return pl.pallas_call(
        _mod_body,
        grid=(N // tn,),
        in_specs=[pl.BlockSpec((SUBLANES, D), lambda j: (0, 0)),
                  pl.BlockSpec((None, D, tn), lambda j: (layer, 0, j)),
                  pl.BlockSpec((None, 1, tn), lambda j: (layer, 0, j))],
        out_specs=pl.BlockSpec((SUBLANES, tn), lambda j: (0, j)),
        out_shape=jax.ShapeDtypeStruct((SUBLANES, N), F32),
        compiler_params=_cp("parallel"),
        name="modulation",
    )(cc, mod_w, mod_b.reshape(L, 1, N))


def _mod_spec(which, n_ctx_tiles, D):
    return pl.BlockSpec((None, 1, D), lambda b, i: (jnp.where(i < n_ctx_tiles, CTX_ROW, b) * N_MOD + which, 0, 0))


def _rms(x, g):
    return x * lax.rsqrt(jnp.mean(x * x, axis=-1, keepdims=True) + NORM_EPS) * g


def _join_norm_body(ctx_ref, x_ref, g_ref, sh_ref, sc_ref, x_o, h_o, *, n_ctx_tiles):
    i = pl.program_id(1)

    def emit(src_ref):
        x = src_ref[...]
        x_o[...] = x
        h_o[...] = (_rms(x, g_ref[...]) * (1.0 + sc_ref[...]) + sh_ref[...]).astype(h_o.dtype)

    @pl.when(i < n_ctx_tiles)
    def _():
        emit(ctx_ref)

    @pl.when(i >= n_ctx_tiles)
    def _():
        emit(x_ref)


def _route(logits):
    lane = lax.broadcasted_iota(jnp.int32, logits.shape, 1)
    lane_f = lane.astype(F32)
    neg = jnp.float32(-1e30)
    far = jnp.float32(1e9)
    first_at = lambda hit: jnp.min(jnp.where(hit, lane_f, far), axis=-1, keepdims=True).astype(jnp.int32)
    gl = jnp.where(lane < N_GROUPS, logits, neg)
    gmax = jnp.max(gl, axis=-1, keepdims=True)
    gidx = first_at(gl == gmax)
    p_sel = 1.0 / jnp.sum(jnp.exp(gl - gmax), axis=-1, keepdims=True)
    lo = N_GROUPS + gidx * EXPERTS_PER_GROUP
    el = jnp.where((lane >= lo) & (lane < lo + EXPERTS_PER_GROUP), logits, neg)
    m1 = jnp.max(el, axis=-1, keepdims=True)
    i1 = first_at(el == m1)
    el2 = jnp.where(lane == i1, neg, el)
    m2 = jnp.max(el2, axis=-1, keepdims=True)
    i2 = first_at(el2 == m2)
    t = jnp.exp(m2 - m1)
    w1 = p_sel / (1.0 + t)
    w2 = p_sel * t / (1.0 + t)
    rw = jnp.where(lane == 0, w1, jnp.where(lane == 1, w2, 0.0))
    re = jnp.where(lane == 0, i1 - N_GROUPS, jnp.where(lane == 1, i2 - N_GROUPS, 0))
    return rw, re


def _join_norm(ctx, x, g, mods, shift_i, scale_i):
    B, C, D = ctx.shape
    tt = TOK_TILE
    n_ctx_tiles = C // tt
    T = C + x.shape[1]
    tok = pl.BlockSpec((None, tt, D), lambda b, i: (b, i, 0))
    return pl.pallas_call(
        functools.partial(_join_norm_body, n_ctx_tiles=n_ctx_tiles),
        grid=(B, T // tt),
        in_specs=[pl.BlockSpec((None, tt, D), lambda b, i: (b, jnp.minimum(i, n_ctx_tiles - 1), 0)),
                  pl.BlockSpec((None, tt, D), lambda b, i: (b, jnp.maximum(i - n_ctx_tiles, 0), 0)),
                  pl.BlockSpec((1, D), lambda b, i: (0, 0)),
                  _mod_spec(shift_i, n_ctx_tiles, D),
                  _mod_spec(scale_i, n_ctx_tiles, D)],
        out_specs=[tok, tok],
        out_shape=[jax.ShapeDtypeStruct((B, T, D), F32), jax.ShapeDtypeStruct((B, T, D), BF16)],
        compiler_params=_cp("parallel", "parallel"),
        name="join_norm",
    )(ctx, x, g.reshape(1, D), mods, mods)


def _mm_body(a_ref, w_ref, o_ref):
    o_ref[...] = jnp.dot(a_ref[...], w_ref[...], preferred_element_type=F32).astype(o_ref.dtype)


def _matmul(a, w, tm=PROJ_TILE):
    M, K = a.shape
    N = w.shape[1]
    assert M % tm == 0
    return pl.pallas_call(
        _mm_body,
        grid=(M // tm,),
        in_specs=[pl.BlockSpec((tm, K), lambda i: (i, 0)),
                  pl.BlockSpec((K, N), lambda i: (0, 0))],
        out_specs=pl.BlockSpec((tm, N), lambda i: (i, 0)),
        out_shape=jax.ShapeDtypeStruct((M, N), F32),
        compiler_params=_cp("parallel"),
        name="token_matmul",
    )(a, w)


def _out_proj_router_body(*refs, n_parts, n_ctx, tm, tile0):
    a_refs = refs[:n_parts]
    w_ref, x_ref, ml_ref, mc_ref, g_ref, wr_ref, br_ref, x_o, h_o, rw_o, re_o = refs[n_parts:]
    acc = None
    k0 = 0
    for a_ref in a_refs:
        k1 = k0 + a_ref.shape[-1]
        part = jnp.dot(a_ref[...], w_ref[k0:k1, :], preferred_element_type=F32)
        acc = part if acc is None else acc + part
        k0 = k1
    row = (pl.program_id(1) + tile0) * tm + lax.broadcasted_iota(jnp.int32, (tm, 1), 0)
    is_ctx = row < n_ctx
    mod = lambda which: jnp.where(is_ctx, mc_ref[which], ml_ref[which])
    x = x_ref[...] + mod(2) * acc
    x_o[...] = x
    h = _rms(x, g_ref[...]) * (1.0 + mod(4)) + mod(3)
    h_o[...] = h
    h_hi, h_lo = _bf16_terms(h, 2)
    w_hi, w_lo = _bf16_terms(wr_ref[...], 2)
    both = jnp.dot(h_hi, jnp.concatenate([w_hi, w_lo], axis=1), preferred_element_type=F32)
    logits = (both[:, 0:LANES] + both[:, LANES:2 * LANES]
              + jnp.dot(h_lo, w_hi, preferred_element_type=F32) + br_ref[...])
    rw, re = _route(logits)
    rw_o[...] = rw
    re_o[...] = re


def _out_proj_router(parts, w, X, mods, n_ctx, tm, latents_only, g2, wr, br):
    B, T, D = X.shape
    tile0 = n_ctx // tm if latents_only else 0
    assert T % tm == 0 and (n_ctx % tm == 0 or not latents_only)
    assert sum(p.shape[-1] for p in parts) == w.shape[0]
    rows_out = T - tile0 * tm
    mods6 = mods.reshape(SUBLANES, N_MOD, 1, D)
    tok_in = lambda wd: pl.BlockSpec((None, tm, wd), lambda b, i: (b, i + tile0, 0))
    tok_out = lambda wd: pl.BlockSpec((None, tm, wd), lambda b, i: (b, i, 0))
    shape = lambda wd, dt: jax.ShapeDtypeStruct((B, rows_out, wd), dt)
    return pl.pallas_call(
        functools.partial(_out_proj_router_body, n_parts=len(parts), n_ctx=n_ctx, tm=tm, tile0=tile0),
        grid=(B, T // tm - tile0),
        in_specs=[tok_in(p.shape[-1]) for p in parts] + [
                  pl.BlockSpec(w.shape, lambda b, i: (0, 0)),
                  tok_in(D),
                  pl.BlockSpec((None, N_MOD, 1, D), lambda b, i: (b, 0, 0, 0)),
                  pl.BlockSpec((None, N_MOD, 1, D), lambda b, i: (CTX_ROW, 0, 0, 0)),
                  pl.BlockSpec((1, D), lambda b, i: (0, 0)),
                  pl.BlockSpec((D, LANES), lambda b, i: (0, 0)),
                  pl.BlockSpec((1, LANES), lambda b, i: (0, 0))],
        out_specs=[tok_out(D), tok_out(D), tok_out(LANES), tok_out(LANES)],
        out_shape=[shape(D, F32), shape(D, F32), shape(LANES, F32), shape(LANES, jnp.int32)],
        compiler_params=_cp("parallel", "parallel"),
        name="out_proj_router",
    )(*parts, w, X, mods6, mods6, g2.reshape(1, D), wr, br)


def _moe_residual(x_ref, y0_ref, y1_ref, rw_ref, gate_ref):
    rw = rw_ref[...]
    moe = rw[:, 0:1] * y0_ref[...].astype(F32) + rw[:, 1:2] * y1_ref[...].astype(F32)
    return x_ref[...] + gate_ref[...] * moe


def _moe_residual_norm_body(x_ref, y0_ref, y1_ref, rw_ref, gate_ref, g_ref, sh_ref, sc_ref, x_o, h_o):
    x = _moe_residual(x_ref, y0_ref, y1_ref, rw_ref, gate_ref)
    x_o[...] = x
    h_o[...] = (_rms(x, g_ref[...]) * (1.0 + sc_ref[...]) + sh_ref[...]).astype(h_o.dtype)


def _moe_residual_final_body(x_ref, y0_ref, y1_ref, rw_ref, gate_ref, g_ref, o_ref):
    o_ref[...] = _rms(_moe_residual(x_ref, y0_ref, y1_ref, rw_ref, gate_ref), g_ref[...])


def _moe_residual_norm(X, Y0, Y1, rw, mods, gate_i, n_ctx_tiles, g_next, mods_next):
    B, T, D = X.shape
    tt = TOK_TILE
    blk = pl.BlockSpec((None, tt, D), lambda b, i: (b, i, 0))
    return pl.pallas_call(
        _moe_residual_norm_body,
        grid=(B, T // tt),
        in_specs=[blk, blk, blk, pl.BlockSpec((None, tt, LANES), lambda b, i: (b, i, 0)),
                  _mod_spec(gate_i, n_ctx_tiles, D), pl.BlockSpec((1, D), lambda b, i: (0, 0)),
                  _mod_spec(0, n_ctx_tiles, D), _mod_spec(1, n_ctx_tiles, D)],
        out_specs=[blk, blk],
        out_shape=[jax.ShapeDtypeStruct((B, T, D), F32), jax.ShapeDtypeStruct((B, T, D), BF16)],
        compiler_params=_cp("parallel", "parallel"),
        name="moe_residual_norm",
    )(X, Y0, Y1, rw, mods, g_next.reshape(1, D), mods_next, mods_next)


def _moe_residual_final(X, Y0, Y1, rw, mods, gate_i, n_ctx_tiles, g_final):
    B, T, D = X.shape
    tt = TOK_TILE
    S = T - n_ctx_tiles * tt
    lat = lambda w: pl.BlockSpec((None, tt, w), lambda b, i: (b, i + n_ctx_tiles, 0))
    return pl.pallas_call(
        _moe_residual_final_body,
        grid=(B, S // tt),
        in_specs=[lat(D), lat(D), lat(D), lat(LANES),
                  pl.BlockSpec((None, 1, D), lambda b, i: (b * N_MOD + gate_i, 0, 0)),
                  pl.BlockSpec((1, D), lambda b, i: (0, 0))],
        out_specs=pl.BlockSpec((None, tt, D), lambda b, i: (b, i, 0)),
        out_shape=jax.ShapeDtypeStruct((B, S, D), F32),
        compiler_params=_cp("parallel", "parallel"),
        name="moe_residual_final",
    )(X, Y0, Y1, rw, mods, g_final.reshape(1, D))


def _rwkv_prep_body(p_ref, pv_ref, nx_ref, mup_ref, mun_ref, w0_ref, wup_ref, a0_ref, aup_ref, gup_ref,
                    kk_ref, ka_ref, rk_ref, e_ref,
                    r_o, v_o, kh_o, lw_o, b_o, kt_o, g_o, bon_o, *, n_tiles, tt):
    i = pl.program_id(1)
    p = p_ref[...]
    seq_first = i <= 1
    seq_last = (i == 0) | (i == n_tiles - 1)
    prow = jnp.where(seq_first, 0.0, pv_ref[SUBLANES - 1:SUBLANES, :])
    nrow = jnp.where(seq_last, 0.0, nx_ref[0:1, :])
    rid = lax.broadcasted_iota(jnp.int32, (tt, 1), 0)
    prev = jnp.where(rid == 0, prow, pltpu.roll(p, 1, 0))
    nxt = jnp.where(rid == tt - 1, nrow, pltpu.roll(p, tt - 1, 0))
    z = p + mup_ref[...] * (prev - p) + mun_ref[...] * (nxt - p)

    W = RWKV_W
    r = z[:, 0:W]
    k = z[:, W:2 * W]
    v = z[:, 2 * W:3 * W]
    lowrank = z[:, LOWRANK_OFF:LOWRANK_OFF + LANES]
    gd = z[:, GATE_OFF:GATE_OFF + GATE_RANK]
    head_sum = e_ref[...]

    kap = k * kk_ref[...]
    ss = _dot_terms(kap * kap, head_sum, 2, 1)
    khat = kap * lax.rsqrt(ss + 1e-12)
    wd_t = jnp.tanh(lowrank)
    g_o[...] = _dot_terms(_sigmoid(gd), gup_ref[...], 1, 1)
    r_o[...] = r
    v_o[...] = v
    kh_o[...] = khat
    kt_sum = None
    for d in range(2):
        dec = _dot_terms(wd_t, wup_ref[d], 2, 2)
        lw_o[d] = -DECAY_SCALE * _sigmoid(w0_ref[d:d + 1, :] + dec)
        a = _sigmoid(a0_ref[d:d + 1, :] + _dot_terms(lowrank, aup_ref[d], 1, 1))
        kt = k * (1.0 + (a - 1.0) * ka_ref[...])
        kt_o[d] = kt
        b_o[d] = a * khat
        kt_sum = kt if kt_sum is None else kt_sum + kt
    bsum = _dot_terms(r * kt_sum * rk_ref[...], head_sum, 2, 1)
    bon_o[...] = bsum * v


def _rwkv_prep(pr, mu_prev, mu_next, w0, wup_pad, a0, aup_pad, g_up, k_k, k_a, r_k, head_sum):
    B, T, _ = pr.shape
    tt = TOK_TILE
    W = RWKV_W
    n_tiles = T // tt
    n8 = tt // SUBLANES
    row = lambda v: v.reshape(1, -1)
    full = lambda a: pl.BlockSpec(a.shape, lambda b, i: (0,) * a.ndim)
    tok = pl.BlockSpec((None, tt, W), lambda b, i: (b, i, 0))
    tok2 = pl.BlockSpec((None, 2, tt, W), lambda b, i: (b, 0, i, 0))
    s1 = jax.ShapeDtypeStruct((B, T, W), F32)
    s2 = jax.ShapeDtypeStruct((B, 2, T, W), F32)
    consts = [row(mu_prev), row(mu_next), w0, wup_pad, a0, aup_pad, g_up, row(k_k), row(k_a), row(r_k), head_sum]
    return pl.pallas_call(
        functools.partial(_rwkv_prep_body, n_tiles=n_tiles, tt=tt),
        grid=(B, n_tiles),
        in_specs=[pl.BlockSpec((None, tt, RWKV_COLS), lambda b, i: (b, i, 0)),
                  pl.BlockSpec((None, SUBLANES, RWKV_COLS), lambda b, i: (b, jnp.maximum(i * n8 - 1, 0), 0)),
                  pl.BlockSpec((None, SUBLANES, RWKV_COLS), lambda b, i: (b, jnp.minimum((i + 1) * n8, T // SUBLANES - 1), 0)),
                  ] + [full(a) for a in consts],
        out_specs=[tok, tok, tok, tok2, tok2, tok2, tok, tok],
        out_shape=[s1, s1, s1, s2, s2, s2, s1, s1],
        compiler_params=_cp("parallel", "parallel"),
        name="rwkv_prep",
    )(pr, pr, pr, *consts)


def _stack_heads(x):
    lane = lax.broadcasted_iota(jnp.int32, x.shape, 1)
    first = lane < RWKV_HEAD
    return jnp.concatenate([jnp.where(first, x, 0.0), jnp.where(first, 0.0, x)], axis=0)


def _dot(a, b):
    return jnp.dot(a.astype(BF16), b.astype(BF16), preferred_element_type=F32)


def _dot_nt(a, b):
    return lax.dot_general(a.astype(BF16), b.astype(BF16), (((1,), (1,)), ((), ())), preferred_element_type=F32)


def _dot_tn(a, b):
    return jnp.dot(a.T.astype(BF16), b.astype(BF16), preferred_element_type=F32)


def _chunk_operands(r, v, kh, lw, b, kt, reverse):
    L = CHUNK
    ti = lax.broadcasted_iota(jnp.int32, (L, L), 0)
    tj = lax.broadcasted_iota(jnp.int32, (L, L), 1)
    tri = jnp.where((ti <= tj) if reverse else (ti >= tj), 1.0, 0.0)
    lam = _dot_terms(tri, lw, 1, 3)
    tot = lam[0:1, :] if reverse else lam[L - 1:L, :]
    e_n = jnp.exp(-lam)
    e_g = jnp.exp(tot - lam)
    full = dict(A=kh * jnp.exp(lam - lw), R=r * jnp.exp(lam), Kn=kt * e_n, Bn=b * e_n, Kg=kt * e_g, Bg=b * e_g, V=v)
    e_tot = jnp.exp(tot)
    pairs = []
    for p in range(RWKV_W // LANES):
        sl = slice(p * LANES, (p + 1) * LANES)
        ops = {k: a[:, sl] for k, a in full.items()}
        ops["e_tot"] = e_tot[:, sl]
        ops["reverse"] = reverse
        pairs.append(ops)
    return pairs


def _chunk_masks(reverse):
    L = CHUNK
    t = lax.broadcasted_iota(jnp.int32, (L, 2 * L), 0)
    i = lax.broadcasted_iota(jnp.int32, (L, 2 * L), 1) & (L - 1)
    before = (i > t) if reverse else (i < t)
    return before, before | (i == t), i == t


def _chunks_prepare(chains):
    L = CHUNK
    P2 = 2 * L
    n = len(chains)
    masks = {rev: _chunk_masks(rev) for rev in {c["reverse"] for c in chains}}
    strict = [masks[c["reverse"]][0] for c in chains]
    incl = [masks[c["reverse"]][1] for c in chains]
    eye = masks[chains[0]["reverse"]][2]
    bi = lax.broadcasted_iota(jnp.int32, (P2, P2), 0)
    bj = lax.broadcasted_iota(jnp.int32, (P2, P2), 1)
    same_head = (bi >= L) == (bj >= L)
    eye2 = bi == bj
    stack = _stack_heads
    cat0 = lambda *xs: jnp.concatenate(xs, axis=0)
    cat1 = lambda *xs: jnp.concatenate(xs, axis=1)

    big = [_dot_nt(cat0(c["A"], c["R"]), cat0(stack(c["Bn"]), stack(c["Kn"]))) for c in chains]
    Mb = [jnp.where(strict[i], big[i][0:L, 0:P2], 0.0) for i in range(n)]
    Mkv = [jnp.where(strict[i], big[i][0:L, P2:2 * P2], 0.0) for i in range(n)]
    Pb = [jnp.where(incl[i], big[i][L:P2, 0:P2], 0.0) for i in range(n)]
    Pkv = [jnp.where(incl[i], big[i][L:P2, P2:2 * P2], 0.0) for i in range(n)]

    Pw = [-m for m in Mb]
    Tm = [jnp.where(eye, 1.0, 0.0) + p for p in Pw]
    Pw = [_dot(p, stack(p)) for p in Pw]
    for _ in range(int(math.log2(L)) - 2):
        PT = [_dot(cat0(p, t), stack(p)) for p, t in zip(Pw, Tm)]
        Tm = [t + pt[L:P2] for t, pt in zip(Tm, PT)]
        Pw = [pt[0:L] for pt in PT]
    Tm = [t + _dot(t, stack(p)) for t, p in zip(Tm, Pw)]

    Vs = [stack(c["V"]) for c in chains]
    MPV = [_dot(cat0(Mkv[i], Pkv[i]), Vs[i]) for i in range(n)]
    TAM = [_dot(Tm[i], cat1(stack(chains[i]["A"]), stack(MPV[i][0:L]))) for i in range(n)]
    PB = [_dot(Pb[i], cat1(stack(TAM[i][:, 0:P2]), stack(TAM[i][:, P2:2 * P2]))) for i in range(n)]
    BG = [_dot_tn(chains[i]["Bg"], TAM[i]) for i in range(n)]
    KV = [_dot_tn(chains[i]["Kg"], chains[i]["V"]) for i in range(n)]
    prepared = []
    for i in range(n):
        RA = chains[i]["R"] - PB[i][:, 0:P2]
        G2 = jnp.where(eye2, chains[i]["e_tot"], 0.0) - jnp.where(same_head, BG[i][:, 0:P2], 0.0)
        H2 = jnp.where(same_head, KV[i] - BG[i][:, P2:2 * P2], 0.0)
        prepared.append((cat0(RA, G2), MPV[i][L:P2] - PB[i][:, P2:2 * P2], H2))
    return prepared


def _chunk_advance(prepared, state):
    lhs, y0, h2 = prepared
    out = _dot(lhs, state)
    return out[0:CHUNK] + y0, out[CHUNK:] + h2


def _rwkv_scan_body(rf_ref, vf_ref, khf_ref, rb_ref, vb_ref, khb_ref, lwf_ref, bf_ref, ktf_ref, lwb_ref, bb_ref, ktb_ref,
                    yf_ref, yb_ref, s_ref):
    @pl.when(pl.program_id(1) == 0)
    def _():
        s_ref[...] = jnp.zeros_like(s_ref)

    n_pairs = RWKV_W // LANES
    fwd_refs = (rf_ref, vf_ref, khf_ref, lwf_ref, bf_ref, ktf_ref)
    bwd_refs = (rb_ref, vb_ref, khb_ref, lwb_ref, bb_ref, ktb_ref)
    slots = []
    for c in range(SCAN_CHUNKS):
        lo_f = c * CHUNK
        lo_b = (SCAN_CHUNKS - 1 - c) * CHUNK
        slots.append((_chunk_operands(*(ref[lo_f:lo_f + CHUNK, :] for ref in fwd_refs), False), lo_f,
                      _chunk_operands(*(ref[lo_b:lo_b + CHUNK, :] for ref in bwd_refs), True), lo_b))
    prepared = _chunks_prepare([ch for s in slots for ch in s[0] + s[2]])
    states = [s_ref[i] for i in range(2 * n_pairs)]
    for c, (_, lo_f, _, lo_b) in enumerate(slots):
        ys = []
        for i in range(2 * n_pairs):
            y, states[i] = _chunk_advance(prepared[c * 2 * n_pairs + i], states[i])
            ys.append(y)
        yf_ref[lo_f:lo_f + CHUNK, :] = jnp.concatenate(ys[:n_pairs], axis=1)
        yb_ref[lo_b:lo_b + CHUNK, :] = jnp.concatenate(ys[n_pairs:], axis=1)
    for i in range(2 * n_pairs):
        s_ref[i] = states[i]


def _rwkv_scan(r, v, kh, lw, b, kt, n_ctx):
    B, T, W = r.shape
    blk = SCAN_CHUNKS * CHUNK
    assert T % blk == 0 and n_ctx % blk == 0
    nc = T // blk
    ncc = n_ctx // blk
    rev = lambda j: jnp.where(j < ncc, ncc - 1 - j, nc + ncc - 1 - j)
    fwd1 = pl.BlockSpec((None, blk, W), lambda bb, j: (bb, j, 0))
    bwd1 = pl.BlockSpec((None, blk, W), lambda bb, j: (bb, rev(j), 0))
    fwd2 = pl.BlockSpec((None, None, blk, W), lambda bb, j: (bb, 0, j, 0))
    bwd2 = pl.BlockSpec((None, None, blk, W), lambda bb, j: (bb, 1, rev(j), 0))
    out = jax.ShapeDtypeStruct((B, T, W), F32)
    return pl.pallas_call(
        _rwkv_scan_body,
        grid=(B, nc),
        in_specs=[fwd1, fwd1, fwd1, bwd1, bwd1, bwd1, fwd2, fwd2, fwd2, bwd2, bwd2, bwd2],
        out_specs=[fwd1, bwd1],
        out_shape=[out, out],
        scratch_shapes=[pltpu.VMEM((2 * W // LANES, LANES, LANES), F32)],
        compiler_params=_cp("parallel", "arbitrary"),
        name="rwkv_scan",
    )(r, v, kh, r, v, kh, lw, b, kt, lw, b, kt)


def _rwkv_readout_body(yf_ref, yb_ref, bon_ref, g_ref, gng_ref, gnb_ref, e_ref, o_ref):
    y = yf_ref[...] + yb_ref[...]
    head_mean = e_ref[...] * (1.0 / RWKV_HEAD)
    mu = _dot_terms(y, head_mean, 2, 1)
    yc = y - mu
    var = _dot_terms(yc * yc, head_mean, 2, 1)
    yn = yc * lax.rsqrt(var + GN_EPS) * gng_ref[...] + gnb_ref[...]
    o_ref[...] = ((yn + bon_ref[...]) * g_ref[...]).astype(o_ref.dtype)


def _rwkv_readout(yf, yb, bonus, g, gn_g, gn_b, head_sum):
    B, T, W = yf.shape
    tt = TOK_TILE
    tok = pl.BlockSpec((None, tt, W), lambda b, i: (b, i, 0))
    row = pl.BlockSpec((1, W), lambda b, i: (0, 0))
    return pl.pallas_call(
        _rwkv_readout_body,
        grid=(B, T // tt),
        in_specs=[tok, tok, tok, tok, row, row, pl.BlockSpec((W, W), lambda b, i: (0, 0))],
        out_specs=tok,
        out_shape=jax.ShapeDtypeStruct((B, T, W), BF16),
        compiler_params=_cp("parallel", "parallel"),
        name="rwkv_readout",
    )(yf, yb, bonus, g, gn_g.reshape(1, W), gn_b.reshape(1, W), head_sum)


def _rope(y, cos, sin_signed, quarter):
    lane = lax.broadcasted_iota(jnp.int32, y.shape, 1)
    first = (lane & (2 * quarter - 1)) < quarter
    partner = jnp.where(first, pltpu.roll(y, LANES - quarter, 1), pltpu.roll(y, quarter, 1))
    return y * cos + partner * sin_signed


def _gqa_prep_body(p, qg_ref, kg_ref, cos_ref, sin_ref, q_o, k_o, v_o):
    cos = cos_ref[...]
    sin = sin_ref[...]
    scale = GQA_HEAD ** -0.5 * LOG2E
    for h in range(GQA_Q_HEADS):
        sl = slice(h * GQA_HEAD, (h + 1) * GQA_HEAD)
        q = _rms(p[:, sl], qg_ref[...])
        q_o[:, sl] = (_rope(q, cos, sin, GQA_HEAD // 4) * scale).astype(q_o.dtype)
    for h in range(GQA_KV_HEADS):
        sl = slice(h * GQA_HEAD, (h + 1) * GQA_HEAD)
        k = _rms(p[:, GQA_Q_COLS + h * GQA_HEAD:GQA_Q_COLS + (h + 1) * GQA_HEAD], kg_ref[...])
        k_o[:, sl] = _rope(k, cos, sin, GQA_HEAD // 4).astype(k_o.dtype)
    v_o[...] = p[:, GQA_Q_COLS + GQA_KV_COLS:GQA_COLS].astype(v_o.dtype)


def _row_blocks(n_rows):
    return [slice(r, r + TOK_TILE) for r in range(0, n_rows, TOK_TILE)]


def _gqa_proj_body(h_ref, w_ref, qg_ref, kg_ref, cos_ref, sin_ref, q_o, k_o, v_o):
    for rows in _row_blocks(h_ref.shape[0]):
        p = jnp.dot(h_ref[rows, :], w_ref[...], preferred_element_type=F32)
        _gqa_prep_body(p, qg_ref, kg_ref, cos_ref.at[rows, :], sin_ref.at[rows, :],
                       q_o.at[rows, :], k_o.at[rows, :], v_o.at[rows, :])


def _gqa_proj(h, w, q_norm_g, k_norm_g, cos, sin):
    B, T, D = h.shape
    tt = PROJ_TILE
    tok = lambda w_: pl.BlockSpec((None, tt, w_), lambda b, i: (b, i, 0))
    row = pl.BlockSpec((1, GQA_HEAD), lambda b, i: (0, 0))
    tab = pl.BlockSpec((tt, LANES), lambda b, i: (i, 0))
    return pl.pallas_call(
        _gqa_proj_body,
        grid=(B, T // tt),
        in_specs=[tok(D), pl.BlockSpec(w.shape, lambda b, i: (0, 0)), row, row, tab, tab],
        out_specs=[tok(GQA_Q_COLS), tok(GQA_KV_COLS), tok(GQA_KV_COLS)],
        out_shape=[jax.ShapeDtypeStruct((B, T, GQA_Q_COLS), BF16),
                   jax.ShapeDtypeStruct((B, T, GQA_KV_COLS), BF16),
                   jax.ShapeDtypeStruct((B, T, GQA_KV_COLS), BF16)],
        compiler_params=_cp("parallel", "parallel"),
        name="gqa_proj",
    )(h, w, q_norm_g.reshape(1, -1), k_norm_g.reshape(1, -1), cos, sin)


def _mla_prep_body(p, qg_ref, wq_ref, kvg_ref, wkv_ref, cos_ref, sin_ref, q_o, k_o, v_o):
    cos = cos_ref[...]
    sin = sin_ref[...]
    scale = (MLA_NOPE + MLA_ROPE) ** -0.5 * LOG2E
    cq = _rms(p[:, 0:MLA_Q_RANK], qg_ref[...])
    q = jnp.dot(cq.astype(BF16), wq_ref[...], preferred_element_type=F32) * scale
    ckv = _rms(p[:, MLA_Q_RANK:MLA_Q_RANK + MLA_KV_RANK], kvg_ref[...])
    kv = jnp.dot(ckv.astype(BF16), wkv_ref[...], preferred_element_type=F32)
    kr = _rope(p[:, MLA_Q_RANK + MLA_KV_RANK:MLA_COLS_PAD], cos, sin, MLA_ROPE // 4).astype(k_o.dtype)
    for h in range(MLA_HEADS):
        lo = h * MLA_DK
        q_o[:, lo:lo + LANES] = q[:, lo:lo + LANES].astype(q_o.dtype)
        q_o[:, lo + LANES:lo + MLA_DK] = _rope(q[:, lo + LANES:lo + MLA_DK], cos, sin, MLA_ROPE // 4).astype(q_o.dtype)
        k_o[:, lo:lo + LANES] = kv[:, h * MLA_NOPE:(h + 1) * MLA_NOPE].astype(k_o.dtype)
        k_o[:, lo + LANES:lo + MLA_DK] = kr
    v_o[...] = kv[:, MLA_HEADS * MLA_NOPE:].astype(v_o.dtype)


def _mla_proj_body(h_ref, w_ref, qg_ref, wq_ref, kvg_ref, wkv_ref, cos_ref, sin_ref, q_o, k_o, v_o):
    for rows in _row_blocks(h_ref.shape[0]):
        p = jnp.dot(h_ref[rows, :], w_ref[...], preferred_element_type=F32)
        _mla_prep_body(p, qg_ref, wq_ref, kvg_ref, wkv_ref, cos_ref.at[rows, :], sin_ref.at[rows, :],
                       q_o.at[rows, :], k_o.at[rows, :], v_o.at[rows, :])


def _mla_proj(h, w, q_norm_g, wq, kv_norm_g, wkv, cos, sin):
    B, T, D = h.shape
    tt = PROJ_TILE
    tok = lambda w_: pl.BlockSpec((None, tt, w_), lambda b, i: (b, i, 0))
    full = lambda a: pl.BlockSpec(a.shape, lambda b, i: (0,) * a.ndim)
    tab = pl.BlockSpec((tt, LANES), lambda b, i: (i, 0))
    qg = q_norm_g.reshape(1, -1)
    kvg = kv_norm_g.reshape(1, -1)
    return pl.pallas_call(
        _mla_proj_body,
        grid=(B, T // tt),
        in_specs=[tok(D), full(w), full(qg), full(wq), full(kvg), full(wkv), tab, tab],
        out_specs=[tok(MLA_HEADS * MLA_DK), tok(MLA_HEADS * MLA_DK), tok(MLA_HEADS * MLA_V)],
        out_shape=[jax.ShapeDtypeStruct((B, T, MLA_HEADS * MLA_DK), BF16),
                   jax.ShapeDtypeStruct((B, T, MLA_HEADS * MLA_DK), BF16),
                   jax.ShapeDtypeStruct((B, T, MLA_HEADS * MLA_V), BF16)],
        compiler_params=_cp("parallel", "parallel"),
        name="mla_proj",
    )(h, w, qg, wq, kvg, wkv, cos, sin)


def _attn_body(q_ref, k_ref, v_ref, o_ref, *, hq, hkv, dk, dv, n_ctx_tiles, n_ctx, n_all):
    rep = hq // hkv

    def run(nk):
        for g in range(hkv):
            kg = k_ref[0:nk, g * dk:(g + 1) * dk]
            vg = v_ref[0:nk, g * dv:(g + 1) * dv]
            v_aug = jnp.concatenate([vg, jnp.ones_like(vg)], axis=1)
            for h in range(g * rep, (g + 1) * rep):
                s = _dot_nt(q_ref[:, h * dk:(h + 1) * dk], kg)
                p = jnp.exp2(s - jnp.max(s, axis=-1, keepdims=True))
                o = jnp.dot(p.astype(BF16), v_aug, preferred_element_type=F32)
                o_ref[:, h * dv:(h + 1) * dv] = (o[:, 0:dv] / o[:, dv:dv + 1]).astype(o_ref.dtype)

    @pl.when(pl.program_id(1) < n_ctx_tiles)
    def _():
        run(n_ctx)

    @pl.when(pl.program_id(1) >= n_ctx_tiles)
    def _():
        run(n_all)


def _attention(q, k, v, hq, hkv, dk, dv, n_ctx):
    B, T, _ = q.shape
    tq = TOK_TILE
    return pl.pallas_call(
        functools.partial(_attn_body, hq=hq, hkv=hkv, dk=dk, dv=dv, n_ctx_tiles=n_ctx // tq, n_ctx=n_ctx, n_all=T),
        grid=(B, T // tq),
        in_specs=[pl.BlockSpec((None, tq, hq * dk), lambda b, i: (b, i, 0)),
                  pl.BlockSpec((None, T, hkv * dk), lambda b, i: (b, 0, 0)),
                  pl.BlockSpec((None, T, hkv * dv), lambda b, i: (b, 0, 0))],
        out_specs=pl.BlockSpec((None, tq, hq * dv), lambda b, i: (b, i, 0)),
        out_shape=jax.ShapeDtypeStruct((B, T, hq * dv), BF16),
        compiler_params=_cp("parallel", "parallel"),
        name="attention",
    )(q, k, v)


def _rope_tables(T, n_ctx, n_rot):
    quarter = n_rot // 4
    t = jnp.arange(T - n_ctx)
    row = (t // GRID_W).astype(F32)
    col = (t % GRID_W).astype(F32)
    inv = ROPE_THETA ** (-jnp.arange(quarter, dtype=F32) / quarter)
    ar = row[:, None] * inv[None, :]
    ac = col[:, None] * inv[None, :]
    pad = LANES - n_rot
    cos = jnp.concatenate([jnp.cos(ar), jnp.cos(ar), jnp.cos(ac), jnp.cos(ac), jnp.ones((T - n_ctx, pad), F32)], axis=1)
    sin = jnp.concatenate([-jnp.sin(ar), jnp.sin(ar), -jnp.sin(ac), jnp.sin(ac), jnp.zeros((T - n_ctx, pad), F32)], axis=1)
    cos = jnp.concatenate([jnp.ones((n_ctx, LANES), F32), cos], axis=0)
    sin = jnp.concatenate([jnp.zeros((n_ctx, LANES), F32), sin], axis=0)
    return cos, sin


def _moe_body(te_ref, first_ref, slot_ref, nxt_ref, nv_ref, hs_ref, w1_hbm, w3_hbm, w2_hbm, y_ref,
              f1, f3, f2, c1, c3, c2, sem, *, layer):
    i = pl.program_id(0)

    def weight_copies(e, s):
        return (pltpu.make_async_copy(w1_hbm.at[layer, e], f1.at[s], sem.at[s, 0]),
                pltpu.make_async_copy(w3_hbm.at[layer, e], f3.at[s], sem.at[s, 1]),
                pltpu.make_async_copy(w2_hbm.at[layer, e], f2.at[s], sem.at[s, 2]))

    @pl.when(i == 0)
    def _():
        for cp in weight_copies(te_ref[0], 0):
            cp.start()

    @pl.when(first_ref[i] == 1)
    def _():
        s = slot_ref[i]
        for cp in weight_copies(te_ref[i], s):
            cp.wait()

        @pl.when(nxt_ref[i] >= 0)
        def _():
            for cp in weight_copies(nxt_ref[i], 1 - s):
                cp.start()

        c1[...] = f1[s].astype(BF16)
        c3[...] = f3[s].astype(BF16)
        c2[...] = f2[s].astype(BF16)

    @pl.when(i < nv_ref[0])
    def _():
        hs = hs_ref[...].astype(BF16)
        a = jnp.dot(hs, c1[...], preferred_element_type=F32)
        b = jnp.dot(hs, c3[...], preferred_element_type=F32)
        act = a * _sigmoid(a) * b
        y = jnp.dot(act.astype(BF16), c2[...], preferred_element_type=F32)
        y_ref[...] = y.astype(y_ref.dtype)

    @pl.when(i >= nv_ref[0])
    def _():
        y_ref[...] = jnp.zeros_like(y_ref)


def _moe_experts(tile_expert, run_first, run_slot, run_next, n_valid, hs, w1, w3, w2, layer):
    NP, D = hs.shape
    tm = MOE_TM
    DE = w1.shape[-1]
    grid_spec = pltpu.PrefetchScalarGridSpec(
        num_scalar_prefetch=5,
        grid=(NP // tm,),
        in_specs=[pl.BlockSpec((tm, D), lambda i, te, fi, sl, nx, nv: (jnp.minimum(i, nv[0] - 1), 0)),
                  pl.BlockSpec(memory_space=pl.ANY),
                  pl.BlockSpec(memory_space=pl.ANY),
                  pl.BlockSpec(memory_space=pl.ANY)],
        out_specs=pl.BlockSpec((tm, D), lambda i, *_: (i, 0)),
        scratch_shapes=[pltpu.VMEM((2, D, DE), F32), pltpu.VMEM((2, D, DE), F32), pltpu.VMEM((2, DE, D), F32),
                        pltpu.VMEM((D, DE), BF16), pltpu.VMEM((D, DE), BF16), pltpu.VMEM((DE, D), BF16),
                        pltpu.SemaphoreType.DMA((2, 3))],
    )
    return pl.pallas_call(
        functools.partial(_moe_body, layer=layer),
        grid_spec=grid_spec,
        out_shape=jax.ShapeDtypeStruct((NP, D), BF16),
        compiler_params=_cp("arbitrary"),
        name="moe_experts",
    )(tile_expert, run_first, run_slot, run_next, n_valid, hs, w1, w3, w2)


def _moe(h2, re, w1, w3, w2, layer):
    N = h2.shape[0]
    tm = MOE_TM
    n_tiles = (2 * N) // tm + N_EXPERTS
    e_flat = re.reshape(2 * N)
    onehot = (e_flat[:, None] == jnp.arange(N_EXPERTS, dtype=jnp.int32)[None, :]).astype(jnp.int32)
    csum = jnp.cumsum(onehot, axis=0)
    rank = jnp.sum(onehot * (csum - 1), axis=1)
    counts = csum[-1]
    ptiles = (counts + tm - 1) // tm
    tile_end = jnp.cumsum(ptiles)
    tile_start = tile_end - ptiles
    pos = tile_start[e_flat] * tm + rank
    n_valid = tile_end[-1:].astype(jnp.int32)
    tile_ids = jnp.arange(n_tiles, dtype=jnp.int32)
    tile_expert = jnp.minimum(jnp.sum((tile_end[None, :] <= tile_ids[:, None]).astype(jnp.int32), axis=1), N_EXPERTS - 1)
    experts = jnp.arange(N_EXPERTS, dtype=jnp.int32)
    owns = ptiles > 0
    run_no = jnp.cumsum(owns.astype(jnp.int32)) - 1
    later = jnp.where(owns[None, :] & (experts[None, :] > experts[:, None]), experts[None, :], N_EXPERTS)
    next_run = jnp.min(later, axis=1)
    next_run = jnp.where(next_run < N_EXPERTS, next_run, -1).astype(jnp.int32)
    run_first = ((tile_ids == tile_start[tile_expert]) & (tile_ids < n_valid[0])).astype(jnp.int32)
    run_slot = (run_no[tile_expert] % 2).astype(jnp.int32)
    run_next = next_run[tile_expert]
    spread = jnp.arange(n_tiles * tm, dtype=jnp.int32) % N
    src = spread.at[pos].set(jnp.arange(2 * N, dtype=jnp.int32) // 2)
    rows = lambda a, idx: a.at[idx].get(mode="promise_in_bounds")
    hs = rows(h2, src)
    y = _moe_experts(tile_expert, run_first, run_slot, run_next, n_valid, hs, w1, w3, w2, layer)
    pos2 = pos.reshape(N, 2)
    return rows(y, pos2[:, 0]), rows(y, pos2[:, 1])


def kernel(x, c, ctx, c_ctx, mod_w, mod_b, norm1_g, norm2_g, w_in, w_out, shift_prev, shift_next, decay_w0, decay_up, iclr_a0, iclr_up, gate_up, k_k, k_a, r_k, gn_g, gn_b, q_norm_g, k_norm_g, mla_q_norm_g, mla_w_uq, mla_kv_norm_g, mla_w_ukv, router_gw, router_gb, router_ew, router_eb, exp_w1, exp_w3, exp_w2, final_norm_g):
    B, S, D = x.shape
    C = ctx.shape[1]
    T = C + S
    depth = mod_w.shape[0]
    assert C == TOK_TILE and S % TOK_TILE == 0 and B <= CTX_ROW
    n_ctx_tiles = C // TOK_TILE

    cc = jnp.zeros((SUBLANES, D), F32).at[:B].set(c).at[CTX_ROW].set(c_ctx)
    mods_all = [_modulation(cc, mod_w, mod_b, l).reshape(SUBLANES * N_MOD, 1, D) for l in range(depth)]
    X, h = _join_norm(ctx, x, norm1_g[0], mods_all[0], 0, 1)

    cos_g, sin_g = _rope_tables(T, C, GQA_HEAD)
    cos_m, sin_m = _rope_tables(T, C, MLA_ROPE)
    hid = jnp.arange(RWKV_W) // RWKV_HEAD
    head_sum = (hid[:, None] == hid[None, :]).astype(F32)

    for l in range(depth):
        mods = mods_all[l]
        w_r = w_in[l][:, :RWKV_COLS].astype(BF16)
        w_g = w_in[l][:, RWKV_COLS:RWKV_COLS + GQA_COLS].astype(BF16)
        w_m = jnp.pad(w_in[l][:, RWKV_COLS + GQA_COLS:], ((0, 0), (0, MLA_COLS_PAD - MLA_COLS))).astype(BF16)

        pr = _matmul(h.reshape(B * T, D), w_r).reshape(B, T, RWKV_COLS)

        wup_pad = jnp.pad(decay_up[l], ((0, 0), (0, ICLR_RANK), (0, 0)))
        aup_pad = jnp.pad(iclr_up[l], ((0, 0), (DECAY_RANK, 0), (0, 0)))
        r, v, kh, lw, b, kt, g, bonus = _rwkv_prep(pr, shift_prev[l], shift_next[l], decay_w0[l], wup_pad,
                                                  iclr_a0[l], aup_pad, gate_up[l], k_k[l], k_a[l], r_k[l], head_sum)
        yf, yb = _rwkv_scan(r, v, kh, lw, b, kt, C)
        o_r = _rwkv_readout(yf, yb, bonus, g, gn_g[l], gn_b[l], head_sum)

        q, k, vv = _gqa_proj(h, w_g, q_norm_g[l], k_norm_g[l], cos_g, sin_g)
        o_g = _attention(q, k, vv, GQA_Q_HEADS, GQA_KV_HEADS, GQA_HEAD, GQA_HEAD, C)

        wq = mla_w_uq[l].reshape(MLA_Q_RANK, MLA_HEADS, MLA_NOPE + MLA_ROPE)
        wq = jnp.pad(wq, ((0, 0), (0, 0), (0, MLA_DK - MLA_NOPE - MLA_ROPE))).reshape(MLA_Q_RANK, MLA_HEADS * MLA_DK)
        wkv = mla_w_ukv[l].reshape(MLA_KV_RANK, MLA_HEADS, MLA_NOPE + MLA_V)
        wkv = jnp.concatenate([wkv[:, :, :MLA_NOPE].reshape(MLA_KV_RANK, -1), wkv[:, :, MLA_NOPE:].reshape(MLA_KV_RANK, -1)], axis=1)
        qm, km, vm = _mla_proj(h, w_m, mla_q_norm_g[l], wq.astype(BF16), mla_kv_norm_g[l], wkv.astype(BF16), cos_m, sin_m)
        o_m = _attention(qm, km, vm, MLA_HEADS, MLA_HEADS, MLA_DK, MLA_V, C)

        last = l + 1 == depth
        wr = jnp.pad(jnp.concatenate([router_gw[l], router_ew[l]], axis=1), ((0, 0), (0, LANES - N_GROUPS - N_EXPERTS)))
        br = jnp.pad(jnp.concatenate([router_gb[l], router_eb[l]]), (0, LANES - N_GROUPS - N_EXPERTS)).reshape(1, LANES)
        X, h2, rw, re = _out_proj_router([o_r, o_g, o_m], w_out[l].astype(BF16), X, mods, C,
                                         TOK_TILE if last else OUT_TILE, last, norm2_g[l], wr, br)
        rows = X.shape[1]
        ctx_tiles = 0 if last else n_ctx_tiles

        y0, y1 = _moe(h2.reshape(B * rows, D), re.reshape(B * rows, LANES)[:, :2], exp_w1, exp_w3, exp_w2, l)
        y0 = y0.reshape(B, rows, D)
        y1 = y1.reshape(B, rows, D)
        if last:
            out = _moe_residual_final(X, y0, y1, rw, mods, 5, ctx_tiles, final_norm_g)
        else:
            X, h = _moe_residual_norm(X, y0, y1, rw, mods, 5, ctx_tiles, norm1_g[l + 1], mods_all[l + 1])
    return out
```

```python
import functools
import math

import jax
import jax.numpy as jnp
from jax import lax
from jax.experimental import pallas as pl
from jax.experimental.pallas import tpu as pltpu

F32 = jnp.float32
BF16 = jnp.bfloat16

V7X_VMEM_BYTES = 64 * 1024 * 1024
VMEM_LIMIT = V7X_VMEM_BYTES - 8 * 1024 * 1024
LANES = 128
SUBLANES = 8

GRID_W = 64
ROPE_THETA = 10000.0
NORM_EPS = 1e-6
GN_EPS = 64e-5
DECAY_SCALE = math.exp(-0.5)
LOG2E = math.log2(math.e)

RWKV_HEAD = 64
RWKV_W = 512
DECAY_RANK = 64
ICLR_RANK = 64
GATE_RANK = 128
RWKV_COLS = 3 * RWKV_W + DECAY_RANK + ICLR_RANK + GATE_RANK
LOWRANK_OFF = 3 * RWKV_W
GATE_OFF = LOWRANK_OFF + DECAY_RANK + ICLR_RANK
CHUNK = 64
SCAN_CHUNKS = 2

GQA_HEAD = 128
GQA_Q_HEADS = 8
GQA_KV_HEADS = 2
GQA_Q_COLS = GQA_Q_HEADS * GQA_HEAD
GQA_KV_COLS = GQA_KV_HEADS * GQA_HEAD
GQA_COLS = GQA_Q_COLS + 2 * GQA_KV_COLS

MLA_HEADS = 4
MLA_NOPE = 128
MLA_ROPE = 64
MLA_V = 128
MLA_Q_RANK = 384
MLA_KV_RANK = 256
MLA_COLS = MLA_Q_RANK + MLA_KV_RANK + MLA_ROPE
MLA_COLS_PAD = 768
MLA_DK = 2 * LANES

N_GROUPS = 4
EXPERTS_PER_GROUP = 8
N_EXPERTS = 32
D_EXPERT = 256
MOE_TM = 256

TOK_TILE = 256
PROJ_TILE = 768
OUT_TILE = 384
N_MOD = 6
CTX_ROW = 4


def _cp(*sem):
    return pltpu.CompilerParams(dimension_semantics=sem, vmem_limit_bytes=VMEM_LIMIT)


def _sigmoid(x):
    return 1.0 / (1.0 + jnp.exp(-x))


def _bf16_terms(x, n):
    terms = []
    for _ in range(n):
        t = x.astype(BF16)
        terms.append(t)
        x = x - t.astype(F32)
    return terms


def _dot_terms(x, w, nx, nw):
    xs = _bf16_terms(x, nx)
    ws = _bf16_terms(w, nw)
    acc = None
    for i in range(nx):
        for j in range(nw):
            if i + j < max(nx, nw):
                p = jnp.dot(xs[i], ws[j], preferred_element_type=F32)
                acc = p if acc is None else acc + p
    return acc


def _mod_body(c_ref, w_ref, b_ref, o_ref):
    c = c_ref[...]
    s = c * _sigmoid(c)
    o_ref[...] = _dot_terms(s, w_ref[...], 2, 2) + b_ref[...]


def _modulation(cc, mod_w, mod_b, layer):
    L, D, N = mod_w.shape
    tn = 1024
    return pl.pallas_call(
        _mod_body,
        grid=(N // tn,),
        in_specs=[pl.BlockSpec((SUBLANES, D), lambda j: (0, 0)),
                  pl.BlockSpec((None, D, tn), lambda j: (layer, 0, j)),
                  pl.BlockSpec((None, 1, tn), lambda j: (layer, 0, j))],
        out_specs=pl.BlockSpec((SUBLANES, tn), lambda j: (0, j)),
        out_shape=jax.ShapeDtypeStruct((SUBLANES, N), F32),
        compiler_params=_cp("parallel"),
        name="modulation",
    )(cc, mod_w, mod_b.reshape(L, 1, N))


def _mod_spec(which, n_ctx_tiles, D):
    return pl.BlockSpec((None, 1, D), lambda b, i: (jnp.where(i < n_ctx_tiles, CTX_ROW, b) * N_MOD + which, 0, 0))


def _rms(x, g):
    return x * lax.rsqrt(jnp.mean(x * x, axis=-1, keepdims=True) + NORM_EPS) * g


def _join_norm_body(ctx_ref, x_ref, g_ref, sh_ref, sc_ref, x_o, h_o, *, n_ctx_tiles):
    i = pl.program_id(1)

    def emit(src_ref):
        x = src_ref[...]
        x_o[...] = x
        h_o[...] = (_rms(x, g_ref[...]) * (1.0 + sc_ref[...]) + sh_ref[...]).astype(h_o.dtype)

    @pl.when(i < n_ctx_tiles)
    def _():
        emit(ctx_ref)

    @pl.when(i >= n_ctx_tiles)
    def _():
        emit(x_ref)


def _route(logits):
    lane = lax.broadcasted_iota(jnp.int32, logits.shape, 1)
    lane_f = lane.astype(F32)
    neg = jnp.float32(-1e30)
    far = jnp.float32(1e9)
    first_at = lambda hit: jnp.min(jnp.where(hit, lane_f, far), axis=-1, keepdims=True).astype(jnp.int32)
    gl = jnp.where(lane < N_GROUPS, logits, neg)
    gmax = jnp.max(gl, axis=-1, keepdims=True)
    gidx = first_at(gl == gmax)
    p_sel = 1.0 / jnp.sum(jnp.exp(gl - gmax), axis=-1, keepdims=True)
    lo = N_GROUPS + gidx * EXPERTS_PER_GROUP
    el = jnp.where((lane >= lo) & (lane < lo + EXPERTS_PER_GROUP), logits, neg)
    m1 = jnp.max(el, axis=-1, keepdims=True)
    i1 = first_at(el == m1)
    el2 = jnp.where(lane == i1, neg, el)
    m2 = jnp.max(el2, axis=-1, keepdims=True)
    i2 = first_at(el2 == m2)
    t = jnp.exp(m2 - m1)
    w1 = p_sel / (1.0 + t)
    w2 = p_sel * t / (1.0 + t)
    rw = jnp.where(lane == 0, w1, jnp.where(lane == 1, w2, 0.0))
    re = jnp.where(lane == 0, i1 - N_GROUPS, jnp.where(lane == 1, i2 - N_GROUPS, 0))
    return rw, re


def _join_norm(ctx, x, g, mods, shift_i, scale_i):
    B, C, D = ctx.shape
    tt = TOK_TILE
    n_ctx_tiles = C // tt
    T = C + x.shape[1]
    tok = pl.BlockSpec((None, tt, D), lambda b, i: (b, i, 0))
    return pl.pallas_call(
        functools.partial(_join_norm_body, n_ctx_tiles=n_ctx_tiles),
        grid=(B, T // tt),
        in_specs=[pl.BlockSpec((None, tt, D), lambda b, i: (b, jnp.minimum(i, n_ctx_tiles - 1), 0)),
                  pl.BlockSpec((None, tt, D), lambda b, i: (b, jnp.maximum(i - n_ctx_tiles, 0), 0)),
                  pl.BlockSpec((1, D), lambda b, i: (0, 0)),
                  _mod_spec(shift_i, n_ctx_tiles, D),
                  _mod_spec(scale_i, n_ctx_tiles, D)],
        out_specs=[tok, tok],
        out_shape=[jax.ShapeDtypeStruct((B, T, D), F32), jax.ShapeDtypeStruct((B, T, D), BF16)],
        compiler_params=_cp("parallel", "parallel"),
        name="join_norm",
    )(ctx, x, g.reshape(1, D), mods, mods)


def _mm_body(a_ref, w_ref, o_ref):
    o_ref[...] = jnp.dot(a_ref[...], w_ref[...], preferred_element_type=F32).astype(o_ref.dtype)


def _matmul(a, w, tm=PROJ_TILE):
    M, K = a.shape
    N = w.shape[1]
    assert M % tm == 0
    return pl.pallas_call(
        _mm_body,
        grid=(M // tm,),
        in_specs=[pl.BlockSpec((tm, K), lambda i: (i, 0)),
                  pl.BlockSpec((K, N), lambda i: (0, 0))],
        out_specs=pl.BlockSpec((tm, N), lambda i: (i, 0)),
        out_shape=jax.ShapeDtypeStruct((M, N), F32),
        compiler_params=_cp("parallel"),
        name="token_matmul",
    )(a, w)


def _rwkv_readout_value(yf_ref, yb_ref, bon_ref, gate_ref, gng_ref, gnb_ref, e_ref):
    y = yf_ref[...] + yb_ref[...]
    head_mean = e_ref[...] * (1.0 / RWKV_HEAD)
    mu = _dot_terms(y, head_mean, 2, 1)
    yc = y - mu
    var = _dot_terms(yc * yc, head_mean, 2, 1)
    yn = yc * lax.rsqrt(var + GN_EPS) * gng_ref[...] + gnb_ref[...]
    return (yn + bon_ref[...]) * gate_ref[...]


def _out_proj_router_body(*refs, n_parts, n_ctx, tm, tile0):
    readout_refs = refs[:7]
    a_refs = refs[7:7 + n_parts]
    w_ref, x_ref, ml_ref, mc_ref, g_ref, wr_ref, br_ref, x_o, h_o, rw_o, re_o = refs[7 + n_parts:]
    o_r = _rwkv_readout_value(*readout_refs).astype(BF16)
    k0 = o_r.shape[-1]
    acc = jnp.dot(o_r, w_ref[0:k0, :], preferred_element_type=F32)
    for a_ref in a_refs:
        k1 = k0 + a_ref.shape[-1]
        acc = acc + jnp.dot(a_ref[...], w_ref[k0:k1, :], preferred_element_type=F32)
        k0 = k1
    row = (pl.program_id(1) + tile0) * tm + lax.broadcasted_iota(jnp.int32, (tm, 1), 0)
    is_ctx = row < n_ctx
    mod = lambda which: jnp.where(is_ctx, mc_ref[which], ml_ref[which])
    x = x_ref[...] + mod(2) * acc
    x_o[...] = x
    h = _rms(x, g_ref[...]) * (1.0 + mod(4)) + mod(3)
    h_o[...] = h
    h_hi, h_lo = _bf16_terms(h, 2)
    w_hi, w_lo = _bf16_terms(wr_ref[...], 2)
    both = jnp.dot(h_hi, jnp.concatenate([w_hi, w_lo], axis=1), preferred_element_type=F32)
    logits = (both[:, 0:LANES] + both[:, LANES:2 * LANES]
              + jnp.dot(h_lo, w_hi, preferred_element_type=F32) + br_ref[...])
    rw, re = _route(logits)
    rw_o[...] = rw
    re_o[...] = re


def _out_proj_router(rwkv, parts, w, X, mods, n_ctx, tm, latents_only, g2, wr, br):
    B, T, D = X.shape
    yf, yb, bonus, gate, gn_g, gn_b, head_sum = rwkv
    W = yf.shape[-1]
    tile0 = n_ctx // tm if latents_only else 0
    assert T % tm == 0 and (n_ctx % tm == 0 or not latents_only)
    assert W + sum(p.shape[-1] for p in parts) == w.shape[0]
    rows_out = T - tile0 * tm
    mods6 = mods.reshape(SUBLANES, N_MOD, 1, D)
    tok_in = lambda wd: pl.BlockSpec((None, tm, wd), lambda b, i: (b, i + tile0, 0))
    tok_out = lambda wd: pl.BlockSpec((None, tm, wd), lambda b, i: (b, i, 0))
    shape = lambda wd, dt: jax.ShapeDtypeStruct((B, rows_out, wd), dt)
    const = lambda a: pl.BlockSpec(a.shape, lambda b, i: (0,) * a.ndim)
    rwkv_args = [yf, yb, bonus, gate, gn_g.reshape(1, W), gn_b.reshape(1, W), head_sum]
    return pl.pallas_call(
        functools.partial(_out_proj_router_body, n_parts=len(parts), n_ctx=n_ctx, tm=tm, tile0=tile0),
        grid=(B, T // tm - tile0),
        in_specs=[tok_in(W)] * 4 + [const(a) for a in rwkv_args[4:]] + [tok_in(p.shape[-1]) for p in parts] + [
                  pl.BlockSpec(w.shape, lambda b, i: (0, 0)),
                  tok_in(D),
                  pl.BlockSpec((None, N_MOD, 1, D), lambda b, i: (b, 0, 0, 0)),
                  pl.BlockSpec((None, N_MOD, 1, D), lambda b, i: (CTX_ROW, 0, 0, 0)),
                  pl.BlockSpec((1, D), lambda b, i: (0, 0)),
                  pl.BlockSpec((D, LANES), lambda b, i: (0, 0)),
                  pl.BlockSpec((1, LANES), lambda b, i: (0, 0))],
        out_specs=[tok_out(D), tok_out(D), tok_out(LANES), tok_out(LANES)],
        out_shape=[shape(D, F32), shape(D, F32), shape(LANES, F32), shape(LANES, jnp.int32)],
        compiler_params=_cp("parallel", "parallel"),
        name="out_proj_router",
    )(*rwkv_args, *parts, w, X, mods6, mods6, g2.reshape(1, D), wr, br)


def _moe_residual(x_ref, y0_ref, y1_ref, rw_ref, gate_ref):
    rw = rw_ref[...]
    moe = rw[:, 0:1] * y0_ref[...].astype(F32) + rw[:, 1:2] * y1_ref[...].astype(F32)
    return x_ref[...] + gate_ref[...] * moe


def _moe_residual_norm_body(x_ref, y0_ref, y1_ref, rw_ref, gate_ref, g_ref, sh_ref, sc_ref, x_o, h_o):
    x = _moe_residual(x_ref, y0_ref, y1_ref, rw_ref, gate_ref)
    x_o[...] = x
    h_o[...] = (_rms(x, g_ref[...]) * (1.0 + sc_ref[...]) + sh_ref[...]).astype(h_o.dtype)


def _moe_residual_final_body(x_ref, y0_ref, y1_ref, rw_ref, gate_ref, g_ref, o_ref):
    o_ref[...] = _rms(_moe_residual(x_ref, y0_ref, y1_ref, rw_ref, gate_ref), g_ref[...])


def _moe_residual_norm(X, Y0, Y1, rw, mods, gate_i, n_ctx_tiles, g_next, mods_next):
    B, T, D = X.shape
    tt = TOK_TILE
    blk = pl.BlockSpec((None, tt, D), lambda b, i: (b, i, 0))
    return pl.pallas_call(
        _moe_residual_norm_body,
        grid=(B, T // tt),
        in_specs=[blk, blk, blk, pl.BlockSpec((None, tt, LANES), lambda b, i: (b, i, 0)),
                  _mod_spec(gate_i, n_ctx_tiles, D), pl.BlockSpec((1, D), lambda b, i: (0, 0)),
                  _mod_spec(0, n_ctx_tiles, D), _mod_spec(1, n_ctx_tiles, D)],
        out_specs=[blk, blk],
        out_shape=[jax.ShapeDtypeStruct((B, T, D), F32), jax.ShapeDtypeStruct((B, T, D), BF16)],
        compiler_params=_cp("parallel", "parallel"),
        name="moe_residual_norm",
    )(X, Y0, Y1, rw, mods, g_next.reshape(1, D), mods_next, mods_next)


def _moe_residual_final(X, Y0, Y1, rw, mods, gate_i, n_ctx_tiles, g_final):
    B, T, D = X.shape
    tt = TOK_TILE
    S = T - n_ctx_tiles * tt
    lat = lambda w: pl.BlockSpec((None, tt, w), lambda b, i: (b, i + n_ctx_tiles, 0))
    return pl.pallas_call(
        _moe_residual_final_body,
        grid=(B, S // tt),
        in_specs=[lat(D), lat(D), lat(D), lat(LANES),
                  pl.BlockSpec((None, 1, D), lambda b, i: (b * N_MOD + gate_i, 0, 0)),
                  pl.BlockSpec((1, D), lambda b, i: (0, 0))],
        out_specs=pl.BlockSpec((None, tt, D), lambda b, i: (b, i, 0)),
        out_shape=jax.ShapeDtypeStruct((B, S, D), F32),
        compiler_params=_cp("parallel", "parallel"),
        name="moe_residual_final",
    )(X, Y0, Y1, rw, mods, g_final.reshape(1, D))


def _rwkv_prep_body(p_ref, pv_ref, nx_ref, mup_ref, mun_ref, w0_ref, wup_ref, a0_ref, aup_ref, gup_ref,
                    kk_ref, ka_ref, rk_ref, e_ref,
                    r_o, v_o, kh_o, lw_o, b_o, kt_o, g_o, bon_o, *, n_tiles, tt):
    i = pl.program_id(1)
    p = p_ref[...]
    seq_first = i <= 1
    seq_last = (i == 0) | (i == n_tiles - 1)
    prow = jnp.where(seq_first, 0.0, pv_ref[SUBLANES - 1:SUBLANES, :])
    nrow = jnp.where(seq_last, 0.0, nx_ref[0:1, :])
    rid = lax.broadcasted_iota(jnp.int32, (tt, 1), 0)
    prev = jnp.where(rid == 0, prow, pltpu.roll(p, 1, 0))
    nxt = jnp.where(rid == tt - 1, nrow, pltpu.roll(p, tt - 1, 0))
    z = p + mup_ref[...] * (prev - p) + mun_ref[...] * (nxt - p)

    W = RWKV_W
    r = z[:, 0:W]
    k = z[:, W:2 * W]
    v = z[:, 2 * W:3 * W]
    lowrank = z[:, LOWRANK_OFF:LOWRANK_OFF + LANES]
    gd = z[:, GATE_OFF:GATE_OFF + GATE_RANK]
    head_sum = e_ref[...]

    kap = k * kk_ref[...]
    ss = _dot_terms(kap * kap, head_sum, 2, 1)
    khat = kap * lax.rsqrt(ss + 1e-12)
    wd_t = jnp.tanh(lowrank)
    g_o[...] = _dot_terms(_sigmoid(gd), gup_ref[...], 1, 1)
    r_o[...] = r
    v_o[...] = v
    kh_o[...] = khat
    kt_sum = None
    for d in range(2):
        dec = _dot_terms(wd_t, wup_ref[d], 2, 2)
        lw_o[d] = -DECAY_SCALE * _sigmoid(w0_ref[d:d + 1, :] + dec)
        a = _sigmoid(a0_ref[d:d + 1, :] + _dot_terms(lowrank, aup_ref[d], 1, 1))
        kt = k * (1.0 + (a - 1.0) * ka_ref[...])
        kt_o[d] = kt
        b_o[d] = a * khat
        kt_sum = kt if kt_sum is None else kt_sum + kt
    bsum = _dot_terms(r * kt_sum * rk_ref[...], head_sum, 2, 1)
    bon_o[...] = bsum * v


def _rwkv_prep(pr, mu_prev, mu_next, w0, wup_pad, a0, aup_pad, g_up, k_k, k_a, r_k, head_sum):
    B, T, _ = pr.shape
    tt = TOK_TILE
    W = RWKV_W
    n_tiles = T // tt
    n8 = tt // SUBLANES
    row = lambda v: v.reshape(1, -1)
    full = lambda a: pl.BlockSpec(a.shape, lambda b, i: (0,) * a.ndim)
    tok = pl.BlockSpec((None, tt, W), lambda b, i: (b, i, 0))
    tok2 = pl.BlockSpec((None, 2, tt, W), lambda b, i: (b, 0, i, 0))
    s1 = jax.ShapeDtypeStruct((B, T, W), F32)
    s2 = jax.ShapeDtypeStruct((B, 2, T, W), F32)
    consts = [row(mu_prev), row(mu_next), w0, wup_pad, a0, aup_pad, g_up, row(k_k), row(k_a), row(r_k), head_sum]
    return pl.pallas_call(
        functools.partial(_rwkv_prep_body, n_tiles=n_tiles, tt=tt),
        grid=(B, n_tiles),
        in_specs=[pl.BlockSpec((None, tt, RWKV_COLS), lambda b, i: (b, i, 0)),
                  pl.BlockSpec((None, SUBLANES, RWKV_COLS), lambda b, i: (b, jnp.maximum(i * n8 - 1, 0), 0)),
                  pl.BlockSpec((None, SUBLANES, RWKV_COLS), lambda b, i: (b, jnp.minimum((i + 1) * n8, T // SUBLANES - 1), 0)),
                  ] + [full(a) for a in consts],
        out_specs=[tok, tok, tok, tok2, tok2, tok2, tok, tok],
        out_shape=[s1, s1, s1, s2, s2, s2, s1, s1],
        compiler_params=_cp("parallel", "parallel"),
        name="rwkv_prep",
    )(pr, pr, pr, *consts)


def _stack_heads(x):
    lane = lax.broadcasted_iota(jnp.int32, x.shape, 1)
    first = lane < RWKV_HEAD
    return jnp.concatenate([jnp.where(first, x, 0.0), jnp.where(first, 0.0, x)], axis=0)


def _dot(a, b):
    return jnp.dot(a.astype(BF16), b.astype(BF16), preferred_element_type=F32)


def _dot_nt(a, b):
    return lax.dot_general(a.astype(BF16), b.astype(BF16), (((1,), (1,)), ((), ())), preferred_element_type=F32)


def _dot_tn(a, b):
    return jnp.dot(a.T.astype(BF16), b.astype(BF16), preferred_element_type=F32)


def _chunk_operands(r, v, kh, lw, b, kt, reverse):
    L = CHUNK
    ti = lax.broadcasted_iota(jnp.int32, (L, L), 0)
    tj = lax.broadcasted_iota(jnp.int32, (L, L), 1)
    tri = jnp.where((ti <= tj) if reverse else (ti >= tj), 1.0, 0.0)
    lam = _dot_terms(tri, lw, 1, 3)
    tot = lam[0:1, :] if reverse else lam[L - 1:L, :]
    e_n = jnp.exp(-lam)
    e_g = jnp.exp(tot - lam)
    full = dict(A=kh * jnp.exp(lam - lw), R=r * jnp.exp(lam), Kn=kt * e_n, Bn=b * e_n, Kg=kt * e_g, Bg=b * e_g, V=v)
    e_tot = jnp.exp(tot)
    pairs = []
    for p in range(RWKV_W // LANES):
        sl = slice(p * LANES, (p + 1) * LANES)
        ops = {k: a[:, sl] for k, a in full.items()}
        ops["e_tot"] = e_tot[:, sl]
        ops["reverse"] = reverse
        pairs.append(ops)
    return pairs


def _chunk_masks(reverse):
    L = CHUNK
    t = lax.broadcasted_iota(jnp.int32, (L, 2 * L), 0)
    i = lax.broadcasted_iota(jnp.int32, (L, 2 * L), 1) & (L - 1)
    before = (i > t) if reverse else (i < t)
    return before, before | (i == t), i == t


def _chunks_prepare(chains):
    L = CHUNK
    P2 = 2 * L
    n = len(chains)
    masks = {rev: _chunk_masks(rev) for rev in {c["reverse"] for c in chains}}
    strict = [masks[c["reverse"]][0] for c in chains]
    incl = [masks[c["reverse"]][1] for c in chains]
    eye = masks[chains[0]["reverse"]][2]
    bi = lax.broadcasted_iota(jnp.int32, (P2, P2), 0)
    bj = lax.broadcasted_iota(jnp.int32, (P2, P2), 1)
    same_head = (bi >= L) == (bj >= L)
    eye2 = bi == bj
    stack = _stack_heads
    cat0 = lambda *xs: jnp.concatenate(xs, axis=0)
    cat1 = lambda *xs: jnp.concatenate(xs, axis=1)

    big = [_dot_nt(cat0(c["A"], c["R"]), cat0(stack(c["Bn"]), stack(c["Kn"]))) for c in chains]
    Mb = [jnp.where(strict[i], big[i][0:L, 0:P2], 0.0) for i in range(n)]
    Mkv = [jnp.where(strict[i], big[i][0:L, P2:2 * P2], 0.0) for i in range(n)]
    Pb = [jnp.where(incl[i], big[i][L:P2, 0:P2], 0.0) for i in range(n)]
    Pkv = [jnp.where(incl[i], big[i][L:P2, P2:2 * P2], 0.0) for i in range(n)]

    Pw = [-m for m in Mb]
    Tm = [jnp.where(eye, 1.0, 0.0) + p for p in Pw]
    Pw = [_dot(p, stack(p)) for p in Pw]
    for _ in range(int(math.log2(L)) - 2):
        PT = [_dot(cat0(p, t), stack(p)) for p, t in zip(Pw, Tm)]
        Tm = [t + pt[L:P2] for t, pt in zip(Tm, PT)]
        Pw = [pt[0:L] for pt in PT]
    Tm = [t + _dot(t, stack(p)) for t, p in zip(Tm, Pw)]

    Vs = [stack(c["V"]) for c in chains]
    MPV = [_dot(cat0(Mkv[i], Pkv[i]), Vs[i]) for i in range(n)]
    TAM = [_dot(Tm[i], cat1(stack(chains[i]["A"]), stack(MPV[i][0:L]))) for i in range(n)]
    PB = [_dot(Pb[i], cat1(stack(TAM[i][:, 0:P2]), stack(TAM[i][:, P2:2 * P2]))) for i in range(n)]
    BG = [_dot_tn(chains[i]["Bg"], TAM[i]) for i in range(n)]
    KV = [_dot_tn(chains[i]["Kg"], chains[i]["V"]) for i in range(n)]
    prepared = []
    for i in range(n):
        RA = chains[i]["R"] - PB[i][:, 0:P2]
        G2 = jnp.where(eye2, chains[i]["e_tot"], 0.0) - jnp.where(same_head, BG[i][:, 0:P2], 0.0)
        H2 = jnp.where(same_head, KV[i] - BG[i][:, P2:2 * P2], 0.0)
        prepared.append((cat0(RA, G2), MPV[i][L:P2] - PB[i][:, P2:2 * P2], H2))
    return prepared


def _chunk_advance(prepared, state):
    lhs, y0, h2 = prepared
    out = _dot(lhs, state)
    return out[0:CHUNK] + y0, out[CHUNK:] + h2


def _rwkv_scan_body(rf_ref, vf_ref, khf_ref, rb_ref, vb_ref, khb_ref, lwf_ref, bf_ref, ktf_ref, lwb_ref, bb_ref, ktb_ref,
                    yf_ref, yb_ref, s_ref):
    @pl.when(pl.program_id(1) == 0)
    def _():
        s_ref[...] = jnp.zeros_like(s_ref)

    n_pairs = RWKV_W // LANES
    fwd_refs = (rf_ref, vf_ref, khf_ref, lwf_ref, bf_ref, ktf_ref)
    bwd_refs = (rb_ref, vb_ref, khb_ref, lwb_ref, bb_ref, ktb_ref)
    slots = []
    for c in range(SCAN_CHUNKS):
        lo_f = c * CHUNK
        lo_b = (SCAN_CHUNKS - 1 - c) * CHUNK
        slots.append((_chunk_operands(*(ref[lo_f:lo_f + CHUNK, :] for ref in fwd_refs), False), lo_f,
                      _chunk_operands(*(ref[lo_b:lo_b + CHUNK, :] for ref in bwd_refs), True), lo_b))
    prepared = _chunks_prepare([ch for s in slots for ch in s[0] + s[2]])
    states = [s_ref[i] for i in range(2 * n_pairs)]
    for c, (_, lo_f, _, lo_b) in enumerate(slots):
        ys = []
        for i in range(2 * n_pairs):
            y, states[i] = _chunk_advance(prepared[c * 2 * n_pairs + i], states[i])
            ys.append(y)
        yf_ref[lo_f:lo_f + CHUNK, :] = jnp.concatenate(ys[:n_pairs], axis=1)
        yb_ref[lo_b:lo_b + CHUNK, :] = jnp.concatenate(ys[n_pairs:], axis=1)
    for i in range(2 * n_pairs):
        s_ref[i] = states[i]


def _rwkv_scan(r, v, kh, lw, b, kt, n_ctx):
    B, T, W = r.shape
    blk = SCAN_CHUNKS * CHUNK
    assert T % blk == 0 and n_ctx % blk == 0
    nc = T // blk
    ncc = n_ctx // blk
    rev = lambda j: jnp.where(j < ncc, ncc - 1 - j, nc + ncc - 1 - j)
    fwd1 = pl.BlockSpec((None, blk, W), lambda bb, j: (bb, j, 0))
    bwd1 = pl.BlockSpec((None, blk, W), lambda bb, j: (bb, rev(j), 0))
    fwd2 = pl.BlockSpec((None, None, blk, W), lambda bb, j: (bb, 0, j, 0))
    bwd2 = pl.BlockSpec((None, None, blk, W), lambda bb, j: (bb, 1, rev(j), 0))
    out = jax.ShapeDtypeStruct((B, T, W), F32)
    return pl.pallas_call(
        _rwkv_scan_body,
        grid=(B, nc),
        in_specs=[fwd1, fwd1, fwd1, bwd1, bwd1, bwd1, fwd2, fwd2, fwd2, bwd2, bwd2, bwd2],
        out_specs=[fwd1, bwd1],
        out_shape=[out, out],
        scratch_shapes=[pltpu.VMEM((2 * W // LANES, LANES, LANES), F32)],
        compiler_params=_cp("parallel", "arbitrary"),
        name="rwkv_scan",
    )(r, v, kh, r, v, kh, lw, b, kt, lw, b, kt)


def _rope(y, cos, sin_signed, quarter):
    lane = lax.broadcasted_iota(jnp.int32, y.shape, 1)
    first = (lane & (2 * quarter - 1)) < quarter
    partner = jnp.where(first, pltpu.roll(y, LANES - quarter, 1), pltpu.roll(y, quarter, 1))
    return y * cos + partner * sin_signed


def _gqa_prep_body(p, qg_ref, kg_ref, cos_ref, sin_ref, q_o, k_o, v_o):
    cos = cos_ref[...]
    sin = sin_ref[...]
    scale = GQA_HEAD ** -0.5 * LOG2E
    for h in range(GQA_Q_HEADS):
        sl = slice(h * GQA_HEAD, (h + 1) * GQA_HEAD)
        q = _rms(p[:, sl], qg_ref[...])
        q_o[:, sl] = (_rope(q, cos, sin, GQA_HEAD // 4) * scale).astype(q_o.dtype)
    for h in range(GQA_KV_HEADS):
        sl = slice(h * GQA_HEAD, (h + 1) * GQA_HEAD)
        k = _rms(p[:, GQA_Q_COLS + h * GQA_HEAD:GQA_Q_COLS + (h + 1) * GQA_HEAD], kg_ref[...])
        k_o[:, sl] = _rope(k, cos, sin, GQA_HEAD // 4).astype(k_o.dtype)
    v_o[...] = p[:, GQA_Q_COLS + GQA_KV_COLS:GQA_COLS].astype(v_o.dtype)


def _row_blocks(n_rows):
    return [slice(r, r + TOK_TILE) for r in range(0, n_rows, TOK_TILE)]


def _gqa_proj_body(h_ref, w_ref, qg_ref, kg_ref, cos_ref, sin_ref, q_o, k_o, v_o):
    for rows in _row_blocks(h_ref.shape[0]):
        p = jnp.dot(h_ref[rows, :], w_ref[...], preferred_element_type=F32)
        _gqa_prep_body(p, qg_ref, kg_ref, cos_ref.at[rows, :], sin_ref.at[rows, :],
                       q_o.at[rows, :], k_o.at[rows, :], v_o.at[rows, :])


def _gqa_proj(h, w, q_norm_g, k_norm_g, cos, sin):
    B, T, D = h.shape
    tt = PROJ_TILE
    tok = lambda w_: pl.BlockSpec((None, tt, w_), lambda b, i: (b, i, 0))
    row = pl.BlockSpec((1, GQA_HEAD), lambda b, i: (0, 0))
    tab = pl.BlockSpec((tt, LANES), lambda b, i: (i, 0))
    return pl.pallas_call(
        _gqa_proj_body,
        grid=(B, T // tt),
        in_specs=[tok(D), pl.BlockSpec(w.shape, lambda b, i: (0, 0)), row, row, tab, tab],
        out_specs=[tok(GQA_Q_COLS), tok(GQA_KV_COLS), tok(GQA_KV_COLS)],
        out_shape=[jax.ShapeDtypeStruct((B, T, GQA_Q_COLS), BF16),
                   jax.ShapeDtypeStruct((B, T, GQA_KV_COLS), BF16),
                   jax.ShapeDtypeStruct((B, T, GQA_KV_COLS), BF16)],
        compiler_params=_cp("parallel", "parallel"),
        name="gqa_proj",
    )(h, w, q_norm_g.reshape(1, -1), k_norm_g.reshape(1, -1), cos, sin)


def _mla_prep_body(p, qg_ref, wq_ref, kvg_ref, wkv_ref, cos_ref, sin_ref, q_o, k_o, v_o):
    cos = cos_ref[...]
    sin = sin_ref[...]
    scale = (MLA_NOPE + MLA_ROPE) ** -0.5 * LOG2E
    cq = _rms(p[:, 0:MLA_Q_RANK], qg_ref[...])
    q = jnp.dot(cq.astype(BF16), wq_ref[...], preferred_element_type=F32) * scale
    ckv = _rms(p[:, MLA_Q_RANK:MLA_Q_RANK + MLA_KV_RANK], kvg_ref[...])
    kv = jnp.dot(ckv.astype(BF16), wkv_ref[...], preferred_element_type=F32)
    kr = _rope(p[:, MLA_Q_RANK + MLA_KV_RANK:MLA_COLS_PAD], cos, sin, MLA_ROPE // 4).astype(k_o.dtype)
    for h in range(MLA_HEADS):
        lo = h * MLA_DK
        q_o[:, lo:lo + LANES] = q[:, lo:lo + LANES].astype(q_o.dtype)
        q_o[:, lo + LANES:lo + MLA_DK] = _rope(q[:, lo + LANES:lo + MLA_DK], cos, sin, MLA_ROPE // 4).astype(q_o.dtype)
        k_o[:, lo:lo + LANES] = kv[:, h * MLA_NOPE:(h + 1) * MLA_NOPE].astype(k_o.dtype)
        k_o[:, lo + LANES:lo + MLA_DK] = kr
    v_o[...] = kv[:, MLA_HEADS * MLA_NOPE:].astype(v_o.dtype)


def _mla_proj_body(h_ref, w_ref, qg_ref, wq_ref, kvg_ref, wkv_ref, cos_ref, sin_ref, q_o, k_o, v_o):
    for rows in _row_blocks(h_ref.shape[0]):
        p = jnp.dot(h_ref[rows, :], w_ref[...], preferred_element_type=F32)
        _mla_prep_body(p, qg_ref, wq_ref, kvg_ref, wkv_ref, cos_ref.at[rows, :], sin_ref.at[rows, :],
                       q_o.at[rows, :], k_o.at[rows, :], v_o.at[rows, :])


def _mla_proj(h, w, q_norm_g, wq, kv_norm_g, wkv, cos, sin):
    B, T, D = h.shape
    tt = PROJ_TILE
    tok = lambda w_: pl.BlockSpec((None, tt, w_), lambda b, i: (b, i, 0))
    full = lambda a: pl.BlockSpec(a.shape, lambda b, i: (0,) * a.ndim)
    tab = pl.BlockSpec((tt, LANES), lambda b, i: (i, 0))
    qg = q_norm_g.reshape(1, -1)
    kvg = kv_norm_g.reshape(1, -1)
    return pl.pallas_call(
        _mla_proj_body,
        grid=(B, T // tt),
        in_specs=[tok(D), full(w), full(qg), full(wq), full(kvg), full(wkv), tab, tab],
        out_specs=[tok(MLA_HEADS * MLA_DK), tok(MLA_HEADS * MLA_DK), tok(MLA_HEADS * MLA_V)],
        out_shape=[jax.ShapeDtypeStruct((B, T, MLA_HEADS * MLA_DK), BF16),
                   jax.ShapeDtypeStruct((B, T, MLA_HEADS * MLA_DK), BF16),
                   jax.ShapeDtypeStruct((B, T, MLA_HEADS * MLA_V), BF16)],
        compiler_params=_cp("parallel", "parallel"),
        name="mla_proj",
    )(h, w, qg, wq, kvg, wkv, cos, sin)


def _attn_body(q_ref, k_ref, v_ref, o_ref, *, hq, hkv, dk, dv, n_ctx_tiles, n_ctx, n_all):
    rep = hq // hkv

    def run(nk):
        for g in range(hkv):
            kg = k_ref[0:nk, g * dk:(g + 1) * dk]
            vg = v_ref[0:nk, g * dv:(g + 1) * dv]
            v_aug = jnp.concatenate([vg, jnp.ones_like(vg)], axis=1)
            for h in range(g * rep, (g + 1) * rep):
                s = _dot_nt(q_ref[:, h * dk:(h + 1) * dk], kg)
                p = jnp.exp2(s - jnp.max(s, axis=-1, keepdims=True))
                o = jnp.dot(p.astype(BF16), v_aug, preferred_element_type=F32)
                o_ref[:, h * dv:(h + 1) * dv] = (o[:, 0:dv] / o[:, dv:dv + 1]).astype(o_ref.dtype)

    @pl.when(pl.program_id(1) < n_ctx_tiles)
    def _():
        run(n_ctx)

    @pl.when(pl.program_id(1) >= n_ctx_tiles)
    def _():
        run(n_all)


def _attention(q, k, v, hq, hkv, dk, dv, n_ctx):
    B, T, _ = q.shape
    tq = TOK_TILE
    return pl.pallas_call(
        functools.partial(_attn_body, hq=hq, hkv=hkv, dk=dk, dv=dv, n_ctx_tiles=n_ctx // tq, n_ctx=n_ctx, n_all=T),
        grid=(B, T // tq),
        in_specs=[pl.BlockSpec((None, tq, hq * dk), lambda b, i: (b, i, 0)),
                  pl.BlockSpec((None, T, hkv * dk), lambda b, i: (b, 0, 0)),
                  pl.BlockSpec((None, T, hkv * dv), lambda b, i: (b, 0, 0))],
        out_specs=pl.BlockSpec((None, tq, hq * dv), lambda b, i: (b, i, 0)),
        out_shape=jax.ShapeDtypeStruct((B, T, hq * dv), BF16),
        compiler_params=_cp("parallel", "parallel"),
        name="attention",
    )(q, k, v)


def _rope_tables(T, n_ctx, n_rot):
    quarter = n_rot // 4
    t = jnp.arange(T - n_ctx)
    row = (t // GRID_W).astype(F32)
    col = (t % GRID_W).astype(F32)
    inv = ROPE_THETA ** (-jnp.arange(quarter, dtype=F32) / quarter)
    ar = row[:, None] * inv[None, :]
    ac = col[:, None] * inv[None, :]
    pad = LANES - n_rot
    cos = jnp.concatenate([jnp.cos(ar), jnp.cos(ar), jnp.cos(ac), jnp.cos(ac), jnp.ones((T - n_ctx, pad), F32)], axis=1)
    sin = jnp.concatenate([-jnp.sin(ar), jnp.sin(ar), -jnp.sin(ac), jnp.sin(ac), jnp.zeros((T - n_ctx, pad), F32)], axis=1)
    cos = jnp.concatenate([jnp.ones((n_ctx, LANES), F32), cos], axis=0)
    sin = jnp.concatenate([jnp.zeros((n_ctx, LANES), F32), sin], axis=0)
    return cos, sin


def _moe_body(te_ref, first_ref, slot_ref, nxt_ref, nv_ref, hs_ref, w1_hbm, w3_hbm, w2_hbm, y_ref,
              f1, f3, f2, c1, c3, c2, sem, *, layer):
    i = pl.program_id(0)

    def weight_copies(e, s):
        return (pltpu.make_async_copy(w1_hbm.at[layer, e], f1.at[s], sem.at[s, 0]),
                pltpu.make_async_copy(w3_hbm.at[layer, e], f3.at[s], sem.at[s, 1]),
                pltpu.make_async_copy(w2_hbm.at[layer, e], f2.at[s], sem.at[s, 2]))

    @pl.when(i == 0)
    def _():
        for cp in weight_copies(te_ref[0], 0):
            cp.start()

    @pl.when(first_ref[i] == 1)
    def _():
        s = slot_ref[i]
        for cp in weight_copies(te_ref[i], s):
            cp.wait()

        @pl.when(nxt_ref[i] >= 0)
        def _():
            for cp in weight_copies(nxt_ref[i], 1 - s):
                cp.start()

        c1[...] = f1[s].astype(BF16)
        c3[...] = f3[s].astype(BF16)
        c2[...] = f2[s].astype(BF16)

    @pl.when(i < nv_ref[0])
    def _():
        hs = hs_ref[...].astype(BF16)
        a = jnp.dot(hs, c1[...], preferred_element_type=F32)
        b = jnp.dot(hs, c3[...], preferred_element_type=F32)
        act = a * _sigmoid(a) * b
        y = jnp.dot(act.astype(BF16), c2[...], preferred_element_type=F32)
        y_ref[...] = y.astype(y_ref.dtype)

    @pl.when(i >= nv_ref[0])
    def _():
        y_ref[...] = jnp.zeros_like(y_ref)


def _moe_experts(tile_expert, run_first, run_slot, run_next, n_valid, hs, w1, w3, w2, layer):
    NP, D = hs.shape
    tm = MOE_TM
    DE = w1.shape[-1]
    grid_spec = pltpu.PrefetchScalarGridSpec(
        num_scalar_prefetch=5,
        grid=(NP // tm,),
        in_specs=[pl.BlockSpec((tm, D), lambda i, te, fi, sl, nx, nv: (jnp.minimum(i, nv[0] - 1), 0)),
                  pl.BlockSpec(memory_space=pl.ANY),
                  pl.BlockSpec(memory_space=pl.ANY),
                  pl.BlockSpec(memory_space=pl.ANY)],
        out_specs=pl.BlockSpec((tm, D), lambda i, *_: (i, 0)),
        scratch_shapes=[pltpu.VMEM((2, D, DE), F32), pltpu.VMEM((2, D, DE), F32), pltpu.VMEM((2, DE, D), F32),
                        pltpu.VMEM((D, DE), BF16), pltpu.VMEM((D, DE), BF16), pltpu.VMEM((DE, D), BF16),
                        pltpu.SemaphoreType.DMA((2, 3))],
    )
    return pl.pallas_call(
        functools.partial(_moe_body, layer=layer),
        grid_spec=grid_spec,
        out_shape=jax.ShapeDtypeStruct((NP, D), BF16),
        compiler_params=_cp("arbitrary"),
        name="moe_experts",
    )(tile_expert, run_first, run_slot, run_next, n_valid, hs, w1, w3, w2)


def _moe(h2, re, w1, w3, w2, layer):
    N = h2.shape[0]
    tm = MOE_TM
    n_tiles = (2 * N) // tm + N_EXPERTS
    e_flat = re.reshape(2 * N)
    onehot = (e_flat[:, None] == jnp.arange(N_EXPERTS, dtype=jnp.int32)[None, :]).astype(jnp.int32)
    csum = jnp.cumsum(onehot, axis=0)
    rank = jnp.sum(onehot * (csum - 1), axis=1)
    counts = csum[-1]
    ptiles = (counts + tm - 1) // tm
    tile_end = jnp.cumsum(ptiles)
    tile_start = tile_end - ptiles
    pos = tile_start[e_flat] * tm + rank
    n_valid = tile_end[-1:].astype(jnp.int32)
    tile_ids = jnp.arange(n_tiles, dtype=jnp.int32)
    tile_expert = jnp.minimum(jnp.sum((tile_end[None, :] <= tile_ids[:, None]).astype(jnp.int32), axis=1), N_EXPERTS - 1)
    experts = jnp.arange(N_EXPERTS, dtype=jnp.int32)
    owns = ptiles > 0
    run_no = jnp.cumsum(owns.astype(jnp.int32)) - 1
    later = jnp.where(owns[None, :] & (experts[None, :] > experts[:, None]), experts[None, :], N_EXPERTS)
    next_run = jnp.min(later, axis=1)
    next_run = jnp.where(next_run < N_EXPERTS, next_run, -1).astype(jnp.int32)
    run_first = ((tile_ids == tile_start[tile_expert]) & (tile_ids < n_valid[0])).astype(jnp.int32)
    run_slot = (run_no[tile_expert] % 2).astype(jnp.int32)
    run_next = next_run[tile_expert]
    spread = jnp.arange(n_tiles * tm, dtype=jnp.int32) % N
    src = spread.at[pos].set(jnp.arange(2 * N, dtype=jnp.int32) // 2)
    rows = lambda a, idx: a.at[idx].get(mode="promise_in_bounds")
    hs = rows(h2, src)
    y = _moe_experts(tile_expert, run_first, run_slot, run_next, n_valid, hs, w1, w3, w2, layer)
    pos2 = pos.reshape(N, 2)
    return rows(y, pos2[:, 0]), rows(y, pos2[:, 1])


def kernel(x, c, ctx, c_ctx, mod_w, mod_b, norm1_g, norm2_g, w_in, w_out, shift_prev, shift_next, decay_w0, decay_up, iclr_a0, iclr_up, gate_up, k_k, k_a, r_k, gn_g, gn_b, q_norm_g, k_norm_g, mla_q_norm_g, mla_w_uq, mla_kv_norm_g, mla_w_ukv, router_gw, router_gb, router_ew, router_eb, exp_w1, exp_w3, exp_w2, final_norm_g):
    B, S, D = x.shape
    C = ctx.shape[1]
    T = C + S
    depth = mod_w.shape[0]
    assert C == TOK_TILE and S % TOK_TILE == 0 and B <= CTX_ROW
    n_ctx_tiles = C // TOK_TILE

    cc = jnp.zeros((SUBLANES, D), F32).at[:B].set(c).at[CTX_ROW].set(c_ctx)
    mods_all = [_modulation(cc, mod_w, mod_b, l).reshape(SUBLANES * N_MOD, 1, D) for l in range(depth)]
    X, h = _join_norm(ctx, x, norm1_g[0], mods_all[0], 0, 1)

    cos_g, sin_g = _rope_tables(T, C, GQA_HEAD)
    cos_m, sin_m = _rope_tables(T, C, MLA_ROPE)
    hid = jnp.arange(RWKV_W) // RWKV_HEAD
    head_sum = (hid[:, None] == hid[None, :]).astype(F32)

    for l in range(depth):
        mods = mods_all[l]
        w_r = w_in[l][:, :RWKV_COLS].astype(BF16)
        w_g = w_in[l][:, RWKV_COLS:RWKV_COLS + GQA_COLS].astype(BF16)
        w_m = jnp.pad(w_in[l][:, RWKV_COLS + GQA_COLS:], ((0, 0), (0, MLA_COLS_PAD - MLA_COLS))).astype(BF16)

        pr = _matmul(h.reshape(B * T, D), w_r).reshape(B, T, RWKV_COLS)

        wup_pad = jnp.pad(decay_up[l], ((0, 0), (0, ICLR_RANK), (0, 0)))
        aup_pad = jnp.pad(iclr_up[l], ((0, 0), (DECAY_RANK, 0), (0, 0)))
        r, v, kh, lw, b, kt, g, bonus = _rwkv_prep(pr, shift_prev[l], shift_next[l], decay_w0[l], wup_pad,
                                                  iclr_a0[l], aup_pad, gate_up[l], k_k[l], k_a[l], r_k[l], head_sum)
        yf, yb = _rwkv_scan(r, v, kh, lw, b, kt, C)
        rwkv_out = (yf, yb, bonus, g, gn_g[l], gn_b[l], head_sum)

        q, k, vv = _gqa_proj(h, w_g, q_norm_g[l], k_norm_g[l], cos_g, sin_g)
        o_g = _attention(q, k, vv, GQA_Q_HEADS, GQA_KV_HEADS, GQA_HEAD, GQA_HEAD, C)

        wq = mla_w_uq[l].reshape(MLA_Q_RANK, MLA_HEADS, MLA_NOPE + MLA_ROPE)
        wq = jnp.pad(wq, ((0, 0), (0, 0), (0, MLA_DK - MLA_NOPE - MLA_ROPE))).reshape(MLA_Q_RANK, MLA_HEADS * MLA_DK)
        wkv = mla_w_ukv[l].reshape(MLA_KV_RANK, MLA_HEADS, MLA_NOPE + MLA_V)
        wkv = jnp.concatenate([wkv[:, :, :MLA_NOPE].reshape(MLA_KV_RANK, -1), wkv[:, :, MLA_NOPE:].reshape(MLA_KV_RANK, -1)], axis=1)
        qm, km, vm = _mla_proj(h, w_m, mla_q_norm_g[l], wq.astype(BF16), mla_kv_norm_g[l], wkv.astype(BF16), cos_m, sin_m)
        o_m = _attention(qm, km, vm, MLA_HEADS, MLA_HEADS, MLA_DK, MLA_V, C)

        last = l + 1 == depth
        wr = jnp.pad(jnp.concatenate([router_gw[l], router_ew[l]], axis=1), ((0, 0), (0, LANES - N_GROUPS - N_EXPERTS)))
        br = jnp.pad(jnp.concatenate([router_gb[l], router_eb[l]]), (0, LANES - N_GROUPS - N_EXPERTS)).reshape(1, LANES)
        X, h2, rw, re = _out_proj_router(rwkv_out, [o_g, o_m], w_out[l].astype(BF16), X, mods, C,
                                         TOK_TILE if last else OUT_TILE, last, norm2_g[l], wr, br)
        rows = X.shape[1]
        ctx_tiles = 0 if last else n_ctx_tiles

        y0, y1 = _moe(h2.reshape(B * rows, D), re.reshape(B * rows, LANES)[:, :2], exp_w1, exp_w3, exp_w2, l)
        y0 = y0.reshape(B, rows, D)
        y1 = y1.reshape(B, rows, D)
        if last:
            out = _moe_residual_final(X, y0, y1, rw, mods, 5, ctx_tiles, final_norm_g)
        else:
            X, h = _moe_residual_norm(X, y0, y1, rw, mods, 5, ctx_tiles, norm1_g[l + 1], mods_all[l + 1])
    return out
```

```python
import functools
import math

import jax
import jax.numpy as jnp
from jax import lax
from jax.experimental import pallas as pl
from jax.experimental.pallas import tpu as pltpu

F32 = jnp.float32
BF16 = jnp.bfloat16

V7X_VMEM_BYTES = 64 * 1024 * 1024
VMEM_LIMIT = V7X_VMEM_BYTES - 8 * 1024 * 1024
LANES = 128
SUBLANES = 8

GRID_W = 64
ROPE_THETA = 10000.0
NORM_EPS = 1e-6
GN_EPS = 64e-5
DECAY_SCALE = math.exp(-0.5)
LOG2E = math.log2(math.e)

RWKV_HEAD = 64
RWKV_W = 512
DECAY_RANK = 64
ICLR_RANK = 64
GATE_RANK = 128
RWKV_COLS = 3 * RWKV_W + DECAY_RANK + ICLR_RANK + GATE_RANK
LOWRANK_OFF = 3 * RWKV_W
GATE_OFF = LOWRANK_OFF + DECAY_RANK + ICLR_RANK
CHUNK = 64
SCAN_CHUNKS = 2

GQA_HEAD = 128
GQA_Q_HEADS = 8
GQA_KV_HEADS = 2
GQA_Q_COLS = GQA_Q_HEADS * GQA_HEAD
GQA_KV_COLS = GQA_KV_HEADS * GQA_HEAD
GQA_COLS = GQA_Q_COLS + 2 * GQA_KV_COLS

MLA_HEADS = 4
MLA_NOPE = 128
MLA_ROPE = 64
MLA_V = 128
MLA_Q_RANK = 384
MLA_KV_RANK = 256
MLA_COLS = MLA_Q_RANK + MLA_KV_RANK + MLA_ROPE
MLA_COLS_PAD = 768
MLA_DK = 2 * LANES

N_GROUPS = 4
EXPERTS_PER_GROUP = 8
N_EXPERTS = 32
D_EXPERT = 256
MOE_TM = 256
WEIGHT_DMA_PRIORITY = 1

TOK_TILE = 256
PROJ_TILE = 768
OUT_TILE = 384
N_MOD = 6
CTX_ROW = 4


def _cp(*sem):
    return pltpu.CompilerParams(dimension_semantics=sem, vmem_limit_bytes=VMEM_LIMIT)


def _sigmoid(x):
    return 1.0 / (1.0 + jnp.exp(-x))


def _bf16_terms(x, n):
    terms = []
    for _ in range(n):
        t = x.astype(BF16)
        terms.append(t)
        x = x - t.astype(F32)
    return terms


def _dot_terms(x, w, nx, nw):
    xs = _bf16_terms(x, nx)
    ws = _bf16_terms(w, nw)
    acc = None
    for i in range(nx):
        for j in range(nw):
            if i + j < max(nx, nw):
                p = jnp.dot(xs[i], ws[j], preferred_element_type=F32)
                acc = p if acc is None else acc + p
    return acc


def _mod_body(c_ref, w_ref, b_ref, o_ref):
    c = c_ref[...]
    s = c * _sigmoid(c)
    o_ref[...] = _dot_terms(s, w_ref[...], 2, 2) + b_ref[...]


def _modulation(cc, mod_w, mod_b, layer):
    L, D, N = mod_w.shape
    tn = 1024
    return pl.pallas_call(
        _mod_body,
        grid=(N // tn,),
        in_specs=[pl.BlockSpec((SUBLANES, D), lambda j: (0, 0)),
                  pl.BlockSpec((None, D, tn), lambda j: (layer, 0, j)),
                  pl.BlockSpec((None, 1, tn), lambda j: (layer, 0, j))],
        out_specs=pl.BlockSpec((SUBLANES, tn), lambda j: (0, j)),
        out_shape=jax.ShapeDtypeStruct((SUBLANES, N), F32),
        compiler_params=_cp("parallel"),
        name="modulation",
    )(cc, mod_w, mod_b.reshape(L, 1, N))


def _mod_spec(which, n_ctx_tiles, D):
    return pl.BlockSpec((None, 1, D), lambda b, i: (jnp.where(i < n_ctx_tiles, CTX_ROW, b) * N_MOD + which, 0, 0))


def _rms(x, g):
    return x * lax.rsqrt(jnp.mean(x * x, axis=-1, keepdims=True) + NORM_EPS) * g


def _join_norm_body(ctx_ref, x_ref, g_ref, sh_ref, sc_ref, x_o, h_o, *, n_ctx_tiles):
    i = pl.program_id(1)

    def emit(src_ref):
        x = src_ref[...]
        x_o[...] = x
        h_o[...] = (_rms(x, g_ref[...]) * (1.0 + sc_ref[...]) + sh_ref[...]).astype(h_o.dtype)

    @pl.when(i < n_ctx_tiles)
    def _():
        emit(ctx_ref)

    @pl.when(i >= n_ctx_tiles)
    def _():
        emit(x_ref)


def _route(logits):
    lane = lax.broadcasted_iota(jnp.int32, logits.shape, 1)
    lane_f = lane.astype(F32)
    neg = jnp.float32(-1e30)
    far = jnp.float32(1e9)
    first_at = lambda hit: jnp.min(jnp.where(hit, lane_f, far), axis=-1, keepdims=True).astype(jnp.int32)
    gl = jnp.where(lane < N_GROUPS, logits, neg)
    gmax = jnp.max(gl, axis=-1, keepdims=True)
    gidx = first_at(gl == gmax)
    p_sel = 1.0 / jnp.sum(jnp.exp(gl - gmax), axis=-1, keepdims=True)
    lo = N_GROUPS + gidx * EXPERTS_PER_GROUP
    el = jnp.where((lane >= lo) & (lane < lo + EXPERTS_PER_GROUP), logits, neg)
    m1 = jnp.max(el, axis=-1, keepdims=True)
    i1 = first_at(el == m1)
    el2 = jnp.where(lane == i1, neg, el)
    m2 = jnp.max(el2, axis=-1, keepdims=True)
    i2 = first_at(el2 == m2)
    t = jnp.exp(m2 - m1)
    w1 = p_sel / (1.0 + t)
    w2 = p_sel * t / (1.0 + t)
    rw = jnp.where(lane == 0, w1, jnp.where(lane == 1, w2, 0.0))
    re = jnp.where(lane == 0, i1 - N_GROUPS, jnp.where(lane == 1, i2 - N_GROUPS, 0))
    return rw, re


def _join_norm(ctx, x, g, mods, shift_i, scale_i):
    B, C, D = ctx.shape
    tt = TOK_TILE
    n_ctx_tiles = C // tt
    T = C + x.shape[1]
    tok = pl.BlockSpec((None, tt, D), lambda b, i: (b, i, 0))
    return pl.pallas_call(
        functools.partial(_join_norm_body, n_ctx_tiles=n_ctx_tiles),
        grid=(B, T // tt),
        in_specs=[pl.BlockSpec((None, tt, D), lambda b, i: (b, jnp.minimum(i, n_ctx_tiles - 1), 0)),
                  pl.BlockSpec((None, tt, D), lambda b, i: (b, jnp.maximum(i - n_ctx_tiles, 0), 0)),
                  pl.BlockSpec((1, D), lambda b, i: (0, 0)),
                  _mod_spec(shift_i, n_ctx_tiles, D),
                  _mod_spec(scale_i, n_ctx_tiles, D)],
        out_specs=[tok, tok],
        out_shape=[jax.ShapeDtypeStruct((B, T, D), F32), jax.ShapeDtypeStruct((B, T, D), BF16)],
        compiler_params=_cp("parallel", "parallel"),
        name="join_norm",
    )(ctx, x, g.reshape(1, D), mods, mods)


def _mm_body(a_ref, w_ref, o_ref):
    o_ref[...] = jnp.dot(a_ref[...], w_ref[...], preferred_element_type=F32).astype(o_ref.dtype)


def _matmul(a, w, tm=PROJ_TILE):
    M, K = a.shape
    N = w.shape[1]
    assert M % tm == 0
    return pl.pallas_call(
        _mm_body,
        grid=(M // tm,),
        in_specs=[pl.BlockSpec((tm, K), lambda i: (i, 0)),
                  pl.BlockSpec((K, N), lambda i: (0, 0))],
        out_specs=pl.BlockSpec((tm, N), lambda i: (i, 0)),
        out_shape=jax.ShapeDtypeStruct((M, N), F32),
        compiler_params=_cp("parallel"),
        name="token_matmul",
    )(a, w)


def _rwkv_readout_value(yf_ref, yb_ref, bon_ref, gate_ref, gng_ref, gnb_ref, e_ref):
    y = yf_ref[...] + yb_ref[...]
    head_mean = e_ref[...] * (1.0 / RWKV_HEAD)
    mu = _dot_terms(y, head_mean, 2, 1)
    yc = y - mu
    var = _dot_terms(yc * yc, head_mean, 2, 1)
    yn = yc * lax.rsqrt(var + GN_EPS) * gng_ref[...] + gnb_ref[...]
    return (yn + bon_ref[...]) * gate_ref[...]


def _out_proj_router_body(*refs, n_parts, n_ctx, tm, tile0):
    readout_refs = refs[:7]
    a_refs = refs[7:7 + n_parts]
    w_ref, x_ref, ml_ref, mc_ref, g_ref, wr_ref, br_ref, x_o, h_o, rw_o, re_o = refs[7 + n_parts:]
    o_r = _rwkv_readout_value(*readout_refs).astype(BF16)
    k0 = o_r.shape[-1]
    acc = jnp.dot(o_r, w_ref[0:k0, :], preferred_element_type=F32)
    for a_ref in a_refs:
        k1 = k0 + a_ref.shape[-1]
        acc = acc + jnp.dot(a_ref[...], w_ref[k0:k1, :], preferred_element_type=F32)
        k0 = k1
    row = (pl.program_id(1) + tile0) * tm + lax.broadcasted_iota(jnp.int32, (tm, 1), 0)
    is_ctx = row < n_ctx
    mod = lambda which: jnp.where(is_ctx, mc_ref[which], ml_ref[which])
    x = x_ref[...] + mod(2) * acc
    x_o[...] = x
    h = _rms(x, g_ref[...]) * (1.0 + mod(4)) + mod(3)
    h_o[...] = h
    h_hi, h_lo = _bf16_terms(h, 2)
    w_hi, w_lo = _bf16_terms(wr_ref[...], 2)
    both = jnp.dot(h_hi, jnp.concatenate([w_hi, w_lo], axis=1), preferred_element_type=F32)
    logits = (both[:, 0:LANES] + both[:, LANES:2 * LANES]
              + jnp.dot(h_lo, w_hi, preferred_element_type=F32) + br_ref[...])
    rw, re = _route(logits)
    rw_o[...] = rw
    re_o[...] = re


def _out_proj_router(rwkv, parts, w, X, mods, n_ctx, tm, latents_only, g2, wr, br):
    B, T, D = X.shape
    yf, yb, bonus, gate, gn_g, gn_b, head_sum = rwkv
    W = yf.shape[-1]
    tile0 = n_ctx // tm if latents_only else 0
    assert T % tm == 0 and (n_ctx % tm == 0 or not latents_only)
    assert W + sum(p.shape[-1] for p in parts) == w.shape[0]
    rows_out = T - tile0 * tm
    mods6 = mods.reshape(SUBLANES, N_MOD, 1, D)
    tok_in = lambda wd: pl.BlockSpec((None, tm, wd), lambda b, i: (b, i + tile0, 0))
    tok_out = lambda wd: pl.BlockSpec((None, tm, wd), lambda b, i: (b, i, 0))
    shape = lambda wd, dt: jax.ShapeDtypeStruct((B, rows_out, wd), dt)
    const = lambda a: pl.BlockSpec(a.shape, lambda b, i: (0,) * a.ndim)
    rwkv_args = [yf, yb, bonus, gate, gn_g.reshape(1, W), gn_b.reshape(1, W), head_sum]
    return pl.pallas_call(
        functools.partial(_out_proj_router_body, n_parts=len(parts), n_ctx=n_ctx, tm=tm, tile0=tile0),
        grid=(B, T // tm - tile0),
        in_specs=[tok_in(W)] * 4 + [const(a) for a in rwkv_args[4:]] + [tok_in(p.shape[-1]) for p in parts] + [
                  pl.BlockSpec(w.shape, lambda b, i: (0, 0)),
                  tok_in(D),
                  pl.BlockSpec((None, N_MOD, 1, D), lambda b, i: (b, 0, 0, 0)),
                  pl.BlockSpec((None, N_MOD, 1, D), lambda b, i: (CTX_ROW, 0, 0, 0)),
                  pl.BlockSpec((1, D), lambda b, i: (0, 0)),
                  pl.BlockSpec((D, LANES), lambda b, i: (0, 0)),
                  pl.BlockSpec((1, LANES), lambda b, i: (0, 0))],
        out_specs=[tok_out(D), tok_out(D), tok_out(LANES), tok_out(LANES)],
        out_shape=[shape(D, F32), shape(D, F32), shape(LANES, F32), shape(LANES, jnp.int32)],
        compiler_params=_cp("parallel", "parallel"),
        name="out_proj_router",
    )(*rwkv_args, *parts, w, X, mods6, mods6, g2.reshape(1, D), wr, br)


def _moe_residual(x_ref, y0_ref, y1_ref, rw_ref, gate_ref):
    rw = rw_ref[...]
    moe = rw[:, 0:1] * y0_ref[...].astype(F32) + rw[:, 1:2] * y1_ref[...].astype(F32)
    return x_ref[...] + gate_ref[...] * moe


def _moe_residual_norm_body(x_ref, y0_ref, y1_ref, rw_ref, gate_ref, g_ref, sh_ref, sc_ref, x_o, h_o):
    x = _moe_residual(x_ref, y0_ref, y1_ref, rw_ref, gate_ref)
    x_o[...] = x
    h_o[...] = (_rms(x, g_ref[...]) * (1.0 + sc_ref[...]) + sh_ref[...]).astype(h_o.dtype)


def _moe_residual_final_body(x_ref, y0_ref, y1_ref, rw_ref, gate_ref, g_ref, o_ref):
    o_ref[...] = _rms(_moe_residual(x_ref, y0_ref, y1_ref, rw_ref, gate_ref), g_ref[...])


def _moe_residual_norm(X, Y0, Y1, rw, mods, gate_i, n_ctx_tiles, g_next, mods_next):
    B, T, D = X.shape
    tt = TOK_TILE
    blk = pl.BlockSpec((None, tt, D), lambda b, i: (b, i, 0))
    return pl.pallas_call(
        _moe_residual_norm_body,
        grid=(B, T // tt),
        in_specs=[blk, blk, blk, pl.BlockSpec((None, tt, LANES), lambda b, i: (b, i, 0)),
                  _mod_spec(gate_i, n_ctx_tiles, D), pl.BlockSpec((1, D), lambda b, i: (0, 0)),
                  _mod_spec(0, n_ctx_tiles, D), _mod_spec(1, n_ctx_tiles, D)],
        out_specs=[blk, blk],
        out_shape=[jax.ShapeDtypeStruct((B, T, D), F32), jax.ShapeDtypeStruct((B, T, D), BF16)],
        compiler_params=_cp("parallel", "parallel"),
        name="moe_residual_norm",
    )(X, Y0, Y1, rw, mods, g_next.reshape(1, D), mods_next, mods_next)


def _moe_residual_final(X, Y0, Y1, rw, mods, gate_i, n_ctx_tiles, g_final):
    B, T, D = X.shape
    tt = TOK_TILE
    S = T - n_ctx_tiles * tt
    lat = lambda w: pl.BlockSpec((None, tt, w), lambda b, i: (b, i + n_ctx_tiles, 0))
    return pl.pallas_call(
        _moe_residual_final_body,
        grid=(B, S // tt),
        in_specs=[lat(D), lat(D), lat(D), lat(LANES),
                  pl.BlockSpec((None, 1, D), lambda b, i: (b * N_MOD + gate_i, 0, 0)),
                  pl.BlockSpec((1, D), lambda b, i: (0, 0))],
        out_specs=pl.BlockSpec((None, tt, D), lambda b, i: (b, i, 0)),
        out_shape=jax.ShapeDtypeStruct((B, S, D), F32),
        compiler_params=_cp("parallel", "parallel"),
        name="moe_residual_final",
    )(X, Y0, Y1, rw, mods, g_final.reshape(1, D))


def _rwkv_prep_body(p_ref, pv_ref, nx_ref, mup_ref, mun_ref, w0_ref, wup_ref, a0_ref, aup_ref, gup_ref,
                    kk_ref, ka_ref, rk_ref, e_ref,
                    r_o, v_o, kh_o, lw_o, b_o, kt_o, g_o, bon_o, *, n_tiles, tt):
    i = pl.program_id(1)
    p = p_ref[...]
    seq_first = i <= 1
    seq_last = (i == 0) | (i == n_tiles - 1)
    prow = jnp.where(seq_first, 0.0, pv_ref[SUBLANES - 1:SUBLANES, :])
    nrow = jnp.where(seq_last, 0.0, nx_ref[0:1, :])
    rid = lax.broadcasted_iota(jnp.int32, (tt, 1), 0)
    prev = jnp.where(rid == 0, prow, pltpu.roll(p, 1, 0))
    nxt = jnp.where(rid == tt - 1, nrow, pltpu.roll(p, tt - 1, 0))
    z = p + mup_ref[...] * (prev - p) + mun_ref[...] * (nxt - p)

    W = RWKV_W
    r = z[:, 0:W]
    k = z[:, W:2 * W]
    v = z[:, 2 * W:3 * W]
    lowrank = z[:, LOWRANK_OFF:LOWRANK_OFF + LANES]
    gd = z[:, GATE_OFF:GATE_OFF + GATE_RANK]
    head_sum = e_ref[...]

    kap = k * kk_ref[...]
    ss = _dot_terms(kap * kap, head_sum, 2, 1)
    khat = kap * lax.rsqrt(ss + 1e-12)
    wd_t = jnp.tanh(lowrank)
    g_o[...] = _dot_terms(_sigmoid(gd), gup_ref[...], 1, 1)
    r_o[...] = r
    v_o[...] = v
    kh_o[...] = khat
    kt_sum = None
    for d in range(2):
        dec = _dot_terms(wd_t, wup_ref[d], 2, 2)
        lw_o[d] = -DECAY_SCALE * _sigmoid(w0_ref[d:d + 1, :] + dec)
        a = _sigmoid(a0_ref[d:d + 1, :] + _dot_terms(lowrank, aup_ref[d], 1, 1))
        kt = k * (1.0 + (a - 1.0) * ka_ref[...])
        kt_o[d] = kt
        b_o[d] = a * khat
        kt_sum = kt if kt_sum is None else kt_sum + kt
    bsum = _dot_terms(r * kt_sum * rk_ref[...], head_sum, 2, 1)
    bon_o[...] = bsum * v


def _rwkv_prep(pr, mu_prev, mu_next, w0, wup_pad, a0, aup_pad, g_up, k_k, k_a, r_k, head_sum):
    B, T, _ = pr.shape
    tt = TOK_TILE
    W = RWKV_W
    n_tiles = T // tt
    n8 = tt // SUBLANES
    row = lambda v: v.reshape(1, -1)
    full = lambda a: pl.BlockSpec(a.shape, lambda b, i: (0,) * a.ndim)
    tok = pl.BlockSpec((None, tt, W), lambda b, i: (b, i, 0))
    tok2 = pl.BlockSpec((None, 2, tt, W), lambda b, i: (b, 0, i, 0))
    s1 = jax.ShapeDtypeStruct((B, T, W), F32)
    s2 = jax.ShapeDtypeStruct((B, 2, T, W), F32)
    consts = [row(mu_prev), row(mu_next), w0, wup_pad, a0, aup_pad, g_up, row(k_k), row(k_a), row(r_k), head_sum]
    return pl.pallas_call(
        functools.partial(_rwkv_prep_body, n_tiles=n_tiles, tt=tt),
        grid=(B, n_tiles),
        in_specs=[pl.BlockSpec((None, tt, RWKV_COLS), lambda b, i: (b, i, 0)),
                  pl.BlockSpec((None, SUBLANES, RWKV_COLS), lambda b, i: (b, jnp.maximum(i * n8 - 1, 0), 0)),
                  pl.BlockSpec((None, SUBLANES, RWKV_COLS), lambda b, i: (b, jnp.minimum((i + 1) * n8, T // SUBLANES - 1), 0)),
                  ] + [full(a) for a in consts],
        out_specs=[tok, tok, tok, tok2, tok2, tok2, tok, tok],
        out_shape=[s1, s1, s1, s2, s2, s2, s1, s1],
        compiler_params=_cp("parallel", "parallel"),
        name="rwkv_prep",
    )(pr, pr, pr, *consts)


def _stack_heads(x):
    lane = lax.broadcasted_iota(jnp.int32, x.shape, 1)
    first = lane < RWKV_HEAD
    return jnp.concatenate([jnp.where(first, x, 0.0), jnp.where(first, 0.0, x)], axis=0)


def _dot(a, b):
    return jnp.dot(a.astype(BF16), b.astype(BF16), preferred_element_type=F32)


def _dot_nt(a, b):
    return lax.dot_general(a.astype(BF16), b.astype(BF16), (((1,), (1,)), ((), ())), preferred_element_type=F32)


def _dot_tn(a, b):
    return jnp.dot(a.T.astype(BF16), b.astype(BF16), preferred_element_type=F32)


def _chunk_operands(r, v, kh, lw, b, kt, reverse):
    L = CHUNK
    ti = lax.broadcasted_iota(jnp.int32, (L, L), 0)
    tj = lax.broadcasted_iota(jnp.int32, (L, L), 1)
    tri = jnp.where((ti <= tj) if reverse else (ti >= tj), 1.0, 0.0)
    lam = _dot_terms(tri, lw, 1, 3)
    tot = lam[0:1, :] if reverse else lam[L - 1:L, :]
    e_n = jnp.exp(-lam)
    e_g = jnp.exp(tot - lam)
    full = dict(A=kh * jnp.exp(lam - lw), R=r * jnp.exp(lam), Kn=kt * e_n, Bn=b * e_n, Kg=kt * e_g, Bg=b * e_g, V=v)
    e_tot = jnp.exp(tot)
    pairs = []
    for p in range(RWKV_W // LANES):
        sl = slice(p * LANES, (p + 1) * LANES)
        ops = {k: a[:, sl] for k, a in full.items()}
        ops["e_tot"] = e_tot[:, sl]
        ops["reverse"] = reverse
        pairs.append(ops)
    return pairs


def _chunk_masks(reverse):
    L = CHUNK
    t = lax.broadcasted_iota(jnp.int32, (L, 2 * L), 0)
    i = lax.broadcasted_iota(jnp.int32, (L, 2 * L), 1) & (L - 1)
    before = (i > t) if reverse else (i < t)
    return before, before | (i == t), i == t


def _chunks_prepare(chains):
    L = CHUNK
    P2 = 2 * L
    n = len(chains)
    masks = {rev: _chunk_masks(rev) for rev in {c["reverse"] for c in chains}}
    strict = [masks[c["reverse"]][0] for c in chains]
    incl = [masks[c["reverse"]][1] for c in chains]
    eye = masks[chains[0]["reverse"]][2]
    bi = lax.broadcasted_iota(jnp.int32, (P2, P2), 0)
    bj = lax.broadcasted_iota(jnp.int32, (P2, P2), 1)
    same_head = (bi >= L) == (bj >= L)
    eye2 = bi == bj
    stack = _stack_heads
    cat0 = lambda *xs: jnp.concatenate(xs, axis=0)
    cat1 = lambda *xs: jnp.concatenate(xs, axis=1)

    big = [_dot_nt(cat0(c["A"], c["R"]), cat0(stack(c["Bn"]), stack(c["Kn"]))) for c in chains]
    Mb = [jnp.where(strict[i], big[i][0:L, 0:P2], 0.0) for i in range(n)]
    Mkv = [jnp.where(strict[i], big[i][0:L, P2:2 * P2], 0.0) for i in range(n)]
    Pb = [jnp.where(incl[i], big[i][L:P2, 0:P2], 0.0) for i in range(n)]
    Pkv = [jnp.where(incl[i], big[i][L:P2, P2:2 * P2], 0.0) for i in range(n)]

    Pw = [-m for m in Mb]
    Tm = [jnp.where(eye, 1.0, 0.0) + p for p in Pw]
    Pw = [_dot(p, stack(p)) for p in Pw]
    for _ in range(int(math.log2(L)) - 2):
        PT = [_dot(cat0(p, t), stack(p)) for p, t in zip(Pw, Tm)]
        Tm = [t + pt[L:P2] for t, pt in zip(Tm, PT)]
        Pw = [pt[0:L] for pt in PT]
    Tm = [t + _dot(t, stack(p)) for t, p in zip(Tm, Pw)]

    Vs = [stack(c["V"]) for c in chains]
    MPV = [_dot(cat0(Mkv[i], Pkv[i]), Vs[i]) for i in range(n)]
    TAM = [_dot(Tm[i], cat1(stack(chains[i]["A"]), stack(MPV[i][0:L]))) for i in range(n)]
    PB = [_dot(Pb[i], cat1(stack(TAM[i][:, 0:P2]), stack(TAM[i][:, P2:2 * P2]))) for i in range(n)]
    BG = [_dot_tn(chains[i]["Bg"], TAM[i]) for i in range(n)]
    KV = [_dot_tn(chains[i]["Kg"], chains[i]["V"]) for i in range(n)]
    prepared = []
    for i in range(n):
        RA = chains[i]["R"] - PB[i][:, 0:P2]
        G2 = jnp.where(eye2, chains[i]["e_tot"], 0.0) - jnp.where(same_head, BG[i][:, 0:P2], 0.0)
        H2 = jnp.where(same_head, KV[i] - BG[i][:, P2:2 * P2], 0.0)
        prepared.append((cat0(RA, G2), MPV[i][L:P2] - PB[i][:, P2:2 * P2], H2))
    return prepared


def _chunk_advance(prepared, state):
    lhs, y0, h2 = prepared
    out = _dot(lhs, state)
    return out[0:CHUNK] + y0, out[CHUNK:] + h2


def _rwkv_scan_body(rf_ref, vf_ref, khf_ref, rb_ref, vb_ref, khb_ref, lwf_ref, bf_ref, ktf_ref, lwb_ref, bb_ref, ktb_ref,
                    yf_ref, yb_ref, s_ref):
    @pl.when(pl.program_id(1) == 0)
    def _():
        s_ref[...] = jnp.zeros_like(s_ref)

    n_pairs = RWKV_W // LANES
    fwd_refs = (rf_ref, vf_ref, khf_ref, lwf_ref, bf_ref, ktf_ref)
    bwd_refs = (rb_ref, vb_ref, khb_ref, lwb_ref, bb_ref, ktb_ref)
    slots = []
    for c in range(SCAN_CHUNKS):
        lo_f = c * CHUNK
        lo_b = (SCAN_CHUNKS - 1 - c) * CHUNK
        slots.append((_chunk_operands(*(ref[lo_f:lo_f + CHUNK, :] for ref in fwd_refs), False), lo_f,
                      _chunk_operands(*(ref[lo_b:lo_b + CHUNK, :] for ref in bwd_refs), True), lo_b))
    prepared = _chunks_prepare([ch for s in slots for ch in s[0] + s[2]])
    states = [s_ref[i] for i in range(2 * n_pairs)]
    for c, (_, lo_f, _, lo_b) in enumerate(slots):
        ys = []
        for i in range(2 * n_pairs):
            y, states[i] = _chunk_advance(prepared[c * 2 * n_pairs + i], states[i])
            ys.append(y)
        yf_ref[lo_f:lo_f + CHUNK, :] = jnp.concatenate(ys[:n_pairs], axis=1)
        yb_ref[lo_b:lo_b + CHUNK, :] = jnp.concatenate(ys[n_pairs:], axis=1)
    for i in range(2 * n_pairs):
        s_ref[i] = states[i]


def _rwkv_scan(r, v, kh, lw, b, kt, n_ctx):
    B, T, W = r.shape
    blk = SCAN_CHUNKS * CHUNK
    assert T % blk == 0 and n_ctx % blk == 0
    nc = T // blk
    ncc = n_ctx // blk
    rev = lambda j: jnp.where(j < ncc, ncc - 1 - j, nc + ncc - 1 - j)
    fwd1 = pl.BlockSpec((None, blk, W), lambda bb, j: (bb, j, 0))
    bwd1 = pl.BlockSpec((None, blk, W), lambda bb, j: (bb, rev(j), 0))
    fwd2 = pl.BlockSpec((None, None, blk, W), lambda bb, j: (bb, 0, j, 0))
    bwd2 = pl.BlockSpec((None, None, blk, W), lambda bb, j: (bb, 1, rev(j), 0))
    out = jax.ShapeDtypeStruct((B, T, W), F32)
    return pl.pallas_call(
        _rwkv_scan_body,
        grid=(B, nc),
        in_specs=[fwd1, fwd1, fwd1, bwd1, bwd1, bwd1, fwd2, fwd2, fwd2, bwd2, bwd2, bwd2],
        out_specs=[fwd1, bwd1],
        out_shape=[out, out],
        scratch_shapes=[pltpu.VMEM((2 * W // LANES, LANES, LANES), F32)],
        compiler_params=_cp("parallel", "arbitrary"),
        name="rwkv_scan",
    )(r, v, kh, r, v, kh, lw, b, kt, lw, b, kt)


def _rope(y, cos, sin_signed, quarter):
    lane = lax.broadcasted_iota(jnp.int32, y.shape, 1)
    first = (lane & (2 * quarter - 1)) < quarter
    partner = jnp.where(first, pltpu.roll(y, LANES - quarter, 1), pltpu.roll(y, quarter, 1))
    return y * cos + partner * sin_signed


def _gqa_prep_body(p, qg_ref, kg_ref, cos_ref, sin_ref, q_o, k_o, v_o):
    cos = cos_ref[...]
    sin = sin_ref[...]
    scale = GQA_HEAD ** -0.5 * LOG2E
    for h in range(GQA_Q_HEADS):
        sl = slice(h * GQA_HEAD, (h + 1) * GQA_HEAD)
        q = _rms(p[:, sl], qg_ref[...])
        q_o[:, sl] = (_rope(q, cos, sin, GQA_HEAD // 4) * scale).astype(q_o.dtype)
    for h in range(GQA_KV_HEADS):
        sl = slice(h * GQA_HEAD, (h + 1) * GQA_HEAD)
        k = _rms(p[:, GQA_Q_COLS + h * GQA_HEAD:GQA_Q_COLS + (h + 1) * GQA_HEAD], kg_ref[...])
        k_o[:, sl] = _rope(k, cos, sin, GQA_HEAD // 4).astype(k_o.dtype)
    v_o[...] = p[:, GQA_Q_COLS + GQA_KV_COLS:GQA_COLS].astype(v_o.dtype)


def _row_blocks(n_rows):
    return [slice(r, r + TOK_TILE) for r in range(0, n_rows, TOK_TILE)]


def _gqa_proj_body(h_ref, w_ref, qg_ref, kg_ref, cos_ref, sin_ref, q_o, k_o, v_o):
    for rows in _row_blocks(h_ref.shape[0]):
        p = jnp.dot(h_ref[rows, :], w_ref[...], preferred_element_type=F32)
        _gqa_prep_body(p, qg_ref, kg_ref, cos_ref.at[rows, :], sin_ref.at[rows, :],
                       q_o.at[rows, :], k_o.at[rows, :], v_o.at[rows, :])


def _gqa_proj(h, w, q_norm_g, k_norm_g, cos, sin):
    B, T, D = h.shape
    tt = PROJ_TILE
    tok = lambda w_: pl.BlockSpec((None, tt, w_), lambda b, i: (b, i, 0))
    row = pl.BlockSpec((1, GQA_HEAD), lambda b, i: (0, 0))
    tab = pl.BlockSpec((tt, LANES), lambda b, i: (i, 0))
    return pl.pallas_call(
        _gqa_proj_body,
        grid=(B, T // tt),
        in_specs=[tok(D), pl.BlockSpec(w.shape, lambda b, i: (0, 0)), row, row, tab, tab],
        out_specs=[tok(GQA_Q_COLS), tok(GQA_KV_COLS), tok(GQA_KV_COLS)],
        out_shape=[jax.ShapeDtypeStruct((B, T, GQA_Q_COLS), BF16),
                   jax.ShapeDtypeStruct((B, T, GQA_KV_COLS), BF16),
                   jax.ShapeDtypeStruct((B, T, GQA_KV_COLS), BF16)],
        compiler_params=_cp("parallel", "parallel"),
        name="gqa_proj",
    )(h, w, q_norm_g.reshape(1, -1), k_norm_g.reshape(1, -1), cos, sin)


def _mla_prep_body(p, qg_ref, wq_ref, kvg_ref, wkv_ref, cos_ref, sin_ref, q_o, k_o, v_o):
    cos = cos_ref[...]
    sin = sin_ref[...]
    scale = (MLA_NOPE + MLA_ROPE) ** -0.5 * LOG2E
    cq = _rms(p[:, 0:MLA_Q_RANK], qg_ref[...])
    q = jnp.dot(cq.astype(BF16), wq_ref[...], preferred_element_type=F32) * scale
    ckv = _rms(p[:, MLA_Q_RANK:MLA_Q_RANK + MLA_KV_RANK], kvg_ref[...])
    kv = jnp.dot(ckv.astype(BF16), wkv_ref[...], preferred_element_type=F32)
    kr = _rope(p[:, MLA_Q_RANK + MLA_KV_RANK:MLA_COLS_PAD], cos, sin, MLA_ROPE // 4).astype(k_o.dtype)
    for h in range(MLA_HEADS):
        lo = h * MLA_DK
        q_o[:, lo:lo + LANES] = q[:, lo:lo + LANES].astype(q_o.dtype)
        q_o[:, lo + LANES:lo + MLA_DK] = _rope(q[:, lo + LANES:lo + MLA_DK], cos, sin, MLA_ROPE // 4).astype(q_o.dtype)
        k_o[:, lo:lo + LANES] = kv[:, h * MLA_NOPE:(h + 1) * MLA_NOPE].astype(k_o.dtype)
        k_o[:, lo + LANES:lo + MLA_DK] = kr
    v_o[...] = kv[:, MLA_HEADS * MLA_NOPE:].astype(v_o.dtype)


def _mla_proj_body(h_ref, w_ref, qg_ref, wq_ref, kvg_ref, wkv_ref, cos_ref, sin_ref, q_o, k_o, v_o):
    for rows in _row_blocks(h_ref.shape[0]):
        p = jnp.dot(h_ref[rows, :], w_ref[...], preferred_element_type=F32)
        _mla_prep_body(p, qg_ref, wq_ref, kvg_ref, wkv_ref, cos_ref.at[rows, :], sin_ref.at[rows, :],
                       q_o.at[rows, :], k_o.at[rows, :], v_o.at[rows, :])


def _mla_proj(h, w, q_norm_g, wq, kv_norm_g, wkv, cos, sin):
    B, T, D = h.shape
    tt = PROJ_TILE
    tok = lambda w_: pl.BlockSpec((None, tt, w_), lambda b, i: (b, i, 0))
    full = lambda a: pl.BlockSpec(a.shape, lambda b, i: (0,) * a.ndim)
    tab = pl.BlockSpec((tt, LANES), lambda b, i: (i, 0))
    qg = q_norm_g.reshape(1, -1)
    kvg = kv_norm_g.reshape(1, -1)
    return pl.pallas_call(
        _mla_proj_body,
        grid=(B, T // tt),
        in_specs=[tok(D), full(w), full(qg), full(wq), full(kvg), full(wkv), tab, tab],
        out_specs=[tok(MLA_HEADS * MLA_DK), tok(MLA_HEADS * MLA_DK), tok(MLA_HEADS * MLA_V)],
        out_shape=[jax.ShapeDtypeStruct((B, T, MLA_HEADS * MLA_DK), BF16),
                   jax.ShapeDtypeStruct((B, T, MLA_HEADS * MLA_DK), BF16),
                   jax.ShapeDtypeStruct((B, T, MLA_HEADS * MLA_V), BF16)],
        compiler_params=_cp("parallel", "parallel"),
        name="mla_proj",
    )(h, w, qg, wq, kvg, wkv, cos, sin)


def _attn_body(q_ref, k_ref, v_ref, o_ref, *, hq, hkv, dk, dv, n_ctx_tiles, n_ctx, n_all):
    rep = hq // hkv

    def run(nk):
        for g in range(hkv):
            kg = k_ref[0:nk, g * dk:(g + 1) * dk]
            vg = v_ref[0:nk, g * dv:(g + 1) * dv]
            v_aug = jnp.concatenate([vg, jnp.ones_like(vg)], axis=1)
            for h in range(g * rep, (g + 1) * rep):
                s = _dot_nt(q_ref[:, h * dk:(h + 1) * dk], kg)
                p = jnp.exp2(s - jnp.max(s, axis=-1, keepdims=True))
                o = jnp.dot(p.astype(BF16), v_aug, preferred_element_type=F32)
                o_ref[:, h * dv:(h + 1) * dv] = (o[:, 0:dv] / o[:, dv:dv + 1]).astype(o_ref.dtype)

    @pl.when(pl.program_id(1) < n_ctx_tiles)
    def _():
        run(n_ctx)

    @pl.when(pl.program_id(1) >= n_ctx_tiles)
    def _():
        run(n_all)


def _attention(q, k, v, hq, hkv, dk, dv, n_ctx):
    B, T, _ = q.shape
    tq = TOK_TILE
    return pl.pallas_call(
        functools.partial(_attn_body, hq=hq, hkv=hkv, dk=dk, dv=dv, n_ctx_tiles=n_ctx // tq, n_ctx=n_ctx, n_all=T),
        grid=(B, T // tq),
        in_specs=[pl.BlockSpec((None, tq, hq * dk), lambda b, i: (b, i, 0)),
                  pl.BlockSpec((None, T, hkv * dk), lambda b, i: (b, 0, 0)),
                  pl.BlockSpec((None, T, hkv * dv), lambda b, i: (b, 0, 0))],
        out_specs=pl.BlockSpec((None, tq, hq * dv), lambda b, i: (b, i, 0)),
        out_shape=jax.ShapeDtypeStruct((B, T, hq * dv), BF16),
        compiler_params=_cp("parallel", "parallel"),
        name="attention",
    )(q, k, v)


def _rope_tables(T, n_ctx, n_rot):
    quarter = n_rot // 4
    t = jnp.arange(T - n_ctx)
    row = (t // GRID_W).astype(F32)
    col = (t % GRID_W).astype(F32)
    inv = ROPE_THETA ** (-jnp.arange(quarter, dtype=F32) / quarter)
    ar = row[:, None] * inv[None, :]
    ac = col[:, None] * inv[None, :]
    pad = LANES - n_rot
    cos = jnp.concatenate([jnp.cos(ar), jnp.cos(ar), jnp.cos(ac), jnp.cos(ac), jnp.ones((T - n_ctx, pad), F32)], axis=1)
    sin = jnp.concatenate([-jnp.sin(ar), jnp.sin(ar), -jnp.sin(ac), jnp.sin(ac), jnp.zeros((T - n_ctx, pad), F32)], axis=1)
    cos = jnp.concatenate([jnp.ones((n_ctx, LANES), F32), cos], axis=0)
    sin = jnp.concatenate([jnp.zeros((n_ctx, LANES), F32), sin], axis=0)
    return cos, sin


def _moe_body(te_ref, first_ref, slot_ref, nxt_ref, nv_ref, hs_ref, w1_hbm, w3_hbm, w2_hbm, y_ref,
              f1, f3, f2, c1, c3, c2, sem, *, layer):
    i = pl.program_id(0)

    def weight_copies(e, s):
        return (pltpu.make_async_copy(w1_hbm.at[layer, e], f1.at[s], sem.at[s, 0]),
                pltpu.make_async_copy(w3_hbm.at[layer, e], f3.at[s], sem.at[s, 1]),
                pltpu.make_async_copy(w2_hbm.at[layer, e], f2.at[s], sem.at[s, 2]))

    @pl.when(i == 0)
    def _():
        for cp in weight_copies(te_ref[0], 0):
            cp.start()

    @pl.when(first_ref[i] == 1)
    def _():
        s = slot_ref[i]
        for cp in weight_copies(te_ref[i], s):
            cp.wait()

        @pl.when(nxt_ref[i] >= 0)
        def _():
            for cp in weight_copies(nxt_ref[i], 1 - s):
                cp.start(priority=WEIGHT_DMA_PRIORITY)

        c1[...] = f1[s].astype(BF16)
        c3[...] = f3[s].astype(BF16)
        c2[...] = f2[s].astype(BF16)

    @pl.when(i < nv_ref[0])
    def _():
        hs = hs_ref[...].astype(BF16)
        a = jnp.dot(hs, c1[...], preferred_element_type=F32)
        b = jnp.dot(hs, c3[...], preferred_element_type=F32)
        act = a * _sigmoid(a) * b
        y = jnp.dot(act.astype(BF16), c2[...], preferred_element_type=F32)
        y_ref[...] = y.astype(y_ref.dtype)

    @pl.when(i >= nv_ref[0])
    def _():
        y_ref[...] = jnp.zeros_like(y_ref)


def _moe_experts(tile_expert, run_first, run_slot, run_next, n_valid, hs, w1, w3, w2, layer):
    NP, D = hs.shape
    tm = MOE_TM
    DE = w1.shape[-1]
    grid_spec = pltpu.PrefetchScalarGridSpec(
        num_scalar_prefetch=5,
        grid=(NP // tm,),
        in_specs=[pl.BlockSpec((tm, D), lambda i, te, fi, sl, nx, nv: (jnp.minimum(i, nv[0] - 1), 0)),
                  pl.BlockSpec(memory_space=pl.ANY),
                  pl.BlockSpec(memory_space=pl.ANY),
                  pl.BlockSpec(memory_space=pl.ANY)],
        out_specs=pl.BlockSpec((tm, D), lambda i, *_: (i, 0)),
        scratch_shapes=[pltpu.VMEM((2, D, DE), F32), pltpu.VMEM((2, D, DE), F32), pltpu.VMEM((2, DE, D), F32),
                        pltpu.VMEM((D, DE), BF16), pltpu.VMEM((D, DE), BF16), pltpu.VMEM((DE, D), BF16),
                        pltpu.SemaphoreType.DMA((2, 3))],
    )
    return pl.pallas_call(
        functools.partial(_moe_body, layer=layer),
        grid_spec=grid_spec,
        out_shape=jax.ShapeDtypeStruct((NP, D), BF16),
        compiler_params=_cp("arbitrary"),
        name="moe_experts",
    )(tile_expert, run_first, run_slot, run_next, n_valid, hs, w1, w3, w2)


def _moe(h2, re, w1, w3, w2, layer):
    N = h2.shape[0]
    tm = MOE_TM
    n_tiles = (2 * N) // tm + N_EXPERTS
    e_flat = re.reshape(2 * N)
    onehot = (e_flat[:, None] == jnp.arange(N_EXPERTS, dtype=jnp.int32)[None, :]).astype(jnp.int32)
    csum = jnp.cumsum(onehot, axis=0)
    rank = jnp.sum(onehot * (csum - 1), axis=1)
    counts = csum[-1]
    ptiles = (counts + tm - 1) // tm
    tile_end = jnp.cumsum(ptiles)
    tile_start = tile_end - ptiles
    pos = tile_start[e_flat] * tm + rank
    n_valid = tile_end[-1:].astype(jnp.int32)
    tile_ids = jnp.arange(n_tiles, dtype=jnp.int32)
    tile_expert = jnp.minimum(jnp.sum((tile_end[None, :] <= tile_ids[:, None]).astype(jnp.int32), axis=1), N_EXPERTS - 1)
    experts = jnp.arange(N_EXPERTS, dtype=jnp.int32)
    owns = ptiles > 0
    run_no = jnp.cumsum(owns.astype(jnp.int32)) - 1
    later = jnp.where(owns[None, :] & (experts[None, :] > experts[:, None]), experts[None, :], N_EXPERTS)
    next_run = jnp.min(later, axis=1)
    next_run = jnp.where(next_run < N_EXPERTS, next_run, -1).astype(jnp.int32)
    run_first = ((tile_ids == tile_start[tile_expert]) & (tile_ids < n_valid[0])).astype(jnp.int32)
    run_slot = (run_no[tile_expert] % 2).astype(jnp.int32)
    run_next = next_run[tile_expert]
    spread = jnp.arange(n_tiles * tm, dtype=jnp.int32) % N
    src = spread.at[pos].set(jnp.arange(2 * N, dtype=jnp.int32) // 2)
    rows = lambda a, idx: a.at[idx].get(mode="promise_in_bounds")
    hs = rows(h2, src)
    y = _moe_experts(tile_expert, run_first, run_slot, run_next, n_valid, hs, w1, w3, w2, layer)
    pos2 = pos.reshape(N, 2)
    return rows(y, pos2[:, 0]), rows(y, pos2[:, 1])


def kernel(x, c, ctx, c_ctx, mod_w, mod_b, norm1_g, norm2_g, w_in, w_out, shift_prev, shift_next, decay_w0, decay_up, iclr_a0, iclr_up, gate_up, k_k, k_a, r_k, gn_g, gn_b, q_norm_g, k_norm_g, mla_q_norm_g, mla_w_uq, mla_kv_norm_g, mla_w_ukv, router_gw, router_gb, router_ew, router_eb, exp_w1, exp_w3, exp_w2, final_norm_g):
    B, S, D = x.shape
    C = ctx.shape[1]
    T = C + S
    depth = mod_w.shape[0]
    assert C == TOK_TILE and S % TOK_TILE == 0 and B <= CTX_ROW
    n_ctx_tiles = C // TOK_TILE

    cc = jnp.zeros((SUBLANES, D), F32).at[:B].set(c).at[CTX_ROW].set(c_ctx)
    mods_all = [_modulation(cc, mod_w, mod_b, l).reshape(SUBLANES * N_MOD, 1, D) for l in range(depth)]
    X, h = _join_norm(ctx, x, norm1_g[0], mods_all[0], 0, 1)

    cos_g, sin_g = _rope_tables(T, C, GQA_HEAD)
    cos_m, sin_m = _rope_tables(T, C, MLA_ROPE)
    hid = jnp.arange(RWKV_W) // RWKV_HEAD
    head_sum = (hid[:, None] == hid[None, :]).astype(F32)

    for l in range(depth):
        mods = mods_all[l]
        w_r = w_in[l][:, :RWKV_COLS].astype(BF16)
        w_g = w_in[l][:, RWKV_COLS:RWKV_COLS + GQA_COLS].astype(BF16)
        w_m = jnp.pad(w_in[l][:, RWKV_COLS + GQA_COLS:], ((0, 0), (0, MLA_COLS_PAD - MLA_COLS))).astype(BF16)

        pr = _matmul(h.reshape(B * T, D), w_r).reshape(B, T, RWKV_COLS)

        wup_pad = jnp.pad(decay_up[l], ((0, 0), (0, ICLR_RANK), (0, 0)))
        aup_pad = jnp.pad(iclr_up[l], ((0, 0), (DECAY_RANK, 0), (0, 0)))
        r, v, kh, lw, b, kt, g, bonus = _rwkv_prep(pr, shift_prev[l], shift_next[l], decay_w0[l], wup_pad,
                                                  iclr_a0[l], aup_pad, gate_up[l], k_k[l], k_a[l], r_k[l], head_sum)
        yf, yb = _rwkv_scan(r, v, kh, lw, b, kt, C)
        rwkv_out = (yf, yb, bonus, g, gn_g[l], gn_b[l], head_sum)

        q, k, vv = _gqa_proj(h, w_g, q_norm_g[l], k_norm_g[l], cos_g, sin_g)
        o_g = _attention(q, k, vv, GQA_Q_HEADS, GQA_KV_HEADS, GQA_HEAD, GQA_HEAD, C)

        wq = mla_w_uq[l].reshape(MLA_Q_RANK, MLA_HEADS, MLA_NOPE + MLA_ROPE)
        wq = jnp.pad(wq, ((0, 0), (0, 0), (0, MLA_DK - MLA_NOPE - MLA_ROPE))).reshape(MLA_Q_RANK, MLA_HEADS * MLA_DK)
        wkv = mla_w_ukv[l].reshape(MLA_KV_RANK, MLA_HEADS, MLA_NOPE + MLA_V)
        wkv = jnp.concatenate([wkv[:, :, :MLA_NOPE].reshape(MLA_KV_RANK, -1), wkv[:, :, MLA_NOPE:].reshape(MLA_KV_RANK, -1)], axis=1)
        qm, km, vm = _mla_proj(h, w_m, mla_q_norm_g[l], wq.astype(BF16), mla_kv_norm_g[l], wkv.astype(BF16), cos_m, sin_m)
        o_m = _attention(qm, km, vm, MLA_HEADS, MLA_HEADS, MLA_DK, MLA_V, C)

        last = l + 1 == depth
        wr = jnp.pad(jnp.concatenate([router_gw[l], router_ew[l]], axis=1), ((0, 0), (0, LANES - N_GROUPS - N_EXPERTS)))
        br = jnp.pad(jnp.concatenate([router_gb[l], router_eb[l]]), (0, LANES - N_GROUPS - N_EXPERTS)).reshape(1, LANES)
        X, h2, rw, re = _out_proj_router(rwkv_out, [o_g, o_m], w_out[l].astype(BF16), X, mods, C,
                                         TOK_TILE if last else OUT_TILE, last, norm2_g[l], wr, br)
        rows = X.shape[1]
        ctx_tiles = 0 if last else n_ctx_tiles

        y0, y1 = _moe(h2.reshape(B * rows, D), re.reshape(B * rows, LANES)[:, :2], exp_w1, exp_w3, exp_w2, l)
        y0 = y0.reshape(B, rows, D)
        y1 = y1.reshape(B, rows, D)
        if last:
            out = _moe_residual_final(X, y0, y1, rw, mods, 5, ctx_tiles, final_norm_g)
        else:
            X, h = _moe_residual_norm(X, y0, y1, rw, mods, 5, ctx_tiles, norm1_g[l + 1], mods_all[l + 1])
    return out
```
